```python
import math
import jax
import jax.numpy as jnp
from jax import lax
import numpy as np

D_MODEL = 2048
BATCH = 2
SEQ = 4096
DEPTH = 2

CHUNK = 64
Q_BLOCK = 128
HEAD_DIM = 128
FOX_HEADS = 6
SB_HEADS = 6
DIFF_HEADS = 4
DIFF_QK_DIM = HEAD_DIM // 2
DIFF_V_DIM = HEAD_DIM
FOX_WIDTH = FOX_HEADS * HEAD_DIM
SB_WIDTH = SB_HEADS * HEAD_DIM
DIFF_QK_WIDTH = DIFF_HEADS * 2 * DIFF_QK_DIM
DIFF_WIDTH = DIFF_HEADS * DIFF_V_DIM
N_BRANCHES = 3
IN_SPLITS = (FOX_WIDTH, FOX_WIDTH, FOX_WIDTH, SB_WIDTH, SB_WIDTH, SB_WIDTH, DIFF_QK_WIDTH, DIFF_QK_WIDTH, DIFF_WIDTH, FOX_HEADS, D_MODEL, D_MODEL, D_MODEL)
IN_WIDTH = 3 * FOX_WIDTH + 3 * SB_WIDTH + 2 * DIFF_QK_WIDTH + DIFF_WIDTH + FOX_HEADS + N_BRANCHES * D_MODEL
ROPE_THETA = 10000.0
RMS_EPS = 1e-6
N_MOD = 6
PEER_HEADS = 8
PEER_N_KEYS = 128
PEER_EXPERTS = PEER_N_KEYS * PEER_N_KEYS
PEER_TOPK = 16
PEER_KEY_DIM = 256
PEER_HALF = PEER_KEY_DIM // 2
PEER_TOKEN_BLOCK = 128

kernel_name = 'hybrid_fox_sb_diff_peer_block'


def rms_norm(x, g):
    xf = x.astype(jnp.float32)
    y = xf * lax.rsqrt(jnp.mean(xf * xf, axis=-1, keepdims=True) + RMS_EPS)
    return (y * g.astype(jnp.float32)).astype(x.dtype)


def rope(x, positions):
    d = x.shape[-1]
    inv_freq = ROPE_THETA ** (-jnp.arange(0, d, 2, dtype=jnp.float32) / d)
    ang = positions.astype(jnp.float32)[..., None] * inv_freq
    cos = jnp.cos(ang)[:, :, None, :]
    sin = jnp.sin(ang)[:, :, None, :]
    x1, x2 = x[..., : d // 2], x[..., d // 2:]
    return jnp.concatenate([x1 * cos - x2 * sin, x1 * sin + x2 * cos], axis=-1)


def _split_q_blocks(a):
    b, s = a.shape[:2]
    a = a.reshape((b, s // Q_BLOCK, Q_BLOCK) + a.shape[2:])
    return jnp.moveaxis(a, 1, 0)


def _merge_q_blocks(o):
    o = jnp.moveaxis(o, 0, 1)
    return o.reshape((o.shape[0], o.shape[1] * o.shape[2]) + o.shape[3:])


def forgetting_attention(q, k, v, log_f):
    seq = q.shape[1]
    scale = q.shape[-1] ** -0.5
    cum = jnp.cumsum(log_f, axis=1)
    cum_k = jnp.transpose(cum, (0, 2, 1))
    s_idx = jnp.arange(seq)

    def block(args):
        bi, q_blk, cum_q = args
        t_idx = bi * Q_BLOCK + jnp.arange(Q_BLOCK)
        logits = jnp.einsum('bqhd,bshd->bhqs', q_blk, k) * scale
        logits = logits + jnp.transpose(cum_q, (0, 2, 1))[..., None] - cum_k[:, :, None, :]
        mask = s_idx[None, :] <= t_idx[:, None]
        p = jax.nn.softmax(jnp.where(mask, logits, -jnp.inf), axis=-1)
        return jnp.einsum('bhqs,bshd->bqhd', p, v)

    nb = seq // Q_BLOCK
    out = lax.map(block, (jnp.arange(nb), _split_q_blocks(q), _split_q_blocks(cum)))
    return _merge_q_blocks(out)


def stick_breaking_attention(q, k, v):
    seq = q.shape[1]
    scale = q.shape[-1] ** -0.5
    s_idx = jnp.arange(seq)

    def block(args):
        bi, q_blk = args
        t_idx = bi * Q_BLOCK + jnp.arange(Q_BLOCK)
        z = jnp.einsum('bqhd,bshd->bhqs', q_blk, k) * scale
        mask = s_idx[None, :] < t_idx[:, None]
        log_beta = jax.nn.log_sigmoid(z)
        log_keep = jnp.where(mask, jax.nn.log_sigmoid(-z), 0.0)
        later = lax.cumsum(log_keep, axis=3, reverse=True) - log_keep
        w = jnp.where(mask, jnp.exp(log_beta + later), 0.0)
        return jnp.einsum('bhqs,bshd->bqhd', w, v)

    nb = seq // Q_BLOCK
    out = lax.map(block, (jnp.arange(nb), _split_q_blocks(q)))
    return _merge_q_blocks(out)


def differential_attention(q, k, v, lam):
    seq = q.shape[1]
    scale = q.shape[-1] ** -0.5
    k1, k2 = k[..., 0, :], k[..., 1, :]
    chunk_k = jnp.arange(seq) // CHUNK

    def block(args):
        bi, q1_blk, q2_blk = args
        chunk_q = (bi * Q_BLOCK + jnp.arange(Q_BLOCK)) // CHUNK
        mask = chunk_k[None, :] <= chunk_q[:, None]
        l1 = jnp.einsum('bqhd,bshd->bhqs', q1_blk, k1) * scale
        l2 = jnp.einsum('bqhd,bshd->bhqs', q2_blk, k2) * scale
        p1 = jax.nn.softmax(jnp.where(mask, l1, -jnp.inf), axis=-1)
        p2 = jax.nn.softmax(jnp.where(mask, l2, -jnp.inf), axis=-1)
        return jnp.einsum('bhqs,bshd->bqhd', p1 - lam[None, :, None, None] * p2, v)

    nb = seq // Q_BLOCK
    out = lax.map(block, (jnp.arange(nb), _split_q_blocks(q[..., 0, :]), _split_q_blocks(q[..., 1, :])))
    return _merge_q_blocks(out)


def token_mixer(h, positions, w_in, b_forget, lam_q1, lam_k1, lam_q2, lam_k2, lam_init,
                subln_g, w_br_fox, w_br_sb, w_br_diff, w_o):
    b, s, _ = h.shape
    f32 = jnp.float32
    dt = h.dtype
    proj = h @ w_in
    (fq, fk, fv, sq, sk, sv, dq, dk, dv, f_logit, g_fox, g_sb, g_diff) = jnp.split(
        proj, np.cumsum(IN_SPLITS)[:-1].tolist(), axis=-1)

    def heads(a, n, dh):
        return a.reshape(b, s, n, dh).astype(f32)

    log_f = jax.nn.log_sigmoid(f_logit.astype(f32) + b_forget.astype(f32))
    o_fox = forgetting_attention(heads(fq, FOX_HEADS, HEAD_DIM), heads(fk, FOX_HEADS, HEAD_DIM),
                                 heads(fv, FOX_HEADS, HEAD_DIM), log_f)
    o_sb = stick_breaking_attention(heads(sq, SB_HEADS, HEAD_DIM), heads(sk, SB_HEADS, HEAD_DIM),
                                    heads(sv, SB_HEADS, HEAD_DIM))
    q_d = rope(heads(dq, 2 * DIFF_HEADS, DIFF_QK_DIM), positions).reshape(b, s, DIFF_HEADS, 2, DIFF_QK_DIM)
    k_d = rope(heads(dk, 2 * DIFF_HEADS, DIFF_QK_DIM), positions).reshape(b, s, DIFF_HEADS, 2, DIFF_QK_DIM)
    v_d = heads(dv, DIFF_HEADS, DIFF_V_DIM)
    lam = (jnp.exp(jnp.sum(lam_q1.astype(f32) * lam_k1.astype(f32), axis=-1))
           - jnp.exp(jnp.sum(lam_q2.astype(f32) * lam_k2.astype(f32), axis=-1)) + lam_init)
    o_diff = differential_attention(q_d, k_d, v_d, lam)
    o_diff = rms_norm(o_diff, subln_g) * (1.0 - lam_init)

    y = (jax.nn.sigmoid(g_fox) * (o_fox.reshape(b, s, FOX_WIDTH).astype(dt) @ w_br_fox)
         + jax.nn.sigmoid(g_sb) * (o_sb.reshape(b, s, SB_WIDTH).astype(dt) @ w_br_sb)
         + jax.nn.sigmoid(g_diff) * (o_diff.reshape(b, s, DIFF_WIDTH).astype(dt) @ w_br_diff))
    return y @ w_o


def peer_ffn(h, w_pq, sub_keys, u_table, v_table):
    b, s, d = h.shape
    t = b * s
    hf = h.reshape(t, d)
    q = (hf @ w_pq).reshape(t, PEER_HEADS, 2, PEER_HALF).astype(jnp.float32)
    scores = jnp.einsum('thpd,hpkd->thpk', q, sub_keys.astype(jnp.float32))
    s_top, i_top = lax.top_k(scores, PEER_TOPK)
    cand = (s_top[:, :, 0, :, None] + s_top[:, :, 1, None, :]).reshape(t, PEER_HEADS, PEER_TOPK * PEER_TOPK)
    cand_idx = (i_top[:, :, 0, :, None] * PEER_N_KEYS + i_top[:, :, 1, None, :]).reshape(t, PEER_HEADS, PEER_TOPK * PEER_TOPK)
    best, pos = lax.top_k(cand, PEER_TOPK)
    expert = jnp.take_along_axis(cand_idx, pos, axis=-1).reshape(t, PEER_HEADS * PEER_TOPK)
    gate = jax.nn.softmax(best, axis=-1).reshape(t, PEER_HEADS * PEER_TOPK).astype(h.dtype)

    def block(args):
        xb, eb, gb = args
        hid = jnp.einsum('td,ted->te', xb, u_table[eb])
        act = gb * jax.nn.gelu(hid, approximate=False)
        return jnp.einsum('te,ted->td', act, v_table[eb])

    nblk = t // PEER_TOKEN_BLOCK
    out = lax.map(block, (hf.reshape(nblk, PEER_TOKEN_BLOCK, d),
                          expert.reshape(nblk, PEER_TOKEN_BLOCK, -1),
                          gate.reshape(nblk, PEER_TOKEN_BLOCK, -1)))
    return out.reshape(b, s, d)


def setup_inputs(seed: int = 0) -> dict:
    key = jax.random.key(seed)
    ks = jax.random.split(key, 24)
    f32 = jnp.float32
    d = D_MODEL

    def nrm(k, shape, scale):
        return jax.random.normal(k, shape, f32) * scale

    offset = jax.random.randint(ks[2], (BATCH, 1), 0, 32768, dtype=jnp.int32)
    positions = offset + jnp.arange(SEQ, dtype=jnp.int32)[None, :]
    return {
        'x': nrm(ks[0], (BATCH, SEQ, d), 1.0),
        'c': nrm(ks[1], (BATCH, d), 1.0),
        'positions': positions,
        'ada_w': nrm(ks[3], (DEPTH, d, N_MOD * d), 0.5 * d ** -0.5),
        'ada_b': nrm(ks[4], (DEPTH, N_MOD * d), 0.01),
        'norm1_g': 1.0 + nrm(ks[5], (DEPTH, d), 0.02),
        'norm2_g': 1.0 + nrm(ks[6], (DEPTH, d), 0.02),
        'w_in': nrm(ks[7], (DEPTH, d, IN_WIDTH), d ** -0.5),
        'b_forget': 2.0 + nrm(ks[8], (DEPTH, FOX_HEADS), 0.5),
        'lam_q1': nrm(ks[9], (DEPTH, DIFF_HEADS, DIFF_QK_DIM), 0.1),
        'lam_k1': nrm(ks[10], (DEPTH, DIFF_HEADS, DIFF_QK_DIM), 0.1),
        'lam_q2': nrm(ks[11], (DEPTH, DIFF_HEADS, DIFF_QK_DIM), 0.1),
        'lam_k2': nrm(ks[12], (DEPTH, DIFF_HEADS, DIFF_QK_DIM), 0.1),
        'diff_subln_g': 1.0 + nrm(ks[13], (DEPTH, DIFF_V_DIM), 0.02),
        'w_br_fox': nrm(ks[14], (DEPTH, FOX_WIDTH, d), FOX_WIDTH ** -0.5),
        'w_br_sb': nrm(ks[15], (DEPTH, SB_WIDTH, d), SB_WIDTH ** -0.5),
        'w_br_diff': nrm(ks[16], (DEPTH, DIFF_WIDTH, d), DIFF_WIDTH ** -0.5),
        'w_o': nrm(ks[17], (DEPTH, d, d), d ** -0.5),
        'peer_wq': nrm(ks[18], (DEPTH, d, PEER_HEADS * PEER_KEY_DIM), d ** -0.5),
        'peer_sub_keys': nrm(ks[19], (DEPTH, PEER_HEADS, 2, PEER_N_KEYS, PEER_HALF), PEER_HALF ** -0.5),
        'peer_u': nrm(ks[20], (DEPTH, PEER_EXPERTS, d), d ** -0.5),
        'peer_v': nrm(ks[21], (DEPTH, PEER_EXPERTS, d), 1.0),
        'final_norm_g': 1.0 + nrm(ks[22], (d,), 0.02),
    }


def reference(x, c, positions, ada_w, ada_b, norm1_g, norm2_g, w_in, b_forget,
              lam_q1, lam_k1, lam_q2, lam_k2, diff_subln_g, w_br_fox, w_br_sb, w_br_diff,
              w_o, peer_wq, peer_sub_keys, peer_u, peer_v, final_norm_g):
    cond = jax.nn.silu(c)
    for l in range(DEPTH):
        lam_init = 0.8 - 0.6 * math.exp(-0.3 * l)
        mod = cond @ ada_w[l] + ada_b[l]
        sh1, sc1, gt1, sh2, sc2, gt2 = jnp.split(mod, N_MOD, axis=-1)
        h = rms_norm(x, norm1_g[l]) * (1.0 + sc1[:, None, :]) + sh1[:, None, :]
        x = x + gt1[:, None, :] * token_mixer(h, positions, w_in[l], b_forget[l], lam_q1[l], lam_k1[l],
                                               lam_q2[l], lam_k2[l], lam_init, diff_subln_g[l],
                                               w_br_fox[l], w_br_sb[l], w_br_diff[l], w_o[l])
        h = rms_norm(x, norm2_g[l]) * (1.0 + sc2[:, None, :]) + sh2[:, None, :]
        x = x + gt2[:, None, :] * peer_ffn(h, peer_wq[l], peer_sub_keys[l], peer_u[l], peer_v[l])
    return rms_norm(x, final_norm_g)
```

```python
import functools
import math

import jax
import jax.numpy as jnp
import numpy as np
from jax import lax
from jax.experimental import pallas as pl
from jax.experimental.pallas import tpu as pltpu

F32 = jnp.float32
BF16 = jnp.bfloat16

D_MODEL = 2048
DEPTH = 2
CHUNK = 64
HEAD_DIM = 128
FOX_HEADS = 6
SB_HEADS = 6
DIFF_HEADS = 4
DIFF_QK_DIM = HEAD_DIM // 2
FOX_WIDTH = FOX_HEADS * HEAD_DIM
SB_WIDTH = SB_HEADS * HEAD_DIM
DIFF_WIDTH = DIFF_HEADS * HEAD_DIM
QKV_WIDTH = 3 * FOX_WIDTH + 3 * SB_WIDTH + 3 * DIFF_WIDTH
GATE_OFFSET = QKV_WIDTH + FOX_HEADS
ROPE_THETA = 10000.0
RMS_EPS = 1e-6
N_MOD = 6
PEER_HEADS = 8
PEER_N_KEYS = 128
PEER_TOPK = 16
PEER_HALF = 128

LANES = 128
VMEM_LIMIT = 56 * 1024 * 1024
NEG_INF = float("-inf")

COL_FQ, COL_FK, COL_FV = 0, FOX_HEADS, 2 * FOX_HEADS
COL_SQ, COL_SK, COL_SV = 18, 24, 30
COL_DQ, COL_DK, COL_DV = 36, 40, 44
COL_GATE = QKV_WIDTH // LANES


def _params(*sem):
    return pltpu.CompilerParams(dimension_semantics=sem, vmem_limit_bytes=VMEM_LIMIT)


def _mod_kernel(c_ref, w_ref, b_ref, o_ref):
    c = c_ref[...]
    cond = c * jax.nn.sigmoid(c)
    hi = cond.astype(BF16)
    lo = (cond - hi.astype(F32)).astype(BF16)
    w = w_ref[0].astype(BF16)
    acc = jnp.dot(hi, w, preferred_element_type=F32) + jnp.dot(lo, w, preferred_element_type=F32)
    o_ref[0] = acc + b_ref[0]


def _mod_call(c_pad, ada_w, ada_b):
    depth, d, n = ada_w.shape
    tn = 1024
    return pl.pallas_call(
        _mod_kernel,
        grid=(depth, n // tn),
        in_specs=[
            pl.BlockSpec((8, d), lambda l, j: (0, 0)),
            pl.BlockSpec((1, d, tn), lambda l, j: (l, 0, j)),
            pl.BlockSpec((1, 1, tn), lambda l, j: (l, 0, j)),
        ],
        out_specs=pl.BlockSpec((1, 8, tn), lambda l, j: (l, 0, j)),
        out_shape=jax.ShapeDtypeStruct((depth, 8, n), F32),
        compiler_params=_params("arbitrary", "arbitrary"),
        name="adaln_mod",
    )(c_pad, ada_w, ada_b.reshape(depth, 1, n))


def _norm_mod_kernel(x_ref, g_ref, sc_ref, sh_ref, *o_refs, transposed):
    x = x_ref[0]
    y = x * lax.rsqrt(jnp.mean(x * x, axis=-1, keepdims=True) + RMS_EPS) * g_ref[...]
    h = y * (1.0 + sc_ref[0]) + sh_ref[0]
    o_refs[0][0] = h.astype(o_refs[0].dtype)
    if transposed:
        o_refs[1][0] = h.T.astype(o_refs[1].dtype)


def _norm_mod_call(x, g, sc, sh, out_dtype, transposed=False):
    b, s, d = x.shape
    tr = 256
    out_shape = [jax.ShapeDtypeStruct((b, s, d), out_dtype)]
    out_specs = [pl.BlockSpec((1, tr, d), lambda i, j: (i, j, 0))]
    if transposed:
        out_shape.append(jax.ShapeDtypeStruct((b, d, s), out_dtype))
        out_specs.append(pl.BlockSpec((1, d, tr), lambda i, j: (i, 0, j)))
    return pl.pallas_call(
        functools.partial(_norm_mod_kernel, transposed=transposed),
        grid=(b, s // tr),
        in_specs=[
            pl.BlockSpec((1, tr, d), lambda i, j: (i, j, 0)),
            pl.BlockSpec((1, d), lambda i, j: (0, 0)),
            pl.BlockSpec((1, 1, d), lambda i, j: (i, 0, 0)),
            pl.BlockSpec((1, 1, d), lambda i, j: (i, 0, 0)),
        ],
        out_specs=out_specs,
        out_shape=out_shape,
        compiler_params=_params("arbitrary", "arbitrary"),
        name="norm_mod",
    )(x, g, sc, sh)


def _matmul_kernel(a_ref, w_ref, o_ref):
    o_ref[...] = jnp.dot(a_ref[...], w_ref[...], preferred_element_type=F32).astype(o_ref.dtype)


def _matmul_call(a, w, out_dtype, bm=1024, bn=1024):
    m, k = a.shape
    n = w.shape[1]
    bn = min(bn, n)
    return pl.pallas_call(
        _matmul_kernel,
        grid=(m // bm, n // bn),
        in_specs=[pl.BlockSpec((bm, k), lambda i, j: (i, 0)),
                  pl.BlockSpec((k, bn), lambda i, j: (0, j))],
        out_specs=pl.BlockSpec((bm, bn), lambda i, j: (i, j)),
        out_shape=jax.ShapeDtypeStruct((m, n), out_dtype),
        compiler_params=_params("arbitrary", "arbitrary"),
        name="matmul",
    )(a, w)


def _matmul_residual_kernel(a_ref, w_ref, x_ref, gt_ref, o_ref):
    acc = jnp.dot(a_ref[0], w_ref[...], preferred_element_type=F32)
    o_ref[0] = x_ref[0] + gt_ref[0] * acc


def _matmul_residual_call(a, w, x, gt, bm=1024, bn=1024):
    b, s, k = a.shape
    n = w.shape[1]
    return pl.pallas_call(
        _matmul_residual_kernel,
        grid=(b, s // bm, n // bn),
        in_specs=[pl.BlockSpec((1, bm, k), lambda bi, i, j: (bi, i, 0)),
                  pl.BlockSpec((k, bn), lambda bi, i, j: (0, j)),
                  pl.BlockSpec((1, bm, bn), lambda bi, i, j: (bi, i, j)),
                  pl.BlockSpec((1, 1, bn), lambda bi, i, j: (bi, 0, j))],
        out_specs=pl.BlockSpec((1, bm, bn), lambda bi, i, j: (bi, i, j)),
        out_shape=jax.ShapeDtypeStruct((b, s, n), F32),
        compiler_params=_params("arbitrary", "arbitrary", "arbitrary"),
        name="matmul_residual",
    )(a, w, x, gt)


def _split3(x):
    hi = x.astype(BF16)
    r = x - hi.astype(F32)
    mid = r.astype(BF16)
    lo = (r - mid.astype(F32)).astype(BF16)
    return hi, mid, lo


def _forget_cumsum_kernel(fl_ref, bf_ref, col_ref, row_ref):
    s = fl_ref.shape[1]
    blk = 256
    logf = jax.nn.log_sigmoid(fl_ref[0] + bf_ref[...])
    r = lax.broadcasted_iota(jnp.int32, (blk, blk), 0)
    c = lax.broadcasted_iota(jnp.int32, (blk, blk), 1)
    tri = jnp.where(c <= r, 1.0, 0.0).astype(BF16)
    carry = jnp.zeros((1, LANES), F32)
    parts = []
    for i in range(s // blk):
        hi, mid, lo = _split3(logf[i * blk:(i + 1) * blk])
        cs = (jnp.dot(tri, hi, preferred_element_type=F32)
              + jnp.dot(tri, mid, preferred_element_type=F32)
              + jnp.dot(tri, lo, preferred_element_type=F32)) + carry
        carry = cs[blk - 1:blk, :]
        parts.append(cs)
    cum = jnp.concatenate(parts, axis=0)
    row_ref[0] = cum.T[:8, :]
    for h in range(FOX_HEADS):
        col_ref[0, h] = jnp.broadcast_to(cum[:, h:h + 1], (s, LANES))


def _forget_cumsum_call(f_logit, b_forget_pad):
    b, s, _ = f_logit.shape
    return pl.pallas_call(
        _forget_cumsum_kernel,
        grid=(b,),
        in_specs=[pl.BlockSpec((1, s, LANES), lambda i: (i, 0, 0)),
                  pl.BlockSpec((1, LANES), lambda i: (0, 0))],
        out_specs=[pl.BlockSpec((1, FOX_HEADS, s, LANES), lambda i: (i, 0, 0, 0)),
                   pl.BlockSpec((1, 8, s), lambda i: (i, 0, 0))],
        out_shape=[jax.ShapeDtypeStruct((b, FOX_HEADS, s, LANES), F32),
                   jax.ShapeDtypeStruct((b, 8, s), F32)],
        compiler_params=_params("arbitrary"),
        name="forget_cumsum",
    )(f_logit, b_forget_pad)


ATT_TILE = 256


def _qk(q, k):
    return lax.dot_general(q, k, (((1,), (1,)), ((), ())), preferred_element_type=F32)


def _fox_kernel(q_ref, k_ref, v_ref, fc_ref, fr_ref, o_ref):
    t = ATT_TILE
    qi = pl.program_id(2)
    h = pl.program_id(1)
    q = q_ref[...]
    fcol = jnp.tile(fc_ref[0, 0], (1, t // LANES))
    scale = HEAD_DIM ** -0.5

    def logits(j):
        kb = k_ref[pl.ds(pl.multiple_of(j * t, t), t), :]
        frow = fr_ref[0, pl.ds(h, 1), pl.ds(pl.multiple_of(j * t, t), t)]
        return _qk(q, kb) * scale + (fcol - frow)

    def update(carry, s, j):
        m, l, acc = carry
        m_new = jnp.maximum(m, jnp.max(s, axis=-1, keepdims=True))
        alpha = jnp.exp(m - m_new)
        p = jnp.exp(s - m_new)
        l = alpha * l + jnp.sum(p, axis=-1, keepdims=True)
        vb = v_ref[pl.ds(pl.multiple_of(j * t, t), t), :]
        acc = alpha * acc + jnp.dot(p.astype(BF16), vb, preferred_element_type=F32)
        return m_new, l, acc

    def body(j, carry):
        return update(carry, logits(j), j)

    init = (jnp.full((t, 1), NEG_INF, F32), jnp.zeros((t, 1), F32), jnp.zeros((t, HEAD_DIM), F32))
    r = lax.broadcasted_iota(jnp.int32, (t, t), 0)
    c = lax.broadcasted_iota(jnp.int32, (t, t), 1)
    s_diag = jnp.where(c <= r, logits(qi), NEG_INF)
    carry = update(init, s_diag, qi)
    m, l, acc = lax.fori_loop(0, qi, body, carry)
    o_ref[...] = (acc / l).astype(o_ref.dtype)


def _fox_call(proj, f_col, f_row, batch, seq):
    t = ATT_TILE
    nq = seq // t
    return pl.pallas_call(
        _fox_kernel,
        grid=(batch, FOX_HEADS, nq),
        in_specs=[
            pl.BlockSpec((t, HEAD_DIM), lambda b, h, i: (b * nq + i, COL_FQ + h)),
            pl.BlockSpec((seq, HEAD_DIM), lambda b, h, i: (b, COL_FK + h)),
            pl.BlockSpec((seq, HEAD_DIM), lambda b, h, i: (b, COL_FV + h)),
            pl.BlockSpec((1, 1, t, LANES), lambda b, h, i: (b, h, i, 0)),
            pl.BlockSpec((1, 8, seq), lambda b, h, i: (b, 0, 0)),
        ],
        out_specs=pl.BlockSpec((t, HEAD_DIM), lambda b, h, i: (b * nq + i, h)),
        out_shape=jax.ShapeDtypeStruct((batch * seq, FOX_WIDTH), BF16),
        compiler_params=_params("arbitrary", "arbitrary", "arbitrary"),
        name="fox_attention",
    )(proj, proj, proj, f_col, f_row)


def _sb_kernel(q_ref, k_ref, v_ref, tri_ref, o_ref):
    t = ATT_TILE
    qi = pl.program_id(2)
    q = q_ref[...]
    tri = tri_ref[...]
    scale = HEAD_DIM ** -0.5

    def block(j, run, acc, mask):
        start = pl.multiple_of(j * t, t)
        kb = k_ref[pl.ds(start, t), :]
        z = _qk(q, kb) * scale
        lk = -(jnp.maximum(z, 0.0) + jnp.log(1.0 + jnp.exp(-jnp.abs(z))))
        if mask is not None:
            lk = jnp.where(mask, lk, 0.0)
        hi = lk.astype(BF16)
        lo = (lk - hi.astype(F32)).astype(BF16)
        later = (jnp.dot(hi, tri, preferred_element_type=F32)
                 + jnp.dot(lo, tri, preferred_element_type=F32))
        w = jnp.exp(z + lk + later + run)
        if mask is not None:
            w = jnp.where(mask, w, 0.0)
        vb = v_ref[pl.ds(start, t), :]
        acc = acc + jnp.dot(w.astype(BF16), vb, preferred_element_type=F32)
        run = run + jnp.sum(lk, axis=-1, keepdims=True)
        return run, acc

    r = lax.broadcasted_iota(jnp.int32, (t, t), 0)
    c = lax.broadcasted_iota(jnp.int32, (t, t), 1)
    run, acc = block(qi, jnp.zeros((t, 1), F32), jnp.zeros((t, HEAD_DIM), F32), c < r)

    def body(i, carry):
        return block(qi - 1 - i, carry[0], carry[1], None)

    run, acc = lax.fori_loop(0, qi, body, (run, acc))
    o_ref[...] = acc.astype(o_ref.dtype)


def _sb_call(proj, tri, batch, seq):
    t = ATT_TILE
    nq = seq // t
    return pl.pallas_call(
        _sb_kernel,
        grid=(batch, SB_HEADS, nq),
        in_specs=[
            pl.BlockSpec((t, HEAD_DIM), lambda b, h, i: (b * nq + i, COL_SQ + h)),
            pl.BlockSpec((seq, HEAD_DIM), lambda b, h, i: (b, COL_SK + h)),
            pl.BlockSpec((seq, HEAD_DIM), lambda b, h, i: (b, COL_SV + h)),
            pl.BlockSpec((t, t), lambda b, h, i: (0, 0)),
        ],
        out_specs=pl.BlockSpec((t, HEAD_DIM), lambda b, h, i: (b * nq + i, h)),
        out_shape=jax.ShapeDtypeStruct((batch * seq, SB_WIDTH), BF16),
        compiler_params=_params("arbitrary", "arbitrary", "arbitrary"),
        name="sb_attention",
    )(proj, proj, proj, tri)


def _rope_kernel(x_ref, pos_ref, freq_ref, o_ref, *, out_scale):
    half = DIFF_QK_DIM // 2
    ang = pos_ref[...].astype(F32) * freq_ref[...]
    cos = jnp.cos(ang)
    sin = jnp.sin(ang)
    lane = lax.broadcasted_iota(jnp.int32, ang.shape, 1)
    first = (lane % DIFF_QK_DIM) < half
    sin_signed = jnp.where(first, -sin, sin)
    for cb in range(x_ref.shape[1] // LANES):
        x = x_ref[:, cb * LANES:(cb + 1) * LANES].astype(F32)
        partner = jnp.where(first, pltpu.roll(x, LANES - half, 1), pltpu.roll(x, half, 1))
        y = x * cos + partner * sin_signed
        o_ref[:, cb * LANES:(cb + 1) * LANES] = (y * out_scale).astype(o_ref.dtype)


def _rope_call(proj, pos_col, freq_row, col_block, out_scale):
    tokens = proj.shape[0]
    tr = 512
    width = DIFF_HEADS * HEAD_DIM
    return pl.pallas_call(
        functools.partial(_rope_kernel, out_scale=out_scale),
        grid=(tokens // tr,),
        in_specs=[pl.BlockSpec((tr, width), lambda i: (i, col_block * LANES // width)),
                  pl.BlockSpec((tr, 1), lambda i: (i, 0)),
                  pl.BlockSpec((1, LANES), lambda i: (0, 0))],
        out_specs=pl.BlockSpec((tr, width), lambda i: (i, 0)),
        out_shape=jax.ShapeDtypeStruct((tokens, width), BF16),
        compiler_params=_params("arbitrary"),
        name="rope",
    )(proj, pos_col, freq_row)


def _diff_kernel(q_ref, k_ref, v_ref, lq1_ref, lk1_ref, lq2_ref, lk2_ref, g_ref, o_ref, *, lam_init):
    t = ATT_TILE
    qi = pl.program_id(2)
    q = q_ref[...]
    lane = lax.broadcasted_iota(jnp.int32, q.shape, 1)
    zero = jnp.zeros_like(q)
    q1 = jnp.where(lane < DIFF_QK_DIM, q, zero)
    q2 = jnp.where(lane < DIFF_QK_DIM, zero, q)
    lam = (jnp.exp(jnp.sum(lq1_ref[0] * lk1_ref[0], axis=-1, keepdims=True))
           - jnp.exp(jnp.sum(lq2_ref[0] * lk2_ref[0], axis=-1, keepdims=True)) + lam_init)

    def update(carry, s, vb):
        m, l, acc = carry
        m_new = jnp.maximum(m, jnp.max(s, axis=-1, keepdims=True))
        alpha = jnp.exp(m - m_new)
        p = jnp.exp(s - m_new)
        l = alpha * l + jnp.sum(p, axis=-1, keepdims=True)
        acc = alpha * acc + jnp.dot(p.astype(BF16), vb, preferred_element_type=F32)
        return m_new, l, acc

    def block(j, carry, mask):
        start = pl.multiple_of(j * t, t)
        kb = k_ref[pl.ds(start, t), :]
        vb = v_ref[pl.ds(start, t), :]
        s1 = _qk(q1, kb)
        s2 = _qk(q2, kb)
        if mask is not None:
            s1 = jnp.where(mask, s1, NEG_INF)
            s2 = jnp.where(mask, s2, NEG_INF)
        return update(carry[0], s1, vb), update(carry[1], s2, vb)

    def init():
        return (jnp.full((t, 1), NEG_INF, F32), jnp.zeros((t, 1), F32), jnp.zeros((t, HEAD_DIM), F32))

    r = lax.broadcasted_iota(jnp.int32, (t, t), 0)
    c = lax.broadcasted_iota(jnp.int32, (t, t), 1)
    carry = block(qi, (init(), init()), (c // CHUNK) <= (r // CHUNK))
    (m1, l1, a1), (m2, l2, a2) = lax.fori_loop(0, qi, lambda j, cr: block(j, cr, None), carry)
    o = a1 / l1 - lam * (a2 / l2)
    o = o * lax.rsqrt(jnp.mean(o * o, axis=-1, keepdims=True) + RMS_EPS) * g_ref[...]
    o_ref[...] = (o * (1.0 - lam_init)).astype(o_ref.dtype)


def _diff_call(qd, kd, proj, lam_q1, lam_k1, lam_q2, lam_k2, subln_g, lam_init, batch, seq):
    t = ATT_TILE
    nq = seq // t
    lam_spec = pl.BlockSpec((1, 1, DIFF_QK_DIM), lambda b, h, i: (h, 0, 0))
    r3 = lambda a: a.reshape(DIFF_HEADS, 1, DIFF_QK_DIM)
    return pl.pallas_call(
        functools.partial(_diff_kernel, lam_init=lam_init),
        grid=(batch, DIFF_HEADS, nq),
        in_specs=[
            pl.BlockSpec((t, HEAD_DIM), lambda b, h, i: (b * nq + i, h)),
            pl.BlockSpec((seq, HEAD_DIM), lambda b, h, i: (b, h)),
            pl.BlockSpec((seq, HEAD_DIM), lambda b, h, i: (b, COL_DV + h)),
            lam_spec, lam_spec, lam_spec, lam_spec,
            pl.BlockSpec((1, HEAD_DIM), lambda b, h, i: (0, 0)),
        ],
        out_specs=pl.BlockSpec((t, HEAD_DIM), lambda b, h, i: (b * nq + i, h)),
        out_shape=jax.ShapeDtypeStruct((batch * seq, DIFF_WIDTH), BF16),
        compiler_params=_params("arbitrary", "arbitrary", "arbitrary"),
        name="diff_attention",
    )(qd, kd, proj, r3(lam_q1), r3(lam_k1), r3(lam_q2), r3(lam_k2), subln_g.reshape(1, HEAD_DIM))


def _merge_kernel(of_ref, os_ref, od_ref, wf_ref, ws_ref, wd_ref, gf_ref, gs_ref, gd_ref, o_ref):
    def branch(o_r, w_r, g_r):
        return jax.nn.sigmoid(g_r[...].astype(F32)) * jnp.dot(o_r[...], w_r[...], preferred_element_type=F32)

    y = branch(of_ref, wf_ref, gf_ref) + branch(os_ref, ws_ref, gs_ref) + branch(od_ref, wd_ref, gd_ref)
    o_ref[...] = y.astype(o_ref.dtype)


def _merge_call(o_fox, o_sb, o_diff, w_fox, w_sb, w_diff, proj, bm=1024, bn=512):
    m = o_fox.shape[0]
    d = w_fox.shape[1]
    gate_blocks = d // bn

    def gate_spec(k):
        return pl.BlockSpec((bm, bn), lambda i, j: (i, QKV_WIDTH // bn + k * gate_blocks + j))

    return pl.pallas_call(
        _merge_kernel,
        grid=(m // bm, d // bn),
        in_specs=[
            pl.BlockSpec((bm, FOX_WIDTH), lambda i, j: (i, 0)),
            pl.BlockSpec((bm, SB_WIDTH), lambda i, j: (i, 0)),
            pl.BlockSpec((bm, DIFF_WIDTH), lambda i, j: (i, 0)),
            pl.BlockSpec((FOX_WIDTH, bn), lambda i, j: (0, j)),
            pl.BlockSpec((SB_WIDTH, bn), lambda i, j: (0, j)),
            pl.BlockSpec((DIFF_WIDTH, bn), lambda i, j: (0, j)),
            gate_spec(0), gate_spec(1), gate_spec(2),
        ],
        out_specs=pl.BlockSpec((bm, bn), lambda i, j: (i, j)),
        out_shape=jax.ShapeDtypeStruct((m, d), BF16),
        compiler_params=_params("arbitrary", "arbitrary"),
        name="branch_merge",
    )(o_fox, o_sb, o_diff, w_fox, w_sb, w_diff, proj, proj, proj)


def _extract_top(cur, count):
    rows = []
    for _ in range(count):
        m = jnp.max(cur, axis=0, keepdims=True)
        rows.append(m)
        cur = jnp.where(cur == m, NEG_INF, cur)
    return rows


def _peer_topk_kernel(q_ref, keys_ref, s1_ref, s2_ref, st_ref):
    k = PEER_TOPK
    for h in range(PEER_HEADS):
        tops = []
        for p, s_ref in ((0, s1_ref), (1, s2_ref)):
            col = (2 * h + p) * PEER_HALF
            st = _qk(keys_ref[h, p], q_ref[:, col:col + PEER_HALF])
            s_ref[h] = st
            tops.append(_extract_top(st, k))
        a_rows, b_rows = tops
        b = jnp.concatenate(b_rows, axis=0)
        cand = jnp.concatenate([a_row + b for a_row in a_rows], axis=0)
        best = _extract_top(cand, k)
        z = jnp.zeros_like(best[0])
        for row in best:
            z = z + jnp.exp(row - best[0])
        st_ref[h] = jnp.concatenate([best[k - 1], a_rows[0], b_rows[0], z, z, z, z, z], axis=0)


def _peer_topk_call(qp, keys):
    tokens = qp.shape[0]
    tq = 512
    score_spec = pl.BlockSpec((PEER_HEADS, PEER_N_KEYS, tq), lambda i: (0, 0, i))
    score_shape = jax.ShapeDtypeStruct((PEER_HEADS, PEER_N_KEYS, tokens), F32)
    return pl.pallas_call(
        _peer_topk_kernel,
        grid=(tokens // tq,),
        in_specs=[pl.BlockSpec((tq, qp.shape[1]), lambda i: (i, 0)),
                  pl.BlockSpec(keys.shape, lambda i: (0, 0, 0, 0))],
        out_specs=[score_spec, score_spec, pl.BlockSpec((PEER_HEADS, 8, tq), lambda i: (0, 0, i))],
        out_shape=[score_shape, score_shape, jax.ShapeDtypeStruct((PEER_HEADS, 8, tokens), F32)],
        compiler_params=_params("arbitrary"),
        name="peer_topk",
    )(qp, keys)


PEER_TM = 512
PEER_ROWS = 4


def _peer_ffn_kernel(ht_ref, u_ref, vt_ref, s1_ref, s2_ref, st_ref, x_ref, gt_ref, o_ref, acc_ref, b_ref):
    e = pl.program_id(1)

    @pl.when(e == 0)
    def _():
        acc_ref[...] = jnp.zeros_like(acc_ref)
        for h in range(PEER_HEADS):
            b_ref[h] = jnp.exp(s2_ref[h] - st_ref[h, 2:3, :])

    hid = jnp.dot(u_ref[...], ht_ref[0], preferred_element_type=F32)
    acts = []
    for ii in range(PEER_ROWS):
        i = e * PEER_ROWS + ii
        gate = jnp.zeros((PEER_N_KEYS, hid.shape[1]), F32)
        for h in range(PEER_HEADS):
            s1 = s1_ref[h, pl.ds(i, 1), :]
            a_row = jnp.exp(s1 - st_ref[h, 1:2, :]) / st_ref[h, 3:4, :]
            chosen = (s1 + s2_ref[h]) >= st_ref[h, 0:1, :]
            gate = gate + jnp.where(chosen, a_row * b_ref[h], 0.0)
        hblk = hid[ii * PEER_N_KEYS:(ii + 1) * PEER_N_KEYS]
        gelu = 0.5 * hblk * (1.0 + lax.erf(hblk * (2.0 ** -0.5)))
        acts.append((gate * gelu).astype(BF16))
    act = jnp.concatenate(acts, axis=0)
    acc_ref[...] += jnp.dot(vt_ref[...], act, preferred_element_type=F32)

    @pl.when(e == pl.num_programs(1) - 1)
    def _():
        o_ref[0] = x_ref[0] + gt_ref[0] * acc_ref[...].T


def _peer_ffn_call(ht, u, vt, s1t, s2t, stats, x, gt):
    b, d, s = ht.shape
    tm = PEER_TM
    te = PEER_ROWS * PEER_N_KEYS
    n_e = u.shape[0] // te
    per_b = s // tm
    tok = lambda i: (0, 0, i)
    return pl.pallas_call(
        _peer_ffn_kernel,
        grid=(b * per_b, n_e),
        in_specs=[
            pl.BlockSpec((1, d, tm), lambda i, e: (i // per_b, 0, i % per_b)),
            pl.BlockSpec((te, d), lambda i, e: (e, 0)),
            pl.BlockSpec((d, te), lambda i, e: (0, e)),
            pl.BlockSpec((PEER_HEADS, PEER_N_KEYS, tm), lambda i, e: tok(i)),
            pl.BlockSpec((PEER_HEADS, PEER_N_KEYS, tm), lambda i, e: tok(i)),
            pl.BlockSpec((PEER_HEADS, 8, tm), lambda i, e: tok(i)),
            pl.BlockSpec((1, tm, d), lambda i, e: (i // per_b, i % per_b, 0)),
            pl.BlockSpec((1, 1, d), lambda i, e: (i // per_b, 0, 0)),
        ],
        out_specs=pl.BlockSpec((1, tm, d), lambda i, e: (i // per_b, i % per_b, 0)),
        out_shape=jax.ShapeDtypeStruct(x.shape, F32),
        scratch_shapes=[pltpu.VMEM((d, tm), F32), pltpu.VMEM((PEER_HEADS, PEER_N_KEYS, tm), F32)],
        compiler_params=_params("arbitrary", "arbitrary"),
        name="peer_ffn",
    )(ht, u, vt, s1t, s2t, stats, x, gt)


def _regroup_w_in(w):
    main = jnp.concatenate([w[:, :QKV_WIDTH], w[:, GATE_OFFSET:]], axis=1).astype(BF16)
    forget = jnp.pad(w[:, QKV_WIDTH:GATE_OFFSET], ((0, 0), (0, LANES - FOX_HEADS))).astype(BF16)
    return main, forget


def kernel(x, c, positions, ada_w, ada_b, norm1_g, norm2_g, w_in, b_forget, lam_q1, lam_k1, lam_q2, lam_k2,
           diff_subln_g, w_br_fox, w_br_sb, w_br_diff, w_o, peer_wq, peer_sub_keys, peer_u, peer_v,
           final_norm_g):
    batch, seq, d = x.shape
    tokens = batch * seq
    assert batch <= 8 and seq % 1024 == 0 and d == D_MODEL

    c_pad = jnp.zeros((8, d), F32).at[:batch].set(c)
    mod = _mod_call(c_pad, ada_w, ada_b)[:, :batch].reshape(DEPTH, batch, N_MOD, 1, d)
    pos_col = positions.reshape(tokens, 1)
    inv_freq = ROPE_THETA ** (-jnp.arange(0, DIFF_QK_DIM, 2, dtype=F32) / DIFF_QK_DIM)
    freq_row = jnp.tile(inv_freq, LANES // inv_freq.shape[0]).reshape(1, LANES)
    ri = np.arange(ATT_TILE)
    tri_after = jnp.asarray(ri[:, None] > ri[None, :], BF16)

    for l in range(DEPTH):
        lam_init = 0.8 - 0.6 * math.exp(-0.3 * l)
        sh1, sc1, gt1, sh2, sc2, gt2 = (mod[l, :, k] for k in range(N_MOD))

        h1 = _norm_mod_call(x, norm1_g[l].reshape(1, d), sc1, sh1, BF16)[0].reshape(tokens, d)
        w_main, w_forget = _regroup_w_in(w_in[l])
        proj = _matmul_call(h1, w_main, BF16)
        f_logit = _matmul_call(h1, w_forget, F32).reshape(batch, seq, LANES)
        b_forget_pad = jnp.pad(b_forget[l], (0, LANES - FOX_HEADS)).reshape(1, LANES)
        f_col, f_row = _forget_cumsum_call(f_logit, b_forget_pad)

        o_fox = _fox_call(proj, f_col, f_row, batch, seq)
        o_sb = _sb_call(proj, tri_after, batch, seq)
        qd = _rope_call(proj, pos_col, freq_row, COL_DQ, DIFF_QK_DIM ** -0.5)
        kd = _rope_call(proj, pos_col, freq_row, COL_DK, 1.0)
        o_diff = _diff_call(qd, kd, proj, lam_q1[l], lam_k1[l], lam_q2[l], lam_k2[l], diff_subln_g[l],
                            lam_init, batch, seq)

        y = _merge_call(o_fox, o_sb, o_diff, w_br_fox[l].astype(BF16), w_br_sb[l].astype(BF16),
                        w_br_diff[l].astype(BF16), proj)
        x = _matmul_residual_call(y.reshape(batch, seq, d), w_o[l].astype(BF16), x, gt1)

        h2, h2t = _norm_mod_call(x, norm2_g[l].reshape(1, d), sc2, sh2, BF16, transposed=True)
        qp = _matmul_call(h2.reshape(tokens, d), peer_wq[l].astype(BF16), BF16)
        s1t, s2t, stats = _peer_topk_call(qp, peer_sub_keys[l].astype(BF16))
        x = _peer_ffn_call(h2t, peer_u[l].astype(BF16), peer_v[l].T.astype(BF16), s1t, s2t, stats, x, gt2)

    zeros = jnp.zeros((batch, 1, d), F32)
    return _norm_mod_call(x, final_norm_g.reshape(1, d), zeros, zeros, F32)[0]
```

```python
import functools
import math

import jax
import jax.numpy as jnp
import numpy as np
from jax import lax
from jax.experimental import pallas as pl
from jax.experimental.pallas import tpu as pltpu

F32 = jnp.float32
BF16 = jnp.bfloat16

D_MODEL = 2048
DEPTH = 2
CHUNK = 64
HEAD_DIM = 128
FOX_HEADS = 6
SB_HEADS = 6
DIFF_HEADS = 4
DIFF_QK_DIM = HEAD_DIM // 2
FOX_WIDTH = FOX_HEADS * HEAD_DIM
SB_WIDTH = SB_HEADS * HEAD_DIM
DIFF_WIDTH = DIFF_HEADS * HEAD_DIM
QKV_WIDTH = 3 * FOX_WIDTH + 3 * SB_WIDTH + 3 * DIFF_WIDTH
GATE_OFFSET = QKV_WIDTH + FOX_HEADS
ROPE_THETA = 10000.0
RMS_EPS = 1e-6
N_MOD = 6
PEER_HEADS = 8
PEER_N_KEYS = 128
PEER_TOPK = 16
PEER_HALF = 128

LANES = 128
VMEM_LIMIT = 56 * 1024 * 1024
NEG_INF = float("-inf")

COL_FQ, COL_FK, COL_FV = 0, FOX_HEADS, 2 * FOX_HEADS
COL_SQ, COL_SK, COL_SV = 18, 24, 30
COL_DQ, COL_DK, COL_DV = 36, 40, 44
COL_GATE = QKV_WIDTH // LANES


def _params(*sem):
    return pltpu.CompilerParams(dimension_semantics=sem, vmem_limit_bytes=VMEM_LIMIT)


def _mod_kernel(c_ref, w_ref, b_ref, o_ref):
    c = c_ref[...]
    cond = c * jax.nn.sigmoid(c)
    hi = cond.astype(BF16)
    lo = (cond - hi.astype(F32)).astype(BF16)
    w = w_ref[0].astype(BF16)
    acc = jnp.dot(hi, w, preferred_element_type=F32) + jnp.dot(lo, w, preferred_element_type=F32)
    o_ref[0] = acc + b_ref[0]


def _mod_call(c_pad, ada_w, ada_b):
    depth, d, n = ada_w.shape
    tn = 1024
    return pl.pallas_call(
        _mod_kernel,
        grid=(depth, n // tn),
        in_specs=[
            pl.BlockSpec((8, d), lambda l, j: (0, 0)),
            pl.BlockSpec((1, d, tn), lambda l, j: (l, 0, j)),
            pl.BlockSpec((1, 1, tn), lambda l, j: (l, 0, j)),
        ],
        out_specs=pl.BlockSpec((1, 8, tn), lambda l, j: (l, 0, j)),
        out_shape=jax.ShapeDtypeStruct((depth, 8, n), F32),
        compiler_params=_params("arbitrary", "arbitrary"),
        name="adaln_mod",
    )(c_pad, ada_w, ada_b.reshape(depth, 1, n))


def _norm_mod_kernel(x_ref, g_ref, sc_ref, sh_ref, *o_refs, transposed):
    x = x_ref[0]
    y = x * lax.rsqrt(jnp.mean(x * x, axis=-1, keepdims=True) + RMS_EPS) * g_ref[...]
    h = y * (1.0 + sc_ref[0]) + sh_ref[0]
    o_refs[0][0] = h.astype(o_refs[0].dtype)
    if transposed:
        o_refs[1][0] = h.T.astype(o_refs[1].dtype)


def _norm_mod_call(x, g, sc, sh, out_dtype, transposed=False):
    b, s, d = x.shape
    tr = 256
    out_shape = [jax.ShapeDtypeStruct((b, s, d), out_dtype)]
    out_specs = [pl.BlockSpec((1, tr, d), lambda i, j: (i, j, 0))]
    if transposed:
        out_shape.append(jax.ShapeDtypeStruct((b, d, s), out_dtype))
        out_specs.append(pl.BlockSpec((1, d, tr), lambda i, j: (i, 0, j)))
    return pl.pallas_call(
        functools.partial(_norm_mod_kernel, transposed=transposed),
        grid=(b, s // tr),
        in_specs=[
            pl.BlockSpec((1, tr, d), lambda i, j: (i, j, 0)),
            pl.BlockSpec((1, d), lambda i, j: (0, 0)),
            pl.BlockSpec((1, 1, d), lambda i, j: (i, 0, 0)),
            pl.BlockSpec((1, 1, d), lambda i, j: (i, 0, 0)),
        ],
        out_specs=out_specs,
        out_shape=out_shape,
        compiler_params=_params("arbitrary", "arbitrary"),
        name="norm_mod",
    )(x, g, sc, sh)


def _matmul_kernel(a_ref, w_ref, o_ref):
    o_ref[...] = jnp.dot(a_ref[...], w_ref[...], preferred_element_type=F32).astype(o_ref.dtype)


def _matmul_call(a, w, out_dtype, bm=1024, bn=1024):
    m, k = a.shape
    n = w.shape[1]
    bn = min(bn, n)
    return pl.pallas_call(
        _matmul_kernel,
        grid=(m // bm, n // bn),
        in_specs=[pl.BlockSpec((bm, k), lambda i, j: (i, 0)),
                  pl.BlockSpec((k, bn), lambda i, j: (0, j))],
        out_specs=pl.BlockSpec((bm, bn), lambda i, j: (i, j)),
        out_shape=jax.ShapeDtypeStruct((m, n), out_dtype),
        compiler_params=_params("arbitrary", "arbitrary"),
        name="matmul",
    )(a, w)


def _matmul_residual_kernel(a_ref, w_ref, x_ref, gt_ref, o_ref):
    acc = jnp.dot(a_ref[0], w_ref[...], preferred_element_type=F32)
    o_ref[0] = x_ref[0] + gt_ref[0] * acc


def _matmul_residual_call(a, w, x, gt, bm=1024, bn=1024):
    b, s, k = a.shape
    n = w.shape[1]
    return pl.pallas_call(
        _matmul_residual_kernel,
        grid=(b, s // bm, n // bn),
        in_specs=[pl.BlockSpec((1, bm, k), lambda bi, i, j: (bi, i, 0)),
                  pl.BlockSpec((k, bn), lambda bi, i, j: (0, j)),
                  pl.BlockSpec((1, bm, bn), lambda bi, i, j: (bi, i, j)),
                  pl.BlockSpec((1, 1, bn), lambda bi, i, j: (bi, 0, j))],
        out_specs=pl.BlockSpec((1, bm, bn), lambda bi, i, j: (bi, i, j)),
        out_shape=jax.ShapeDtypeStruct((b, s, n), F32),
        compiler_params=_params("arbitrary", "arbitrary", "arbitrary"),
        name="matmul_residual",
    )(a, w, x, gt)


def _split3(x):
    hi = x.astype(BF16)
    r = x - hi.astype(F32)
    mid = r.astype(BF16)
    lo = (r - mid.astype(F32)).astype(BF16)
    return hi, mid, lo


def _forget_cumsum_kernel(fl_ref, bf_ref, col_ref, row_ref):
    s = fl_ref.shape[1]
    blk = 256
    logf = jax.nn.log_sigmoid(fl_ref[0] + bf_ref[...])
    r = lax.broadcasted_iota(jnp.int32, (blk, blk), 0)
    c = lax.broadcasted_iota(jnp.int32, (blk, blk), 1)
    tri = jnp.where(c <= r, 1.0, 0.0).astype(BF16)
    carry = jnp.zeros((1, LANES), F32)
    parts = []
    for i in range(s // blk):
        hi, mid, lo = _split3(logf[i * blk:(i + 1) * blk])
        cs = (jnp.dot(tri, hi, preferred_element_type=F32)
              + jnp.dot(tri, mid, preferred_element_type=F32)
              + jnp.dot(tri, lo, preferred_element_type=F32)) + carry
        carry = cs[blk - 1:blk, :]
        parts.append(cs)
    cum = jnp.concatenate(parts, axis=0)
    row_ref[0] = cum.T[:8, :]
    for h in range(FOX_HEADS):
        col_ref[0, h] = jnp.broadcast_to(cum[:, h:h + 1], (s, LANES))


def _forget_cumsum_call(f_logit, b_forget_pad):
    b, s, _ = f_logit.shape
    return pl.pallas_call(
        _forget_cumsum_kernel,
        grid=(b,),
        in_specs=[pl.BlockSpec((1, s, LANES), lambda i: (i, 0, 0)),
                  pl.BlockSpec((1, LANES), lambda i: (0, 0))],
        out_specs=[pl.BlockSpec((1, FOX_HEADS, s, LANES), lambda i: (i, 0, 0, 0)),
                   pl.BlockSpec((1, 8, s), lambda i: (i, 0, 0))],
        out_shape=[jax.ShapeDtypeStruct((b, FOX_HEADS, s, LANES), F32),
                   jax.ShapeDtypeStruct((b, 8, s), F32)],
        compiler_params=_params("arbitrary"),
        name="forget_cumsum",
    )(f_logit, b_forget_pad)


ATT_TILE = 512


def _qk(q, k):
    return lax.dot_general(q, k, (((1,), (1,)), ((), ())), preferred_element_type=F32)


def _kv_block(k_ref, v_ref, j):
    start = pl.multiple_of(j * ATT_TILE, ATT_TILE)
    return start, k_ref[pl.ds(start, ATT_TILE), :], v_ref[pl.ds(start, ATT_TILE), :]


def _tile_iotas():
    r = lax.broadcasted_iota(jnp.int32, (ATT_TILE, ATT_TILE), 0)
    c = lax.broadcasted_iota(jnp.int32, (ATT_TILE, ATT_TILE), 1)
    return r, c


def _softmax_state():
    t = ATT_TILE
    return [pltpu.VMEM((t, LANES), F32), pltpu.VMEM((t, LANES), F32), pltpu.VMEM((t, HEAD_DIM), F32)]


def _softmax_reset(m_ref, l_ref, acc_ref):
    m_ref[...] = jnp.full(m_ref.shape, NEG_INF, F32)
    l_ref[...] = jnp.zeros(l_ref.shape, F32)
    acc_ref[...] = jnp.zeros(acc_ref.shape, F32)


def _softmax_update(m_ref, l_ref, acc_ref, s, vb):
    m_old = m_ref[...]
    m_new = jnp.maximum(m_old, jnp.max(s, axis=-1, keepdims=True))
    alpha = jnp.exp(m_old - m_new)
    p = jnp.exp(s - jnp.tile(m_new, (1, s.shape[1] // LANES)))
    l_ref[...] = alpha * l_ref[...] + jnp.sum(p, axis=-1, keepdims=True)
    acc_ref[...] = alpha * acc_ref[...] + jnp.dot(p.astype(BF16), vb, preferred_element_type=F32)
    m_ref[...] = m_new


def _fox_kernel(q_ref, k_ref, v_ref, fc_ref, fr_ref, o_ref, m_ref, l_ref, acc_ref):
    t = ATT_TILE
    qi = pl.program_id(2)
    h = pl.program_id(1)
    q = q_ref[...]
    scale = HEAD_DIM ** -0.5

    def step(j, masked):
        start, kb, vb = _kv_block(k_ref, v_ref, j)
        fcol = jnp.tile(fc_ref[0, 0], (1, t // LANES))
        frow = fr_ref[0, pl.ds(h, 1), pl.ds(start, t)]
        s = _qk(q, kb) * scale + (fcol - frow)
        if masked:
            r, c = _tile_iotas()
            s = jnp.where(c <= r, s, NEG_INF)
        _softmax_update(m_ref, l_ref, acc_ref, s, vb)

    _softmax_reset(m_ref, l_ref, acc_ref)
    step(qi, masked=True)

    @pl.loop(0, qi)
    def _(j):
        step(j, masked=False)

    o_ref[...] = (acc_ref[...] / l_ref[...]).astype(o_ref.dtype)


def _fox_call(proj, f_col, f_row, batch, seq):
    tq = ATT_TILE
    nq = seq // tq
    return pl.pallas_call(
        _fox_kernel,
        grid=(batch, FOX_HEADS, nq),
        in_specs=[
            pl.BlockSpec((tq, HEAD_DIM), lambda b, h, i: (b * nq + i, COL_FQ + h)),
            pl.BlockSpec((seq, HEAD_DIM), lambda b, h, i: (b, COL_FK + h)),
            pl.BlockSpec((seq, HEAD_DIM), lambda b, h, i: (b, COL_FV + h)),
            pl.BlockSpec((1, 1, tq, LANES), lambda b, h, i: (b, h, i, 0)),
            pl.BlockSpec((1, 8, seq), lambda b, h, i: (b, 0, 0)),
        ],
        out_specs=pl.BlockSpec((tq, HEAD_DIM), lambda b, h, i: (b * nq + i, h)),
        out_shape=jax.ShapeDtypeStruct((batch * seq, FOX_WIDTH), BF16),
        scratch_shapes=_softmax_state(),
        compiler_params=_params("arbitrary", "arbitrary", "arbitrary"),
        name="fox_attention",
    )(proj, proj, proj, f_col, f_row)


SB_SEG = 256


def _sb_kernel(q_ref, k_ref, v_ref, tri_ref, o_ref, run_ref, acc_ref):
    t = ATT_TILE
    qi = pl.program_id(2)
    q = q_ref[...]
    scale = HEAD_DIM ** -0.5

    def step(j, masked):
        _, kb, vb = _kv_block(k_ref, v_ref, j)
        tri = tri_ref[...]
        z = _qk(q, kb) * scale
        lk = -(jnp.maximum(z, 0.0) + jnp.log(1.0 + jnp.exp(-jnp.abs(z))))
        if masked:
            r, c = _tile_iotas()
            strictly_causal = c < r
            lk = jnp.where(strictly_causal, lk, 0.0)
        hi = lk.astype(BF16)
        lo = (lk - hi.astype(F32)).astype(BF16)
        run = run_ref[...]
        laters = []
        for g in reversed(range(t // SB_SEG)):
            sl = slice(g * SB_SEG, (g + 1) * SB_SEG)
            laters.append(jnp.dot(hi[:, sl], tri, preferred_element_type=F32)
                          + jnp.dot(lo[:, sl], tri, preferred_element_type=F32)
                          + jnp.tile(run, (1, SB_SEG // LANES)))
            run = run + jnp.sum(lk[:, sl], axis=-1, keepdims=True)
        later = jnp.concatenate(laters[::-1], axis=1)
        w = jnp.exp(z + lk + later)
        if masked:
            w = jnp.where(strictly_causal, w, 0.0)
        acc_ref[...] += jnp.dot(w.astype(BF16), vb, preferred_element_type=F32)
        run_ref[...] = run

    run_ref[...] = jnp.zeros(run_ref.shape, F32)
    acc_ref[...] = jnp.zeros(acc_ref.shape, F32)
    step(qi, masked=True)

    @pl.loop(0, qi)
    def _(i):
        step(qi - 1 - i, masked=False)

    o_ref[...] = acc_ref[...].astype(o_ref.dtype)


def _sb_call(proj, tri, batch, seq):
    tq = ATT_TILE
    nq = seq // tq
    return pl.pallas_call(
        _sb_kernel,
        grid=(batch, SB_HEADS, nq),
        in_specs=[
            pl.BlockSpec((tq, HEAD_DIM), lambda b, h, i: (b * nq + i, COL_SQ + h)),
            pl.BlockSpec((seq, HEAD_DIM), lambda b, h, i: (b, COL_SK + h)),
            pl.BlockSpec((seq, HEAD_DIM), lambda b, h, i: (b, COL_SV + h)),
            pl.BlockSpec((SB_SEG, SB_SEG), lambda b, h, i: (0, 0)),
        ],
        out_specs=pl.BlockSpec((tq, HEAD_DIM), lambda b, h, i: (b * nq + i, h)),
        out_shape=jax.ShapeDtypeStruct((batch * seq, SB_WIDTH), BF16),
        scratch_shapes=[pltpu.VMEM((tq, LANES), F32), pltpu.VMEM((tq, HEAD_DIM), F32)],
        compiler_params=_params("arbitrary", "arbitrary", "arbitrary"),
        name="sb_attention",
    )(proj, proj, proj, tri)


def _rope_kernel(x_ref, pos_ref, freq_ref, o_ref, *, out_scale):
    half = DIFF_QK_DIM // 2
    ang = pos_ref[...].astype(F32) * freq_ref[...]
    cos = jnp.cos(ang)
    sin = jnp.sin(ang)
    lane = lax.broadcasted_iota(jnp.int32, ang.shape, 1)
    first = (lane % DIFF_QK_DIM) < half
    sin_signed = jnp.where(first, -sin, sin)
    for cb in range(x_ref.shape[1] // LANES):
        x = x_ref[:, cb * LANES:(cb + 1) * LANES].astype(F32)
        partner = jnp.where(first, pltpu.roll(x, LANES - half, 1), pltpu.roll(x, half, 1))
        y = x * cos + partner * sin_signed
        o_ref[:, cb * LANES:(cb + 1) * LANES] = (y * out_scale).astype(o_ref.dtype)


def _rope_call(proj, pos_col, freq_row, col_block, out_scale):
    tokens = proj.shape[0]
    tr = 512
    width = DIFF_HEADS * HEAD_DIM
    return pl.pallas_call(
        functools.partial(_rope_kernel, out_scale=out_scale),
        grid=(tokens // tr,),
        in_specs=[pl.BlockSpec((tr, width), lambda i: (i, col_block * LANES // width)),
                  pl.BlockSpec((tr, 1), lambda i: (i, 0)),
                  pl.BlockSpec((1, LANES), lambda i: (0, 0))],
        out_specs=pl.BlockSpec((tr, width), lambda i: (i, 0)),
        out_shape=jax.ShapeDtypeStruct((tokens, width), BF16),
        compiler_params=_params("arbitrary"),
        name="rope",
    )(proj, pos_col, freq_row)


def _diff_kernel(q_ref, k_ref, v_ref, lq1_ref, lk1_ref, lq2_ref, lk2_ref, g_ref, o_ref,
                 m1_ref, l1_ref, a1_ref, m2_ref, l2_ref, a2_ref, *, lam_init):
    t = ATT_TILE
    qi = pl.program_id(2)
    q = q_ref[...]
    lane = lax.broadcasted_iota(jnp.int32, q.shape, 1)
    zero = jnp.zeros_like(q)
    q1 = jnp.where(lane < DIFF_QK_DIM, q, zero)
    q2 = jnp.where(lane < DIFF_QK_DIM, zero, q)
    state1 = (m1_ref, l1_ref, a1_ref)
    state2 = (m2_ref, l2_ref, a2_ref)

    def step(j, masked):
        _, kb, vb = _kv_block(k_ref, v_ref, j)
        for qc, state in ((q1, state1), (q2, state2)):
            s = _qk(qc, kb)
            if masked:
                r, c = _tile_iotas()
                s = jnp.where((c // CHUNK) <= (r // CHUNK), s, NEG_INF)
            _softmax_update(*state, s, vb)

    _softmax_reset(*state1)
    _softmax_reset(*state2)
    step(qi, masked=True)

    @pl.loop(0, qi)
    def _(j):
        step(j, masked=False)

    lam = (jnp.exp(jnp.sum(lq1_ref[0] * lk1_ref[0], axis=-1, keepdims=True))
           - jnp.exp(jnp.sum(lq2_ref[0] * lk2_ref[0], axis=-1, keepdims=True)) + lam_init)
    o = a1_ref[...] / l1_ref[...] - lam * (a2_ref[...] / l2_ref[...])
    o = o * lax.rsqrt(jnp.mean(o * o, axis=-1, keepdims=True) + RMS_EPS) * g_ref[...]
    o_ref[...] = (o * (1.0 - lam_init)).astype(o_ref.dtype)


def _diff_call(qd, kd, proj, lam_q1, lam_k1, lam_q2, lam_k2, subln_g, lam_init, batch, seq):
    t = ATT_TILE
    nq = seq // t
    lam_spec = pl.BlockSpec((1, 1, DIFF_QK_DIM), lambda b, h, i: (h, 0, 0))
    r3 = lambda a: a.reshape(DIFF_HEADS, 1, DIFF_QK_DIM)
    return pl.pallas_call(
        functools.partial(_diff_kernel, lam_init=lam_init),
        grid=(batch, DIFF_HEADS, nq),
        in_specs=[
            pl.BlockSpec((t, HEAD_DIM), lambda b, h, i: (b * nq + i, h)),
            pl.BlockSpec((seq, HEAD_DIM), lambda b, h, i: (b, h)),
            pl.BlockSpec((seq, HEAD_DIM), lambda b, h, i: (b, COL_DV + h)),
            lam_spec, lam_spec, lam_spec, lam_spec,
            pl.BlockSpec((1, HEAD_DIM), lambda b, h, i: (0, 0)),
        ],
        out_specs=pl.BlockSpec((t, HEAD_DIM), lambda b, h, i: (b * nq + i, h)),
        out_shape=jax.ShapeDtypeStruct((batch * seq, DIFF_WIDTH), BF16),
        scratch_shapes=_softmax_state() + _softmax_state(),
        compiler_params=_params("arbitrary", "arbitrary", "arbitrary"),
        name="diff_attention",
    )(qd, kd, proj, r3(lam_q1), r3(lam_k1), r3(lam_q2), r3(lam_k2), subln_g.reshape(1, HEAD_DIM))


def _merge_kernel(of_ref, os_ref, od_ref, wf_ref, ws_ref, wd_ref, gf_ref, gs_ref, gd_ref, o_ref):
    def branch(o_r, w_r, g_r):
        return jax.nn.sigmoid(g_r[...].astype(F32)) * jnp.dot(o_r[...], w_r[...], preferred_element_type=F32)

    y = branch(of_ref, wf_ref, gf_ref) + branch(os_ref, ws_ref, gs_ref) + branch(od_ref, wd_ref, gd_ref)
    o_ref[...] = y.astype(o_ref.dtype)


def _merge_call(o_fox, o_sb, o_diff, w_fox, w_sb, w_diff, proj, bm=1024, bn=512):
    m = o_fox.shape[0]
    d = w_fox.shape[1]
    gate_blocks = d // bn

    def gate_spec(k):
        return pl.BlockSpec((bm, bn), lambda i, j: (i, QKV_WIDTH // bn + k * gate_blocks + j))

    return pl.pallas_call(
        _merge_kernel,
        grid=(m // bm, d // bn),
        in_specs=[
            pl.BlockSpec((bm, FOX_WIDTH), lambda i, j: (i, 0)),
            pl.BlockSpec((bm, SB_WIDTH), lambda i, j: (i, 0)),
            pl.BlockSpec((bm, DIFF_WIDTH), lambda i, j: (i, 0)),
            pl.BlockSpec((FOX_WIDTH, bn), lambda i, j: (0, j)),
            pl.BlockSpec((SB_WIDTH, bn), lambda i, j: (0, j)),
            pl.BlockSpec((DIFF_WIDTH, bn), lambda i, j: (0, j)),
            gate_spec(0), gate_spec(1), gate_spec(2),
        ],
        out_specs=pl.BlockSpec((bm, bn), lambda i, j: (i, j)),
        out_shape=jax.ShapeDtypeStruct((m, d), BF16),
        compiler_params=_params("arbitrary", "arbitrary"),
        name="branch_merge",
    )(o_fox, o_sb, o_diff, w_fox, w_sb, w_diff, proj, proj, proj)


NOT_TOP = 64.0


def _extract_top(cur, count, want_rank=False):
    rows = []
    rank = jnp.full(cur.shape, NOT_TOP, F32) if want_rank else None
    for r in range(count):
        m = jnp.max(cur, axis=0, keepdims=True)
        rows.append(m)
        hit = cur == m
        if want_rank:
            rank = jnp.where(hit, float(r), rank)
        cur = jnp.where(hit, NEG_INF, cur)
    return rows, rank


def _peer_topk_kernel(q_ref, keys_ref, a_ref, l_ref, b_ref, r2_ref):
    k = PEER_TOPK
    for h in range(PEER_HEADS):
        col = 2 * h * PEER_HALF
        s1 = _qk(keys_ref[h, 0], q_ref[:, col:col + PEER_HALF])
        s2 = _qk(keys_ref[h, 1], q_ref[:, col + PEER_HALF:col + 2 * PEER_HALF])
        a_rows, rank1 = _extract_top(s1, k, want_rank=True)
        b_rows, rank2 = _extract_top(s2, k, want_rank=True)
        b_top = jnp.concatenate(b_rows, axis=0)
        counts = [k // (k1 + 1) for k1 in range(k)]
        cands = [a_rows[k1] + b_top[:counts[k1]] for k1 in range(k)]
        n_cand = sum(counts)
        pad = jnp.full((-n_cand % 8, b_top.shape[1]), NEG_INF, F32)
        best, _ = _extract_top(jnp.concatenate(cands + [pad], axis=0), k)
        tau = best[k - 1]
        z = jnp.zeros_like(tau)
        for row in best:
            z = z + jnp.exp(row - best[0])
        l_row = jnp.zeros(s1.shape, F32)
        for k1 in range(k):
            l_k = jnp.sum(jnp.where(cands[k1] >= tau, 1.0, 0.0), axis=0, keepdims=True)
            l_row = jnp.where(rank1 == float(k1), l_k, l_row)
        a_ref[h] = jnp.exp(s1 - a_rows[0]) / z
        l_ref[h] = l_row
        b_ref[h] = jnp.exp(s2 - b_rows[0]).astype(b_ref.dtype)
        r2_ref[h] = rank2.astype(r2_ref.dtype)


def _peer_topk_call(qp, keys):
    tokens = qp.shape[0]
    tq = 512
    spec = pl.BlockSpec((PEER_HEADS, PEER_N_KEYS, tq), lambda i: (0, 0, i))
    shape = lambda dt: jax.ShapeDtypeStruct((PEER_HEADS, PEER_N_KEYS, tokens), dt)
    return pl.pallas_call(
        _peer_topk_kernel,
        grid=(tokens // tq,),
        in_specs=[pl.BlockSpec((tq, qp.shape[1]), lambda i: (i, 0)),
                  pl.BlockSpec(keys.shape, lambda i: (0, 0, 0, 0))],
        out_specs=[spec, spec, spec, spec],
        out_shape=[shape(F32), shape(F32), shape(BF16), shape(BF16)],
        compiler_params=_params("arbitrary"),
        name="peer_topk",
    )(qp, keys)


PEER_TM = 512
PEER_ROWS = 4


def _peer_ffn_kernel(ht_ref, u_ref, vt_ref, a_ref, l_ref, b_ref, r2_ref, x_ref, gt_ref, o_ref, acc_ref):
    e = pl.program_id(1)

    @pl.when(e == 0)
    def _():
        acc_ref[...] = jnp.zeros_like(acc_ref)

    hid = jnp.dot(u_ref[...], ht_ref[0], preferred_element_type=F32)
    acts = []
    for ii in range(PEER_ROWS):
        i = e * PEER_ROWS + ii
        gate = jnp.zeros((PEER_N_KEYS, hid.shape[1]), BF16)
        for h in range(PEER_HEADS):
            a_row = a_ref[h, pl.ds(i, 1), :].astype(BF16)
            l_row = l_ref[h, pl.ds(i, 1), :].astype(BF16)
            chosen = r2_ref[h] < l_row
            gate = gate + jnp.where(chosen, b_ref[h], jnp.zeros((), BF16)) * a_row
        hblk = hid[ii * PEER_N_KEYS:(ii + 1) * PEER_N_KEYS]
        gelu = 0.5 * hblk * (1.0 + lax.erf(hblk * (2.0 ** -0.5)))
        acts.append(gate * gelu.astype(BF16))
    act = jnp.concatenate(acts, axis=0)
    acc_ref[...] += jnp.dot(vt_ref[...], act, preferred_element_type=F32)

    @pl.when(e == pl.num_programs(1) - 1)
    def _():
        o_ref[0] = x_ref[0] + gt_ref[0] * acc_ref[...].T


def _peer_ffn_call(ht, u, vt, a_t, l_t, b_t, r2_t, x, gt):
    b, d, s = ht.shape
    tm = PEER_TM
    te = PEER_ROWS * PEER_N_KEYS
    n_e = u.shape[0] // te
    per_b = s // tm
    key_spec = pl.BlockSpec((PEER_HEADS, PEER_N_KEYS, tm), lambda i, e: (0, 0, i))
    return pl.pallas_call(
        _peer_ffn_kernel,
        grid=(b * per_b, n_e),
        in_specs=[
            pl.BlockSpec((1, d, tm), lambda i, e: (i // per_b, 0, i % per_b)),
            pl.BlockSpec((te, d), lambda i, e: (e, 0)),
            pl.BlockSpec((d, te), lambda i, e: (0, e)),
            key_spec, key_spec, key_spec, key_spec,
            pl.BlockSpec((1, tm, d), lambda i, e: (i // per_b, i % per_b, 0)),
            pl.BlockSpec((1, 1, d), lambda i, e: (i // per_b, 0, 0)),
        ],
        out_specs=pl.BlockSpec((1, tm, d), lambda i, e: (i // per_b, i % per_b, 0)),
        out_shape=jax.ShapeDtypeStruct(x.shape, F32),
        scratch_shapes=[pltpu.VMEM((d, tm), F32)],
        compiler_params=_params("arbitrary", "arbitrary"),
        name="peer_ffn",
    )(ht, u, vt, a_t, l_t, b_t, r2_t, x, gt)


def _peer_tables_kernel(u_ref, v_ref, ub_ref, vt_ref):
    ub_ref[...] = u_ref[0].astype(BF16)
    vt_ref[...] = v_ref[0].T.astype(BF16)


def _peer_tables_call(u, v, layer):
    _, e, d = u.shape
    te = 512
    return pl.pallas_call(
        _peer_tables_kernel,
        grid=(e // te,),
        in_specs=[pl.BlockSpec((1, te, d), lambda i: (layer, i, 0)),
                  pl.BlockSpec((1, te, d), lambda i: (layer, i, 0))],
        out_specs=[pl.BlockSpec((te, d), lambda i: (i, 0)), pl.BlockSpec((d, te), lambda i: (0, i))],
        out_shape=[jax.ShapeDtypeStruct((e, d), BF16), jax.ShapeDtypeStruct((d, e), BF16)],
        compiler_params=_params("arbitrary"),
        name="peer_tables",
    )(u, v)


def _regroup_w_in(w):
    main = jnp.concatenate([w[:, :QKV_WIDTH], w[:, GATE_OFFSET:]], axis=1).astype(BF16)
    forget = jnp.pad(w[:, QKV_WIDTH:GATE_OFFSET], ((0, 0), (0, LANES - FOX_HEADS))).astype(BF16)
    return main, forget


def kernel(x, c, positions, ada_w, ada_b, norm1_g, norm2_g, w_in, b_forget, lam_q1, lam_k1, lam_q2, lam_k2,
           diff_subln_g, w_br_fox, w_br_sb, w_br_diff, w_o, peer_wq, peer_sub_keys, peer_u, peer_v,
           final_norm_g):
    batch, seq, d = x.shape
    tokens = batch * seq
    assert batch <= 8 and seq % 1024 == 0 and d == D_MODEL

    c_pad = jnp.zeros((8, d), F32).at[:batch].set(c)
    mod = _mod_call(c_pad, ada_w, ada_b)[:, :batch].reshape(DEPTH, batch, N_MOD, 1, d)
    pos_col = positions.reshape(tokens, 1)
    inv_freq = ROPE_THETA ** (-jnp.arange(0, DIFF_QK_DIM, 2, dtype=F32) / DIFF_QK_DIM)
    freq_row = jnp.tile(inv_freq, LANES // inv_freq.shape[0]).reshape(1, LANES)
    ri = np.arange(SB_SEG)
    tri_after = jnp.asarray(ri[:, None] > ri[None, :], BF16)

    for l in range(DEPTH):
        lam_init = 0.8 - 0.6 * math.exp(-0.3 * l)
        sh1, sc1, gt1, sh2, sc2, gt2 = (mod[l, :, k] for k in range(N_MOD))

        h1 = _norm_mod_call(x, norm1_g[l].reshape(1, d), sc1, sh1, BF16)[0].reshape(tokens, d)
        w_main, w_forget = _regroup_w_in(w_in[l])
        proj = _matmul_call(h1, w_main, BF16)
        f_logit = _matmul_call(h1, w_forget, F32).reshape(batch, seq, LANES)
        b_forget_pad = jnp.pad(b_forget[l], (0, LANES - FOX_HEADS)).reshape(1, LANES)
        f_col, f_row = _forget_cumsum_call(f_logit, b_forget_pad)

        o_fox = _fox_call(proj, f_col, f_row, batch, seq)
        o_sb = _sb_call(proj, tri_after, batch, seq)
        qd = _rope_call(proj, pos_col, freq_row, COL_DQ, DIFF_QK_DIM ** -0.5)
        kd = _rope_call(proj, pos_col, freq_row, COL_DK, 1.0)
        o_diff = _diff_call(qd, kd, proj, lam_q1[l], lam_k1[l], lam_q2[l], lam_k2[l], diff_subln_g[l],
                            lam_init, batch, seq)

        y = _merge_call(o_fox, o_sb, o_diff, w_br_fox[l].astype(BF16), w_br_sb[l].astype(BF16),
                        w_br_diff[l].astype(BF16), proj)
        x = _matmul_residual_call(y.reshape(batch, seq, d), w_o[l].astype(BF16), x, gt1)

        h2, h2t = _norm_mod_call(x, norm2_g[l].reshape(1, d), sc2, sh2, BF16, transposed=True)
        qp = _matmul_call(h2.reshape(tokens, d), peer_wq[l].astype(BF16), BF16)
        a_t, l_t, b_t, r2_t = _peer_topk_call(qp, peer_sub_keys[l].astype(BF16))
        u_bf, vt_bf = _peer_tables_call(peer_u, peer_v, l)
        x = _peer_ffn_call(h2t, u_bf, vt_bf, a_t, l_t, b_t, r2_t, x, gt2)

    zeros = jnp.zeros((batch, 1, d), F32)
    return _norm_mod_call(x, final_norm_g.reshape(1, d), zeros, zeros, F32)[0]
```

```python
import functools
import math

import jax
import jax.numpy as jnp
import numpy as np
from jax import lax
from jax.experimental import pallas as pl
from jax.experimental.pallas import tpu as pltpu

F32 = jnp.float32
BF16 = jnp.bfloat16

D_MODEL = 2048
DEPTH = 2
CHUNK = 64
HEAD_DIM = 128
FOX_HEADS = 6
SB_HEADS = 6
DIFF_HEADS = 4
DIFF_QK_DIM = HEAD_DIM // 2
FOX_WIDTH = FOX_HEADS * HEAD_DIM
SB_WIDTH = SB_HEADS * HEAD_DIM
DIFF_WIDTH = DIFF_HEADS * HEAD_DIM
QKV_WIDTH = 3 * FOX_WIDTH + 3 * SB_WIDTH + 3 * DIFF_WIDTH
GATE_OFFSET = QKV_WIDTH + FOX_HEADS
ROPE_THETA = 10000.0
RMS_EPS = 1e-6
N_MOD = 6
N_BRANCHES = 3
PEER_HEADS = 8
PEER_N_KEYS = 128
PEER_TOPK = 16
PEER_HALF = 128

LANES = 128
VMEM_LIMIT = 56 * 1024 * 1024
NEG_INF = float("-inf")

COL_FQ, COL_FK, COL_FV = 0, FOX_HEADS, 2 * FOX_HEADS
COL_SQ, COL_SK, COL_SV = 18, 24, 30
COL_DQ, COL_DK, COL_DV = 36, 40, 44
COL_GATE = QKV_WIDTH // LANES


def _params(*sem):
    return pltpu.CompilerParams(dimension_semantics=sem, vmem_limit_bytes=VMEM_LIMIT)


def _mod_kernel(c_ref, w_ref, b_ref, o_ref):
    c = c_ref[...]
    cond = c * jax.nn.sigmoid(c)
    hi = cond.astype(BF16)
    lo = (cond - hi.astype(F32)).astype(BF16)
    w = w_ref[0].astype(BF16)
    acc = jnp.dot(hi, w, preferred_element_type=F32) + jnp.dot(lo, w, preferred_element_type=F32)
    o_ref[0] = acc + b_ref[0]


def _mod_call(c_pad, ada_w, ada_b):
    depth, d, n = ada_w.shape
    tn = 1024
    return pl.pallas_call(
        _mod_kernel,
        grid=(depth, n // tn),
        in_specs=[
            pl.BlockSpec((8, d), lambda l, j: (0, 0)),
            pl.BlockSpec((1, d, tn), lambda l, j: (l, 0, j)),
            pl.BlockSpec((1, 1, tn), lambda l, j: (l, 0, j)),
        ],
        out_specs=pl.BlockSpec((1, 8, tn), lambda l, j: (l, 0, j)),
        out_shape=jax.ShapeDtypeStruct((depth, 8, n), F32),
        compiler_params=_params("arbitrary", "arbitrary"),
        name="adaln_mod",
    )(c_pad, ada_w, ada_b.reshape(depth, 1, n))


def _norm_mod_kernel(x_ref, g_ref, sc_ref, sh_ref, *o_refs, transposed):
    x = x_ref[0]
    y = x * lax.rsqrt(jnp.mean(x * x, axis=-1, keepdims=True) + RMS_EPS) * g_ref[...]
    h = y * (1.0 + sc_ref[0]) + sh_ref[0]
    o_refs[0][0] = h.astype(o_refs[0].dtype)
    if transposed:
        o_refs[1][0] = h.T.astype(o_refs[1].dtype)


def _norm_mod_call(x, g, sc, sh, out_dtype, transposed=False):
    b, s, d = x.shape
    tr = 256
    out_shape = [jax.ShapeDtypeStruct((b, s, d), out_dtype)]
    out_specs = [pl.BlockSpec((1, tr, d), lambda i, j: (i, j, 0))]
    if transposed:
        out_shape.append(jax.ShapeDtypeStruct((b, d, s), out_dtype))
        out_specs.append(pl.BlockSpec((1, d, tr), lambda i, j: (i, 0, j)))
    return pl.pallas_call(
        functools.partial(_norm_mod_kernel, transposed=transposed),
        grid=(b, s // tr),
        in_specs=[
            pl.BlockSpec((1, tr, d), lambda i, j: (i, j, 0)),
            pl.BlockSpec((1, d), lambda i, j: (0, 0)),
            pl.BlockSpec((1, 1, d), lambda i, j: (i, 0, 0)),
            pl.BlockSpec((1, 1, d), lambda i, j: (i, 0, 0)),
        ],
        out_specs=out_specs,
        out_shape=out_shape,
        compiler_params=_params("arbitrary", "arbitrary"),
        name="norm_mod",
    )(x, g, sc, sh)


def _matmul_kernel(a_ref, w_ref, o_ref):
    o_ref[...] = jnp.dot(a_ref[...], w_ref[...], preferred_element_type=F32).astype(o_ref.dtype)


def _matmul_call(a, w, out_dtype, bm=1024, bn=1024):
    m, k = a.shape
    n = w.shape[1]
    bn = min(bn, n)
    return pl.pallas_call(
        _matmul_kernel,
        grid=(m // bm, n // bn),
        in_specs=[pl.BlockSpec((bm, k), lambda i, j: (i, 0)),
                  pl.BlockSpec((k, bn), lambda i, j: (0, j))],
        out_specs=pl.BlockSpec((bm, bn), lambda i, j: (i, j)),
        out_shape=jax.ShapeDtypeStruct((m, n), out_dtype),
        compiler_params=_params("arbitrary", "arbitrary"),
        name="matmul",
    )(a, w)


def _proj_kernel(a_ref, w_ref, *rest, shift, keep):
    o_ref, wb_ref = rest[-2:]

    @pl.when(pl.program_id(1) == 0)
    def _():
        w = w_ref[0]
        if shift:
            w = jnp.concatenate([w[:, shift:], rest[0][0][:, :shift]], axis=1)
        if keep is not None:
            lane = lax.broadcasted_iota(jnp.int32, w.shape, 1)
            w = jnp.where(lane < keep, w, 0.0)
        wb_ref[...] = w.astype(BF16)

    o_ref[...] = jnp.dot(a_ref[...], wb_ref[...], preferred_element_type=F32).astype(o_ref.dtype)


def _proj_call(a, w, layer, col0, n, out_dtype, bn, keep=None, bm=1024):
    m, k = a.shape
    shift = col0 % LANES
    base = (col0 - shift) // bn
    assert (col0 - shift) % bn == 0 and n % bn == 0 and bn % LANES == 0
    next_stride = bn // LANES
    in_specs = [pl.BlockSpec((bm, k), lambda j, i: (i, 0)),
                pl.BlockSpec((1, k, bn), lambda j, i: (layer, 0, base + j))]
    operands = [a, w]
    if shift:
        in_specs.append(pl.BlockSpec((1, k, LANES), lambda j, i: (layer, 0, (base + j + 1) * next_stride)))
        operands.append(w)
    return pl.pallas_call(
        functools.partial(_proj_kernel, shift=shift, keep=keep),
        grid=(n // bn, m // bm),
        in_specs=in_specs,
        out_specs=pl.BlockSpec((bm, bn), lambda j, i: (i, j)),
        out_shape=jax.ShapeDtypeStruct((m, n), out_dtype),
        scratch_shapes=[pltpu.VMEM((k, bn), BF16)],
        compiler_params=_params("arbitrary", "arbitrary"),
        name="proj",
    )(*operands)


def _matmul_residual_kernel(a_ref, w_ref, x_ref, gt_ref, o_ref):
    acc = jnp.dot(a_ref[0], w_ref[...], preferred_element_type=F32)
    o_ref[0] = x_ref[0] + gt_ref[0] * acc


def _matmul_residual_call(a, w, x, gt, bm=1024, bn=1024):
    b, s, k = a.shape
    n = w.shape[1]
    return pl.pallas_call(
        _matmul_residual_kernel,
        grid=(b, s // bm, n // bn),
        in_specs=[pl.BlockSpec((1, bm, k), lambda bi, i, j: (bi, i, 0)),
                  pl.BlockSpec((k, bn), lambda bi, i, j: (0, j)),
                  pl.BlockSpec((1, bm, bn), lambda bi, i, j: (bi, i, j)),
                  pl.BlockSpec((1, 1, bn), lambda bi, i, j: (bi, 0, j))],
        out_specs=pl.BlockSpec((1, bm, bn), lambda bi, i, j: (bi, i, j)),
        out_shape=jax.ShapeDtypeStruct((b, s, n), F32),
        compiler_params=_params("arbitrary", "arbitrary", "arbitrary"),
        name="matmul_residual",
    )(a, w, x, gt)


def _split3(x):
    hi = x.astype(BF16)
    r = x - hi.astype(F32)
    mid = r.astype(BF16)
    lo = (r - mid.astype(F32)).astype(BF16)
    return hi, mid, lo


def _forget_cumsum_kernel(fl_ref, bf_ref, col_ref, row_ref):
    s = fl_ref.shape[1]
    blk = 256
    logf = jax.nn.log_sigmoid(fl_ref[0] + bf_ref[...])
    r = lax.broadcasted_iota(jnp.int32, (blk, blk), 0)
    c = lax.broadcasted_iota(jnp.int32, (blk, blk), 1)
    tri = jnp.where(c <= r, 1.0, 0.0).astype(BF16)
    carry = jnp.zeros((1, LANES), F32)
    parts = []
    for i in range(s // blk):
        hi, mid, lo = _split3(logf[i * blk:(i + 1) * blk])
        cs = (jnp.dot(tri, hi, preferred_element_type=F32)
              + jnp.dot(tri, mid, preferred_element_type=F32)
              + jnp.dot(tri, lo, preferred_element_type=F32)) + carry
        carry = cs[blk - 1:blk, :]
        parts.append(cs)
    cum = jnp.concatenate(parts, axis=0)
    row_ref[0] = cum.T[:8, :]
    for h in range(FOX_HEADS):
        col_ref[0, h] = jnp.broadcast_to(cum[:, h:h + 1], (s, LANES))


def _forget_cumsum_call(f_logit, b_forget_pad):
    b, s, _ = f_logit.shape
    return pl.pallas_call(
        _forget_cumsum_kernel,
        grid=(b,),
        in_specs=[pl.BlockSpec((1, s, LANES), lambda i: (i, 0, 0)),
                  pl.BlockSpec((1, LANES), lambda i: (0, 0))],
        out_specs=[pl.BlockSpec((1, FOX_HEADS, s, LANES), lambda i: (i, 0, 0, 0)),
                   pl.BlockSpec((1, 8, s), lambda i: (i, 0, 0))],
        out_shape=[jax.ShapeDtypeStruct((b, FOX_HEADS, s, LANES), F32),
                   jax.ShapeDtypeStruct((b, 8, s), F32)],
        compiler_params=_params("arbitrary"),
        name="forget_cumsum",
    )(f_logit, b_forget_pad)


ATT_TILE = 512


def _qk(q, k):
    return lax.dot_general(q, k, (((1,), (1,)), ((), ())), preferred_element_type=F32)


def _kv_block(k_ref, v_ref, j):
    start = pl.multiple_of(j * ATT_TILE, ATT_TILE)
    return start, k_ref[pl.ds(start, ATT_TILE), :], v_ref[pl.ds(start, ATT_TILE), :]


def _tile_iotas():
    r = lax.broadcasted_iota(jnp.int32, (ATT_TILE, ATT_TILE), 0)
    c = lax.broadcasted_iota(jnp.int32, (ATT_TILE, ATT_TILE), 1)
    return r, c


def _softmax_state():
    t = ATT_TILE
    return [pltpu.VMEM((t, LANES), F32), pltpu.VMEM((t, LANES), F32), pltpu.VMEM((t, HEAD_DIM), F32)]


def _softmax_reset(m_ref, l_ref, acc_ref):
    m_ref[...] = jnp.full(m_ref.shape, NEG_INF, F32)
    l_ref[...] = jnp.zeros(l_ref.shape, F32)
    acc_ref[...] = jnp.zeros(acc_ref.shape, F32)


def _softmax_update(m_ref, l_ref, acc_ref, s, vb):
    m_old = m_ref[...]
    m_new = jnp.maximum(m_old, jnp.max(s, axis=-1, keepdims=True))
    alpha = jnp.exp(m_old - m_new)
    p = jnp.exp(s - jnp.tile(m_new, (1, s.shape[1] // LANES)))
    l_ref[...] = alpha * l_ref[...] + jnp.sum(p, axis=-1, keepdims=True)
    acc_ref[...] = alpha * acc_ref[...] + jnp.dot(p.astype(BF16), vb, preferred_element_type=F32)
    m_ref[...] = m_new


def _fox_kernel(q_ref, k_ref, v_ref, fc_ref, fr_ref, o_ref, m_ref, l_ref, acc_ref):
    t = ATT_TILE
    qi = pl.program_id(2)
    h = pl.program_id(1)
    q = q_ref[...]
    scale = HEAD_DIM ** -0.5

    def step(j, masked):
        start, kb, vb = _kv_block(k_ref, v_ref, j)
        fcol = jnp.tile(fc_ref[0, 0], (1, t // LANES))
        frow = fr_ref[0, pl.ds(h, 1), pl.ds(start, t)]
        s = _qk(q, kb) * scale + (fcol - frow)
        if masked:
            r, c = _tile_iotas()
            s = jnp.where(c <= r, s, NEG_INF)
        _softmax_update(m_ref, l_ref, acc_ref, s, vb)

    _softmax_reset(m_ref, l_ref, acc_ref)
    step(qi, masked=True)

    @pl.loop(0, qi)
    def _(j):
        step(j, masked=False)

    o_ref[...] = (acc_ref[...] / l_ref[...]).astype(o_ref.dtype)


def _fox_call(proj, f_col, f_row, batch, seq):
    tq = ATT_TILE
    nq = seq // tq
    return pl.pallas_call(
        _fox_kernel,
        grid=(batch, FOX_HEADS, nq),
        in_specs=[
            pl.BlockSpec((tq, HEAD_DIM), lambda b, h, i: (b * nq + i, COL_FQ + h)),
            pl.BlockSpec((seq, HEAD_DIM), lambda b, h, i: (b, COL_FK + h)),
            pl.BlockSpec((seq, HEAD_DIM), lambda b, h, i: (b, COL_FV + h)),
            pl.BlockSpec((1, 1, tq, LANES), lambda b, h, i: (b, h, i, 0)),
            pl.BlockSpec((1, 8, seq), lambda b, h, i: (b, 0, 0)),
        ],
        out_specs=pl.BlockSpec((tq, HEAD_DIM), lambda b, h, i: (b * nq + i, h)),
        out_shape=jax.ShapeDtypeStruct((batch * seq, FOX_WIDTH), BF16),
        scratch_shapes=_softmax_state(),
        compiler_params=_params("arbitrary", "arbitrary", "arbitrary"),
        name="fox_attention",
    )(proj, proj, proj, f_col, f_row)


SB_SEG = 256


def _sb_kernel(q_ref, k_ref, v_ref, tri_ref, o_ref, run_ref, acc_ref):
    t = ATT_TILE
    qi = pl.program_id(2)
    q = q_ref[...]
    scale = HEAD_DIM ** -0.5

    def step(j, masked):
        _, kb, vb = _kv_block(k_ref, v_ref, j)
        tri = tri_ref[...]
        z = _qk(q, kb) * scale
        lk = -(jnp.maximum(z, 0.0) + jnp.log(1.0 + jnp.exp(-jnp.abs(z))))
        if masked:
            r, c = _tile_iotas()
            strictly_causal = c < r
            lk = jnp.where(strictly_causal, lk, 0.0)
        hi = lk.astype(BF16)
        lo = (lk - hi.astype(F32)).astype(BF16)
        run = run_ref[...]
        laters = []
        for g in reversed(range(t // SB_SEG)):
            sl = slice(g * SB_SEG, (g + 1) * SB_SEG)
            laters.append(jnp.dot(hi[:, sl], tri, preferred_element_type=F32)
                          + jnp.dot(lo[:, sl], tri, preferred_element_type=F32)
                          + jnp.tile(run, (1, SB_SEG // LANES)))
            run = run + jnp.sum(lk[:, sl], axis=-1, keepdims=True)
        later = jnp.concatenate(laters[::-1], axis=1)
        w = jnp.exp(z + lk + later)
        if masked:
            w = jnp.where(strictly_causal, w, 0.0)
        acc_ref[...] += jnp.dot(w.astype(BF16), vb, preferred_element_type=F32)
        run_ref[...] = run

    run_ref[...] = jnp.zeros(run_ref.shape, F32)
    acc_ref[...] = jnp.zeros(acc_ref.shape, F32)
    step(qi, masked=True)

    @pl.loop(0, qi)
    def _(i):
        step(qi - 1 - i, masked=False)

    o_ref[...] = acc_ref[...].astype(o_ref.dtype)


def _sb_call(proj, tri, batch, seq):
    tq = ATT_TILE
    nq = seq // tq
    return pl.pallas_call(
        _sb_kernel,
        grid=(batch, SB_HEADS, nq),
        in_specs=[
            pl.BlockSpec((tq, HEAD_DIM), lambda b, h, i: (b * nq + i, COL_SQ + h)),
            pl.BlockSpec((seq, HEAD_DIM), lambda b, h, i: (b, COL_SK + h)),
            pl.BlockSpec((seq, HEAD_DIM), lambda b, h, i: (b, COL_SV + h)),
            pl.BlockSpec((SB_SEG, SB_SEG), lambda b, h, i: (0, 0)),
        ],
        out_specs=pl.BlockSpec((tq, HEAD_DIM), lambda b, h, i: (b * nq + i, h)),
        out_shape=jax.ShapeDtypeStruct((batch * seq, SB_WIDTH), BF16),
        scratch_shapes=[pltpu.VMEM((tq, LANES), F32), pltpu.VMEM((tq, HEAD_DIM), F32)],
        compiler_params=_params("arbitrary", "arbitrary", "arbitrary"),
        name="sb_attention",
    )(proj, proj, proj, tri)


def _rope_kernel(x_ref, pos_ref, freq_ref, o_ref, *, out_scale):
    half = DIFF_QK_DIM // 2
    ang = pos_ref[...].astype(F32) * freq_ref[...]
    cos = jnp.cos(ang)
    sin = jnp.sin(ang)
    lane = lax.broadcasted_iota(jnp.int32, ang.shape, 1)
    first = (lane % DIFF_QK_DIM) < half
    sin_signed = jnp.where(first, -sin, sin)
    for cb in range(x_ref.shape[1] // LANES):
        x = x_ref[:, cb * LANES:(cb + 1) * LANES].astype(F32)
        partner = jnp.where(first, pltpu.roll(x, LANES - half, 1), pltpu.roll(x, half, 1))
        y = x * cos + partner * sin_signed
        o_ref[:, cb * LANES:(cb + 1) * LANES] = (y * out_scale).astype(o_ref.dtype)


def _rope_call(proj, pos_col, freq_row, col_block, out_scale):
    tokens = proj.shape[0]
    tr = 512
    width = DIFF_HEADS * HEAD_DIM
    return pl.pallas_call(
        functools.partial(_rope_kernel, out_scale=out_scale),
        grid=(tokens // tr,),
        in_specs=[pl.BlockSpec((tr, width), lambda i: (i, col_block * LANES // width)),
                  pl.BlockSpec((tr, 1), lambda i: (i, 0)),
                  pl.BlockSpec((1, LANES), lambda i: (0, 0))],
        out_specs=pl.BlockSpec((tr, width), lambda i: (i, 0)),
        out_shape=jax.ShapeDtypeStruct((tokens, width), BF16),
        compiler_params=_params("arbitrary"),
        name="rope",
    )(proj, pos_col, freq_row)


def _diff_kernel(q_ref, k_ref, v_ref, lq1_ref, lk1_ref, lq2_ref, lk2_ref, g_ref, o_ref,
                 m1_ref, l1_ref, a1_ref, m2_ref, l2_ref, a2_ref, *, lam_init):
    t = ATT_TILE
    qi = pl.program_id(2)
    q = q_ref[...]
    lane = lax.broadcasted_iota(jnp.int32, q.shape, 1)
    zero = jnp.zeros_like(q)
    q1 = jnp.where(lane < DIFF_QK_DIM, q, zero)
    q2 = jnp.where(lane < DIFF_QK_DIM, zero, q)
    state1 = (m1_ref, l1_ref, a1_ref)
    state2 = (m2_ref, l2_ref, a2_ref)

    def step(j, masked):
        _, kb, vb = _kv_block(k_ref, v_ref, j)
        for qc, state in ((q1, state1), (q2, state2)):
            s = _qk(qc, kb)
            if masked:
                r, c = _tile_iotas()
                s = jnp.where((c // CHUNK) <= (r // CHUNK), s, NEG_INF)
            _softmax_update(*state, s, vb)

    _softmax_reset(*state1)
    _softmax_reset(*state2)
    step(qi, masked=True)

    @pl.loop(0, qi)
    def _(j):
        step(j, masked=False)

    lam = (jnp.exp(jnp.sum(lq1_ref[0] * lk1_ref[0], axis=-1, keepdims=True))
           - jnp.exp(jnp.sum(lq2_ref[0] * lk2_ref[0], axis=-1, keepdims=True)) + lam_init)
    o = a1_ref[...] / l1_ref[...] - lam * (a2_ref[...] / l2_ref[...])
    o = o * lax.rsqrt(jnp.mean(o * o, axis=-1, keepdims=True) + RMS_EPS) * g_ref[...]
    o_ref[...] = (o * (1.0 - lam_init)).astype(o_ref.dtype)


def _diff_call(qd, kd, proj, lam_q1, lam_k1, lam_q2, lam_k2, subln_g, lam_init, batch, seq):
    t = ATT_TILE
    nq = seq // t
    lam_spec = pl.BlockSpec((1, 1, DIFF_QK_DIM), lambda b, h, i: (h, 0, 0))
    r3 = lambda a: a.reshape(DIFF_HEADS, 1, DIFF_QK_DIM)
    return pl.pallas_call(
        functools.partial(_diff_kernel, lam_init=lam_init),
        grid=(batch, DIFF_HEADS, nq),
        in_specs=[
            pl.BlockSpec((t, HEAD_DIM), lambda b, h, i: (b * nq + i, h)),
            pl.BlockSpec((seq, HEAD_DIM), lambda b, h, i: (b, h)),
            pl.BlockSpec((seq, HEAD_DIM), lambda b, h, i: (b, COL_DV + h)),
            lam_spec, lam_spec, lam_spec, lam_spec,
            pl.BlockSpec((1, HEAD_DIM), lambda b, h, i: (0, 0)),
        ],
        out_specs=pl.BlockSpec((t, HEAD_DIM), lambda b, h, i: (b * nq + i, h)),
        out_shape=jax.ShapeDtypeStruct((batch * seq, DIFF_WIDTH), BF16),
        scratch_shapes=_softmax_state() + _softmax_state(),
        compiler_params=_params("arbitrary", "arbitrary", "arbitrary"),
        name="diff_attention",
    )(qd, kd, proj, r3(lam_q1), r3(lam_k1), r3(lam_q2), r3(lam_k2), subln_g.reshape(1, HEAD_DIM))


def _merge_kernel(of_ref, os_ref, od_ref, wf_ref, ws_ref, wd_ref, gf_ref, gs_ref, gd_ref, o_ref):
    def branch(o_r, w_r, g_r):
        return jax.nn.sigmoid(g_r[...].astype(F32)) * jnp.dot(o_r[...], w_r[...], preferred_element_type=F32)

    y = branch(of_ref, wf_ref, gf_ref) + branch(os_ref, ws_ref, gs_ref) + branch(od_ref, wd_ref, gd_ref)
    o_ref[...] = y.astype(o_ref.dtype)


def _merge_call(o_fox, o_sb, o_diff, w_fox, w_sb, w_diff, gates, bm=1024, bn=512):
    m = o_fox.shape[0]
    d = w_fox.shape[1]
    gate_blocks = d // bn

    def gate_spec(k):
        return pl.BlockSpec((bm, bn), lambda i, j: (i, k * gate_blocks + j))

    return pl.pallas_call(
        _merge_kernel,
        grid=(m // bm, d // bn),
        in_specs=[
            pl.BlockSpec((bm, FOX_WIDTH), lambda i, j: (i, 0)),
            pl.BlockSpec((bm, SB_WIDTH), lambda i, j: (i, 0)),
            pl.BlockSpec((bm, DIFF_WIDTH), lambda i, j: (i, 0)),
            pl.BlockSpec((FOX_WIDTH, bn), lambda i, j: (0, j)),
            pl.BlockSpec((SB_WIDTH, bn), lambda i, j: (0, j)),
            pl.BlockSpec((DIFF_WIDTH, bn), lambda i, j: (0, j)),
            gate_spec(0), gate_spec(1), gate_spec(2),
        ],
        out_specs=pl.BlockSpec((bm, bn), lambda i, j: (i, j)),
        out_shape=jax.ShapeDtypeStruct((m, d), BF16),
        compiler_params=_params("arbitrary", "arbitrary"),
        name="branch_merge",
    )(o_fox, o_sb, o_diff, w_fox, w_sb, w_diff, gates, gates, gates)


NOT_TOP = 64.0


def _extract_top(cur, count, want_rank=False):
    rows = []
    rank = jnp.full(cur.shape, NOT_TOP, F32) if want_rank else None
    for r in range(count):
        m = jnp.max(cur, axis=0, keepdims=True)
        rows.append(m)
        hit = cur == m
        if want_rank:
            rank = jnp.where(hit, float(r), rank)
        cur = jnp.where(hit, NEG_INF, cur)
    return rows, rank


def _peer_topk_kernel(q_ref, keys_ref, a_ref, l_ref, b_ref, r2_ref):
    k = PEER_TOPK
    for h in range(PEER_HEADS):
        col = 2 * h * PEER_HALF
        s1 = _qk(keys_ref[h, 0], q_ref[:, col:col + PEER_HALF])
        s2 = _qk(keys_ref[h, 1], q_ref[:, col + PEER_HALF:col + 2 * PEER_HALF])
        a_rows, rank1 = _extract_top(s1, k, want_rank=True)
        b_rows, rank2 = _extract_top(s2, k, want_rank=True)
        b_top = jnp.concatenate(b_rows, axis=0)
        counts = [k // (k1 + 1) for k1 in range(k)]
        cands = [a_rows[k1] + b_top[:counts[k1]] for k1 in range(k)]
        n_cand = sum(counts)
        pad = jnp.full((-n_cand % 8, b_top.shape[1]), NEG_INF, F32)
        best, _ = _extract_top(jnp.concatenate(cands + [pad], axis=0), k)
        tau = best[k - 1]
        z = jnp.zeros_like(tau)
        for row in best:
            z = z + jnp.exp(row - best[0])
        l_row = jnp.zeros(s1.shape, F32)
        for k1 in range(k):
            l_k = jnp.sum(jnp.where(cands[k1] >= tau, 1.0, 0.0), axis=0, keepdims=True)
            l_row = jnp.where(rank1 == float(k1), l_k, l_row)
        a_ref[h] = jnp.exp(s1 - a_rows[0]) / z
        l_ref[h] = l_row
        b_ref[h] = jnp.exp(s2 - b_rows[0]).astype(b_ref.dtype)
        r2_ref[h] = rank2.astype(r2_ref.dtype)


def _peer_topk_call(qp, keys):
    tokens = qp.shape[0]
    tq = 512
    spec = pl.BlockSpec((PEER_HEADS, PEER_N_KEYS, tq), lambda i: (0, 0, i))
    shape = lambda dt: jax.ShapeDtypeStruct((PEER_HEADS, PEER_N_KEYS, tokens), dt)
    return pl.pallas_call(
        _peer_topk_kernel,
        grid=(tokens // tq,),
        in_specs=[pl.BlockSpec((tq, qp.shape[1]), lambda i: (i, 0)),
                  pl.BlockSpec(keys.shape, lambda i: (0, 0, 0, 0))],
        out_specs=[spec, spec, spec, spec],
        out_shape=[shape(F32), shape(F32), shape(BF16), shape(BF16)],
        compiler_params=_params("arbitrary"),
        name="peer_topk",
    )(qp, keys)


PEER_TM = 512
PEER_ROWS = 4


def _peer_ffn_kernel(ht_ref, u_ref, vt_ref, a_ref, l_ref, b_ref, r2_ref, x_ref, gt_ref, o_ref, acc_ref):
    e = pl.program_id(1)

    @pl.when(e == 0)
    def _():
        acc_ref[...] = jnp.zeros_like(acc_ref)

    hid = jnp.dot(u_ref[...], ht_ref[0], preferred_element_type=F32)
    acts = []
    for ii in range(PEER_ROWS):
        i = e * PEER_ROWS + ii
        gate = jnp.zeros((PEER_N_KEYS, hid.shape[1]), BF16)
        for h in range(PEER_HEADS):
            a_row = a_ref[h, pl.ds(i, 1), :].astype(BF16)
            l_row = l_ref[h, pl.ds(i, 1), :].astype(BF16)
            chosen = r2_ref[h] < l_row
            gate = gate + jnp.where(chosen, b_ref[h], jnp.zeros((), BF16)) * a_row
        hblk = hid[ii * PEER_N_KEYS:(ii + 1) * PEER_N_KEYS]
        gelu = 0.5 * hblk * (1.0 + lax.erf(hblk * (2.0 ** -0.5)))
        acts.append(gate * gelu.astype(BF16))
    act = jnp.concatenate(acts, axis=0)
    acc_ref[...] += jnp.dot(vt_ref[...], act, preferred_element_type=F32)

    @pl.when(e == pl.num_programs(1) - 1)
    def _():
        o_ref[0] = x_ref[0] + gt_ref[0] * acc_ref[...].T


def _peer_ffn_call(ht, u, vt, a_t, l_t, b_t, r2_t, x, gt):
    b, d, s = ht.shape
    tm = PEER_TM
    te = PEER_ROWS * PEER_N_KEYS
    n_e = u.shape[0] // te
    per_b = s // tm
    key_spec = pl.BlockSpec((PEER_HEADS, PEER_N_KEYS, tm), lambda i, e: (0, 0, i))
    return pl.pallas_call(
        _peer_ffn_kernel,
        grid=(b * per_b, n_e),
        in_specs=[
            pl.BlockSpec((1, d, tm), lambda i, e: (i // per_b, 0, i % per_b)),
            pl.BlockSpec((te, d), lambda i, e: (e, 0)),
            pl.BlockSpec((d, te), lambda i, e: (0, e)),
            key_spec, key_spec, key_spec, key_spec,
            pl.BlockSpec((1, tm, d), lambda i, e: (i // per_b, i % per_b, 0)),
            pl.BlockSpec((1, 1, d), lambda i, e: (i // per_b, 0, 0)),
        ],
        out_specs=pl.BlockSpec((1, tm, d), lambda i, e: (i // per_b, i % per_b, 0)),
        out_shape=jax.ShapeDtypeStruct(x.shape, F32),
        scratch_shapes=[pltpu.VMEM((d, tm), F32)],
        compiler_params=_params("arbitrary", "arbitrary"),
        name="peer_ffn",
    )(ht, u, vt, a_t, l_t, b_t, r2_t, x, gt)


def _peer_tables_kernel(u_ref, v_ref, ub_ref, vt_ref):
    ub_ref[...] = u_ref[0].astype(BF16)
    vt_ref[...] = v_ref[0].T.astype(BF16)


def _peer_tables_call(u, v, layer):
    _, e, d = u.shape
    te = 512
    return pl.pallas_call(
        _peer_tables_kernel,
        grid=(e // te,),
        in_specs=[pl.BlockSpec((1, te, d), lambda i: (layer, i, 0)),
                  pl.BlockSpec((1, te, d), lambda i: (layer, i, 0))],
        out_specs=[pl.BlockSpec((te, d), lambda i: (i, 0)), pl.BlockSpec((d, te), lambda i: (0, i))],
        out_shape=[jax.ShapeDtypeStruct((e, d), BF16), jax.ShapeDtypeStruct((d, e), BF16)],
        compiler_params=_params("arbitrary"),
        name="peer_tables",
    )(u, v)


def kernel(x, c, positions, ada_w, ada_b, norm1_g, norm2_g, w_in, b_forget, lam_q1, lam_k1, lam_q2, lam_k2,
           diff_subln_g, w_br_fox, w_br_sb, w_br_diff, w_o, peer_wq, peer_sub_keys, peer_u, peer_v,
           final_norm_g):
    batch, seq, d = x.shape
    tokens = batch * seq
    assert batch <= 8 and seq % 1024 == 0 and d == D_MODEL

    c_pad = jnp.zeros((8, d), F32).at[:batch].set(c)
    mod = _mod_call(c_pad, ada_w, ada_b)[:, :batch].reshape(DEPTH, batch, N_MOD, 1, d)
    pos_col = positions.reshape(tokens, 1)
    inv_freq = ROPE_THETA ** (-jnp.arange(0, DIFF_QK_DIM, 2, dtype=F32) / DIFF_QK_DIM)
    freq_row = jnp.tile(inv_freq, LANES // inv_freq.shape[0]).reshape(1, LANES)
    ri = np.arange(SB_SEG)
    tri_after = jnp.asarray(ri[:, None] > ri[None, :], BF16)

    for l in range(DEPTH):
        lam_init = 0.8 - 0.6 * math.exp(-0.3 * l)
        sh1, sc1, gt1, sh2, sc2, gt2 = (mod[l, :, k] for k in range(N_MOD))

        h1 = _norm_mod_call(x, norm1_g[l].reshape(1, d), sc1, sh1, BF16)[0].reshape(tokens, d)
        proj = _proj_call(h1, w_in, l, 0, QKV_WIDTH, BF16, bn=1024)
        gates = _proj_call(h1, w_in, l, GATE_OFFSET, N_BRANCHES * d, BF16, bn=1024)
        f_logit = _proj_call(h1, w_in, l, QKV_WIDTH, LANES, F32, bn=LANES, keep=FOX_HEADS)
        f_logit = f_logit.reshape(batch, seq, LANES)
        b_forget_pad = jnp.pad(b_forget[l], (0, LANES - FOX_HEADS)).reshape(1, LANES)
        f_col, f_row = _forget_cumsum_call(f_logit, b_forget_pad)

        o_fox = _fox_call(proj, f_col, f_row, batch, seq)
        o_sb = _sb_call(proj, tri_after, batch, seq)
        qd = _rope_call(proj, pos_col, freq_row, COL_DQ, DIFF_QK_DIM ** -0.5)
        kd = _rope_call(proj, pos_col, freq_row, COL_DK, 1.0)
        o_diff = _diff_call(qd, kd, proj, lam_q1[l], lam_k1[l], lam_q2[l], lam_k2[l], diff_subln_g[l],
                            lam_init, batch, seq)

        y = _merge_call(o_fox, o_sb, o_diff, w_br_fox[l].astype(BF16), w_br_sb[l].astype(BF16),
                        w_br_diff[l].astype(BF16), gates)
        x = _matmul_residual_call(y.reshape(batch, seq, d), w_o[l].astype(BF16), x, gt1)

        h2, h2t = _norm_mod_call(x, norm2_g[l].reshape(1, d), sc2, sh2, BF16, transposed=True)
        qp = _proj_call(h2.reshape(tokens, d), peer_wq, l, 0, peer_wq.shape[2], BF16, bn=1024)
        a_t, l_t, b_t, r2_t = _peer_topk_call(qp, peer_sub_keys[l].astype(BF16))
        u_bf, vt_bf = _peer_tables_call(peer_u, peer_v, l)
        x = _peer_ffn_call(h2t, u_bf, vt_bf, a_t, l_t, b_t, r2_t, x, gt2)

    zeros = jnp.zeros((batch, 1, d), F32)
    return _norm_mod_call(x, final_norm_g.reshape(1, d), zeros, zeros, F32)[0]
```

```python
import functools
import math

import jax
import jax.numpy as jnp
import numpy as np
from jax import lax
from jax.experimental import pallas as pl
from jax.experimental.pallas import tpu as pltpu

F32 = jnp.float32
BF16 = jnp.bfloat16

D_MODEL = 2048
DEPTH = 2
CHUNK = 64
HEAD_DIM = 128
FOX_HEADS = 6
SB_HEADS = 6
DIFF_HEADS = 4
DIFF_QK_DIM = HEAD_DIM // 2
FOX_WIDTH = FOX_HEADS * HEAD_DIM
SB_WIDTH = SB_HEADS * HEAD_DIM
DIFF_WIDTH = DIFF_HEADS * HEAD_DIM
QKV_WIDTH = 3 * FOX_WIDTH + 3 * SB_WIDTH + 3 * DIFF_WIDTH
GATE_OFFSET = QKV_WIDTH + FOX_HEADS
ROPE_THETA = 10000.0
RMS_EPS = 1e-6
N_MOD = 6
N_BRANCHES = 3
PEER_HEADS = 8
PEER_N_KEYS = 128
PEER_TOPK = 16
PEER_HALF = 128

LANES = 128
VMEM_LIMIT = 56 * 1024 * 1024
NEG_INF = float("-inf")

COL_FQ, COL_FK, COL_FV = 0, FOX_HEADS, 2 * FOX_HEADS
COL_SQ, COL_SK, COL_SV = 18, 24, 30
COL_DQ, COL_DK, COL_DV = 36, 40, 44
COL_GATE = QKV_WIDTH // LANES


def _params(*sem):
    return pltpu.CompilerParams(dimension_semantics=sem, vmem_limit_bytes=VMEM_LIMIT)


def _mod_kernel(c_ref, w_ref, b_ref, o_ref):
    c = c_ref[...]
    cond = c * jax.nn.sigmoid(c)
    hi = cond.astype(BF16)
    lo = (cond - hi.astype(F32)).astype(BF16)
    w = w_ref[0].astype(BF16)
    acc = jnp.dot(hi, w, preferred_element_type=F32) + jnp.dot(lo, w, preferred_element_type=F32)
    o_ref[0] = acc + b_ref[0]


def _mod_call(c_pad, ada_w, ada_b):
    depth, d, n = ada_w.shape
    tn = 1024
    return pl.pallas_call(
        _mod_kernel,
        grid=(depth, n // tn),
        in_specs=[
            pl.BlockSpec((8, d), lambda l, j: (0, 0)),
            pl.BlockSpec((1, d, tn), lambda l, j: (l, 0, j)),
            pl.BlockSpec((1, 1, tn), lambda l, j: (l, 0, j)),
        ],
        out_specs=pl.BlockSpec((1, 8, tn), lambda l, j: (l, 0, j)),
        out_shape=jax.ShapeDtypeStruct((depth, 8, n), F32),
        compiler_params=_params("arbitrary", "arbitrary"),
        name="adaln_mod",
    )(c_pad, ada_w, ada_b.reshape(depth, 1, n))


def _norm_mod_kernel(x_ref, g_ref, sc_ref, sh_ref, *o_refs, transposed):
    x = x_ref[0]
    y = x * lax.rsqrt(jnp.mean(x * x, axis=-1, keepdims=True) + RMS_EPS) * g_ref[...]
    h = y * (1.0 + sc_ref[0]) + sh_ref[0]
    o_refs[0][0] = h.astype(o_refs[0].dtype)
    if transposed:
        o_refs[1][0] = h.T.astype(o_refs[1].dtype)


def _norm_mod_call(x, g, sc, sh, out_dtype, transposed=False):
    b, s, d = x.shape
    tr = 256
    out_shape = [jax.ShapeDtypeStruct((b, s, d), out_dtype)]
    out_specs = [pl.BlockSpec((1, tr, d), lambda i, j: (i, j, 0))]
    if transposed:
        out_shape.append(jax.ShapeDtypeStruct((b, d, s), out_dtype))
        out_specs.append(pl.BlockSpec((1, d, tr), lambda i, j: (i, 0, j)))
    return pl.pallas_call(
        functools.partial(_norm_mod_kernel, transposed=transposed),
        grid=(b, s // tr),
        in_specs=[
            pl.BlockSpec((1, tr, d), lambda i, j: (i, j, 0)),
            pl.BlockSpec((1, d), lambda i, j: (0, 0)),
            pl.BlockSpec((1, 1, d), lambda i, j: (i, 0, 0)),
            pl.BlockSpec((1, 1, d), lambda i, j: (i, 0, 0)),
        ],
        out_specs=out_specs,
        out_shape=out_shape,
        compiler_params=_params("arbitrary", "arbitrary"),
        name="norm_mod",
    )(x, g, sc, sh)


def _matmul_kernel(a_ref, w_ref, o_ref):
    o_ref[...] = jnp.dot(a_ref[...], w_ref[...], preferred_element_type=F32).astype(o_ref.dtype)


def _matmul_call(a, w, out_dtype, bm=1024, bn=1024):
    m, k = a.shape
    n = w.shape[1]
    bn = min(bn, n)
    return pl.pallas_call(
        _matmul_kernel,
        grid=(m // bm, n // bn),
        in_specs=[pl.BlockSpec((bm, k), lambda i, j: (i, 0)),
                  pl.BlockSpec((k, bn), lambda i, j: (0, j))],
        out_specs=pl.BlockSpec((bm, bn), lambda i, j: (i, j)),
        out_shape=jax.ShapeDtypeStruct((m, n), out_dtype),
        compiler_params=_params("arbitrary", "arbitrary"),
        name="matmul",
    )(a, w)


def _proj_kernel(a_ref, w_ref, *rest, shift, keep):
    o_ref, wb_ref = rest[-2:]

    @pl.when(pl.program_id(1) == 0)
    def _():
        w = w_ref[0]
        if shift:
            w = jnp.concatenate([w[:, shift:], rest[0][0][:, :shift]], axis=1)
        if keep is not None:
            lane = lax.broadcasted_iota(jnp.int32, w.shape, 1)
            w = jnp.where(lane < keep, w, 0.0)
        wb_ref[...] = w.astype(BF16)

    o_ref[...] = jnp.dot(a_ref[...], wb_ref[...], preferred_element_type=F32).astype(o_ref.dtype)


def _proj_call(a, w, layer, col0, n, out_dtype, bn, keep=None, bm=1024):
    m, k = a.shape
    shift = col0 % LANES
    base = (col0 - shift) // bn
    assert (col0 - shift) % bn == 0 and n % bn == 0 and bn % LANES == 0
    next_stride = bn // LANES
    in_specs = [pl.BlockSpec((bm, k), lambda j, i: (i, 0)),
                pl.BlockSpec((1, k, bn), lambda j, i: (layer, 0, base + j))]
    operands = [a, w]
    if shift:
        in_specs.append(pl.BlockSpec((1, k, LANES), lambda j, i: (layer, 0, (base + j + 1) * next_stride)))
        operands.append(w)
    return pl.pallas_call(
        functools.partial(_proj_kernel, shift=shift, keep=keep),
        grid=(n // bn, m // bm),
        in_specs=in_specs,
        out_specs=pl.BlockSpec((bm, bn), lambda j, i: (i, j)),
        out_shape=jax.ShapeDtypeStruct((m, n), out_dtype),
        scratch_shapes=[pltpu.VMEM((k, bn), BF16)],
        compiler_params=_params("arbitrary", "arbitrary"),
        name="proj",
    )(*operands)


def _matmul_residual_kernel(a_ref, w_ref, x_ref, gt_ref, o_ref):
    acc = jnp.dot(a_ref[0], w_ref[...], preferred_element_type=F32)
    o_ref[0] = x_ref[0] + gt_ref[0] * acc


def _matmul_residual_call(a, w, x, gt, bm=1024, bn=1024):
    b, s, k = a.shape
    n = w.shape[1]
    return pl.pallas_call(
        _matmul_residual_kernel,
        grid=(b, s // bm, n // bn),
        in_specs=[pl.BlockSpec((1, bm, k), lambda bi, i, j: (bi, i, 0)),
                  pl.BlockSpec((k, bn), lambda bi, i, j: (0, j)),
                  pl.BlockSpec((1, bm, bn), lambda bi, i, j: (bi, i, j)),
                  pl.BlockSpec((1, 1, bn), lambda bi, i, j: (bi, 0, j))],
        out_specs=pl.BlockSpec((1, bm, bn), lambda bi, i, j: (bi, i, j)),
        out_shape=jax.ShapeDtypeStruct((b, s, n), F32),
        compiler_params=_params("arbitrary", "arbitrary", "arbitrary"),
        name="matmul_residual",
    )(a, w, x, gt)


def _split3(x):
    hi = x.astype(BF16)
    r = x - hi.astype(F32)
    mid = r.astype(BF16)
    lo = (r - mid.astype(F32)).astype(BF16)
    return hi, mid, lo


def _forget_cumsum_kernel(fl_ref, bf_ref, col_ref, row_ref):
    s = fl_ref.shape[1]
    blk = 256
    logf = jax.nn.log_sigmoid(fl_ref[0] + bf_ref[...])
    r = lax.broadcasted_iota(jnp.int32, (blk, blk), 0)
    c = lax.broadcasted_iota(jnp.int32, (blk, blk), 1)
    tri = jnp.where(c <= r, 1.0, 0.0).astype(BF16)
    carry = jnp.zeros((1, LANES), F32)
    parts = []
    for i in range(s // blk):
        hi, mid, lo = _split3(logf[i * blk:(i + 1) * blk])
        cs = (jnp.dot(tri, hi, preferred_element_type=F32)
              + jnp.dot(tri, mid, preferred_element_type=F32)
              + jnp.dot(tri, lo, preferred_element_type=F32)) + carry
        carry = cs[blk - 1:blk, :]
        parts.append(cs)
    cum = jnp.concatenate(parts, axis=0)
    row_ref[0] = cum.T[:8, :]
    for h in range(FOX_HEADS):
        col_ref[0, h] = jnp.broadcast_to(cum[:, h:h + 1], (s, LANES))


def _forget_cumsum_call(f_logit, b_forget_pad):
    b, s, _ = f_logit.shape
    return pl.pallas_call(
        _forget_cumsum_kernel,
        grid=(b,),
        in_specs=[pl.BlockSpec((1, s, LANES), lambda i: (i, 0, 0)),
                  pl.BlockSpec((1, LANES), lambda i: (0, 0))],
        out_specs=[pl.BlockSpec((1, FOX_HEADS, s, LANES), lambda i: (i, 0, 0, 0)),
                   pl.BlockSpec((1, 8, s), lambda i: (i, 0, 0))],
        out_shape=[jax.ShapeDtypeStruct((b, FOX_HEADS, s, LANES), F32),
                   jax.ShapeDtypeStruct((b, 8, s), F32)],
        compiler_params=_params("arbitrary"),
        name="forget_cumsum",
    )(f_logit, b_forget_pad)


ATT_TILE = 512


def _qk(q, k):
    return lax.dot_general(q, k, (((1,), (1,)), ((), ())), preferred_element_type=F32)


def _kv_block(k_ref, v_ref, j):
    start = pl.multiple_of(j * ATT_TILE, ATT_TILE)
    return start, k_ref[pl.ds(start, ATT_TILE), :], v_ref[pl.ds(start, ATT_TILE), :]


def _tile_iotas():
    r = lax.broadcasted_iota(jnp.int32, (ATT_TILE, ATT_TILE), 0)
    c = lax.broadcasted_iota(jnp.int32, (ATT_TILE, ATT_TILE), 1)
    return r, c


def _softmax_state():
    t = ATT_TILE
    return [pltpu.VMEM((t, LANES), F32), pltpu.VMEM((t, LANES), F32), pltpu.VMEM((t, HEAD_DIM), F32)]


def _softmax_reset(m_ref, l_ref, acc_ref):
    m_ref[...] = jnp.full(m_ref.shape, NEG_INF, F32)
    l_ref[...] = jnp.zeros(l_ref.shape, F32)
    acc_ref[...] = jnp.zeros(acc_ref.shape, F32)


def _softmax_update(m_ref, l_ref, acc_ref, s, vb):
    m_old = m_ref[...]
    m_new = jnp.maximum(m_old, jnp.max(s, axis=-1, keepdims=True))
    alpha = jnp.exp(m_old - m_new)
    p = jnp.exp(s - jnp.tile(m_new, (1, s.shape[1] // LANES)))
    l_ref[...] = alpha * l_ref[...] + jnp.sum(p, axis=-1, keepdims=True)
    acc_ref[...] = alpha * acc_ref[...] + jnp.dot(p.astype(BF16), vb, preferred_element_type=F32)
    m_ref[...] = m_new


def _fox_kernel(q_ref, k_ref, v_ref, fc_ref, fr_ref, o_ref, m_ref, l_ref, acc_ref):
    t = ATT_TILE
    qi = pl.program_id(2)
    h = pl.program_id(1)
    q = (q_ref[...].astype(F32) * (HEAD_DIM ** -0.5)).astype(BF16)

    def step(j, masked):
        start, kb, vb = _kv_block(k_ref, v_ref, j)
        fcol = jnp.tile(fc_ref[0, 0], (1, t // LANES))
        frow = fr_ref[0, pl.ds(h, 1), pl.ds(start, t)]
        s = _qk(q, kb) + (fcol - frow)
        if masked:
            r, c = _tile_iotas()
            s = jnp.where(c <= r, s, NEG_INF)
        _softmax_update(m_ref, l_ref, acc_ref, s, vb)

    _softmax_reset(m_ref, l_ref, acc_ref)
    step(qi, masked=True)

    @pl.loop(0, qi)
    def _(j):
        step(j, masked=False)

    o_ref[...] = (acc_ref[...] / l_ref[...]).astype(o_ref.dtype)


def _fox_call(proj, f_col, f_row, batch, seq):
    tq = ATT_TILE
    nq = seq // tq
    return pl.pallas_call(
        _fox_kernel,
        grid=(batch, FOX_HEADS, nq),
        in_specs=[
            pl.BlockSpec((tq, HEAD_DIM), lambda b, h, i: (b * nq + i, COL_FQ + h)),
            pl.BlockSpec((seq, HEAD_DIM), lambda b, h, i: (b, COL_FK + h)),
            pl.BlockSpec((seq, HEAD_DIM), lambda b, h, i: (b, COL_FV + h)),
            pl.BlockSpec((1, 1, tq, LANES), lambda b, h, i: (b, h, i, 0)),
            pl.BlockSpec((1, 8, seq), lambda b, h, i: (b, 0, 0)),
        ],
        out_specs=pl.BlockSpec((tq, HEAD_DIM), lambda b, h, i: (b * nq + i, h)),
        out_shape=jax.ShapeDtypeStruct((batch * seq, FOX_WIDTH), BF16),
        scratch_shapes=_softmax_state(),
        compiler_params=_params("arbitrary", "arbitrary", "arbitrary"),
        name="fox_attention",
    )(proj, proj, proj, f_col, f_row)


SB_SEG = 256


def _sb_kernel(q_ref, k_ref, v_ref, tri_ref, o_ref, run_ref, acc_ref):
    t = ATT_TILE
    qi = pl.program_id(2)
    qn = (q_ref[...].astype(F32) * -(HEAD_DIM ** -0.5)).astype(BF16)

    def step(j, masked):
        _, kb, vb = _kv_block(k_ref, v_ref, j)
        tri2 = tri_ref[...]
        u = _qk(qn, kb)
        lk = jnp.minimum(u, 0.0) - jnp.log(1.0 + jnp.exp(-jnp.abs(u)))
        if masked:
            r, c = _tile_iotas()
            strictly_causal = c < r
            lk = jnp.where(strictly_causal, lk, 0.0)
        hi = lk.astype(BF16)
        lo = (lk - hi.astype(F32)).astype(BF16)
        run = run_ref[...]
        laters = []
        for g in reversed(range(t // SB_SEG)):
            sl = slice(g * SB_SEG, (g + 1) * SB_SEG)
            hi_lo = jnp.concatenate([hi[:, sl], lo[:, sl]], axis=1)
            laters.append(jnp.dot(hi_lo, tri2, preferred_element_type=F32) + jnp.tile(run, (1, SB_SEG // LANES)))
            run = run + jnp.sum(lk[:, sl], axis=-1, keepdims=True)
        later = jnp.concatenate(laters[::-1], axis=1)
        w = jnp.exp(lk - u + later)
        if masked:
            w = jnp.where(strictly_causal, w, 0.0)
        acc_ref[...] += jnp.dot(w.astype(BF16), vb, preferred_element_type=F32)
        run_ref[...] = run

    run_ref[...] = jnp.zeros(run_ref.shape, F32)
    acc_ref[...] = jnp.zeros(acc_ref.shape, F32)
    step(qi, masked=True)

    @pl.loop(0, qi)
    def _(i):
        step(qi - 1 - i, masked=False)

    o_ref[...] = acc_ref[...].astype(o_ref.dtype)


def _sb_call(proj, tri, batch, seq):
    tq = ATT_TILE
    nq = seq // tq
    return pl.pallas_call(
        _sb_kernel,
        grid=(batch, SB_HEADS, nq),
        in_specs=[
            pl.BlockSpec((tq, HEAD_DIM), lambda b, h, i: (b * nq + i, COL_SQ + h)),
            pl.BlockSpec((seq, HEAD_DIM), lambda b, h, i: (b, COL_SK + h)),
            pl.BlockSpec((seq, HEAD_DIM), lambda b, h, i: (b, COL_SV + h)),
            pl.BlockSpec((2 * SB_SEG, SB_SEG), lambda b, h, i: (0, 0)),
        ],
        out_specs=pl.BlockSpec((tq, HEAD_DIM), lambda b, h, i: (b * nq + i, h)),
        out_shape=jax.ShapeDtypeStruct((batch * seq, SB_WIDTH), BF16),
        scratch_shapes=[pltpu.VMEM((tq, LANES), F32), pltpu.VMEM((tq, HEAD_DIM), F32)],
        compiler_params=_params("arbitrary", "arbitrary", "arbitrary"),
        name="sb_attention",
    )(proj, proj, proj, tri)


def _rope_kernel(x_ref, pos_ref, freq_ref, o_ref, *, out_scale):
    half = DIFF_QK_DIM // 2
    ang = pos_ref[...].astype(F32) * freq_ref[...]
    cos = jnp.cos(ang)
    sin = jnp.sin(ang)
    lane = lax.broadcasted_iota(jnp.int32, ang.shape, 1)
    first = (lane % DIFF_QK_DIM) < half
    sin_signed = jnp.where(first, -sin, sin)
    for cb in range(x_ref.shape[1] // LANES):
        x = x_ref[:, cb * LANES:(cb + 1) * LANES].astype(F32)
        partner = jnp.where(first, pltpu.roll(x, LANES - half, 1), pltpu.roll(x, half, 1))
        y = x * cos + partner * sin_signed
        o_ref[:, cb * LANES:(cb + 1) * LANES] = (y * out_scale).astype(o_ref.dtype)


def _rope_call(proj, pos_col, freq_row, col_block, out_scale):
    tokens = proj.shape[0]
    tr = 512
    width = DIFF_HEADS * HEAD_DIM
    return pl.pallas_call(
        functools.partial(_rope_kernel, out_scale=out_scale),
        grid=(tokens // tr,),
        in_specs=[pl.BlockSpec((tr, width), lambda i: (i, col_block * LANES // width)),
                  pl.BlockSpec((tr, 1), lambda i: (i, 0)),
                  pl.BlockSpec((1, LANES), lambda i: (0, 0))],
        out_specs=pl.BlockSpec((tr, width), lambda i: (i, 0)),
        out_shape=jax.ShapeDtypeStruct((tokens, width), BF16),
        compiler_params=_params("arbitrary"),
        name="rope",
    )(proj, pos_col, freq_row)


def _diff_kernel(q_ref, k_ref, v_ref, lq1_ref, lk1_ref, lq2_ref, lk2_ref, g_ref, o_ref,
                 m1_ref, l1_ref, a1_ref, m2_ref, l2_ref, a2_ref, *, lam_init):
    t = ATT_TILE
    qi = pl.program_id(2)
    q = q_ref[...]
    lane = lax.broadcasted_iota(jnp.int32, q.shape, 1)
    zero = jnp.zeros_like(q)
    q1 = jnp.where(lane < DIFF_QK_DIM, q, zero)
    q2 = jnp.where(lane < DIFF_QK_DIM, zero, q)
    state1 = (m1_ref, l1_ref, a1_ref)
    state2 = (m2_ref, l2_ref, a2_ref)

    def step(j, masked):
        _, kb, vb = _kv_block(k_ref, v_ref, j)
        for qc, state in ((q1, state1), (q2, state2)):
            s = _qk(qc, kb)
            if masked:
                r, c = _tile_iotas()
                s = jnp.where((c // CHUNK) <= (r // CHUNK), s, NEG_INF)
            _softmax_update(*state, s, vb)

    _softmax_reset(*state1)
    _softmax_reset(*state2)
    step(qi, masked=True)

    @pl.loop(0, qi)
    def _(j):
        step(j, masked=False)

    lam = (jnp.exp(jnp.sum(lq1_ref[0] * lk1_ref[0], axis=-1, keepdims=True))
           - jnp.exp(jnp.sum(lq2_ref[0] * lk2_ref[0], axis=-1, keepdims=True)) + lam_init)
    o = a1_ref[...] / l1_ref[...] - lam * (a2_ref[...] / l2_ref[...])
    o = o * lax.rsqrt(jnp.mean(o * o, axis=-1, keepdims=True) + RMS_EPS) * g_ref[...]
    o_ref[...] = (o * (1.0 - lam_init)).astype(o_ref.dtype)


def _diff_call(qd, kd, proj, lam_q1, lam_k1, lam_q2, lam_k2, subln_g, lam_init, batch, seq):
    t = ATT_TILE
    nq = seq // t
    lam_spec = pl.BlockSpec((1, 1, DIFF_QK_DIM), lambda b, h, i: (h, 0, 0))
    r3 = lambda a: a.reshape(DIFF_HEADS, 1, DIFF_QK_DIM)
    return pl.pallas_call(
        functools.partial(_diff_kernel, lam_init=lam_init),
        grid=(batch, DIFF_HEADS, nq),
        in_specs=[
            pl.BlockSpec((t, HEAD_DIM), lambda b, h, i: (b * nq + i, h)),
            pl.BlockSpec((seq, HEAD_DIM), lambda b, h, i: (b, h)),
            pl.BlockSpec((seq, HEAD_DIM), lambda b, h, i: (b, COL_DV + h)),
            lam_spec, lam_spec, lam_spec, lam_spec,
            pl.BlockSpec((1, HEAD_DIM), lambda b, h, i: (0, 0)),
        ],
        out_specs=pl.BlockSpec((t, HEAD_DIM), lambda b, h, i: (b * nq + i, h)),
        out_shape=jax.ShapeDtypeStruct((batch * seq, DIFF_WIDTH), BF16),
        scratch_shapes=_softmax_state() + _softmax_state(),
        compiler_params=_params("arbitrary", "arbitrary", "arbitrary"),
        name="diff_attention",
    )(qd, kd, proj, r3(lam_q1), r3(lam_k1), r3(lam_q2), r3(lam_k2), subln_g.reshape(1, HEAD_DIM))


def _merge_kernel(of_ref, os_ref, od_ref, wf_ref, ws_ref, wd_ref, gf_ref, gs_ref, gd_ref, o_ref):
    def branch(o_r, w_r, g_r):
        return jax.nn.sigmoid(g_r[...].astype(F32)) * jnp.dot(o_r[...], w_r[...], preferred_element_type=F32)

    y = branch(of_ref, wf_ref, gf_ref) + branch(os_ref, ws_ref, gs_ref) + branch(od_ref, wd_ref, gd_ref)
    o_ref[...] = y.astype(o_ref.dtype)


def _merge_call(o_fox, o_sb, o_diff, w_fox, w_sb, w_diff, gates, bm=1024, bn=512):
    m = o_fox.shape[0]
    d = w_fox.shape[1]
    gate_blocks = d // bn

    def gate_spec(k):
        return pl.BlockSpec((bm, bn), lambda i, j: (i, k * gate_blocks + j))

    return pl.pallas_call(
        _merge_kernel,
        grid=(m // bm, d // bn),
        in_specs=[
            pl.BlockSpec((bm, FOX_WIDTH), lambda i, j: (i, 0)),
            pl.BlockSpec((bm, SB_WIDTH), lambda i, j: (i, 0)),
            pl.BlockSpec((bm, DIFF_WIDTH), lambda i, j: (i, 0)),
            pl.BlockSpec((FOX_WIDTH, bn), lambda i, j: (0, j)),
            pl.BlockSpec((SB_WIDTH, bn), lambda i, j: (0, j)),
            pl.BlockSpec((DIFF_WIDTH, bn), lambda i, j: (0, j)),
            gate_spec(0), gate_spec(1), gate_spec(2),
        ],
        out_specs=pl.BlockSpec((bm, bn), lambda i, j: (i, j)),
        out_shape=jax.ShapeDtypeStruct((m, d), BF16),
        compiler_params=_params("arbitrary", "arbitrary"),
        name="branch_merge",
    )(o_fox, o_sb, o_diff, w_fox, w_sb, w_diff, gates, gates, gates)


NOT_TOP = 64.0


def _extract_top(cur, count, want_rank=False):
    rows = []
    rank = jnp.full(cur.shape, NOT_TOP, F32) if want_rank else None
    for r in range(count):
        m = jnp.max(cur, axis=0, keepdims=True)
        rows.append(m)
        hit = cur == m
        if want_rank:
            rank = jnp.where(hit, float(r), rank)
        cur = jnp.where(hit, NEG_INF, cur)
    return rows, rank


def _peer_topk_kernel(q_ref, keys_ref, a_ref, l_ref, b_ref, r2_ref):
    k = PEER_TOPK
    for h in range(PEER_HEADS):
        col = 2 * h * PEER_HALF
        s1 = _qk(keys_ref[h, 0], q_ref[:, col:col + PEER_HALF])
        s2 = _qk(keys_ref[h, 1], q_ref[:, col + PEER_HALF:col + 2 * PEER_HALF])
        a_rows, rank1 = _extract_top(s1, k, want_rank=True)
        b_rows, rank2 = _extract_top(s2, k, want_rank=True)
        b_top = jnp.concatenate(b_rows, axis=0)
        counts = [k // (k1 + 1) for k1 in range(k)]
        cands = [a_rows[k1] + b_top[:counts[k1]] for k1 in range(k)]
        n_cand = sum(counts)
        pad = jnp.full((-n_cand % 8, b_top.shape[1]), NEG_INF, F32)
        best, _ = _extract_top(jnp.concatenate(cands + [pad], axis=0), k)
        tau = best[k - 1]
        z = jnp.zeros_like(tau)
        for row in best:
            z = z + jnp.exp(row - best[0])
        l_row = jnp.zeros(s1.shape, F32)
        for k1 in range(k):
            l_k = jnp.sum(jnp.where(cands[k1] >= tau, 1.0, 0.0), axis=0, keepdims=True)
            l_row = jnp.where(rank1 == float(k1), l_k, l_row)
        a_ref[h] = jnp.exp(s1 - a_rows[0]) / z
        l_ref[h] = l_row
        b_ref[h] = jnp.exp(s2 - b_rows[0]).astype(b_ref.dtype)
        r2_ref[h] = rank2.astype(r2_ref.dtype)


def _peer_topk_call(qp, keys):
    tokens = qp.shape[0]
    tq = 512
    spec = pl.BlockSpec((PEER_HEADS, PEER_N_KEYS, tq), lambda i: (0, 0, i))
    shape = lambda dt: jax.ShapeDtypeStruct((PEER_HEADS, PEER_N_KEYS, tokens), dt)
    return pl.pallas_call(
        _peer_topk_kernel,
        grid=(tokens // tq,),
        in_specs=[pl.BlockSpec((tq, qp.shape[1]), lambda i: (i, 0)),
                  pl.BlockSpec(keys.shape, lambda i: (0, 0, 0, 0))],
        out_specs=[spec, spec, spec, spec],
        out_shape=[shape(F32), shape(F32), shape(BF16), shape(BF16)],
        compiler_params=_params("arbitrary"),
        name="peer_topk",
    )(qp, keys)


PEER_TM = 512
PEER_ROWS = 4


def _peer_ffn_kernel(ht_ref, u_ref, vt_ref, a_ref, l_ref, b_ref, r2_ref, x_ref, gt_ref, o_ref,
                     acc_ref, hid_ref, act_ref, *, n_tiles):
    s = pl.program_id(1)
    assert n_tiles % 2 == 0

    d_chunk = acc_ref.shape[0] // PEER_ROWS

    def up(slot, c):
        rows = slice(c * PEER_N_KEYS, (c + 1) * PEER_N_KEYS)
        hid_ref[slot, rows, :] = jnp.dot(u_ref[rows, :], ht_ref[0], preferred_element_type=F32)

    def activate(slot, c):
        i = (s - 1) * PEER_ROWS + c
        gate = jnp.zeros((PEER_N_KEYS, hid_ref.shape[2]), BF16)
        for h in range(PEER_HEADS):
            a_row = a_ref[h, pl.ds(i, 1), :].astype(BF16)
            l_row = l_ref[h, pl.ds(i, 1), :].astype(BF16)
            chosen = r2_ref[h] < l_row
            gate = gate + jnp.where(chosen, b_ref[h], jnp.zeros((), BF16)) * a_row
        rows = slice(c * PEER_N_KEYS, (c + 1) * PEER_N_KEYS)
        hblk = hid_ref[slot, rows, :]
        gelu = 0.5 * hblk * (1.0 + lax.erf(hblk * (2.0 ** -0.5)))
        act_ref[slot, rows, :] = gate * gelu.astype(BF16)

    def down(slot, c):
        rows = slice(c * d_chunk, (c + 1) * d_chunk)
        acc_ref[rows, :] += jnp.dot(vt_ref[rows, :], act_ref[slot], preferred_element_type=F32)

    def stages(parity, do_up, do_act, do_down):
        for c in range(PEER_ROWS):
            if do_down:
                down(parity, c)
            if do_act:
                activate(1 - parity, c)
            if do_up:
                up(parity, c)

    @pl.when(s == 0)
    def _():
        acc_ref[...] = jnp.zeros_like(acc_ref)
        stages(0, True, False, False)

    @pl.when(s == 1)
    def _():
        stages(1, True, True, False)

    steady = (s >= 2) & (s < n_tiles)
    for parity in range(2):
        @pl.when(steady & (s % 2 == parity))
        def _():
            stages(parity, True, True, True)

    @pl.when(s == n_tiles)
    def _():
        stages(0, False, True, True)

    @pl.when(s == n_tiles + 1)
    def _():
        stages(1, False, False, True)
        o_ref[0] = x_ref[0] + gt_ref[0] * acc_ref[...].T


def _peer_ffn_call(ht, u, vt, a_t, l_t, b_t, r2_t, x, gt):
    b, d, s = ht.shape
    tm = PEER_TM
    te = PEER_ROWS * PEER_N_KEYS
    n_e = u.shape[0] // te
    per_b = s // tm
    key_spec = pl.BlockSpec((PEER_HEADS, PEER_N_KEYS, tm), lambda i, e: (0, 0, i))
    return pl.pallas_call(
        functools.partial(_peer_ffn_kernel, n_tiles=n_e),
        grid=(b * per_b, n_e + 2),
        in_specs=[
            pl.BlockSpec((1, d, tm), lambda i, e: (i // per_b, 0, i % per_b)),
            pl.BlockSpec((te, d), lambda i, e: (jnp.minimum(e, n_e - 1), 0)),
            pl.BlockSpec((d, te), lambda i, e: (0, jnp.maximum(e - 2, 0))),
            key_spec, key_spec, key_spec, key_spec,
            pl.BlockSpec((1, tm, d), lambda i, e: (i // per_b, i % per_b, 0)),
            pl.BlockSpec((1, 1, d), lambda i, e: (i // per_b, 0, 0)),
        ],
        out_specs=pl.BlockSpec((1, tm, d), lambda i, e: (i // per_b, i % per_b, 0)),
        out_shape=jax.ShapeDtypeStruct(x.shape, F32),
        scratch_shapes=[pltpu.VMEM((d, tm), F32), pltpu.VMEM((2, te, tm), F32), pltpu.VMEM((2, te, tm), BF16)],
        compiler_params=_params("arbitrary", "arbitrary"),
        name="peer_ffn",
    )(ht, u, vt, a_t, l_t, b_t, r2_t, x, gt)


def _peer_tables_kernel(u_ref, v_ref, ub_ref, vt_ref):
    ub_ref[...] = u_ref[0].astype(BF16)
    vt_ref[...] = v_ref[0].T.astype(BF16)


def _peer_tables_call(u, v, layer):
    _, e, d = u.shape
    te = 512
    return pl.pallas_call(
        _peer_tables_kernel,
        grid=(e // te,),
        in_specs=[pl.BlockSpec((1, te, d), lambda i: (layer, i, 0)),
                  pl.BlockSpec((1, te, d), lambda i: (layer, i, 0))],
        out_specs=[pl.BlockSpec((te, d), lambda i: (i, 0)), pl.BlockSpec((d, te), lambda i: (0, i))],
        out_shape=[jax.ShapeDtypeStruct((e, d), BF16), jax.ShapeDtypeStruct((d, e), BF16)],
        compiler_params=_params("arbitrary"),
        name="peer_tables",
    )(u, v)


def kernel(x, c, positions, ada_w, ada_b, norm1_g, norm2_g, w_in, b_forget, lam_q1, lam_k1, lam_q2, lam_k2,
           diff_subln_g, w_br_fox, w_br_sb, w_br_diff, w_o, peer_wq, peer_sub_keys, peer_u, peer_v,
           final_norm_g):
    batch, seq, d = x.shape
    tokens = batch * seq
    assert batch <= 8 and seq % 1024 == 0 and d == D_MODEL

    c_pad = jnp.zeros((8, d), F32).at[:batch].set(c)
    mod = _mod_call(c_pad, ada_w, ada_b)[:, :batch].reshape(DEPTH, batch, N_MOD, 1, d)
    pos_col = positions.reshape(tokens, 1)
    inv_freq = ROPE_THETA ** (-jnp.arange(0, DIFF_QK_DIM, 2, dtype=F32) / DIFF_QK_DIM)
    freq_row = jnp.tile(inv_freq, LANES // inv_freq.shape[0]).reshape(1, LANES)
    ri = np.arange(SB_SEG)
    tri_after = jnp.asarray(np.tile(ri[:, None] > ri[None, :], (2, 1)), BF16)

    for l in range(DEPTH):
        lam_init = 0.8 - 0.6 * math.exp(-0.3 * l)
        sh1, sc1, gt1, sh2, sc2, gt2 = (mod[l, :, k] for k in range(N_MOD))

        h1 = _norm_mod_call(x, norm1_g[l].reshape(1, d), sc1, sh1, BF16)[0].reshape(tokens, d)
        proj = _proj_call(h1, w_in, l, 0, QKV_WIDTH, BF16, bn=1024)
        gates = _proj_call(h1, w_in, l, GATE_OFFSET, N_BRANCHES * d, BF16, bn=1024)
        f_logit = _proj_call(h1, w_in, l, QKV_WIDTH, LANES, F32, bn=LANES, keep=FOX_HEADS)
        f_logit = f_logit.reshape(batch, seq, LANES)
        b_forget_pad = jnp.pad(b_forget[l], (0, LANES - FOX_HEADS)).reshape(1, LANES)
        f_col, f_row = _forget_cumsum_call(f_logit, b_forget_pad)

        o_fox = _fox_call(proj, f_col, f_row, batch, seq)
        o_sb = _sb_call(proj, tri_after, batch, seq)
        qd = _rope_call(proj, pos_col, freq_row, COL_DQ, DIFF_QK_DIM ** -0.5)
        kd = _rope_call(proj, pos_col, freq_row, COL_DK, 1.0)
        o_diff = _diff_call(qd, kd, proj, lam_q1[l], lam_k1[l], lam_q2[l], lam_k2[l], diff_subln_g[l],
                            lam_init, batch, seq)

        y = _merge_call(o_fox, o_sb, o_diff, w_br_fox[l].astype(BF16), w_br_sb[l].astype(BF16),
                        w_br_diff[l].astype(BF16), gates)
        x = _matmul_residual_call(y.reshape(batch, seq, d), w_o[l].astype(BF16), x, gt1)

        h2, h2t = _norm_mod_call(x, norm2_g[l].reshape(1, d), sc2, sh2, BF16, transposed=True)
        qp = _proj_call(h2.reshape(tokens, d), peer_wq, l, 0, peer_wq.shape[2], BF16, bn=1024)
        a_t, l_t, b_t, r2_t = _peer_topk_call(qp, peer_sub_keys[l].astype(BF16))
        u_bf, vt_bf = _peer_tables_call(peer_u, peer_v, l)
        x = _peer_ffn_call(h2t, u_bf, vt_bf, a_t, l_t, b_t, r2_t, x, gt2)

    zeros = jnp.zeros((batch, 1, d), F32)
    return _norm_mod_call(x, final_norm_g.reshape(1, d), zeros, zeros, F32)[0]
```

```python
import functools
import math

import jax
import jax.numpy as jnp
import numpy as np
from jax import lax
from jax.experimental import pallas as pl
from jax.experimental.pallas import tpu as pltpu

F32 = jnp.float32
BF16 = jnp.bfloat16

D_MODEL = 2048
DEPTH = 2
CHUNK = 64
HEAD_DIM = 128
FOX_HEADS = 6
SB_HEADS = 6
DIFF_HEADS = 4
DIFF_QK_DIM = HEAD_DIM // 2
FOX_WIDTH = FOX_HEADS * HEAD_DIM
SB_WIDTH = SB_HEADS * HEAD_DIM
DIFF_WIDTH = DIFF_HEADS * HEAD_DIM
QKV_WIDTH = 3 * FOX_WIDTH + 3 * SB_WIDTH + 3 * DIFF_WIDTH
GATE_OFFSET = QKV_WIDTH + FOX_HEADS
ROPE_THETA = 10000.0
RMS_EPS = 1e-6
N_MOD = 6
N_BRANCHES = 3
PEER_HEADS = 8
PEER_N_KEYS = 128
PEER_TOPK = 16
PEER_HALF = 128

LANES = 128
BF16_ROWS = 16
VMEM_LIMIT = 56 * 1024 * 1024
NEG_INF = float("-inf")

COL_FQ, COL_FK, COL_FV = 0, FOX_HEADS, 2 * FOX_HEADS
COL_SQ, COL_SK, COL_SV = 18, 24, 30
COL_DQ, COL_DK, COL_DV = 36, 40, 44
COL_GATE = QKV_WIDTH // LANES


def _params(*sem):
    return pltpu.CompilerParams(dimension_semantics=sem, vmem_limit_bytes=VMEM_LIMIT)


def _mod_kernel(c_ref, w_ref, b_ref, o_ref):
    c = c_ref[...]
    cond = c * jax.nn.sigmoid(c)
    hi = cond.astype(BF16)
    lo = (cond - hi.astype(F32)).astype(BF16)
    w = w_ref[0].astype(BF16)
    acc = jnp.dot(hi, w, preferred_element_type=F32) + jnp.dot(lo, w, preferred_element_type=F32)
    o_ref[0] = acc + b_ref[0]


def _mod_call(c_pad, ada_w, ada_b):
    depth, d, n = ada_w.shape
    tn = 1024
    return pl.pallas_call(
        _mod_kernel,
        grid=(depth, n // tn),
        in_specs=[
            pl.BlockSpec((8, d), lambda l, j: (0, 0)),
            pl.BlockSpec((1, d, tn), lambda l, j: (l, 0, j)),
            pl.BlockSpec((1, 1, tn), lambda l, j: (l, 0, j)),
        ],
        out_specs=pl.BlockSpec((1, 8, tn), lambda l, j: (l, 0, j)),
        out_shape=jax.ShapeDtypeStruct((depth, 8, n), F32),
        compiler_params=_params("arbitrary", "arbitrary"),
        name="adaln_mod",
    )(c_pad, ada_w, ada_b.reshape(depth, 1, n))


def _norm_mod_kernel(x_ref, g_ref, sc_ref, sh_ref, *o_refs, transposed):
    x = x_ref[0]
    y = x * lax.rsqrt(jnp.mean(x * x, axis=-1, keepdims=True) + RMS_EPS) * g_ref[...]
    h = y * (1.0 + sc_ref[0]) + sh_ref[0]
    o_refs[0][0] = h.astype(o_refs[0].dtype)
    if transposed:
        o_refs[1][0] = h.T.astype(o_refs[1].dtype)


def _norm_mod_call(x, g, sc, sh, out_dtype, transposed=False):
    b, s, d = x.shape
    tr = 256
    out_shape = [jax.ShapeDtypeStruct((b, s, d), out_dtype)]
    out_specs = [pl.BlockSpec((1, tr, d), lambda i, j: (i, j, 0))]
    if transposed:
        out_shape.append(jax.ShapeDtypeStruct((b, d, s), out_dtype))
        out_specs.append(pl.BlockSpec((1, d, tr), lambda i, j: (i, 0, j)))
    return pl.pallas_call(
        functools.partial(_norm_mod_kernel, transposed=transposed),
        grid=(b, s // tr),
        in_specs=[
            pl.BlockSpec((1, tr, d), lambda i, j: (i, j, 0)),
            pl.BlockSpec((1, d), lambda i, j: (0, 0)),
            pl.BlockSpec((1, 1, d), lambda i, j: (i, 0, 0)),
            pl.BlockSpec((1, 1, d), lambda i, j: (i, 0, 0)),
        ],
        out_specs=out_specs,
        out_shape=out_shape,
        compiler_params=_params("arbitrary", "arbitrary"),
        name="norm_mod",
    )(x, g, sc, sh)


def _matmul_kernel(a_ref, w_ref, o_ref):
    o_ref[...] = jnp.dot(a_ref[...], w_ref[...], preferred_element_type=F32).astype(o_ref.dtype)


def _matmul_call(a, w, out_dtype, bm=1024, bn=1024):
    m, k = a.shape
    n = w.shape[1]
    bn = min(bn, n)
    return pl.pallas_call(
        _matmul_kernel,
        grid=(m // bm, n // bn),
        in_specs=[pl.BlockSpec((bm, k), lambda i, j: (i, 0)),
                  pl.BlockSpec((k, bn), lambda i, j: (0, j))],
        out_specs=pl.BlockSpec((bm, bn), lambda i, j: (i, j)),
        out_shape=jax.ShapeDtypeStruct((m, n), out_dtype),
        compiler_params=_params("arbitrary", "arbitrary"),
        name="matmul",
    )(a, w)


def _proj_kernel(a_ref, w_ref, *rest, shift, keep):
    o_ref, wb_ref = rest[-2:]

    @pl.when(pl.program_id(1) == 0)
    def _():
        w = w_ref[0]
        if shift:
            w = jnp.concatenate([w[:, shift:], rest[0][0][:, :shift]], axis=1)
        if keep is not None:
            lane = lax.broadcasted_iota(jnp.int32, w.shape, 1)
            w = jnp.where(lane < keep, w, 0.0)
        wb_ref[...] = w.astype(BF16)

    o_ref[...] = jnp.dot(a_ref[...], wb_ref[...], preferred_element_type=F32).astype(o_ref.dtype)


def _proj_call(a, w, layer, col0, n, out_dtype, bn, keep=None, bm=1024):
    m, k = a.shape
    shift = col0 % LANES
    base = (col0 - shift) // bn
    assert (col0 - shift) % bn == 0 and n % bn == 0 and bn % LANES == 0
    next_stride = bn // LANES
    in_specs = [pl.BlockSpec((bm, k), lambda j, i: (i, 0)),
                pl.BlockSpec((1, k, bn), lambda j, i: (layer, 0, base + j))]
    operands = [a, w]
    if shift:
        in_specs.append(pl.BlockSpec((1, k, LANES), lambda j, i: (layer, 0, (base + j + 1) * next_stride)))
        operands.append(w)
    return pl.pallas_call(
        functools.partial(_proj_kernel, shift=shift, keep=keep),
        grid=(n // bn, m // bm),
        in_specs=in_specs,
        out_specs=pl.BlockSpec((bm, bn), lambda j, i: (i, j)),
        out_shape=jax.ShapeDtypeStruct((m, n), out_dtype),
        scratch_shapes=[pltpu.VMEM((k, bn), BF16)],
        compiler_params=_params("arbitrary", "arbitrary"),
        name="proj",
    )(*operands)


def _matmul_residual_kernel(a_ref, w_ref, x_ref, gt_ref, o_ref):
    acc = jnp.dot(a_ref[0], w_ref[...], preferred_element_type=F32)
    o_ref[0] = x_ref[0] + gt_ref[0] * acc


def _matmul_residual_call(a, w, x, gt, bm=1024, bn=1024):
    b, s, k = a.shape
    n = w.shape[1]
    return pl.pallas_call(
        _matmul_residual_kernel,
        grid=(b, s // bm, n // bn),
        in_specs=[pl.BlockSpec((1, bm, k), lambda bi, i, j: (bi, i, 0)),
                  pl.BlockSpec((k, bn), lambda bi, i, j: (0, j)),
                  pl.BlockSpec((1, bm, bn), lambda bi, i, j: (bi, i, j)),
                  pl.BlockSpec((1, 1, bn), lambda bi, i, j: (bi, 0, j))],
        out_specs=pl.BlockSpec((1, bm, bn), lambda bi, i, j: (bi, i, j)),
        out_shape=jax.ShapeDtypeStruct((b, s, n), F32),
        compiler_params=_params("arbitrary", "arbitrary", "arbitrary"),
        name="matmul_residual",
    )(a, w, x, gt)


def _split3(x):
    hi = x.astype(BF16)
    r = x - hi.astype(F32)
    mid = r.astype(BF16)
    lo = (r - mid.astype(F32)).astype(BF16)
    return hi, mid, lo


def _forget_cumsum_kernel(fl_ref, bf_ref, col_ref, row_ref):
    s = fl_ref.shape[1]
    blk = 256
    logf = jax.nn.log_sigmoid(fl_ref[0] + bf_ref[...])
    r = lax.broadcasted_iota(jnp.int32, (blk, blk), 0)
    c = lax.broadcasted_iota(jnp.int32, (blk, blk), 1)
    tri = jnp.where(c <= r, 1.0, 0.0).astype(BF16)
    carry = jnp.zeros((1, LANES), F32)
    parts = []
    for i in range(s // blk):
        hi, mid, lo = _split3(logf[i * blk:(i + 1) * blk])
        cs = (jnp.dot(tri, hi, preferred_element_type=F32)
              + jnp.dot(tri, mid, preferred_element_type=F32)
              + jnp.dot(tri, lo, preferred_element_type=F32)) + carry
        carry = cs[blk - 1:blk, :]
        parts.append(cs)
    cum = jnp.concatenate(parts, axis=0)
    row_ref[0] = cum.T[:8, :]
    for h in range(FOX_HEADS):
        col_ref[0, h] = jnp.broadcast_to(cum[:, h:h + 1], (s, LANES))


def _forget_cumsum_call(f_logit, b_forget_pad):
    b, s, _ = f_logit.shape
    return pl.pallas_call(
        _forget_cumsum_kernel,
        grid=(b,),
        in_specs=[pl.BlockSpec((1, s, LANES), lambda i: (i, 0, 0)),
                  pl.BlockSpec((1, LANES), lambda i: (0, 0))],
        out_specs=[pl.BlockSpec((1, FOX_HEADS, s, LANES), lambda i: (i, 0, 0, 0)),
                   pl.BlockSpec((1, 8, s), lambda i: (i, 0, 0))],
        out_shape=[jax.ShapeDtypeStruct((b, FOX_HEADS, s, LANES), F32),
                   jax.ShapeDtypeStruct((b, 8, s), F32)],
        compiler_params=_params("arbitrary"),
        name="forget_cumsum",
    )(f_logit, b_forget_pad)


ATT_TILE = 512


def _qk(q, k):
    return lax.dot_general(q, k, (((1,), (1,)), ((), ())), preferred_element_type=F32)


def _kv_block(k_ref, v_ref, j):
    start = pl.multiple_of(j * ATT_TILE, ATT_TILE)
    return start, k_ref[pl.ds(start, ATT_TILE), :], v_ref[pl.ds(start, ATT_TILE), :]


def _tile_iotas():
    r = lax.broadcasted_iota(jnp.int32, (ATT_TILE, ATT_TILE), 0)
    c = lax.broadcasted_iota(jnp.int32, (ATT_TILE, ATT_TILE), 1)
    return r, c


def _softmax_state():
    t = ATT_TILE
    return [pltpu.VMEM((t, LANES), F32), pltpu.VMEM((t, LANES), F32), pltpu.VMEM((t, HEAD_DIM), F32)]


def _softmax_reset(m_ref, l_ref, acc_ref):
    m_ref[...] = jnp.full(m_ref.shape, NEG_INF, F32)
    l_ref[...] = jnp.zeros(l_ref.shape, F32)
    acc_ref[...] = jnp.zeros(acc_ref.shape, F32)


def _softmax_update(m_ref, l_ref, acc_ref, s, vb):
    m_old = m_ref[...]
    m_new = jnp.maximum(m_old, jnp.max(s, axis=-1, keepdims=True))
    alpha = jnp.exp(m_old - m_new)
    p = jnp.exp(s - jnp.tile(m_new, (1, s.shape[1] // LANES)))
    l_ref[...] = alpha * l_ref[...] + jnp.sum(p, axis=-1, keepdims=True)
    acc_ref[...] = alpha * acc_ref[...] + jnp.dot(p.astype(BF16), vb, preferred_element_type=F32)
    m_ref[...] = m_new


def _fox_kernel(q_ref, k_ref, v_ref, fc_ref, fr_ref, o_ref, m_ref, l_ref, acc_ref):
    t = ATT_TILE
    qi = pl.program_id(2)
    h = pl.program_id(1)
    q = (q_ref[...].astype(F32) * (HEAD_DIM ** -0.5)).astype(BF16)

    def step(j, masked):
        start, kb, vb = _kv_block(k_ref, v_ref, j)
        fcol = jnp.tile(fc_ref[0, 0], (1, t // LANES))
        frow = fr_ref[0, pl.ds(h, 1), pl.ds(start, t)]
        s = _qk(q, kb) + (fcol - frow)
        if masked:
            r, c = _tile_iotas()
            s = jnp.where(c <= r, s, NEG_INF)
        _softmax_update(m_ref, l_ref, acc_ref, s, vb)

    _softmax_reset(m_ref, l_ref, acc_ref)
    step(qi, masked=True)

    @pl.loop(0, qi)
    def _(j):
        step(j, masked=False)

    o_ref[...] = (acc_ref[...] / l_ref[...]).astype(o_ref.dtype)


def _fox_call(proj, f_col, f_row, batch, seq):
    tq = ATT_TILE
    nq = seq // tq
    return pl.pallas_call(
        _fox_kernel,
        grid=(batch, FOX_HEADS, nq),
        in_specs=[
            pl.BlockSpec((tq, HEAD_DIM), lambda b, h, i: (b * nq + i, COL_FQ + h)),
            pl.BlockSpec((seq, HEAD_DIM), lambda b, h, i: (b, COL_FK + h)),
            pl.BlockSpec((seq, HEAD_DIM), lambda b, h, i: (b, COL_FV + h)),
            pl.BlockSpec((1, 1, tq, LANES), lambda b, h, i: (b, h, i, 0)),
            pl.BlockSpec((1, 8, seq), lambda b, h, i: (b, 0, 0)),
        ],
        out_specs=pl.BlockSpec((tq, HEAD_DIM), lambda b, h, i: (b * nq + i, h)),
        out_shape=jax.ShapeDtypeStruct((batch * seq, FOX_WIDTH), BF16),
        scratch_shapes=_softmax_state(),
        compiler_params=_params("arbitrary", "arbitrary", "arbitrary"),
        name="fox_attention",
    )(proj, proj, proj, f_col, f_row)


SB_SEG = 256


def _sb_kernel(q_ref, k_ref, v_ref, tri_ref, o_ref, run_ref, acc_ref):
    t = ATT_TILE
    qi = pl.program_id(2)
    qn = (q_ref[...].astype(F32) * -(HEAD_DIM ** -0.5)).astype(BF16)

    def step(j, masked):
        _, kb, vb = _kv_block(k_ref, v_ref, j)
        tri2 = tri_ref[...]
        u = _qk(qn, kb)
        lk = jnp.minimum(u, 0.0) - jnp.log(1.0 + jnp.exp(-jnp.abs(u)))
        if masked:
            r, c = _tile_iotas()
            strictly_causal = c < r
            lk = jnp.where(strictly_causal, lk, 0.0)
        hi = lk.astype(BF16)
        lo = (lk - hi.astype(F32)).astype(BF16)
        run = run_ref[...]
        laters = []
        for g in reversed(range(t // SB_SEG)):
            sl = slice(g * SB_SEG, (g + 1) * SB_SEG)
            hi_lo = jnp.concatenate([hi[:, sl], lo[:, sl]], axis=1)
            laters.append(jnp.dot(hi_lo, tri2, preferred_element_type=F32) + jnp.tile(run, (1, SB_SEG // LANES)))
            run = run + jnp.sum(lk[:, sl], axis=-1, keepdims=True)
        later = jnp.concatenate(laters[::-1], axis=1)
        w = jnp.exp(lk - u + later)
        if masked:
            w = jnp.where(strictly_causal, w, 0.0)
        acc_ref[...] += jnp.dot(w.astype(BF16), vb, preferred_element_type=F32)
        run_ref[...] = run

    run_ref[...] = jnp.zeros(run_ref.shape, F32)
    acc_ref[...] = jnp.zeros(acc_ref.shape, F32)
    step(qi, masked=True)

    @pl.loop(0, qi)
    def _(i):
        step(qi - 1 - i, masked=False)

    o_ref[...] = acc_ref[...].astype(o_ref.dtype)


def _sb_call(proj, tri, batch, seq):
    tq = ATT_TILE
    nq = seq // tq
    return pl.pallas_call(
        _sb_kernel,
        grid=(batch, SB_HEADS, nq),
        in_specs=[
            pl.BlockSpec((tq, HEAD_DIM), lambda b, h, i: (b * nq + i, COL_SQ + h)),
            pl.BlockSpec((seq, HEAD_DIM), lambda b, h, i: (b, COL_SK + h)),
            pl.BlockSpec((seq, HEAD_DIM), lambda b, h, i: (b, COL_SV + h)),
            pl.BlockSpec((2 * SB_SEG, SB_SEG), lambda b, h, i: (0, 0)),
        ],
        out_specs=pl.BlockSpec((tq, HEAD_DIM), lambda b, h, i: (b * nq + i, h)),
        out_shape=jax.ShapeDtypeStruct((batch * seq, SB_WIDTH), BF16),
        scratch_shapes=[pltpu.VMEM((tq, LANES), F32), pltpu.VMEM((tq, HEAD_DIM), F32)],
        compiler_params=_params("arbitrary", "arbitrary", "arbitrary"),
        name="sb_attention",
    )(proj, proj, proj, tri)


def _rope_kernel(x_ref, pos_ref, freq_ref, o_ref, *, out_scale):
    half = DIFF_QK_DIM // 2
    ang = pos_ref[...].astype(F32) * freq_ref[...]
    cos = jnp.cos(ang)
    sin = jnp.sin(ang)
    lane = lax.broadcasted_iota(jnp.int32, ang.shape, 1)
    first = (lane % DIFF_QK_DIM) < half
    sin_signed = jnp.where(first, -sin, sin)
    for cb in range(x_ref.shape[1] // LANES):
        x = x_ref[:, cb * LANES:(cb + 1) * LANES].astype(F32)
        partner = jnp.where(first, pltpu.roll(x, LANES - half, 1), pltpu.roll(x, half, 1))
        y = x * cos + partner * sin_signed
        o_ref[:, cb * LANES:(cb + 1) * LANES] = (y * out_scale).astype(o_ref.dtype)


def _rope_call(proj, pos_col, freq_row, col_block, out_scale):
    tokens = proj.shape[0]
    tr = 512
    width = DIFF_HEADS * HEAD_DIM
    return pl.pallas_call(
        functools.partial(_rope_kernel, out_scale=out_scale),
        grid=(tokens // tr,),
        in_specs=[pl.BlockSpec((tr, width), lambda i: (i, col_block * LANES // width)),
                  pl.BlockSpec((tr, 1), lambda i: (i, 0)),
                  pl.BlockSpec((1, LANES), lambda i: (0, 0))],
        out_specs=pl.BlockSpec((tr, width), lambda i: (i, 0)),
        out_shape=jax.ShapeDtypeStruct((tokens, width), BF16),
        compiler_params=_params("arbitrary"),
        name="rope",
    )(proj, pos_col, freq_row)


def _diff_kernel(q_ref, k_ref, v_ref, lq1_ref, lk1_ref, lq2_ref, lk2_ref, g_ref, o_ref,
                 m1_ref, l1_ref, a1_ref, m2_ref, l2_ref, a2_ref, *, lam_init):
    t = ATT_TILE
    qi = pl.program_id(2)
    q = q_ref[...]
    lane = lax.broadcasted_iota(jnp.int32, q.shape, 1)
    zero = jnp.zeros_like(q)
    q1 = jnp.where(lane < DIFF_QK_DIM, q, zero)
    q2 = jnp.where(lane < DIFF_QK_DIM, zero, q)
    state1 = (m1_ref, l1_ref, a1_ref)
    state2 = (m2_ref, l2_ref, a2_ref)

    def step(j, masked):
        _, kb, vb = _kv_block(k_ref, v_ref, j)
        for qc, state in ((q1, state1), (q2, state2)):
            s = _qk(qc, kb)
            if masked:
                r, c = _tile_iotas()
                s = jnp.where((c // CHUNK) <= (r // CHUNK), s, NEG_INF)
            _softmax_update(*state, s, vb)

    _softmax_reset(*state1)
    _softmax_reset(*state2)
    step(qi, masked=True)

    @pl.loop(0, qi)
    def _(j):
        step(j, masked=False)

    lam = (jnp.exp(jnp.sum(lq1_ref[0] * lk1_ref[0], axis=-1, keepdims=True))
           - jnp.exp(jnp.sum(lq2_ref[0] * lk2_ref[0], axis=-1, keepdims=True)) + lam_init)
    o = a1_ref[...] / l1_ref[...] - lam * (a2_ref[...] / l2_ref[...])
    o = o * lax.rsqrt(jnp.mean(o * o, axis=-1, keepdims=True) + RMS_EPS) * g_ref[...]
    o_ref[...] = (o * (1.0 - lam_init)).astype(o_ref.dtype)


def _diff_call(qd, kd, proj, lam_q1, lam_k1, lam_q2, lam_k2, subln_g, lam_init, batch, seq):
    t = ATT_TILE
    nq = seq // t
    lam_spec = pl.BlockSpec((1, 1, DIFF_QK_DIM), lambda b, h, i: (h, 0, 0))
    r3 = lambda a: a.reshape(DIFF_HEADS, 1, DIFF_QK_DIM)
    return pl.pallas_call(
        functools.partial(_diff_kernel, lam_init=lam_init),
        grid=(batch, DIFF_HEADS, nq),
        in_specs=[
            pl.BlockSpec((t, HEAD_DIM), lambda b, h, i: (b * nq + i, h)),
            pl.BlockSpec((seq, HEAD_DIM), lambda b, h, i: (b, h)),
            pl.BlockSpec((seq, HEAD_DIM), lambda b, h, i: (b, COL_DV + h)),
            lam_spec, lam_spec, lam_spec, lam_spec,
            pl.BlockSpec((1, HEAD_DIM), lambda b, h, i: (0, 0)),
        ],
        out_specs=pl.BlockSpec((t, HEAD_DIM), lambda b, h, i: (b * nq + i, h)),
        out_shape=jax.ShapeDtypeStruct((batch * seq, DIFF_WIDTH), BF16),
        scratch_shapes=_softmax_state() + _softmax_state(),
        compiler_params=_params("arbitrary", "arbitrary", "arbitrary"),
        name="diff_attention",
    )(qd, kd, proj, r3(lam_q1), r3(lam_k1), r3(lam_q2), r3(lam_k2), subln_g.reshape(1, HEAD_DIM))


def _merge_kernel(of_ref, os_ref, od_ref, wf_ref, ws_ref, wd_ref, gf_ref, gs_ref, gd_ref, o_ref):
    def branch(o_r, w_r, g_r):
        return jax.nn.sigmoid(g_r[...].astype(F32)) * jnp.dot(o_r[...], w_r[...], preferred_element_type=F32)

    y = branch(of_ref, wf_ref, gf_ref) + branch(os_ref, ws_ref, gs_ref) + branch(od_ref, wd_ref, gd_ref)
    o_ref[...] = y.astype(o_ref.dtype)


def _merge_call(o_fox, o_sb, o_diff, w_fox, w_sb, w_diff, gates, bm=1024, bn=512):
    m = o_fox.shape[0]
    d = w_fox.shape[1]
    gate_blocks = d // bn

    def gate_spec(k):
        return pl.BlockSpec((bm, bn), lambda i, j: (i, k * gate_blocks + j))

    return pl.pallas_call(
        _merge_kernel,
        grid=(m // bm, d // bn),
        in_specs=[
            pl.BlockSpec((bm, FOX_WIDTH), lambda i, j: (i, 0)),
            pl.BlockSpec((bm, SB_WIDTH), lambda i, j: (i, 0)),
            pl.BlockSpec((bm, DIFF_WIDTH), lambda i, j: (i, 0)),
            pl.BlockSpec((FOX_WIDTH, bn), lambda i, j: (0, j)),
            pl.BlockSpec((SB_WIDTH, bn), lambda i, j: (0, j)),
            pl.BlockSpec((DIFF_WIDTH, bn), lambda i, j: (0, j)),
            gate_spec(0), gate_spec(1), gate_spec(2),
        ],
        out_specs=pl.BlockSpec((bm, bn), lambda i, j: (i, j)),
        out_shape=jax.ShapeDtypeStruct((m, d), BF16),
        compiler_params=_params("arbitrary", "arbitrary"),
        name="branch_merge",
    )(o_fox, o_sb, o_diff, w_fox, w_sb, w_diff, gates, gates, gates)


NOT_TOP = 64.0


def _extract_top(cur, count, want_rank=False):
    rows = []
    rank = jnp.full(cur.shape, NOT_TOP, F32) if want_rank else None
    for r in range(count):
        m = jnp.max(cur, axis=0, keepdims=True)
        rows.append(m)
        hit = cur == m
        if want_rank:
            rank = jnp.where(hit, float(r), rank)
        cur = jnp.where(hit, NEG_INF, cur)
    return rows, rank


def _peer_topk_kernel(q_ref, keys_ref, a_ref, l_ref, b_ref, r2_ref):
    k = PEER_TOPK
    for h in range(PEER_HEADS):
        col = 2 * h * PEER_HALF
        s1 = _qk(keys_ref[h, 0], q_ref[:, col:col + PEER_HALF])
        s2 = _qk(keys_ref[h, 1], q_ref[:, col + PEER_HALF:col + 2 * PEER_HALF])
        a_rows, rank1 = _extract_top(s1, k, want_rank=True)
        b_rows, rank2 = _extract_top(s2, k, want_rank=True)
        b_top = jnp.concatenate(b_rows, axis=0)
        counts = [k // (k1 + 1) for k1 in range(k)]
        cands = [a_rows[k1] + b_top[:counts[k1]] for k1 in range(k)]
        n_cand = sum(counts)
        pad = jnp.full((-n_cand % 8, b_top.shape[1]), NEG_INF, F32)
        best, _ = _extract_top(jnp.concatenate(cands + [pad], axis=0), k)
        tau = best[k - 1]
        z = jnp.zeros_like(tau)
        for row in best:
            z = z + jnp.exp(row - best[0])
        l_row = jnp.zeros(s1.shape, F32)
        for k1 in range(k):
            l_k = jnp.sum(jnp.where(cands[k1] >= tau, 1.0, 0.0), axis=0, keepdims=True)
            l_row = jnp.where(rank1 == float(k1), l_k, l_row)
        a_ref[h] = jnp.exp(s1 - a_rows[0]) / z
        l_ref[h] = l_row
        b_ref[h] = jnp.exp(s2 - b_rows[0]).astype(b_ref.dtype)
        r2_ref[h] = rank2.astype(r2_ref.dtype)


def _peer_topk_call(qp, keys):
    tokens = qp.shape[0]
    tq = 512
    spec = pl.BlockSpec((PEER_HEADS, PEER_N_KEYS, tq), lambda i: (0, 0, i))
    shape = lambda dt: jax.ShapeDtypeStruct((PEER_HEADS, PEER_N_KEYS, tokens), dt)
    return pl.pallas_call(
        _peer_topk_kernel,
        grid=(tokens // tq,),
        in_specs=[pl.BlockSpec((tq, qp.shape[1]), lambda i: (i, 0)),
                  pl.BlockSpec(keys.shape, lambda i: (0, 0, 0, 0))],
        out_specs=[spec, spec, spec, spec],
        out_shape=[shape(F32), shape(F32), shape(BF16), shape(BF16)],
        compiler_params=_params("arbitrary"),
        name="peer_topk",
    )(qp, keys)


PEER_TM = 1024
PEER_ROWS = 8
PEER_TOKEN_CHUNK = 256
PEER_DOWN_TOKENS = 512
PEER_DOWN_EXPERTS = 1024


def _peer_act_kernel(ht_ref, u_ref, a_ref, l_ref, b_ref, r2_ref, o_ref, hid_ref):
    e = pl.program_id(1)
    tc = PEER_TOKEN_CHUNK
    n_chunks = o_ref.shape[1] // tc

    def up(c):
        hid_ref[c % 2] = jnp.dot(u_ref[...], ht_ref[0, :, c * tc:(c + 1) * tc], preferred_element_type=F32)

    def activate(c):
        cols = slice(c * tc, (c + 1) * tc)
        for r in range(PEER_ROWS):
            rows = slice(r * PEER_N_KEYS, (r + 1) * PEER_N_KEYS)
            hid = hid_ref[c % 2, rows, :]
            i = e * PEER_ROWS + r

            def key_row(ref, h):
                tile = jnp.broadcast_to(ref[h, pl.ds(i, 1), cols], (BF16_ROWS, tc)).astype(BF16)
                return jnp.tile(tile, (PEER_N_KEYS // BF16_ROWS, 1))

            gate = jnp.zeros(hid.shape, BF16)
            for h in range(PEER_HEADS):
                chosen = r2_ref[h, :, cols] < key_row(l_ref, h)
                gate = gate + jnp.where(chosen, b_ref[h, :, cols], jnp.zeros((), BF16)) * key_row(a_ref, h)
            gelu = 0.5 * hid * (1.0 + lax.erf(hid * (2.0 ** -0.5)))
            o_ref[rows, cols] = gate * gelu.astype(BF16)

    up(0)
    for c in range(n_chunks):
        if c + 1 < n_chunks:
            up(c + 1)
        activate(c)


def _peer_down_kernel(vt_ref, act_ref, x_ref, gt_ref, o_ref, acc_ref):
    k = pl.program_id(1)

    @pl.when(k == 0)
    def _():
        acc_ref[...] = jnp.zeros_like(acc_ref)

    acc_ref[...] += jnp.dot(vt_ref[...], act_ref[...], preferred_element_type=F32)

    @pl.when(k == pl.num_programs(1) - 1)
    def _():
        o_ref[0] = x_ref[0] + gt_ref[0] * acc_ref[...].T


def _peer_ffn_call(ht, u, vt, a_t, l_t, b_t, r2_t, x, gt):
    b, d, s = ht.shape
    n_experts = u.shape[0]
    tm = PEER_TM
    te = PEER_ROWS * PEER_N_KEYS
    per_b = s // tm
    key_spec = pl.BlockSpec((PEER_HEADS, PEER_N_KEYS, tm), lambda i, e: (0, 0, i))
    act_t = pl.pallas_call(
        _peer_act_kernel,
        grid=(b * per_b, n_experts // te),
        in_specs=[
            pl.BlockSpec((1, d, tm), lambda i, e: (i // per_b, 0, i % per_b)),
            pl.BlockSpec((te, d), lambda i, e: (e, 0)),
            key_spec, key_spec, key_spec, key_spec,
        ],
        out_specs=pl.BlockSpec((te, tm), lambda i, e: (e, i)),
        out_shape=jax.ShapeDtypeStruct((n_experts, b * s), BF16),
        scratch_shapes=[pltpu.VMEM((2, te, PEER_TOKEN_CHUNK), F32)],
        compiler_params=_params("arbitrary", "arbitrary"),
        name="peer_act",
    )(ht, u, a_t, l_t, b_t, r2_t)

    tn, tk = PEER_DOWN_TOKENS, PEER_DOWN_EXPERTS
    per_b = s // tn
    return pl.pallas_call(
        _peer_down_kernel,
        grid=(b * per_b, n_experts // tk),
        in_specs=[
            pl.BlockSpec((d, tk), lambda i, k: (0, k)),
            pl.BlockSpec((tk, tn), lambda i, k: (k, i)),
            pl.BlockSpec((1, tn, d), lambda i, k: (i // per_b, i % per_b, 0)),
            pl.BlockSpec((1, 1, d), lambda i, k: (i // per_b, 0, 0)),
        ],
        out_specs=pl.BlockSpec((1, tn, d), lambda i, k: (i // per_b, i % per_b, 0)),
        out_shape=jax.ShapeDtypeStruct(x.shape, F32),
        scratch_shapes=[pltpu.VMEM((d, tn), F32)],
        compiler_params=_params("arbitrary", "arbitrary"),
        name="peer_down",
    )(vt, act_t, x, gt)


def _peer_tables_kernel(u_ref, v_ref, ub_ref, vt_ref):
    ub_ref[...] = u_ref[0].astype(BF16)
    vt_ref[...] = v_ref[0].T.astype(BF16)


def _peer_tables_call(u, v, layer):
    _, e, d = u.shape
    te = 512
    return pl.pallas_call(
        _peer_tables_kernel,
        grid=(e // te,),
        in_specs=[pl.BlockSpec((1, te, d), lambda i: (layer, i, 0)),
                  pl.BlockSpec((1, te, d), lambda i: (layer, i, 0))],
        out_specs=[pl.BlockSpec((te, d), lambda i: (i, 0)), pl.BlockSpec((d, te), lambda i: (0, i))],
        out_shape=[jax.ShapeDtypeStruct((e, d), BF16), jax.ShapeDtypeStruct((d, e), BF16)],
        compiler_params=_params("arbitrary"),
        name="peer_tables",
    )(u, v)


def kernel(x, c, positions, ada_w, ada_b, norm1_g, norm2_g, w_in, b_forget, lam_q1, lam_k1, lam_q2, lam_k2,
           diff_subln_g, w_br_fox, w_br_sb, w_br_diff, w_o, peer_wq, peer_sub_keys, peer_u, peer_v,
           final_norm_g):
    batch, seq, d = x.shape
    tokens = batch * seq
    assert batch <= 8 and seq % 1024 == 0 and d == D_MODEL

    c_pad = jnp.zeros((8, d), F32).at[:batch].set(c)
    mod = _mod_call(c_pad, ada_w, ada_b)[:, :batch].reshape(DEPTH, batch, N_MOD, 1, d)
    pos_col = positions.reshape(tokens, 1)
    inv_freq = ROPE_THETA ** (-jnp.arange(0, DIFF_QK_DIM, 2, dtype=F32) / DIFF_QK_DIM)
    freq_row = jnp.tile(inv_freq, LANES // inv_freq.shape[0]).reshape(1, LANES)
    ri = np.arange(SB_SEG)
    tri_after = jnp.asarray(np.tile(ri[:, None] > ri[None, :], (2, 1)), BF16)

    for l in range(DEPTH):
        lam_init = 0.8 - 0.6 * math.exp(-0.3 * l)
        sh1, sc1, gt1, sh2, sc2, gt2 = (mod[l, :, k] for k in range(N_MOD))

        h1 = _norm_mod_call(x, norm1_g[l].reshape(1, d), sc1, sh1, BF16)[0].reshape(tokens, d)
        proj = _proj_call(h1, w_in, l, 0, QKV_WIDTH, BF16, bn=1024)
        gates = _proj_call(h1, w_in, l, GATE_OFFSET, N_BRANCHES * d, BF16, bn=1024)
        f_logit = _proj_call(h1, w_in, l, QKV_WIDTH, LANES, F32, bn=LANES, keep=FOX_HEADS)
        f_logit = f_logit.reshape(batch, seq, LANES)
        b_forget_pad = jnp.pad(b_forget[l], (0, LANES - FOX_HEADS)).reshape(1, LANES)
        f_col, f_row = _forget_cumsum_call(f_logit, b_forget_pad)

        o_fox = _fox_call(proj, f_col, f_row, batch, seq)
        o_sb = _sb_call(proj, tri_after, batch, seq)
        qd = _rope_call(proj, pos_col, freq_row, COL_DQ, DIFF_QK_DIM ** -0.5)
        kd = _rope_call(proj, pos_col, freq_row, COL_DK, 1.0)
        o_diff = _diff_call(qd, kd, proj, lam_q1[l], lam_k1[l], lam_q2[l], lam_k2[l], diff_subln_g[l],
                            lam_init, batch, seq)

        y = _merge_call(o_fox, o_sb, o_diff, w_br_fox[l].astype(BF16), w_br_sb[l].astype(BF16),
                        w_br_diff[l].astype(BF16), gates)
        x = _matmul_residual_call(y.reshape(batch, seq, d), w_o[l].astype(BF16), x, gt1)

        h2, h2t = _norm_mod_call(x, norm2_g[l].reshape(1, d), sc2, sh2, BF16, transposed=True)
        qp = _proj_call(h2.reshape(tokens, d), peer_wq, l, 0, peer_wq.shape[2], BF16, bn=1024)
        a_t, l_t, b_t, r2_t = _peer_topk_call(qp, peer_sub_keys[l].astype(BF16))
        u_bf, vt_bf = _peer_tables_call(peer_u, peer_v, l)
        x = _peer_ffn_call(h2t, u_bf, vt_bf, a_t, l_t, b_t, r2_t, x, gt2)

    zeros = jnp.zeros((batch, 1, d), F32)
    return _norm_mod_call(x, final_norm_g.reshape(1, d), zeros, zeros, F32)[0]
```

```python
import functools
import math

import jax
import jax.numpy as jnp
import numpy as np
from jax import lax
from jax.experimental import pallas as pl
from jax.experimental.pallas import tpu as pltpu

F32 = jnp.float32
BF16 = jnp.bfloat16

D_MODEL = 2048
DEPTH = 2
CHUNK = 64
HEAD_DIM = 128
FOX_HEADS = 6
SB_HEADS = 6
DIFF_HEADS = 4
DIFF_QK_DIM = HEAD_DIM // 2
FOX_WIDTH = FOX_HEADS * HEAD_DIM
SB_WIDTH = SB_HEADS * HEAD_DIM
DIFF_WIDTH = DIFF_HEADS * HEAD_DIM
QKV_WIDTH = 3 * FOX_WIDTH + 3 * SB_WIDTH + 3 * DIFF_WIDTH
GATE_OFFSET = QKV_WIDTH + FOX_HEADS
ROPE_THETA = 10000.0
RMS_EPS = 1e-6
N_MOD = 6
N_BRANCHES = 3
PEER_HEADS = 8
PEER_N_KEYS = 128
PEER_TOPK = 16
PEER_HALF = 128

LANES = 128
BF16_ROWS = 16
VMEM_LIMIT = 56 * 1024 * 1024
NEG_INF = float("-inf")

COL_FQ, COL_FK, COL_FV = 0, FOX_HEADS, 2 * FOX_HEADS
COL_SQ, COL_SK, COL_SV = 18, 24, 30
COL_DQ, COL_DK, COL_DV = 36, 40, 44
COL_GATE = QKV_WIDTH // LANES


def _params(*sem):
    return pltpu.CompilerParams(dimension_semantics=sem, vmem_limit_bytes=VMEM_LIMIT)


def _mod_kernel(c_ref, w_ref, b_ref, o_ref):
    c = c_ref[...]
    cond = c * jax.nn.sigmoid(c)
    hi = cond.astype(BF16)
    lo = (cond - hi.astype(F32)).astype(BF16)
    w = w_ref[0].astype(BF16)
    acc = jnp.dot(hi, w, preferred_element_type=F32) + jnp.dot(lo, w, preferred_element_type=F32)
    o_ref[0] = acc + b_ref[0]


def _mod_call(c_pad, ada_w, ada_b):
    depth, d, n = ada_w.shape
    tn = 1024
    return pl.pallas_call(
        _mod_kernel,
        grid=(depth, n // tn),
        in_specs=[
            pl.BlockSpec((8, d), lambda l, j: (0, 0)),
            pl.BlockSpec((1, d, tn), lambda l, j: (l, 0, j)),
            pl.BlockSpec((1, 1, tn), lambda l, j: (l, 0, j)),
        ],
        out_specs=pl.BlockSpec((1, 8, tn), lambda l, j: (l, 0, j)),
        out_shape=jax.ShapeDtypeStruct((depth, 8, n), F32),
        compiler_params=_params("arbitrary", "arbitrary"),
        name="adaln_mod",
    )(c_pad, ada_w, ada_b.reshape(depth, 1, n))


def _norm_mod_kernel(x_ref, g_ref, sc_ref, sh_ref, *o_refs, transposed):
    x = x_ref[0]
    y = x * lax.rsqrt(jnp.mean(x * x, axis=-1, keepdims=True) + RMS_EPS) * g_ref[...]
    h = y * (1.0 + sc_ref[0]) + sh_ref[0]
    o_refs[0][0] = h.astype(o_refs[0].dtype)
    if transposed:
        o_refs[1][0] = h.T.astype(o_refs[1].dtype)


def _norm_mod_call(x, g, sc, sh, out_dtype, transposed=False):
    b, s, d = x.shape
    tr = 512
    out_shape = [jax.ShapeDtypeStruct((b, s, d), out_dtype)]
    out_specs = [pl.BlockSpec((1, tr, d), lambda i, j: (i, j, 0))]
    if transposed:
        out_shape.append(jax.ShapeDtypeStruct((b, d, s), out_dtype))
        out_specs.append(pl.BlockSpec((1, d, tr), lambda i, j: (i, 0, j)))
    return pl.pallas_call(
        functools.partial(_norm_mod_kernel, transposed=transposed),
        grid=(b, s // tr),
        in_specs=[
            pl.BlockSpec((1, tr, d), lambda i, j: (i, j, 0)),
            pl.BlockSpec((1, d), lambda i, j: (0, 0)),
            pl.BlockSpec((1, 1, d), lambda i, j: (i, 0, 0)),
            pl.BlockSpec((1, 1, d), lambda i, j: (i, 0, 0)),
        ],
        out_specs=out_specs,
        out_shape=out_shape,
        compiler_params=_params("arbitrary", "arbitrary"),
        name="norm_mod",
    )(x, g, sc, sh)


def _matmul_kernel(a_ref, w_ref, o_ref):
    o_ref[...] = jnp.dot(a_ref[...], w_ref[...], preferred_element_type=F32).astype(o_ref.dtype)


def _matmul_call(a, w, out_dtype, bm=1024, bn=1024):
    m, k = a.shape
    n = w.shape[1]
    bn = min(bn, n)
    return pl.pallas_call(
        _matmul_kernel,
        grid=(m // bm, n // bn),
        in_specs=[pl.BlockSpec((bm, k), lambda i, j: (i, 0)),
                  pl.BlockSpec((k, bn), lambda i, j: (0, j))],
        out_specs=pl.BlockSpec((bm, bn), lambda i, j: (i, j)),
        out_shape=jax.ShapeDtypeStruct((m, n), out_dtype),
        compiler_params=_params("arbitrary", "arbitrary"),
        name="matmul",
    )(a, w)


def _proj_kernel(a_ref, w_ref, *rest, shift, keep):
    o_ref, wb_ref = rest[-2:]

    @pl.when(pl.program_id(1) == 0)
    def _():
        w = w_ref[0]
        if shift:
            w = jnp.concatenate([w[:, shift:], rest[0][0][:, :shift]], axis=1)
        if keep is not None:
            lane = lax.broadcasted_iota(jnp.int32, w.shape, 1)
            w = jnp.where(lane < keep, w, 0.0)
        wb_ref[...] = w.astype(BF16)

    o_ref[...] = jnp.dot(a_ref[...], wb_ref[...], preferred_element_type=F32).astype(o_ref.dtype)


def _proj_call(a, w, layer, col0, n, out_dtype, bn, keep=None, bm=1024):
    m, k = a.shape
    shift = col0 % LANES
    base = (col0 - shift) // bn
    assert (col0 - shift) % bn == 0 and n % bn == 0 and bn % LANES == 0
    next_stride = bn // LANES
    in_specs = [pl.BlockSpec((bm, k), lambda j, i: (i, 0)),
                pl.BlockSpec((1, k, bn), lambda j, i: (layer, 0, base + j))]
    operands = [a, w]
    if shift:
        in_specs.append(pl.BlockSpec((1, k, LANES), lambda j, i: (layer, 0, (base + j + 1) * next_stride)))
        operands.append(w)
    return pl.pallas_call(
        functools.partial(_proj_kernel, shift=shift, keep=keep),
        grid=(n // bn, m // bm),
        in_specs=in_specs,
        out_specs=pl.BlockSpec((bm, bn), lambda j, i: (i, j)),
        out_shape=jax.ShapeDtypeStruct((m, n), out_dtype),
        scratch_shapes=[pltpu.VMEM((k, bn), BF16)],
        compiler_params=_params("arbitrary", "arbitrary"),
        name="proj",
    )(*operands)


def _matmul_residual_kernel(a_ref, w_ref, x_ref, gt_ref, o_ref):
    acc = jnp.dot(a_ref[0], w_ref[...], preferred_element_type=F32)
    o_ref[0] = x_ref[0] + gt_ref[0] * acc


def _matmul_residual_call(a, w, x, gt, bm=1024, bn=1024):
    b, s, k = a.shape
    n = w.shape[1]
    return pl.pallas_call(
        _matmul_residual_kernel,
        grid=(b, s // bm, n // bn),
        in_specs=[pl.BlockSpec((1, bm, k), lambda bi, i, j: (bi, i, 0)),
                  pl.BlockSpec((k, bn), lambda bi, i, j: (0, j)),
                  pl.BlockSpec((1, bm, bn), lambda bi, i, j: (bi, i, j)),
                  pl.BlockSpec((1, 1, bn), lambda bi, i, j: (bi, 0, j))],
        out_specs=pl.BlockSpec((1, bm, bn), lambda bi, i, j: (bi, i, j)),
        out_shape=jax.ShapeDtypeStruct((b, s, n), F32),
        compiler_params=_params("arbitrary", "arbitrary", "arbitrary"),
        name="matmul_residual",
    )(a, w, x, gt)


def _split3(x):
    hi = x.astype(BF16)
    r = x - hi.astype(F32)
    mid = r.astype(BF16)
    lo = (r - mid.astype(F32)).astype(BF16)
    return hi, mid, lo


def _forget_cumsum_kernel(fl_ref, bf_ref, col_ref, row_ref):
    s = fl_ref.shape[1]
    blk = 256
    logf = jax.nn.log_sigmoid(fl_ref[0] + bf_ref[...])
    r = lax.broadcasted_iota(jnp.int32, (blk, blk), 0)
    c = lax.broadcasted_iota(jnp.int32, (blk, blk), 1)
    tri = jnp.where(c <= r, 1.0, 0.0).astype(BF16)
    carry = jnp.zeros((1, LANES), F32)
    parts = []
    for i in range(s // blk):
        hi, mid, lo = _split3(logf[i * blk:(i + 1) * blk])
        cs = (jnp.dot(tri, hi, preferred_element_type=F32)
              + jnp.dot(tri, mid, preferred_element_type=F32)
              + jnp.dot(tri, lo, preferred_element_type=F32)) + carry
        carry = cs[blk - 1:blk, :]
        parts.append(cs)
    cum = jnp.concatenate(parts, axis=0)
    row_ref[0] = cum.T[:8, :]
    for h in range(FOX_HEADS):
        col_ref[0, h] = jnp.broadcast_to(cum[:, h:h + 1], (s, LANES))


def _forget_cumsum_call(f_logit, b_forget_pad):
    b, s, _ = f_logit.shape
    return pl.pallas_call(
        _forget_cumsum_kernel,
        grid=(b,),
        in_specs=[pl.BlockSpec((1, s, LANES), lambda i: (i, 0, 0)),
                  pl.BlockSpec((1, LANES), lambda i: (0, 0))],
        out_specs=[pl.BlockSpec((1, FOX_HEADS, s, LANES), lambda i: (i, 0, 0, 0)),
                   pl.BlockSpec((1, 8, s), lambda i: (i, 0, 0))],
        out_shape=[jax.ShapeDtypeStruct((b, FOX_HEADS, s, LANES), F32),
                   jax.ShapeDtypeStruct((b, 8, s), F32)],
        compiler_params=_params("arbitrary"),
        name="forget_cumsum",
    )(f_logit, b_forget_pad)


ATT_TILE = 512


def _qk(q, k):
    return lax.dot_general(q, k, (((1,), (1,)), ((), ())), preferred_element_type=F32)


def _kv_block(k_ref, v_ref, j):
    start = pl.multiple_of(j * ATT_TILE, ATT_TILE)
    return start, k_ref[pl.ds(start, ATT_TILE), :], v_ref[pl.ds(start, ATT_TILE), :]


def _tile_iotas():
    r = lax.broadcasted_iota(jnp.int32, (ATT_TILE, ATT_TILE), 0)
    c = lax.broadcasted_iota(jnp.int32, (ATT_TILE, ATT_TILE), 1)
    return r, c


def _softmax_state():
    t = ATT_TILE
    return [pltpu.VMEM((t, LANES), F32), pltpu.VMEM((t, LANES), F32), pltpu.VMEM((t, HEAD_DIM), F32)]


def _softmax_reset(m_ref, l_ref, acc_ref):
    m_ref[...] = jnp.full(m_ref.shape, NEG_INF, F32)
    l_ref[...] = jnp.zeros(l_ref.shape, F32)
    acc_ref[...] = jnp.zeros(acc_ref.shape, F32)


def _softmax_update(m_ref, l_ref, acc_ref, s, vb):
    m_old = m_ref[...]
    m_new = jnp.maximum(m_old, jnp.max(s, axis=-1, keepdims=True))
    alpha = jnp.exp(m_old - m_new)
    p = jnp.exp(s - jnp.tile(m_new, (1, s.shape[1] // LANES)))
    l_ref[...] = alpha * l_ref[...] + jnp.sum(p, axis=-1, keepdims=True)
    acc_ref[...] = alpha * acc_ref[...] + jnp.dot(p.astype(BF16), vb, preferred_element_type=F32)
    m_ref[...] = m_new


def _fox_kernel(q_ref, k_ref, v_ref, fc_ref, fr_ref, o_ref, m_ref, l_ref, acc_ref):
    t = ATT_TILE
    qi = pl.program_id(2)
    h = pl.program_id(1)
    q = (q_ref[...].astype(F32) * (HEAD_DIM ** -0.5)).astype(BF16)

    def step(j, masked):
        start, kb, vb = _kv_block(k_ref, v_ref, j)
        fcol = jnp.tile(fc_ref[0, 0], (1, t // LANES))
        frow = fr_ref[0, pl.ds(h, 1), pl.ds(start, t)]
        s = _qk(q, kb) + (fcol - frow)
        if masked:
            r, c = _tile_iotas()
            s = jnp.where(c <= r, s, NEG_INF)
        _softmax_update(m_ref, l_ref, acc_ref, s, vb)

    _softmax_reset(m_ref, l_ref, acc_ref)
    step(qi, masked=True)

    @pl.loop(0, qi)
    def _(j):
        step(j, masked=False)

    o_ref[...] = (acc_ref[...] / l_ref[...]).astype(o_ref.dtype)


def _fox_call(proj, f_col, f_row, batch, seq):
    tq = ATT_TILE
    nq = seq // tq
    return pl.pallas_call(
        _fox_kernel,
        grid=(batch, FOX_HEADS, nq),
        in_specs=[
            pl.BlockSpec((tq, HEAD_DIM), lambda b, h, i: (b * nq + i, COL_FQ + h)),
            pl.BlockSpec((seq, HEAD_DIM), lambda b, h, i: (b, COL_FK + h)),
            pl.BlockSpec((seq, HEAD_DIM), lambda b, h, i: (b, COL_FV + h)),
            pl.BlockSpec((1, 1, tq, LANES), lambda b, h, i: (b, h, i, 0)),
            pl.BlockSpec((1, 8, seq), lambda b, h, i: (b, 0, 0)),
        ],
        out_specs=pl.BlockSpec((tq, HEAD_DIM), lambda b, h, i: (b * nq + i, h)),
        out_shape=jax.ShapeDtypeStruct((batch * seq, FOX_WIDTH), BF16),
        scratch_shapes=_softmax_state(),
        compiler_params=_params("arbitrary", "arbitrary", "arbitrary"),
        name="fox_attention",
    )(proj, proj, proj, f_col, f_row)


SB_SEG = 256


def _sb_kernel(q_ref, k_ref, v_ref, tri_ref, o_ref, run_ref, acc_ref):
    t = ATT_TILE
    qi = pl.program_id(2)
    qn = (q_ref[...].astype(F32) * -(HEAD_DIM ** -0.5)).astype(BF16)

    def step(j, masked):
        _, kb, vb = _kv_block(k_ref, v_ref, j)
        tri2 = tri_ref[...]
        u = _qk(qn, kb)
        lk = jnp.minimum(u, 0.0) - jnp.log(1.0 + jnp.exp(-jnp.abs(u)))
        if masked:
            r, c = _tile_iotas()
            strictly_causal = c < r
            lk = jnp.where(strictly_causal, lk, 0.0)
        hi = lk.astype(BF16)
        lo = (lk - hi.astype(F32)).astype(BF16)
        run = run_ref[...]
        laters = []
        for g in reversed(range(t // SB_SEG)):
            sl = slice(g * SB_SEG, (g + 1) * SB_SEG)
            hi_lo = jnp.concatenate([hi[:, sl], lo[:, sl]], axis=1)
            laters.append(jnp.dot(hi_lo, tri2, preferred_element_type=F32) + jnp.tile(run, (1, SB_SEG // LANES)))
            run = run + jnp.sum(lk[:, sl], axis=-1, keepdims=True)
        later = jnp.concatenate(laters[::-1], axis=1)
        w = jnp.exp(lk - u + later)
        if masked:
            w = jnp.where(strictly_causal, w, 0.0)
        acc_ref[...] += jnp.dot(w.astype(BF16), vb, preferred_element_type=F32)
        run_ref[...] = run

    run_ref[...] = jnp.zeros(run_ref.shape, F32)
    acc_ref[...] = jnp.zeros(acc_ref.shape, F32)
    step(qi, masked=True)

    @pl.loop(0, qi)
    def _(i):
        step(qi - 1 - i, masked=False)

    o_ref[...] = acc_ref[...].astype(o_ref.dtype)


def _sb_call(proj, tri, batch, seq):
    tq = ATT_TILE
    nq = seq // tq
    return pl.pallas_call(
        _sb_kernel,
        grid=(batch, SB_HEADS, nq),
        in_specs=[
            pl.BlockSpec((tq, HEAD_DIM), lambda b, h, i: (b * nq + i, COL_SQ + h)),
            pl.BlockSpec((seq, HEAD_DIM), lambda b, h, i: (b, COL_SK + h)),
            pl.BlockSpec((seq, HEAD_DIM), lambda b, h, i: (b, COL_SV + h)),
            pl.BlockSpec((2 * SB_SEG, SB_SEG), lambda b, h, i: (0, 0)),
        ],
        out_specs=pl.BlockSpec((tq, HEAD_DIM), lambda b, h, i: (b * nq + i, h)),
        out_shape=jax.ShapeDtypeStruct((batch * seq, SB_WIDTH), BF16),
        scratch_shapes=[pltpu.VMEM((tq, LANES), F32), pltpu.VMEM((tq, HEAD_DIM), F32)],
        compiler_params=_params("arbitrary", "arbitrary", "arbitrary"),
        name="sb_attention",
    )(proj, proj, proj, tri)


def _rope_kernel(x_ref, pos_ref, freq_ref, o_ref, *, out_scale):
    half = DIFF_QK_DIM // 2
    ang = pos_ref[...].astype(F32) * freq_ref[...]
    cos = jnp.cos(ang)
    sin = jnp.sin(ang)
    lane = lax.broadcasted_iota(jnp.int32, ang.shape, 1)
    first = (lane % DIFF_QK_DIM) < half
    sin_signed = jnp.where(first, -sin, sin)
    for cb in range(x_ref.shape[1] // LANES):
        x = x_ref[:, cb * LANES:(cb + 1) * LANES].astype(F32)
        partner = jnp.where(first, pltpu.roll(x, LANES - half, 1), pltpu.roll(x, half, 1))
        y = x * cos + partner * sin_signed
        o_ref[:, cb * LANES:(cb + 1) * LANES] = (y * out_scale).astype(o_ref.dtype)


def _rope_call(proj, pos_col, freq_row, col_block, out_scale):
    tokens = proj.shape[0]
    tr = 512
    width = DIFF_HEADS * HEAD_DIM
    return pl.pallas_call(
        functools.partial(_rope_kernel, out_scale=out_scale),
        grid=(tokens // tr,),
        in_specs=[pl.BlockSpec((tr, width), lambda i: (i, col_block * LANES // width)),
                  pl.BlockSpec((tr, 1), lambda i: (i, 0)),
                  pl.BlockSpec((1, LANES), lambda i: (0, 0))],
        out_specs=pl.BlockSpec((tr, width), lambda i: (i, 0)),
        out_shape=jax.ShapeDtypeStruct((tokens, width), BF16),
        compiler_params=_params("arbitrary"),
        name="rope",
    )(proj, pos_col, freq_row)


def _diff_kernel(q_ref, k_ref, v_ref, lq1_ref, lk1_ref, lq2_ref, lk2_ref, g_ref, o_ref,
                 m1_ref, l1_ref, a1_ref, m2_ref, l2_ref, a2_ref, *, lam_init):
    t = ATT_TILE
    qi = pl.program_id(2)
    q = q_ref[...]
    lane = lax.broadcasted_iota(jnp.int32, q.shape, 1)
    zero = jnp.zeros_like(q)
    q1 = jnp.where(lane < DIFF_QK_DIM, q, zero)
    q2 = jnp.where(lane < DIFF_QK_DIM, zero, q)
    state1 = (m1_ref, l1_ref, a1_ref)
    state2 = (m2_ref, l2_ref, a2_ref)

    def step(j, masked):
        _, kb, vb = _kv_block(k_ref, v_ref, j)
        for qc, state in ((q1, state1), (q2, state2)):
            s = _qk(qc, kb)
            if masked:
                r, c = _tile_iotas()
                s = jnp.where((c // CHUNK) <= (r // CHUNK), s, NEG_INF)
            _softmax_update(*state, s, vb)

    _softmax_reset(*state1)
    _softmax_reset(*state2)
    step(qi, masked=True)

    @pl.loop(0, qi)
    def _(j):
        step(j, masked=False)

    lam = (jnp.exp(jnp.sum(lq1_ref[0] * lk1_ref[0], axis=-1, keepdims=True))
           - jnp.exp(jnp.sum(lq2_ref[0] * lk2_ref[0], axis=-1, keepdims=True)) + lam_init)
    o = a1_ref[...] / l1_ref[...] - lam * (a2_ref[...] / l2_ref[...])
    o = o * lax.rsqrt(jnp.mean(o * o, axis=-1, keepdims=True) + RMS_EPS) * g_ref[...]
    o_ref[...] = (o * (1.0 - lam_init)).astype(o_ref.dtype)


def _diff_call(qd, kd, proj, lam_q1, lam_k1, lam_q2, lam_k2, subln_g, lam_init, batch, seq):
    t = ATT_TILE
    nq = seq // t
    lam_spec = pl.BlockSpec((1, 1, DIFF_QK_DIM), lambda b, h, i: (h, 0, 0))
    r3 = lambda a: a.reshape(DIFF_HEADS, 1, DIFF_QK_DIM)
    return pl.pallas_call(
        functools.partial(_diff_kernel, lam_init=lam_init),
        grid=(batch, DIFF_HEADS, nq),
        in_specs=[
            pl.BlockSpec((t, HEAD_DIM), lambda b, h, i: (b * nq + i, h)),
            pl.BlockSpec((seq, HEAD_DIM), lambda b, h, i: (b, h)),
            pl.BlockSpec((seq, HEAD_DIM), lambda b, h, i: (b, COL_DV + h)),
            lam_spec, lam_spec, lam_spec, lam_spec,
            pl.BlockSpec((1, HEAD_DIM), lambda b, h, i: (0, 0)),
        ],
        out_specs=pl.BlockSpec((t, HEAD_DIM), lambda b, h, i: (b * nq + i, h)),
        out_shape=jax.ShapeDtypeStruct((batch * seq, DIFF_WIDTH), BF16),
        scratch_shapes=_softmax_state() + _softmax_state(),
        compiler_params=_params("arbitrary", "arbitrary", "arbitrary"),
        name="diff_attention",
    )(qd, kd, proj, r3(lam_q1), r3(lam_k1), r3(lam_q2), r3(lam_k2), subln_g.reshape(1, HEAD_DIM))


def _merge_kernel(of_ref, os_ref, od_ref, wf_ref, ws_ref, wd_ref, gf_ref, gs_ref, gd_ref, o_ref):
    def branch(o_r, w_r, g_r):
        return jax.nn.sigmoid(g_r[...].astype(F32)) * jnp.dot(o_r[...], w_r[...], preferred_element_type=F32)

    y = branch(of_ref, wf_ref, gf_ref) + branch(os_ref, ws_ref, gs_ref) + branch(od_ref, wd_ref, gd_ref)
    o_ref[...] = y.astype(o_ref.dtype)


def _merge_call(o_fox, o_sb, o_diff, w_fox, w_sb, w_diff, gates, bm=1024, bn=1024):
    m = o_fox.shape[0]
    d = w_fox.shape[1]
    gate_blocks = d // bn

    def gate_spec(k):
        return pl.BlockSpec((bm, bn), lambda i, j: (i, k * gate_blocks + j))

    return pl.pallas_call(
        _merge_kernel,
        grid=(m // bm, d // bn),
        in_specs=[
            pl.BlockSpec((bm, FOX_WIDTH), lambda i, j: (i, 0)),
            pl.BlockSpec((bm, SB_WIDTH), lambda i, j: (i, 0)),
            pl.BlockSpec((bm, DIFF_WIDTH), lambda i, j: (i, 0)),
            pl.BlockSpec((FOX_WIDTH, bn), lambda i, j: (0, j)),
            pl.BlockSpec((SB_WIDTH, bn), lambda i, j: (0, j)),
            pl.BlockSpec((DIFF_WIDTH, bn), lambda i, j: (0, j)),
            gate_spec(0), gate_spec(1), gate_spec(2),
        ],
        out_specs=pl.BlockSpec((bm, bn), lambda i, j: (i, j)),
        out_shape=jax.ShapeDtypeStruct((m, d), BF16),
        compiler_params=_params("arbitrary", "arbitrary"),
        name="branch_merge",
    )(o_fox, o_sb, o_diff, w_fox, w_sb, w_diff, gates, gates, gates)


NOT_TOP = 64.0


def _extract_top(cur, count, want_rank=False):
    rows = []
    rank = jnp.full(cur.shape, NOT_TOP, F32) if want_rank else None
    for r in range(count):
        m = jnp.max(cur, axis=0, keepdims=True)
        rows.append(m)
        hit = cur == m
        if want_rank:
            rank = jnp.where(hit, float(r), rank)
        cur = jnp.where(hit, NEG_INF, cur)
    return rows, rank


def _peer_topk_kernel(q_ref, keys_ref, a_ref, l_ref, b_ref, r2_ref):
    k = PEER_TOPK
    for h in range(PEER_HEADS):
        col = 2 * h * PEER_HALF
        s1 = _qk(keys_ref[h, 0], q_ref[:, col:col + PEER_HALF])
        s2 = _qk(keys_ref[h, 1], q_ref[:, col + PEER_HALF:col + 2 * PEER_HALF])
        a_rows, rank1 = _extract_top(s1, k, want_rank=True)
        b_rows, rank2 = _extract_top(s2, k, want_rank=True)
        b_top = jnp.concatenate(b_rows, axis=0)
        counts = [k // (k1 + 1) for k1 in range(k)]
        cands = [a_rows[k1] + b_top[:counts[k1]] for k1 in range(k)]
        n_cand = sum(counts)
        pad = jnp.full((-n_cand % 8, b_top.shape[1]), NEG_INF, F32)
        best, _ = _extract_top(jnp.concatenate(cands + [pad], axis=0), k)
        tau = best[k - 1]
        z = jnp.zeros_like(tau)
        for row in best:
            z = z + jnp.exp(row - best[0])
        l_row = jnp.zeros(s1.shape, F32)
        for k1 in range(k):
            l_k = jnp.sum(jnp.where(cands[k1] >= tau, 1.0, 0.0), axis=0, keepdims=True)
            l_row = jnp.where(rank1 == float(k1), l_k, l_row)
        a_ref[h] = jnp.exp(s1 - a_rows[0]) / z
        l_ref[h] = l_row
        b_ref[h] = jnp.exp(s2 - b_rows[0]).astype(b_ref.dtype)
        r2_ref[h] = rank2.astype(r2_ref.dtype)


def _peer_topk_call(qp, keys):
    tokens = qp.shape[0]
    tq = 512
    spec = pl.BlockSpec((PEER_HEADS, PEER_N_KEYS, tq), lambda i: (0, 0, i))
    shape = lambda dt: jax.ShapeDtypeStruct((PEER_HEADS, PEER_N_KEYS, tokens), dt)
    return pl.pallas_call(
        _peer_topk_kernel,
        grid=(tokens // tq,),
        in_specs=[pl.BlockSpec((tq, qp.shape[1]), lambda i: (i, 0)),
                  pl.BlockSpec(keys.shape, lambda i: (0, 0, 0, 0))],
        out_specs=[spec, spec, spec, spec],
        out_shape=[shape(F32), shape(F32), shape(BF16), shape(BF16)],
        compiler_params=_params("arbitrary"),
        name="peer_topk",
    )(qp, keys)


PEER_TM = 1024
PEER_ROWS = 8
PEER_TOKEN_CHUNK = 256
PEER_DOWN_TOKENS = 512
PEER_DOWN_EXPERTS = 2048


def _peer_act_kernel(ht_ref, u_ref, a_ref, l_ref, b_ref, r2_ref, o_ref, hid_ref):
    e = pl.program_id(1)
    tc = PEER_TOKEN_CHUNK
    n_chunks = o_ref.shape[1] // tc

    def up(c):
        hid_ref[c % 2] = jnp.dot(u_ref[...], ht_ref[0, :, c * tc:(c + 1) * tc], preferred_element_type=F32)

    def activate(c):
        cols = slice(c * tc, (c + 1) * tc)
        for r in range(PEER_ROWS):
            rows = slice(r * PEER_N_KEYS, (r + 1) * PEER_N_KEYS)
            hid = hid_ref[c % 2, rows, :]
            i = e * PEER_ROWS + r

            def key_row(ref, h):
                tile = jnp.broadcast_to(ref[h, pl.ds(i, 1), cols], (BF16_ROWS, tc)).astype(BF16)
                return jnp.tile(tile, (PEER_N_KEYS // BF16_ROWS, 1))

            gate = jnp.zeros(hid.shape, BF16)
            for h in range(PEER_HEADS):
                chosen = r2_ref[h, :, cols] < key_row(l_ref, h)
                gate = gate + jnp.where(chosen, b_ref[h, :, cols], jnp.zeros((), BF16)) * key_row(a_ref, h)
            gelu = 0.5 * hid * (1.0 + lax.erf(hid * (2.0 ** -0.5)))
            o_ref[rows, cols] = gate * gelu.astype(BF16)

    up(0)
    for c in range(n_chunks):
        if c + 1 < n_chunks:
            up(c + 1)
        activate(c)


def _peer_down_kernel(vt_ref, act_ref, x_ref, gt_ref, o_ref, acc_ref):
    k = pl.program_id(1)

    @pl.when(k == 0)
    def _():
        acc_ref[...] = jnp.zeros_like(acc_ref)

    acc_ref[...] += jnp.dot(vt_ref[...], act_ref[...], preferred_element_type=F32)

    @pl.when(k == pl.num_programs(1) - 1)
    def _():
        o_ref[0] = x_ref[0] + gt_ref[0] * acc_ref[...].T


def _peer_ffn_call(ht, u, vt, a_t, l_t, b_t, r2_t, x, gt):
    b, d, s = ht.shape
    n_experts = u.shape[0]
    tm = PEER_TM
    te = PEER_ROWS * PEER_N_KEYS
    per_b = s // tm
    key_spec = pl.BlockSpec((PEER_HEADS, PEER_N_KEYS, tm), lambda i, e: (0, 0, i))
    act_t = pl.pallas_call(
        _peer_act_kernel,
        grid=(b * per_b, n_experts // te),
        in_specs=[
            pl.BlockSpec((1, d, tm), lambda i, e: (i // per_b, 0, i % per_b)),
            pl.BlockSpec((te, d), lambda i, e: (e, 0)),
            key_spec, key_spec, key_spec, key_spec,
        ],
        out_specs=pl.BlockSpec((te, tm), lambda i, e: (e, i)),
        out_shape=jax.ShapeDtypeStruct((n_experts, b * s), BF16),
        scratch_shapes=[pltpu.VMEM((2, te, PEER_TOKEN_CHUNK), F32)],
        compiler_params=_params("arbitrary", "arbitrary"),
        name="peer_act",
    )(ht, u, a_t, l_t, b_t, r2_t)

    tn, tk = PEER_DOWN_TOKENS, PEER_DOWN_EXPERTS
    per_b = s // tn
    return pl.pallas_call(
        _peer_down_kernel,
        grid=(b * per_b, n_experts // tk),
        in_specs=[
            pl.BlockSpec((d, tk), lambda i, k: (0, k)),
            pl.BlockSpec((tk, tn), lambda i, k: (k, i)),
            pl.BlockSpec((1, tn, d), lambda i, k: (i // per_b, i % per_b, 0)),
            pl.BlockSpec((1, 1, d), lambda i, k: (i // per_b, 0, 0)),
        ],
        out_specs=pl.BlockSpec((1, tn, d), lambda i, k: (i // per_b, i % per_b, 0)),
        out_shape=jax.ShapeDtypeStruct(x.shape, F32),
        scratch_shapes=[pltpu.VMEM((d, tn), F32)],
        compiler_params=_params("arbitrary", "arbitrary"),
        name="peer_down",
    )(vt, act_t, x, gt)


def _peer_tables_kernel(u_ref, v_ref, ub_ref, vt_ref):
    ub_ref[...] = u_ref[0].astype(BF16)
    vt_ref[...] = v_ref[0].T.astype(BF16)


def _peer_tables_call(u, v, layer):
    _, e, d = u.shape
    te = 512
    return pl.pallas_call(
        _peer_tables_kernel,
        grid=(e // te,),
        in_specs=[pl.BlockSpec((1, te, d), lambda i: (layer, i, 0)),
                  pl.BlockSpec((1, te, d), lambda i: (layer, i, 0))],
        out_specs=[pl.BlockSpec((te, d), lambda i: (i, 0)), pl.BlockSpec((d, te), lambda i: (0, i))],
        out_shape=[jax.ShapeDtypeStruct((e, d), BF16), jax.ShapeDtypeStruct((d, e), BF16)],
        compiler_params=_params("arbitrary"),
        name="peer_tables",
    )(u, v)


def kernel(x, c, positions, ada_w, ada_b, norm1_g, norm2_g, w_in, b_forget, lam_q1, lam_k1, lam_q2, lam_k2,
           diff_subln_g, w_br_fox, w_br_sb, w_br_diff, w_o, peer_wq, peer_sub_keys, peer_u, peer_v,
           final_norm_g):
    batch, seq, d = x.shape
    tokens = batch * seq
    assert batch <= 8 and seq % 1024 == 0 and d == D_MODEL

    c_pad = jnp.zeros((8, d), F32).at[:batch].set(c)
    mod = _mod_call(c_pad, ada_w, ada_b)[:, :batch].reshape(DEPTH, batch, N_MOD, 1, d)
    pos_col = positions.reshape(tokens, 1)
    inv_freq = ROPE_THETA ** (-jnp.arange(0, DIFF_QK_DIM, 2, dtype=F32) / DIFF_QK_DIM)
    freq_row = jnp.tile(inv_freq, LANES // inv_freq.shape[0]).reshape(1, LANES)
    ri = np.arange(SB_SEG)
    tri_after = jnp.asarray(np.tile(ri[:, None] > ri[None, :], (2, 1)), BF16)

    for l in range(DEPTH):
        lam_init = 0.8 - 0.6 * math.exp(-0.3 * l)
        sh1, sc1, gt1, sh2, sc2, gt2 = (mod[l, :, k] for k in range(N_MOD))

        h1 = _norm_mod_call(x, norm1_g[l].reshape(1, d), sc1, sh1, BF16)[0].reshape(tokens, d)
        proj = _proj_call(h1, w_in, l, 0, QKV_WIDTH, BF16, bn=1024)
        gates = _proj_call(h1, w_in, l, GATE_OFFSET, N_BRANCHES * d, BF16, bn=1024)
        f_logit = _proj_call(h1, w_in, l, QKV_WIDTH, LANES, F32, bn=LANES, keep=FOX_HEADS)
        f_logit = f_logit.reshape(batch, seq, LANES)
        b_forget_pad = jnp.pad(b_forget[l], (0, LANES - FOX_HEADS)).reshape(1, LANES)
        f_col, f_row = _forget_cumsum_call(f_logit, b_forget_pad)

        o_fox = _fox_call(proj, f_col, f_row, batch, seq)
        o_sb = _sb_call(proj, tri_after, batch, seq)
        qd = _rope_call(proj, pos_col, freq_row, COL_DQ, DIFF_QK_DIM ** -0.5)
        kd = _rope_call(proj, pos_col, freq_row, COL_DK, 1.0)
        o_diff = _diff_call(qd, kd, proj, lam_q1[l], lam_k1[l], lam_q2[l], lam_k2[l], diff_subln_g[l],
                            lam_init, batch, seq)

        y = _merge_call(o_fox, o_sb, o_diff, w_br_fox[l].astype(BF16), w_br_sb[l].astype(BF16),
                        w_br_diff[l].astype(BF16), gates)
        x = _matmul_residual_call(y.reshape(batch, seq, d), w_o[l].astype(BF16), x, gt1)

        h2, h2t = _norm_mod_call(x, norm2_g[l].reshape(1, d), sc2, sh2, BF16, transposed=True)
        qp = _proj_call(h2.reshape(tokens, d), peer_wq, l, 0, peer_wq.shape[2], BF16, bn=1024)
        a_t, l_t, b_t, r2_t = _peer_topk_call(qp, peer_sub_keys[l].astype(BF16))
        u_bf, vt_bf = _peer_tables_call(peer_u, peer_v, l)
        x = _peer_ffn_call(h2t, u_bf, vt_bf, a_t, l_t, b_t, r2_t, x, gt2)

    zeros = jnp.zeros((batch, 1, d), F32)
    return _norm_mod_call(x, final_norm_g.reshape(1, d), zeros, zeros, F32)[0]
```

```python
import functools
import math

import jax
import jax.numpy as jnp
import numpy as np
from jax import lax
from jax.experimental import pallas as pl
from jax.experimental.pallas import tpu as pltpu

F32 = jnp.float32
BF16 = jnp.bfloat16

D_MODEL = 2048
DEPTH = 2
CHUNK = 64
HEAD_DIM = 128
FOX_HEADS = 6
SB_HEADS = 6
DIFF_HEADS = 4
DIFF_QK_DIM = HEAD_DIM // 2
FOX_WIDTH = FOX_HEADS * HEAD_DIM
SB_WIDTH = SB_HEADS * HEAD_DIM
DIFF_WIDTH = DIFF_HEADS * HEAD_DIM
QKV_WIDTH = 3 * FOX_WIDTH + 3 * SB_WIDTH + 3 * DIFF_WIDTH
GATE_OFFSET = QKV_WIDTH + FOX_HEADS
ROPE_THETA = 10000.0
RMS_EPS = 1e-6
N_MOD = 6
N_BRANCHES = 3
PEER_HEADS = 8
PEER_N_KEYS = 128
PEER_TOPK = 16
PEER_HALF = 128

LANES = 128
BF16_ROWS = 16
VMEM_LIMIT = 56 * 1024 * 1024
NEG_INF = float("-inf")

COL_FQ, COL_FK, COL_FV = 0, FOX_HEADS, 2 * FOX_HEADS
COL_SQ, COL_SK, COL_SV = 18, 24, 30
COL_DQ, COL_DK, COL_DV = 36, 40, 44
COL_GATE = QKV_WIDTH // LANES


def _params(*sem):
    return pltpu.CompilerParams(dimension_semantics=sem, vmem_limit_bytes=VMEM_LIMIT)


def _mod_kernel(c_ref, w_ref, b_ref, o_ref):
    c = c_ref[...]
    cond = c * jax.nn.sigmoid(c)
    hi = cond.astype(BF16)
    lo = (cond - hi.astype(F32)).astype(BF16)
    w = w_ref[0].astype(BF16)
    acc = jnp.dot(hi, w, preferred_element_type=F32) + jnp.dot(lo, w, preferred_element_type=F32)
    o_ref[0] = acc + b_ref[0]


def _mod_call(c_pad, ada_w, ada_b):
    depth, d, n = ada_w.shape
    tn = 1024
    return pl.pallas_call(
        _mod_kernel,
        grid=(depth, n // tn),
        in_specs=[
            pl.BlockSpec((8, d), lambda l, j: (0, 0)),
            pl.BlockSpec((1, d, tn), lambda l, j: (l, 0, j)),
            pl.BlockSpec((1, 1, tn), lambda l, j: (l, 0, j)),
        ],
        out_specs=pl.BlockSpec((1, 8, tn), lambda l, j: (l, 0, j)),
        out_shape=jax.ShapeDtypeStruct((depth, 8, n), F32),
        compiler_params=_params("arbitrary", "arbitrary"),
        name="adaln_mod",
    )(c_pad, ada_w, ada_b.reshape(depth, 1, n))


def _norm_mod_kernel(x_ref, g_ref, sc_ref, sh_ref, *o_refs, transposed):
    x = x_ref[0]
    y = x * lax.rsqrt(jnp.mean(x * x, axis=-1, keepdims=True) + RMS_EPS) * g_ref[...]
    h = y * (1.0 + sc_ref[0]) + sh_ref[0]
    o_refs[0][0] = h.astype(o_refs[0].dtype)
    if transposed:
        o_refs[1][0] = h.T.astype(o_refs[1].dtype)


def _norm_mod_call(x, g, sc, sh, out_dtype, transposed=False):
    b, s, d = x.shape
    tr = 512
    out_shape = [jax.ShapeDtypeStruct((b, s, d), out_dtype)]
    out_specs = [pl.BlockSpec((1, tr, d), lambda i, j: (i, j, 0))]
    if transposed:
        out_shape.append(jax.ShapeDtypeStruct((b, d, s), out_dtype))
        out_specs.append(pl.BlockSpec((1, d, tr), lambda i, j: (i, 0, j)))
    return pl.pallas_call(
        functools.partial(_norm_mod_kernel, transposed=transposed),
        grid=(b, s // tr),
        in_specs=[
            pl.BlockSpec((1, tr, d), lambda i, j: (i, j, 0)),
            pl.BlockSpec((1, d), lambda i, j: (0, 0)),
            pl.BlockSpec((1, 1, d), lambda i, j: (i, 0, 0)),
            pl.BlockSpec((1, 1, d), lambda i, j: (i, 0, 0)),
        ],
        out_specs=out_specs,
        out_shape=out_shape,
        compiler_params=_params("arbitrary", "arbitrary"),
        name="norm_mod",
    )(x, g, sc, sh)


def _matmul_kernel(a_ref, w_ref, o_ref):
    o_ref[...] = jnp.dot(a_ref[...], w_ref[...], preferred_element_type=F32).astype(o_ref.dtype)


def _matmul_call(a, w, out_dtype, bm=1024, bn=1024):
    m, k = a.shape
    n = w.shape[1]
    bn = min(bn, n)
    return pl.pallas_call(
        _matmul_kernel,
        grid=(m // bm, n // bn),
        in_specs=[pl.BlockSpec((bm, k), lambda i, j: (i, 0)),
                  pl.BlockSpec((k, bn), lambda i, j: (0, j))],
        out_specs=pl.BlockSpec((bm, bn), lambda i, j: (i, j)),
        out_shape=jax.ShapeDtypeStruct((m, n), out_dtype),
        compiler_params=_params("arbitrary", "arbitrary"),
        name="matmul",
    )(a, w)


def _proj_kernel(a_ref, w_ref, *rest, shift, keep):
    o_ref, wb_ref = rest[-2:]

    @pl.when(pl.program_id(1) == 0)
    def _():
        w = w_ref[0]
        if shift:
            w = jnp.concatenate([w[:, shift:], rest[0][0][:, :shift]], axis=1)
        if keep is not None:
            lane = lax.broadcasted_iota(jnp.int32, w.shape, 1)
            w = jnp.where(lane < keep, w, 0.0)
        wb_ref[...] = w.astype(BF16)

    o_ref[...] = jnp.dot(a_ref[...], wb_ref[...], preferred_element_type=F32).astype(o_ref.dtype)


def _proj_call(a, w, layer, col0, n, out_dtype, bn, keep=None, bm=1024):
    m, k = a.shape
    shift = col0 % LANES
    base = (col0 - shift) // bn
    assert (col0 - shift) % bn == 0 and n % bn == 0 and bn % LANES == 0
    next_stride = bn // LANES
    in_specs = [pl.BlockSpec((bm, k), lambda j, i: (i, 0)),
                pl.BlockSpec((1, k, bn), lambda j, i: (layer, 0, base + j))]
    operands = [a, w]
    if shift:
        in_specs.append(pl.BlockSpec((1, k, LANES), lambda j, i: (layer, 0, (base + j + 1) * next_stride)))
        operands.append(w)
    return pl.pallas_call(
        functools.partial(_proj_kernel, shift=shift, keep=keep),
        grid=(n // bn, m // bm),
        in_specs=in_specs,
        out_specs=pl.BlockSpec((bm, bn), lambda j, i: (i, j)),
        out_shape=jax.ShapeDtypeStruct((m, n), out_dtype),
        scratch_shapes=[pltpu.VMEM((k, bn), BF16)],
        compiler_params=_params("arbitrary", "arbitrary"),
        name="proj",
    )(*operands)


def _matmul_residual_kernel(a_ref, w_ref, x_ref, gt_ref, o_ref):
    acc = jnp.dot(a_ref[0], w_ref[...], preferred_element_type=F32)
    o_ref[0] = x_ref[0] + gt_ref[0] * acc


def _matmul_residual_call(a, w, x, gt, bm=1024, bn=1024):
    b, s, k = a.shape
    n = w.shape[1]
    return pl.pallas_call(
        _matmul_residual_kernel,
        grid=(b, s // bm, n // bn),
        in_specs=[pl.BlockSpec((1, bm, k), lambda bi, i, j: (bi, i, 0)),
                  pl.BlockSpec((k, bn), lambda bi, i, j: (0, j)),
                  pl.BlockSpec((1, bm, bn), lambda bi, i, j: (bi, i, j)),
                  pl.BlockSpec((1, 1, bn), lambda bi, i, j: (bi, 0, j))],
        out_specs=pl.BlockSpec((1, bm, bn), lambda bi, i, j: (bi, i, j)),
        out_shape=jax.ShapeDtypeStruct((b, s, n), F32),
        compiler_params=_params("arbitrary", "arbitrary", "arbitrary"),
        name="matmul_residual",
    )(a, w, x, gt)


def _split3(x):
    hi = x.astype(BF16)
    r = x - hi.astype(F32)
    mid = r.astype(BF16)
    lo = (r - mid.astype(F32)).astype(BF16)
    return hi, mid, lo


def _forget_cumsum_kernel(fl_ref, bf_ref, col_ref, row_ref):
    s = fl_ref.shape[1]
    blk = 256
    logf = jax.nn.log_sigmoid(fl_ref[0] + bf_ref[...])
    r = lax.broadcasted_iota(jnp.int32, (blk, blk), 0)
    c = lax.broadcasted_iota(jnp.int32, (blk, blk), 1)
    tri = jnp.where(c <= r, 1.0, 0.0).astype(BF16)
    carry = jnp.zeros((1, LANES), F32)
    parts = []
    for i in range(s // blk):
        hi, mid, lo = _split3(logf[i * blk:(i + 1) * blk])
        cs = (jnp.dot(tri, hi, preferred_element_type=F32)
              + jnp.dot(tri, mid, preferred_element_type=F32)
              + jnp.dot(tri, lo, preferred_element_type=F32)) + carry
        carry = cs[blk - 1:blk, :]
        parts.append(cs)
    cum = jnp.concatenate(parts, axis=0)
    row_ref[0] = cum.T[:8, :]
    for h in range(FOX_HEADS):
        col_ref[0, h] = jnp.broadcast_to(cum[:, h:h + 1], (s, LANES))


def _forget_cumsum_call(f_logit, b_forget_pad):
    b, s, _ = f_logit.shape
    return pl.pallas_call(
        _forget_cumsum_kernel,
        grid=(b,),
        in_specs=[pl.BlockSpec((1, s, LANES), lambda i: (i, 0, 0)),
                  pl.BlockSpec((1, LANES), lambda i: (0, 0))],
        out_specs=[pl.BlockSpec((1, FOX_HEADS, s, LANES), lambda i: (i, 0, 0, 0)),
                   pl.BlockSpec((1, 8, s), lambda i: (i, 0, 0))],
        out_shape=[jax.ShapeDtypeStruct((b, FOX_HEADS, s, LANES), F32),
                   jax.ShapeDtypeStruct((b, 8, s), F32)],
        compiler_params=_params("arbitrary"),
        name="forget_cumsum",
    )(f_logit, b_forget_pad)


ATT_TILE = 512


def _qk(q, k):
    return lax.dot_general(q, k, (((1,), (1,)), ((), ())), preferred_element_type=F32)


def _kv_block(k_ref, v_ref, j):
    start = pl.multiple_of(j * ATT_TILE, ATT_TILE)
    return start, k_ref[pl.ds(start, ATT_TILE), :], v_ref[pl.ds(start, ATT_TILE), :]


def _tile_iotas():
    r = lax.broadcasted_iota(jnp.int32, (ATT_TILE, ATT_TILE), 0)
    c = lax.broadcasted_iota(jnp.int32, (ATT_TILE, ATT_TILE), 1)
    return r, c


def _softmax_state():
    t = ATT_TILE
    return [pltpu.VMEM((t, LANES), F32), pltpu.VMEM((t, LANES), F32), pltpu.VMEM((t, HEAD_DIM), F32)]


def _softmax_reset(m_ref, l_ref, acc_ref):
    m_ref[...] = jnp.full(m_ref.shape, NEG_INF, F32)
    l_ref[...] = jnp.zeros(l_ref.shape, F32)
    acc_ref[...] = jnp.zeros(acc_ref.shape, F32)


def _softmax_update(m_ref, l_ref, acc_ref, s, vb):
    m_old = m_ref[...]
    m_new = jnp.maximum(m_old, jnp.max(s, axis=-1, keepdims=True))
    alpha = jnp.exp(m_old - m_new)
    p = jnp.exp(s - jnp.tile(m_new, (1, s.shape[1] // LANES)))
    l_ref[...] = alpha * l_ref[...] + jnp.sum(p, axis=-1, keepdims=True)
    acc_ref[...] = alpha * acc_ref[...] + jnp.dot(p.astype(BF16), vb, preferred_element_type=F32)
    m_ref[...] = m_new


def _fox_kernel(q_ref, k_ref, v_ref, fc_ref, fr_ref, o_ref, m_ref, l_ref, acc_ref):
    t = ATT_TILE
    qi = pl.program_id(2)
    h = pl.program_id(1)
    q = (q_ref[...].astype(F32) * (HEAD_DIM ** -0.5)).astype(BF16)

    def step(j, masked):
        start, kb, vb = _kv_block(k_ref, v_ref, j)
        fcol = jnp.tile(fc_ref[0, 0], (1, t // LANES))
        frow = fr_ref[0, pl.ds(h, 1), pl.ds(start, t)]
        s = _qk(q, kb) + (fcol - frow)
        if masked:
            r, c = _tile_iotas()
            s = jnp.where(c <= r, s, NEG_INF)
        _softmax_update(m_ref, l_ref, acc_ref, s, vb)

    _softmax_reset(m_ref, l_ref, acc_ref)
    step(qi, masked=True)

    @pl.loop(0, qi)
    def _(j):
        step(j, masked=False)

    o_ref[...] = (acc_ref[...] / l_ref[...]).astype(o_ref.dtype)


def _fox_call(proj, f_col, f_row, batch, seq):
    tq = ATT_TILE
    nq = seq // tq
    return pl.pallas_call(
        _fox_kernel,
        grid=(batch, FOX_HEADS, nq),
        in_specs=[
            pl.BlockSpec((tq, HEAD_DIM), lambda b, h, i: (b * nq + i, COL_FQ + h)),
            pl.BlockSpec((seq, HEAD_DIM), lambda b, h, i: (b, COL_FK + h)),
            pl.BlockSpec((seq, HEAD_DIM), lambda b, h, i: (b, COL_FV + h)),
            pl.BlockSpec((1, 1, tq, LANES), lambda b, h, i: (b, h, i, 0)),
            pl.BlockSpec((1, 8, seq), lambda b, h, i: (b, 0, 0)),
        ],
        out_specs=pl.BlockSpec((tq, HEAD_DIM), lambda b, h, i: (b * nq + i, h)),
        out_shape=jax.ShapeDtypeStruct((batch * seq, FOX_WIDTH), BF16),
        scratch_shapes=_softmax_state(),
        compiler_params=_params("arbitrary", "arbitrary", "arbitrary"),
        name="fox_attention",
    )(proj, proj, proj, f_col, f_row)


SB_SEG = 256


def _sb_kernel(q_ref, k_ref, v_ref, tri_ref, o_ref, run_ref, acc_ref):
    t = ATT_TILE
    qi = pl.program_id(2)
    qn = (q_ref[...].astype(F32) * -(HEAD_DIM ** -0.5)).astype(BF16)

    def step(j, masked):
        _, kb, vb = _kv_block(k_ref, v_ref, j)
        tri2 = tri_ref[...]
        u = _qk(qn, kb)
        lk = jnp.minimum(u, 0.0) - jnp.log(1.0 + jnp.exp(-jnp.abs(u)))
        if masked:
            r, c = _tile_iotas()
            strictly_causal = c < r
            lk = jnp.where(strictly_causal, lk, 0.0)
        hi = lk.astype(BF16)
        lo = (lk - hi.astype(F32)).astype(BF16)
        run = run_ref[...]
        laters = []
        for g in reversed(range(t // SB_SEG)):
            sl = slice(g * SB_SEG, (g + 1) * SB_SEG)
            hi_lo = jnp.concatenate([hi[:, sl], lo[:, sl]], axis=1)
            laters.append(jnp.dot(hi_lo, tri2, preferred_element_type=F32) + jnp.tile(run, (1, SB_SEG // LANES)))
            run = run + jnp.sum(lk[:, sl], axis=-1, keepdims=True)
        later = jnp.concatenate(laters[::-1], axis=1)
        w = jnp.exp(lk - u + later)
        if masked:
            w = jnp.where(strictly_causal, w, 0.0)
        acc_ref[...] += jnp.dot(w.astype(BF16), vb, preferred_element_type=F32)
        run_ref[...] = run

    run_ref[...] = jnp.zeros(run_ref.shape, F32)
    acc_ref[...] = jnp.zeros(acc_ref.shape, F32)
    step(qi, masked=True)

    @pl.loop(0, qi)
    def _(i):
        step(qi - 1 - i, masked=False)

    o_ref[...] = acc_ref[...].astype(o_ref.dtype)


def _sb_call(proj, tri, batch, seq):
    tq = ATT_TILE
    nq = seq // tq
    return pl.pallas_call(
        _sb_kernel,
        grid=(batch, SB_HEADS, nq),
        in_specs=[
            pl.BlockSpec((tq, HEAD_DIM), lambda b, h, i: (b * nq + i, COL_SQ + h)),
            pl.BlockSpec((seq, HEAD_DIM), lambda b, h, i: (b, COL_SK + h)),
            pl.BlockSpec((seq, HEAD_DIM), lambda b, h, i: (b, COL_SV + h)),
            pl.BlockSpec((2 * SB_SEG, SB_SEG), lambda b, h, i: (0, 0)),
        ],
        out_specs=pl.BlockSpec((tq, HEAD_DIM), lambda b, h, i: (b * nq + i, h)),
        out_shape=jax.ShapeDtypeStruct((batch * seq, SB_WIDTH), BF16),
        scratch_shapes=[pltpu.VMEM((tq, LANES), F32), pltpu.VMEM((tq, HEAD_DIM), F32)],
        compiler_params=_params("arbitrary", "arbitrary", "arbitrary"),
        name="sb_attention",
    )(proj, proj, proj, tri)


def _rope_kernel(x_ref, pos_ref, freq_ref, o_ref, *, out_scale):
    half = DIFF_QK_DIM // 2
    ang = pos_ref[...].astype(F32) * freq_ref[...]
    cos = jnp.cos(ang)
    sin = jnp.sin(ang)
    lane = lax.broadcasted_iota(jnp.int32, ang.shape, 1)
    first = (lane % DIFF_QK_DIM) < half
    sin_signed = jnp.where(first, -sin, sin)
    for cb in range(x_ref.shape[1] // LANES):
        x = x_ref[:, cb * LANES:(cb + 1) * LANES].astype(F32)
        partner = jnp.where(first, pltpu.roll(x, LANES - half, 1), pltpu.roll(x, half, 1))
        y = x * cos + partner * sin_signed
        o_ref[:, cb * LANES:(cb + 1) * LANES] = (y * out_scale).astype(o_ref.dtype)


def _rope_call(proj, pos_col, freq_row, col_block, out_scale):
    tokens = proj.shape[0]
    tr = 512
    width = DIFF_HEADS * HEAD_DIM
    return pl.pallas_call(
        functools.partial(_rope_kernel, out_scale=out_scale),
        grid=(tokens // tr,),
        in_specs=[pl.BlockSpec((tr, width), lambda i: (i, col_block * LANES // width)),
                  pl.BlockSpec((tr, 1), lambda i: (i, 0)),
                  pl.BlockSpec((1, LANES), lambda i: (0, 0))],
        out_specs=pl.BlockSpec((tr, width), lambda i: (i, 0)),
        out_shape=jax.ShapeDtypeStruct((tokens, width), BF16),
        compiler_params=_params("arbitrary"),
        name="rope",
    )(proj, pos_col, freq_row)


def _diff_kernel(q_ref, k_ref, v_ref, lq1_ref, lk1_ref, lq2_ref, lk2_ref, g_ref, o_ref,
                 m1_ref, l1_ref, a1_ref, m2_ref, l2_ref, a2_ref, *, lam_init):
    t = ATT_TILE
    qi = pl.program_id(2)
    q = q_ref[...]
    lane = lax.broadcasted_iota(jnp.int32, q.shape, 1)
    zero = jnp.zeros_like(q)
    q1 = jnp.where(lane < DIFF_QK_DIM, q, zero)
    q2 = jnp.where(lane < DIFF_QK_DIM, zero, q)
    state1 = (m1_ref, l1_ref, a1_ref)
    state2 = (m2_ref, l2_ref, a2_ref)

    def step(j, masked):
        _, kb, vb = _kv_block(k_ref, v_ref, j)
        for qc, state in ((q1, state1), (q2, state2)):
            s = _qk(qc, kb)
            if masked:
                r, c = _tile_iotas()
                s = jnp.where((c // CHUNK) <= (r // CHUNK), s, NEG_INF)
            _softmax_update(*state, s, vb)

    _softmax_reset(*state1)
    _softmax_reset(*state2)
    step(qi, masked=True)

    @pl.loop(0, qi)
    def _(j):
        step(j, masked=False)

    lam = (jnp.exp(jnp.sum(lq1_ref[0] * lk1_ref[0], axis=-1, keepdims=True))
           - jnp.exp(jnp.sum(lq2_ref[0] * lk2_ref[0], axis=-1, keepdims=True)) + lam_init)
    o = a1_ref[...] / l1_ref[...] - lam * (a2_ref[...] / l2_ref[...])
    o = o * lax.rsqrt(jnp.mean(o * o, axis=-1, keepdims=True) + RMS_EPS) * g_ref[...]
    o_ref[...] = (o * (1.0 - lam_init)).astype(o_ref.dtype)


def _diff_call(qd, kd, proj, lam_q1, lam_k1, lam_q2, lam_k2, subln_g, lam_init, batch, seq):
    t = ATT_TILE
    nq = seq // t
    lam_spec = pl.BlockSpec((1, 1, DIFF_QK_DIM), lambda b, h, i: (h, 0, 0))
    r3 = lambda a: a.reshape(DIFF_HEADS, 1, DIFF_QK_DIM)
    return pl.pallas_call(
        functools.partial(_diff_kernel, lam_init=lam_init),
        grid=(batch, DIFF_HEADS, nq),
        in_specs=[
            pl.BlockSpec((t, HEAD_DIM), lambda b, h, i: (b * nq + i, h)),
            pl.BlockSpec((seq, HEAD_DIM), lambda b, h, i: (b, h)),
            pl.BlockSpec((seq, HEAD_DIM), lambda b, h, i: (b, COL_DV + h)),
            lam_spec, lam_spec, lam_spec, lam_spec,
            pl.BlockSpec((1, HEAD_DIM), lambda b, h, i: (0, 0)),
        ],
        out_specs=pl.BlockSpec((t, HEAD_DIM), lambda b, h, i: (b * nq + i, h)),
        out_shape=jax.ShapeDtypeStruct((batch * seq, DIFF_WIDTH), BF16),
        scratch_shapes=_softmax_state() + _softmax_state(),
        compiler_params=_params("arbitrary", "arbitrary", "arbitrary"),
        name="diff_attention",
    )(qd, kd, proj, r3(lam_q1), r3(lam_k1), r3(lam_q2), r3(lam_k2), subln_g.reshape(1, HEAD_DIM))


def _merge_kernel(of_ref, os_ref, od_ref, wf_ref, ws_ref, wd_ref, gf_ref, gs_ref, gd_ref, o_ref):
    def branch(o_r, w_r, g_r):
        return jax.nn.sigmoid(g_r[...].astype(F32)) * jnp.dot(o_r[...], w_r[...], preferred_element_type=F32)

    y = branch(of_ref, wf_ref, gf_ref) + branch(os_ref, ws_ref, gs_ref) + branch(od_ref, wd_ref, gd_ref)
    o_ref[...] = y.astype(o_ref.dtype)


def _merge_call(o_fox, o_sb, o_diff, w_fox, w_sb, w_diff, gates, bm=1024, bn=1024):
    m = o_fox.shape[0]
    d = w_fox.shape[1]
    gate_blocks = d // bn

    def gate_spec(k):
        return pl.BlockSpec((bm, bn), lambda i, j: (i, k * gate_blocks + j))

    return pl.pallas_call(
        _merge_kernel,
        grid=(m // bm, d // bn),
        in_specs=[
            pl.BlockSpec((bm, FOX_WIDTH), lambda i, j: (i, 0)),
            pl.BlockSpec((bm, SB_WIDTH), lambda i, j: (i, 0)),
            pl.BlockSpec((bm, DIFF_WIDTH), lambda i, j: (i, 0)),
            pl.BlockSpec((FOX_WIDTH, bn), lambda i, j: (0, j)),
            pl.BlockSpec((SB_WIDTH, bn), lambda i, j: (0, j)),
            pl.BlockSpec((DIFF_WIDTH, bn), lambda i, j: (0, j)),
            gate_spec(0), gate_spec(1), gate_spec(2),
        ],
        out_specs=pl.BlockSpec((bm, bn), lambda i, j: (i, j)),
        out_shape=jax.ShapeDtypeStruct((m, d), BF16),
        compiler_params=_params("arbitrary", "arbitrary"),
        name="branch_merge",
    )(o_fox, o_sb, o_diff, w_fox, w_sb, w_diff, gates, gates, gates)


NOT_TOP = 64.0


def _extract_top(cur, count, want_rank=False):
    rows = []
    rank = jnp.full(cur.shape, NOT_TOP, F32) if want_rank else None
    for r in range(count):
        m = jnp.max(cur, axis=0, keepdims=True)
        rows.append(m)
        hit = cur == m
        if want_rank:
            rank = jnp.where(hit, float(r), rank)
        cur = jnp.where(hit, NEG_INF, cur)
    return rows, rank


def _peer_topk_kernel(q_ref, keys_ref, a_ref, l_ref, b_ref, r2_ref):
    k = PEER_TOPK
    for h in range(PEER_HEADS):
        col = 2 * h * PEER_HALF
        s1 = _qk(keys_ref[h, 0], q_ref[:, col:col + PEER_HALF])
        s2 = _qk(keys_ref[h, 1], q_ref[:, col + PEER_HALF:col + 2 * PEER_HALF])
        a_rows, rank1 = _extract_top(s1, k, want_rank=True)
        b_rows, rank2 = _extract_top(s2, k, want_rank=True)
        b_top = jnp.concatenate(b_rows, axis=0)
        counts = [k // (k1 + 1) for k1 in range(k)]
        cands = [a_rows[k1] + b_top[:counts[k1]] for k1 in range(k)]
        n_cand = sum(counts)
        pad = jnp.full((-n_cand % 8, b_top.shape[1]), NEG_INF, F32)
        best, _ = _extract_top(jnp.concatenate(cands + [pad], axis=0), k)
        tau = best[k - 1]
        z = jnp.zeros_like(tau)
        for row in best:
            z = z + jnp.exp(row - best[0])
        l_row = jnp.zeros(s1.shape, F32)
        for k1 in range(k):
            l_k = jnp.sum(jnp.where(cands[k1] >= tau, 1.0, 0.0), axis=0, keepdims=True)
            l_row = jnp.where(rank1 == float(k1), l_k, l_row)
        a_ref[h] = jnp.exp(s1 - a_rows[0]) / z
        l_ref[h] = l_row
        b_ref[h] = jnp.exp(s2 - b_rows[0]).astype(b_ref.dtype)
        r2_ref[h] = rank2.astype(r2_ref.dtype)


def _peer_topk_call(qp, keys):
    tokens = qp.shape[0]
    tq = 512
    spec = pl.BlockSpec((PEER_HEADS, PEER_N_KEYS, tq), lambda i: (0, 0, i))
    shape = lambda dt: jax.ShapeDtypeStruct((PEER_HEADS, PEER_N_KEYS, tokens), dt)
    return pl.pallas_call(
        _peer_topk_kernel,
        grid=(tokens // tq,),
        in_specs=[pl.BlockSpec((tq, qp.shape[1]), lambda i: (i, 0)),
                  pl.BlockSpec(keys.shape, lambda i: (0, 0, 0, 0))],
        out_specs=[spec, spec, spec, spec],
        out_shape=[shape(F32), shape(F32), shape(BF16), shape(BF16)],
        compiler_params=_params("arbitrary"),
        name="peer_topk",
    )(qp, keys)


PEER_UP_TOKENS = 1024
PEER_UP_EXPERTS = 1024
PEER_TOKEN_CHUNK = 256
PEER_DOWN_TOKENS = 512
PEER_DOWN_ROWS = 8


def _peer_up_kernel(ht_ref, u_ref, o_ref):
    tc = PEER_TOKEN_CHUNK
    for c in range(o_ref.shape[1] // tc):
        cols = slice(c * tc, (c + 1) * tc)
        hid = jnp.dot(u_ref[...], ht_ref[0, :, cols], preferred_element_type=F32)
        gelu = 0.5 * hid * (1.0 + lax.erf(hid * (2.0 ** -0.5)))
        o_ref[:, cols] = gelu.astype(o_ref.dtype)


def _peer_down_kernel(vt_ref, g_ref, a_ref, l_ref, b_ref, r2_ref, x_ref, gt_ref, o_ref, acc_ref, act_ref):
    k = pl.program_id(1)
    tn = act_ref.shape[1]

    @pl.when(k == 0)
    def _():
        acc_ref[...] = jnp.zeros_like(acc_ref)

    for r in range(PEER_DOWN_ROWS):
        rows = slice(r * PEER_N_KEYS, (r + 1) * PEER_N_KEYS)
        i = k * PEER_DOWN_ROWS + r

        def key_row(ref, h):
            tile = jnp.broadcast_to(ref[h, pl.ds(i, 1), :], (BF16_ROWS, tn)).astype(BF16)
            return jnp.tile(tile, (PEER_N_KEYS // BF16_ROWS, 1))

        gate = jnp.zeros((PEER_N_KEYS, tn), BF16)
        for h in range(PEER_HEADS):
            chosen = r2_ref[h] < key_row(l_ref, h)
            gate = gate + jnp.where(chosen, b_ref[h], jnp.zeros((), BF16)) * key_row(a_ref, h)
        act_ref[rows, :] = gate * g_ref[rows, :]

    acc_ref[...] += jnp.dot(vt_ref[...], act_ref[...], preferred_element_type=F32)

    @pl.when(k == pl.num_programs(1) - 1)
    def _():
        o_ref[0] = x_ref[0] + gt_ref[0] * acc_ref[...].T


def _peer_ffn_call(ht, u, vt, a_t, l_t, b_t, r2_t, x, gt):
    b, d, s = ht.shape
    n_experts = u.shape[0]
    tm, te = PEER_UP_TOKENS, PEER_UP_EXPERTS
    per_b = s // tm
    gelu_t = pl.pallas_call(
        _peer_up_kernel,
        grid=(b * per_b, n_experts // te),
        in_specs=[pl.BlockSpec((1, d, tm), lambda i, e: (i // per_b, 0, i % per_b)),
                  pl.BlockSpec((te, d), lambda i, e: (e, 0))],
        out_specs=pl.BlockSpec((te, tm), lambda i, e: (e, i)),
        out_shape=jax.ShapeDtypeStruct((n_experts, b * s), BF16),
        compiler_params=_params("arbitrary", "arbitrary"),
        name="peer_up",
    )(ht, u)

    tn, tk = PEER_DOWN_TOKENS, PEER_DOWN_ROWS * PEER_N_KEYS
    per_b = s // tn
    key_spec = pl.BlockSpec((PEER_HEADS, PEER_N_KEYS, tn), lambda i, k: (0, 0, i))
    return pl.pallas_call(
        _peer_down_kernel,
        grid=(b * per_b, n_experts // tk),
        in_specs=[
            pl.BlockSpec((d, tk), lambda i, k: (0, k)),
            pl.BlockSpec((tk, tn), lambda i, k: (k, i)),
            key_spec, key_spec, key_spec, key_spec,
            pl.BlockSpec((1, tn, d), lambda i, k: (i // per_b, i % per_b, 0)),
            pl.BlockSpec((1, 1, d), lambda i, k: (i // per_b, 0, 0)),
        ],
        out_specs=pl.BlockSpec((1, tn, d), lambda i, k: (i // per_b, i % per_b, 0)),
        out_shape=jax.ShapeDtypeStruct(x.shape, F32),
        scratch_shapes=[pltpu.VMEM((d, tn), F32), pltpu.VMEM((tk, tn), BF16)],
        compiler_params=_params("arbitrary", "arbitrary"),
        name="peer_down",
    )(vt, gelu_t, a_t, l_t, b_t, r2_t, x, gt)


def _peer_tables_kernel(u_ref, v_ref, ub_ref, vt_ref):
    ub_ref[...] = u_ref[0].astype(BF16)
    vt_ref[...] = v_ref[0].T.astype(BF16)


def _peer_tables_call(u, v, layer):
    _, e, d = u.shape
    te = 512
    return pl.pallas_call(
        _peer_tables_kernel,
        grid=(e // te,),
        in_specs=[pl.BlockSpec((1, te, d), lambda i: (layer, i, 0)),
                  pl.BlockSpec((1, te, d), lambda i: (layer, i, 0))],
        out_specs=[pl.BlockSpec((te, d), lambda i: (i, 0)), pl.BlockSpec((d, te), lambda i: (0, i))],
        out_shape=[jax.ShapeDtypeStruct((e, d), BF16), jax.ShapeDtypeStruct((d, e), BF16)],
        compiler_params=_params("arbitrary"),
        name="peer_tables",
    )(u, v)


def kernel(x, c, positions, ada_w, ada_b, norm1_g, norm2_g, w_in, b_forget, lam_q1, lam_k1, lam_q2, lam_k2,
           diff_subln_g, w_br_fox, w_br_sb, w_br_diff, w_o, peer_wq, peer_sub_keys, peer_u, peer_v,
           final_norm_g):
    batch, seq, d = x.shape
    tokens = batch * seq
    assert batch <= 8 and seq % 1024 == 0 and d == D_MODEL

    c_pad = jnp.zeros((8, d), F32).at[:batch].set(c)
    mod = _mod_call(c_pad, ada_w, ada_b)[:, :batch].reshape(DEPTH, batch, N_MOD, 1, d)
    pos_col = positions.reshape(tokens, 1)
    inv_freq = ROPE_THETA ** (-jnp.arange(0, DIFF_QK_DIM, 2, dtype=F32) / DIFF_QK_DIM)
    freq_row = jnp.tile(inv_freq, LANES // inv_freq.shape[0]).reshape(1, LANES)
    ri = np.arange(SB_SEG)
    tri_after = jnp.asarray(np.tile(ri[:, None] > ri[None, :], (2, 1)), BF16)

    for l in range(DEPTH):
        lam_init = 0.8 - 0.6 * math.exp(-0.3 * l)
        sh1, sc1, gt1, sh2, sc2, gt2 = (mod[l, :, k] for k in range(N_MOD))

        h1 = _norm_mod_call(x, norm1_g[l].reshape(1, d), sc1, sh1, BF16)[0].reshape(tokens, d)
        proj = _proj_call(h1, w_in, l, 0, QKV_WIDTH, BF16, bn=1024)
        gates = _proj_call(h1, w_in, l, GATE_OFFSET, N_BRANCHES * d, BF16, bn=1024)
        f_logit = _proj_call(h1, w_in, l, QKV_WIDTH, LANES, F32, bn=LANES, keep=FOX_HEADS)
        f_logit = f_logit.reshape(batch, seq, LANES)
        b_forget_pad = jnp.pad(b_forget[l], (0, LANES - FOX_HEADS)).reshape(1, LANES)
        f_col, f_row = _forget_cumsum_call(f_logit, b_forget_pad)

        o_fox = _fox_call(proj, f_col, f_row, batch, seq)
        o_sb = _sb_call(proj, tri_after, batch, seq)
        qd = _rope_call(proj, pos_col, freq_row, COL_DQ, DIFF_QK_DIM ** -0.5)
        kd = _rope_call(proj, pos_col, freq_row, COL_DK, 1.0)
        o_diff = _diff_call(qd, kd, proj, lam_q1[l], lam_k1[l], lam_q2[l], lam_k2[l], diff_subln_g[l],
                            lam_init, batch, seq)

        y = _merge_call(o_fox, o_sb, o_diff, w_br_fox[l].astype(BF16), w_br_sb[l].astype(BF16),
                        w_br_diff[l].astype(BF16), gates)
        x = _matmul_residual_call(y.reshape(batch, seq, d), w_o[l].astype(BF16), x, gt1)

        h2, h2t = _norm_mod_call(x, norm2_g[l].reshape(1, d), sc2, sh2, BF16, transposed=True)
        qp = _proj_call(h2.reshape(tokens, d), peer_wq, l, 0, peer_wq.shape[2], BF16, bn=1024)
        a_t, l_t, b_t, r2_t = _peer_topk_call(qp, peer_sub_keys[l].astype(BF16))
        u_bf, vt_bf = _peer_tables_call(peer_u, peer_v, l)
        x = _peer_ffn_call(h2t, u_bf, vt_bf, a_t, l_t, b_t, r2_t, x, gt2)

    zeros = jnp.zeros((batch, 1, d), F32)
    return _norm_mod_call(x, final_norm_g.reshape(1, d), zeros, zeros, F32)[0]
```

```python
import functools
import math

import jax
import jax.numpy as jnp
import numpy as np
from jax import lax
from jax.experimental import pallas as pl
from jax.experimental.pallas import tpu as pltpu

F32 = jnp.float32
BF16 = jnp.bfloat16

D_MODEL = 2048
DEPTH = 2
CHUNK = 64
HEAD_DIM = 128
FOX_HEADS = 6
SB_HEADS = 6
DIFF_HEADS = 4
DIFF_QK_DIM = HEAD_DIM // 2
FOX_WIDTH = FOX_HEADS * HEAD_DIM
SB_WIDTH = SB_HEADS * HEAD_DIM
DIFF_WIDTH = DIFF_HEADS * HEAD_DIM
QKV_WIDTH = 3 * FOX_WIDTH + 3 * SB_WIDTH + 3 * DIFF_WIDTH
GATE_OFFSET = QKV_WIDTH + FOX_HEADS
ROPE_THETA = 10000.0
RMS_EPS = 1e-6
N_MOD = 6
N_BRANCHES = 3
PEER_HEADS = 8
PEER_N_KEYS = 128
PEER_TOPK = 16
PEER_HALF = 128

LANES = 128
BF16_ROWS = 16
VMEM_LIMIT = 56 * 1024 * 1024
NEG_INF = float("-inf")

COL_FQ, COL_FK, COL_FV = 0, FOX_HEADS, 2 * FOX_HEADS
COL_SQ, COL_SK, COL_SV = 18, 24, 30
COL_DQ, COL_DK, COL_DV = 36, 40, 44
COL_GATE = QKV_WIDTH // LANES


def _params(*sem):
    return pltpu.CompilerParams(dimension_semantics=sem, vmem_limit_bytes=VMEM_LIMIT)


def _mod_kernel(c_ref, w_ref, b_ref, o_ref):
    c = c_ref[...]
    cond = c * jax.nn.sigmoid(c)
    hi = cond.astype(BF16)
    lo = (cond - hi.astype(F32)).astype(BF16)
    w = w_ref[0].astype(BF16)
    acc = jnp.dot(hi, w, preferred_element_type=F32) + jnp.dot(lo, w, preferred_element_type=F32)
    o_ref[0] = acc + b_ref[0]


def _mod_call(c_pad, ada_w, ada_b):
    depth, d, n = ada_w.shape
    tn = 1024
    return pl.pallas_call(
        _mod_kernel,
        grid=(depth, n // tn),
        in_specs=[
            pl.BlockSpec((8, d), lambda l, j: (0, 0)),
            pl.BlockSpec((1, d, tn), lambda l, j: (l, 0, j)),
            pl.BlockSpec((1, 1, tn), lambda l, j: (l, 0, j)),
        ],
        out_specs=pl.BlockSpec((1, 8, tn), lambda l, j: (l, 0, j)),
        out_shape=jax.ShapeDtypeStruct((depth, 8, n), F32),
        compiler_params=_params("arbitrary", "arbitrary"),
        name="adaln_mod",
    )(c_pad, ada_w, ada_b.reshape(depth, 1, n))


def _norm_mod_kernel(x_ref, g_ref, sc_ref, sh_ref, *o_refs, transposed):
    x = x_ref[0]
    y = x * lax.rsqrt(jnp.mean(x * x, axis=-1, keepdims=True) + RMS_EPS) * g_ref[...]
    h = y * (1.0 + sc_ref[0]) + sh_ref[0]
    o_refs[0][0] = h.astype(o_refs[0].dtype)
    if transposed:
        o_refs[1][0] = h.T.astype(o_refs[1].dtype)


def _norm_mod_call(x, g, sc, sh, out_dtype, transposed=False):
    b, s, d = x.shape
    tr = 512
    out_shape = [jax.ShapeDtypeStruct((b, s, d), out_dtype)]
    out_specs = [pl.BlockSpec((1, tr, d), lambda i, j: (i, j, 0))]
    if transposed:
        out_shape.append(jax.ShapeDtypeStruct((b, d, s), out_dtype))
        out_specs.append(pl.BlockSpec((1, d, tr), lambda i, j: (i, 0, j)))
    return pl.pallas_call(
        functools.partial(_norm_mod_kernel, transposed=transposed),
        grid=(b, s // tr),
        in_specs=[
            pl.BlockSpec((1, tr, d), lambda i, j: (i, j, 0)),
            pl.BlockSpec((1, d), lambda i, j: (0, 0)),
            pl.BlockSpec((1, 1, d), lambda i, j: (i, 0, 0)),
            pl.BlockSpec((1, 1, d), lambda i, j: (i, 0, 0)),
        ],
        out_specs=out_specs,
        out_shape=out_shape,
        compiler_params=_params("arbitrary", "arbitrary"),
        name="norm_mod",
    )(x, g, sc, sh)


def _matmul_kernel(a_ref, w_ref, o_ref):
    o_ref[...] = jnp.dot(a_ref[...], w_ref[...], preferred_element_type=F32).astype(o_ref.dtype)


def _matmul_call(a, w, out_dtype, bm=1024, bn=1024):
    m, k = a.shape
    n = w.shape[1]
    bn = min(bn, n)
    return pl.pallas_call(
        _matmul_kernel,
        grid=(m // bm, n // bn),
        in_specs=[pl.BlockSpec((bm, k), lambda i, j: (i, 0)),
                  pl.BlockSpec((k, bn), lambda i, j: (0, j))],
        out_specs=pl.BlockSpec((bm, bn), lambda i, j: (i, j)),
        out_shape=jax.ShapeDtypeStruct((m, n), out_dtype),
        compiler_params=_params("arbitrary", "arbitrary"),
        name="matmul",
    )(a, w)


def _proj_kernel(a_ref, w_ref, *rest, shift, keep):
    o_ref, wb_ref = rest[-2:]

    @pl.when(pl.program_id(1) == 0)
    def _():
        w = w_ref[0]
        if shift:
            w = jnp.concatenate([w[:, shift:], rest[0][0][:, :shift]], axis=1)
        if keep is not None:
            lane = lax.broadcasted_iota(jnp.int32, w.shape, 1)
            w = jnp.where(lane < keep, w, 0.0)
        wb_ref[...] = w.astype(BF16)

    o_ref[...] = jnp.dot(a_ref[...], wb_ref[...], preferred_element_type=F32).astype(o_ref.dtype)


def _proj_call(a, w, layer, col0, n, out_dtype, bn, keep=None, bm=1024):
    m, k = a.shape
    shift = col0 % LANES
    base = (col0 - shift) // bn
    assert (col0 - shift) % bn == 0 and n % bn == 0 and bn % LANES == 0
    next_stride = bn // LANES
    in_specs = [pl.BlockSpec((bm, k), lambda j, i: (i, 0)),
                pl.BlockSpec((1, k, bn), lambda j, i: (layer, 0, base + j))]
    operands = [a, w]
    if shift:
        in_specs.append(pl.BlockSpec((1, k, LANES), lambda j, i: (layer, 0, (base + j + 1) * next_stride)))
        operands.append(w)
    return pl.pallas_call(
        functools.partial(_proj_kernel, shift=shift, keep=keep),
        grid=(n // bn, m // bm),
        in_specs=in_specs,
        out_specs=pl.BlockSpec((bm, bn), lambda j, i: (i, j)),
        out_shape=jax.ShapeDtypeStruct((m, n), out_dtype),
        scratch_shapes=[pltpu.VMEM((k, bn), BF16)],
        compiler_params=_params("arbitrary", "arbitrary"),
        name="proj",
    )(*operands)


def _matmul_residual_kernel(a_ref, w_ref, x_ref, gt_ref, o_ref):
    acc = jnp.dot(a_ref[0], w_ref[...], preferred_element_type=F32)
    o_ref[0] = x_ref[0] + gt_ref[0] * acc


def _matmul_residual_call(a, w, x, gt, bm=1024, bn=1024):
    b, s, k = a.shape
    n = w.shape[1]
    return pl.pallas_call(
        _matmul_residual_kernel,
        grid=(b, s // bm, n // bn),
        in_specs=[pl.BlockSpec((1, bm, k), lambda bi, i, j: (bi, i, 0)),
                  pl.BlockSpec((k, bn), lambda bi, i, j: (0, j)),
                  pl.BlockSpec((1, bm, bn), lambda bi, i, j: (bi, i, j)),
                  pl.BlockSpec((1, 1, bn), lambda bi, i, j: (bi, 0, j))],
        out_specs=pl.BlockSpec((1, bm, bn), lambda bi, i, j: (bi, i, j)),
        out_shape=jax.ShapeDtypeStruct((b, s, n), F32),
        compiler_params=_params("arbitrary", "arbitrary", "arbitrary"),
        name="matmul_residual",
    )(a, w, x, gt)


def _split3(x):
    hi = x.astype(BF16)
    r = x - hi.astype(F32)
    mid = r.astype(BF16)
    lo = (r - mid.astype(F32)).astype(BF16)
    return hi, mid, lo


def _forget_cumsum_kernel(fl_ref, bf_ref, col_ref, row_ref):
    s = fl_ref.shape[1]
    blk = 256
    logf = jax.nn.log_sigmoid(fl_ref[0] + bf_ref[...])
    r = lax.broadcasted_iota(jnp.int32, (blk, blk), 0)
    c = lax.broadcasted_iota(jnp.int32, (blk, blk), 1)
    tri = jnp.where(c <= r, 1.0, 0.0).astype(BF16)
    carry = jnp.zeros((1, LANES), F32)
    parts = []
    for i in range(s // blk):
        hi, mid, lo = _split3(logf[i * blk:(i + 1) * blk])
        cs = (jnp.dot(tri, hi, preferred_element_type=F32)
              + jnp.dot(tri, mid, preferred_element_type=F32)
              + jnp.dot(tri, lo, preferred_element_type=F32)) + carry
        carry = cs[blk - 1:blk, :]
        parts.append(cs)
    cum = jnp.concatenate(parts, axis=0)
    row_ref[0] = cum.T[:8, :]
    for h in range(FOX_HEADS):
        col_ref[0, h] = jnp.broadcast_to(cum[:, h:h + 1], (s, LANES))


def _forget_cumsum_call(f_logit, b_forget_pad):
    b, s, _ = f_logit.shape
    return pl.pallas_call(
        _forget_cumsum_kernel,
        grid=(b,),
        in_specs=[pl.BlockSpec((1, s, LANES), lambda i: (i, 0, 0)),
                  pl.BlockSpec((1, LANES), lambda i: (0, 0))],
        out_specs=[pl.BlockSpec((1, FOX_HEADS, s, LANES), lambda i: (i, 0, 0, 0)),
                   pl.BlockSpec((1, 8, s), lambda i: (i, 0, 0))],
        out_shape=[jax.ShapeDtypeStruct((b, FOX_HEADS, s, LANES), F32),
                   jax.ShapeDtypeStruct((b, 8, s), F32)],
        compiler_params=_params("arbitrary"),
        name="forget_cumsum",
    )(f_logit, b_forget_pad)


ATT_TILE = 512


def _qk(q, k):
    return lax.dot_general(q, k, (((1,), (1,)), ((), ())), preferred_element_type=F32)


def _kv_block(k_ref, v_ref, j):
    start = pl.multiple_of(j * ATT_TILE, ATT_TILE)
    return start, k_ref[pl.ds(start, ATT_TILE), :], v_ref[pl.ds(start, ATT_TILE), :]


def _tile_iotas():
    r = lax.broadcasted_iota(jnp.int32, (ATT_TILE, ATT_TILE), 0)
    c = lax.broadcasted_iota(jnp.int32, (ATT_TILE, ATT_TILE), 1)
    return r, c


def _softmax_state():
    t = ATT_TILE
    return [pltpu.VMEM((t, LANES), F32), pltpu.VMEM((t, LANES), F32), pltpu.VMEM((t, HEAD_DIM), F32)]


def _softmax_reset(m_ref, l_ref, acc_ref):
    m_ref[...] = jnp.full(m_ref.shape, NEG_INF, F32)
    l_ref[...] = jnp.zeros(l_ref.shape, F32)
    acc_ref[...] = jnp.zeros(acc_ref.shape, F32)


def _softmax_update(m_ref, l_ref, acc_ref, s, vb):
    m_old = m_ref[...]
    m_new = jnp.maximum(m_old, jnp.max(s, axis=-1, keepdims=True))
    alpha = jnp.exp(m_old - m_new)
    p = jnp.exp(s - jnp.tile(m_new, (1, s.shape[1] // LANES)))
    l_ref[...] = alpha * l_ref[...] + jnp.sum(p, axis=-1, keepdims=True)
    acc_ref[...] = alpha * acc_ref[...] + jnp.dot(p.astype(BF16), vb, preferred_element_type=F32)
    m_ref[...] = m_new


def _fox_kernel(q_ref, k_ref, v_ref, fc_ref, fr_ref, o_ref, m_ref, l_ref, acc_ref):
    t = ATT_TILE
    qi = pl.program_id(2)
    h = pl.program_id(1)
    q = (q_ref[...].astype(F32) * (HEAD_DIM ** -0.5)).astype(BF16)

    def step(j, masked):
        start, kb, vb = _kv_block(k_ref, v_ref, j)
        fcol = jnp.tile(fc_ref[0, 0], (1, t // LANES))
        frow = fr_ref[0, pl.ds(h, 1), pl.ds(start, t)]
        s = _qk(q, kb) + (fcol - frow)
        if masked:
            r, c = _tile_iotas()
            s = jnp.where(c <= r, s, NEG_INF)
        _softmax_update(m_ref, l_ref, acc_ref, s, vb)

    _softmax_reset(m_ref, l_ref, acc_ref)
    step(qi, masked=True)

    @pl.loop(0, qi)
    def _(j):
        step(j, masked=False)

    o_ref[...] = (acc_ref[...] / l_ref[...]).astype(o_ref.dtype)


def _fox_call(proj, f_col, f_row, batch, seq):
    tq = ATT_TILE
    nq = seq // tq
    return pl.pallas_call(
        _fox_kernel,
        grid=(batch, FOX_HEADS, nq),
        in_specs=[
            pl.BlockSpec((tq, HEAD_DIM), lambda b, h, i: (b * nq + i, COL_FQ + h)),
            pl.BlockSpec((seq, HEAD_DIM), lambda b, h, i: (b, COL_FK + h)),
            pl.BlockSpec((seq, HEAD_DIM), lambda b, h, i: (b, COL_FV + h)),
            pl.BlockSpec((1, 1, tq, LANES), lambda b, h, i: (b, h, i, 0)),
            pl.BlockSpec((1, 8, seq), lambda b, h, i: (b, 0, 0)),
        ],
        out_specs=pl.BlockSpec((tq, HEAD_DIM), lambda b, h, i: (b * nq + i, h)),
        out_shape=jax.ShapeDtypeStruct((batch * seq, FOX_WIDTH), BF16),
        scratch_shapes=_softmax_state(),
        compiler_params=_params("arbitrary", "arbitrary", "arbitrary"),
        name="fox_attention",
    )(proj, proj, proj, f_col, f_row)


SB_SEG = 256


def _sb_kernel(q_ref, k_ref, v_ref, tri_ref, o_ref, run_ref, acc_ref):
    t = ATT_TILE
    qi = pl.program_id(2)
    log2e = math.log2(math.e)
    qn = (q_ref[...].astype(F32) * -(HEAD_DIM ** -0.5 * log2e)).astype(BF16)

    def step(j, masked):
        _, kb, vb = _kv_block(k_ref, v_ref, j)
        tri2 = tri_ref[...]
        u = _qk(qn, kb)
        lk = jnp.minimum(u, 0.0) - jnp.log(1.0 + jnp.exp2(-jnp.abs(u))) * log2e
        if masked:
            r, c = _tile_iotas()
            strictly_causal = c < r
            lk = jnp.where(strictly_causal, lk, 0.0)
        hi = lk.astype(BF16)
        lo = (lk - hi.astype(F32)).astype(BF16)
        run = run_ref[...]
        laters = []
        for g in reversed(range(t // SB_SEG)):
            sl = slice(g * SB_SEG, (g + 1) * SB_SEG)
            hi_lo = jnp.concatenate([hi[:, sl], lo[:, sl]], axis=1)
            laters.append(jnp.dot(hi_lo, tri2, preferred_element_type=F32) + jnp.tile(run, (1, SB_SEG // LANES)))
            run = run + jnp.sum(lk[:, sl], axis=-1, keepdims=True)
        later = jnp.concatenate(laters[::-1], axis=1)
        w = jnp.exp2(lk - u + later)
        if masked:
            w = jnp.where(strictly_causal, w, 0.0)
        acc_ref[...] += jnp.dot(w.astype(BF16), vb, preferred_element_type=F32)
        run_ref[...] = run

    run_ref[...] = jnp.zeros(run_ref.shape, F32)
    acc_ref[...] = jnp.zeros(acc_ref.shape, F32)
    step(qi, masked=True)

    @pl.loop(0, qi)
    def _(i):
        step(qi - 1 - i, masked=False)

    o_ref[...] = acc_ref[...].astype(o_ref.dtype)


def _sb_call(proj, tri, batch, seq):
    tq = ATT_TILE
    nq = seq // tq
    return pl.pallas_call(
        _sb_kernel,
        grid=(batch, SB_HEADS, nq),
        in_specs=[
            pl.BlockSpec((tq, HEAD_DIM), lambda b, h, i: (b * nq + i, COL_SQ + h)),
            pl.BlockSpec((seq, HEAD_DIM), lambda b, h, i: (b, COL_SK + h)),
            pl.BlockSpec((seq, HEAD_DIM), lambda b, h, i: (b, COL_SV + h)),
            pl.BlockSpec((2 * SB_SEG, SB_SEG), lambda b, h, i: (0, 0)),
        ],
        out_specs=pl.BlockSpec((tq, HEAD_DIM), lambda b, h, i: (b * nq + i, h)),
        out_shape=jax.ShapeDtypeStruct((batch * seq, SB_WIDTH), BF16),
        scratch_shapes=[pltpu.VMEM((tq, LANES), F32), pltpu.VMEM((tq, HEAD_DIM), F32)],
        compiler_params=_params("arbitrary", "arbitrary", "arbitrary"),
        name="sb_attention",
    )(proj, proj, proj, tri)


def _rope_kernel(x_ref, pos_ref, freq_ref, o_ref, *, out_scale):
    half = DIFF_QK_DIM // 2
    ang = pos_ref[...].astype(F32) * freq_ref[...]
    cos = jnp.cos(ang)
    sin = jnp.sin(ang)
    lane = lax.broadcasted_iota(jnp.int32, ang.shape, 1)
    first = (lane % DIFF_QK_DIM) < half
    sin_signed = jnp.where(first, -sin, sin)
    for cb in range(x_ref.shape[1] // LANES):
        x = x_ref[:, cb * LANES:(cb + 1) * LANES].astype(F32)
        partner = jnp.where(first, pltpu.roll(x, LANES - half, 1), pltpu.roll(x, half, 1))
        y = x * cos + partner * sin_signed
        o_ref[:, cb * LANES:(cb + 1) * LANES] = (y * out_scale).astype(o_ref.dtype)


def _rope_call(proj, pos_col, freq_row, col_block, out_scale):
    tokens = proj.shape[0]
    tr = 512
    width = DIFF_HEADS * HEAD_DIM
    return pl.pallas_call(
        functools.partial(_rope_kernel, out_scale=out_scale),
        grid=(tokens // tr,),
        in_specs=[pl.BlockSpec((tr, width), lambda i: (i, col_block * LANES // width)),
                  pl.BlockSpec((tr, 1), lambda i: (i, 0)),
                  pl.BlockSpec((1, LANES), lambda i: (0, 0))],
        out_specs=pl.BlockSpec((tr, width), lambda i: (i, 0)),
        out_shape=jax.ShapeDtypeStruct((tokens, width), BF16),
        compiler_params=_params("arbitrary"),
        name="rope",
    )(proj, pos_col, freq_row)


def _diff_kernel(q_ref, k_ref, v_ref, lq1_ref, lk1_ref, lq2_ref, lk2_ref, g_ref, o_ref,
                 m1_ref, l1_ref, a1_ref, m2_ref, l2_ref, a2_ref, *, lam_init):
    t = ATT_TILE
    qi = pl.program_id(2)
    q = q_ref[...]
    lane = lax.broadcasted_iota(jnp.int32, q.shape, 1)
    zero = jnp.zeros_like(q)
    q1 = jnp.where(lane < DIFF_QK_DIM, q, zero)
    q2 = jnp.where(lane < DIFF_QK_DIM, zero, q)
    state1 = (m1_ref, l1_ref, a1_ref)
    state2 = (m2_ref, l2_ref, a2_ref)

    def step(j, masked):
        _, kb, vb = _kv_block(k_ref, v_ref, j)
        for qc, state in ((q1, state1), (q2, state2)):
            s = _qk(qc, kb)
            if masked:
                r, c = _tile_iotas()
                s = jnp.where((c // CHUNK) <= (r // CHUNK), s, NEG_INF)
            _softmax_update(*state, s, vb)

    _softmax_reset(*state1)
    _softmax_reset(*state2)
    step(qi, masked=True)

    @pl.loop(0, qi)
    def _(j):
        step(j, masked=False)

    lam = (jnp.exp(jnp.sum(lq1_ref[0] * lk1_ref[0], axis=-1, keepdims=True))
           - jnp.exp(jnp.sum(lq2_ref[0] * lk2_ref[0], axis=-1, keepdims=True)) + lam_init)
    o = a1_ref[...] / l1_ref[...] - lam * (a2_ref[...] / l2_ref[...])
    o = o * lax.rsqrt(jnp.mean(o * o, axis=-1, keepdims=True) + RMS_EPS) * g_ref[...]
    o_ref[...] = (o * (1.0 - lam_init)).astype(o_ref.dtype)


def _diff_call(qd, kd, proj, lam_q1, lam_k1, lam_q2, lam_k2, subln_g, lam_init, batch, seq):
    t = ATT_TILE
    nq = seq // t
    lam_spec = pl.BlockSpec((1, 1, DIFF_QK_DIM), lambda b, h, i: (h, 0, 0))
    r3 = lambda a: a.reshape(DIFF_HEADS, 1, DIFF_QK_DIM)
    return pl.pallas_call(
        functools.partial(_diff_kernel, lam_init=lam_init),
        grid=(batch, DIFF_HEADS, nq),
        in_specs=[
            pl.BlockSpec((t, HEAD_DIM), lambda b, h, i: (b * nq + i, h)),
            pl.BlockSpec((seq, HEAD_DIM), lambda b, h, i: (b, h)),
            pl.BlockSpec((seq, HEAD_DIM), lambda b, h, i: (b, COL_DV + h)),
            lam_spec, lam_spec, lam_spec, lam_spec,
            pl.BlockSpec((1, HEAD_DIM), lambda b, h, i: (0, 0)),
        ],
        out_specs=pl.BlockSpec((t, HEAD_DIM), lambda b, h, i: (b * nq + i, h)),
        out_shape=jax.ShapeDtypeStruct((batch * seq, DIFF_WIDTH), BF16),
        scratch_shapes=_softmax_state() + _softmax_state(),
        compiler_params=_params("arbitrary", "arbitrary", "arbitrary"),
        name="diff_attention",
    )(qd, kd, proj, r3(lam_q1), r3(lam_k1), r3(lam_q2), r3(lam_k2), subln_g.reshape(1, HEAD_DIM))


def _merge_kernel(of_ref, os_ref, od_ref, wf_ref, ws_ref, wd_ref, gf_ref, gs_ref, gd_ref, o_ref):
    def branch(o_r, w_r, g_r):
        return jax.nn.sigmoid(g_r[...].astype(F32)) * jnp.dot(o_r[...], w_r[...], preferred_element_type=F32)

    y = branch(of_ref, wf_ref, gf_ref) + branch(os_ref, ws_ref, gs_ref) + branch(od_ref, wd_ref, gd_ref)
    o_ref[...] = y.astype(o_ref.dtype)


def _merge_call(o_fox, o_sb, o_diff, w_fox, w_sb, w_diff, gates, bm=1024, bn=1024):
    m = o_fox.shape[0]
    d = w_fox.shape[1]
    gate_blocks = d // bn

    def gate_spec(k):
        return pl.BlockSpec((bm, bn), lambda i, j: (i, k * gate_blocks + j))

    return pl.pallas_call(
        _merge_kernel,
        grid=(m // bm, d // bn),
        in_specs=[
            pl.BlockSpec((bm, FOX_WIDTH), lambda i, j: (i, 0)),
            pl.BlockSpec((bm, SB_WIDTH), lambda i, j: (i, 0)),
            pl.BlockSpec((bm, DIFF_WIDTH), lambda i, j: (i, 0)),
            pl.BlockSpec((FOX_WIDTH, bn), lambda i, j: (0, j)),
            pl.BlockSpec((SB_WIDTH, bn), lambda i, j: (0, j)),
            pl.BlockSpec((DIFF_WIDTH, bn), lambda i, j: (0, j)),
            gate_spec(0), gate_spec(1), gate_spec(2),
        ],
        out_specs=pl.BlockSpec((bm, bn), lambda i, j: (i, j)),
        out_shape=jax.ShapeDtypeStruct((m, d), BF16),
        compiler_params=_params("arbitrary", "arbitrary"),
        name="branch_merge",
    )(o_fox, o_sb, o_diff, w_fox, w_sb, w_diff, gates, gates, gates)


NOT_TOP = 64.0


def _extract_top(cur, count, want_rank=False):
    rows = []
    rank = jnp.full(cur.shape, NOT_TOP, F32) if want_rank else None
    for r in range(count):
        m = jnp.max(cur, axis=0, keepdims=True)
        rows.append(m)
        hit = cur == m
        if want_rank:
            rank = jnp.where(hit, float(r), rank)
        cur = jnp.where(hit, NEG_INF, cur)
    return rows, rank


def _peer_topk_kernel(q_ref, keys_ref, a_ref, l_ref, b_ref, r2_ref):
    k = PEER_TOPK
    for h in range(PEER_HEADS):
        col = 2 * h * PEER_HALF
        s1 = _qk(keys_ref[h, 0], q_ref[:, col:col + PEER_HALF])
        s2 = _qk(keys_ref[h, 1], q_ref[:, col + PEER_HALF:col + 2 * PEER_HALF])
        a_rows, _ = _extract_top(s1, k)
        b_rows, rank2 = _extract_top(s2, k, want_rank=True)
        b_top = jnp.concatenate(b_rows, axis=0)
        counts = [k // (k1 + 1) for k1 in range(k)]
        cands = [a_rows[k1] + b_top[:counts[k1]] for k1 in range(k)]
        n_cand = sum(counts)
        pad = jnp.full((-n_cand % 8, b_top.shape[1]), NEG_INF, F32)
        best, _ = _extract_top(jnp.concatenate(cands + [pad], axis=0), k)
        tau = best[k - 1]
        z = jnp.zeros_like(tau)
        for row in best:
            z = z + jnp.exp(row - best[0])
        l_row = jnp.zeros(s1.shape, F32)
        for k1 in range(k):
            l_k = jnp.sum(jnp.where(cands[k1] >= tau, 1.0, 0.0), axis=0, keepdims=True)
            l_row = jnp.where(s1 == a_rows[k1], l_k, l_row)
        a_ref[h] = jnp.exp(s1 - a_rows[0]) / z
        l_ref[h] = l_row
        b_ref[h] = jnp.exp(s2 - b_rows[0]).astype(b_ref.dtype)
        r2_ref[h] = rank2.astype(r2_ref.dtype)


def _peer_topk_call(qp, keys):
    tokens = qp.shape[0]
    tq = 512
    spec = pl.BlockSpec((PEER_HEADS, PEER_N_KEYS, tq), lambda i: (0, 0, i))
    shape = lambda dt: jax.ShapeDtypeStruct((PEER_HEADS, PEER_N_KEYS, tokens), dt)
    return pl.pallas_call(
        _peer_topk_kernel,
        grid=(tokens // tq,),
        in_specs=[pl.BlockSpec((tq, qp.shape[1]), lambda i: (i, 0)),
                  pl.BlockSpec(keys.shape, lambda i: (0, 0, 0, 0))],
        out_specs=[spec, spec, spec, spec],
        out_shape=[shape(F32), shape(F32), shape(BF16), shape(BF16)],
        compiler_params=_params("arbitrary"),
        name="peer_topk",
    )(qp, keys)


PEER_UP_TOKENS = 1024
PEER_UP_EXPERTS = 1024
PEER_TOKEN_CHUNK = 256
PEER_DOWN_TOKENS = 512
PEER_DOWN_ROWS = 8


def _peer_up_kernel(ht_ref, u_ref, o_ref):
    tc = PEER_TOKEN_CHUNK
    u = u_ref[0].astype(BF16)
    for c in range(o_ref.shape[1] // tc):
        cols = slice(c * tc, (c + 1) * tc)
        hid = jnp.dot(u, ht_ref[0, :, cols], preferred_element_type=F32)
        gelu = 0.5 * hid * (1.0 + lax.erf(hid * (2.0 ** -0.5)))
        o_ref[:, cols] = gelu.astype(o_ref.dtype)


def _peer_down_kernel(vt_ref, g_ref, a_ref, l_ref, b_ref, r2_ref, x_ref, gt_ref, o_ref, acc_ref, act_ref):
    k = pl.program_id(1)
    tn = act_ref.shape[1]

    @pl.when(k == 0)
    def _():
        acc_ref[...] = jnp.zeros_like(acc_ref)

    for r in range(PEER_DOWN_ROWS):
        rows = slice(r * PEER_N_KEYS, (r + 1) * PEER_N_KEYS)
        i = k * PEER_DOWN_ROWS + r

        def key_row(ref, h):
            tile = jnp.broadcast_to(ref[h, pl.ds(i, 1), :], (BF16_ROWS, tn)).astype(BF16)
            return jnp.tile(tile, (PEER_N_KEYS // BF16_ROWS, 1))

        gate = jnp.zeros((PEER_N_KEYS, tn), BF16)
        for h in range(PEER_HEADS):
            chosen = r2_ref[h] < key_row(l_ref, h)
            gate = gate + jnp.where(chosen, b_ref[h], jnp.zeros((), BF16)) * key_row(a_ref, h)
        act_ref[rows, :] = gate * g_ref[rows, :]

    acc_ref[...] += jnp.dot(vt_ref[...], act_ref[...], preferred_element_type=F32)

    @pl.when(k == pl.num_programs(1) - 1)
    def _():
        o_ref[0] = x_ref[0] + gt_ref[0] * acc_ref[...].T


def _peer_ffn_call(ht, u, layer, vt, a_t, l_t, b_t, r2_t, x, gt):
    b, d, s = ht.shape
    n_experts = u.shape[1]
    tm, te = PEER_UP_TOKENS, PEER_UP_EXPERTS
    per_b = s // tm
    gelu_t = pl.pallas_call(
        _peer_up_kernel,
        grid=(b * per_b, n_experts // te),
        in_specs=[pl.BlockSpec((1, d, tm), lambda i, e: (i // per_b, 0, i % per_b)),
                  pl.BlockSpec((1, te, d), lambda i, e: (layer, e, 0))],
        out_specs=pl.BlockSpec((te, tm), lambda i, e: (e, i)),
        out_shape=jax.ShapeDtypeStruct((n_experts, b * s), BF16),
        compiler_params=_params("arbitrary", "arbitrary"),
        name="peer_up",
    )(ht, u)

    tn, tk = PEER_DOWN_TOKENS, PEER_DOWN_ROWS * PEER_N_KEYS
    per_b = s // tn
    key_spec = pl.BlockSpec((PEER_HEADS, PEER_N_KEYS, tn), lambda i, k: (0, 0, i))
    return pl.pallas_call(
        _peer_down_kernel,
        grid=(b * per_b, n_experts // tk),
        in_specs=[
            pl.BlockSpec((d, tk), lambda i, k: (0, k)),
            pl.BlockSpec((tk, tn), lambda i, k: (k, i)),
            key_spec, key_spec, key_spec, key_spec,
            pl.BlockSpec((1, tn, d), lambda i, k: (i // per_b, i % per_b, 0)),
            pl.BlockSpec((1, 1, d), lambda i, k: (i // per_b, 0, 0)),
        ],
        out_specs=pl.BlockSpec((1, tn, d), lambda i, k: (i // per_b, i % per_b, 0)),
        out_shape=jax.ShapeDtypeStruct(x.shape, F32),
        scratch_shapes=[pltpu.VMEM((d, tn), F32), pltpu.VMEM((tk, tn), BF16)],
        compiler_params=_params("arbitrary", "arbitrary"),
        name="peer_down",
    )(vt, gelu_t, a_t, l_t, b_t, r2_t, x, gt)


def _peer_vt_kernel(v_ref, vt_ref):
    vt_ref[...] = v_ref[0].T.astype(BF16)


def _peer_vt_call(v, layer):
    _, e, d = v.shape
    te = 512
    return pl.pallas_call(
        _peer_vt_kernel,
        grid=(e // te,),
        in_specs=[pl.BlockSpec((1, te, d), lambda i: (layer, i, 0))],
        out_specs=pl.BlockSpec((d, te), lambda i: (0, i)),
        out_shape=jax.ShapeDtypeStruct((d, e), BF16),
        compiler_params=_params("arbitrary"),
        name="peer_vt",
    )(v)


def kernel(x, c, positions, ada_w, ada_b, norm1_g, norm2_g, w_in, b_forget, lam_q1, lam_k1, lam_q2, lam_k2,
           diff_subln_g, w_br_fox, w_br_sb, w_br_diff, w_o, peer_wq, peer_sub_keys, peer_u, peer_v,
           final_norm_g):
    batch, seq, d = x.shape
    tokens = batch * seq
    assert batch <= 8 and seq % 1024 == 0 and d == D_MODEL

    c_pad = jnp.zeros((8, d), F32).at[:batch].set(c)
    mod = _mod_call(c_pad, ada_w, ada_b)[:, :batch].reshape(DEPTH, batch, N_MOD, 1, d)
    pos_col = positions.reshape(tokens, 1)
    inv_freq = ROPE_THETA ** (-jnp.arange(0, DIFF_QK_DIM, 2, dtype=F32) / DIFF_QK_DIM)
    freq_row = jnp.tile(inv_freq, LANES // inv_freq.shape[0]).reshape(1, LANES)
    ri = np.arange(SB_SEG)
    tri_after = jnp.asarray(np.tile(ri[:, None] > ri[None, :], (2, 1)), BF16)

    for l in range(DEPTH):
        lam_init = 0.8 - 0.6 * math.exp(-0.3 * l)
        sh1, sc1, gt1, sh2, sc2, gt2 = (mod[l, :, k] for k in range(N_MOD))

        h1 = _norm_mod_call(x, norm1_g[l].reshape(1, d), sc1, sh1, BF16)[0].reshape(tokens, d)
        proj = _proj_call(h1, w_in, l, 0, QKV_WIDTH, BF16, bn=1024)
        gates = _proj_call(h1, w_in, l, GATE_OFFSET, N_BRANCHES * d, BF16, bn=1024)
        f_logit = _proj_call(h1, w_in, l, QKV_WIDTH, LANES, F32, bn=LANES, keep=FOX_HEADS)
        f_logit = f_logit.reshape(batch, seq, LANES)
        b_forget_pad = jnp.pad(b_forget[l], (0, LANES - FOX_HEADS)).reshape(1, LANES)
        f_col, f_row = _forget_cumsum_call(f_logit, b_forget_pad)

        o_fox = _fox_call(proj, f_col, f_row, batch, seq)
        o_sb = _sb_call(proj, tri_after, batch, seq)
        qd = _rope_call(proj, pos_col, freq_row, COL_DQ, DIFF_QK_DIM ** -0.5)
        kd = _rope_call(proj, pos_col, freq_row, COL_DK, 1.0)
        o_diff = _diff_call(qd, kd, proj, lam_q1[l], lam_k1[l], lam_q2[l], lam_k2[l], diff_subln_g[l],
                            lam_init, batch, seq)

        y = _merge_call(o_fox, o_sb, o_diff, w_br_fox[l].astype(BF16), w_br_sb[l].astype(BF16),
                        w_br_diff[l].astype(BF16), gates)
        x = _matmul_residual_call(y.reshape(batch, seq, d), w_o[l].astype(BF16), x, gt1)

        h2, h2t = _norm_mod_call(x, norm2_g[l].reshape(1, d), sc2, sh2, BF16, transposed=True)
        qp = _proj_call(h2.reshape(tokens, d), peer_wq, l, 0, peer_wq.shape[2], BF16, bn=1024)
        a_t, l_t, b_t, r2_t = _peer_topk_call(qp, peer_sub_keys[l].astype(BF16))
        x = _peer_ffn_call(h2t, peer_u, l, _peer_vt_call(peer_v, l), a_t, l_t, b_t, r2_t, x, gt2)

    zeros = jnp.zeros((batch, 1, d), F32)
    return _norm_mod_call(x, final_norm_g.reshape(1, d), zeros, zeros, F32)[0]
```

```python
import functools
import math

import jax
import jax.numpy as jnp
import numpy as np
from jax import lax
from jax.experimental import pallas as pl
from jax.experimental.pallas import tpu as pltpu

F32 = jnp.float32
BF16 = jnp.bfloat16

D_MODEL = 2048
DEPTH = 2
CHUNK = 64
HEAD_DIM = 128
FOX_HEADS = 6
SB_HEADS = 6
DIFF_HEADS = 4
DIFF_QK_DIM = HEAD_DIM // 2
FOX_WIDTH = FOX_HEADS * HEAD_DIM
SB_WIDTH = SB_HEADS * HEAD_DIM
DIFF_WIDTH = DIFF_HEADS * HEAD_DIM
QKV_WIDTH = 3 * FOX_WIDTH + 3 * SB_WIDTH + 3 * DIFF_WIDTH
GATE_OFFSET = QKV_WIDTH + FOX_HEADS
ROPE_THETA = 10000.0
RMS_EPS = 1e-6
N_MOD = 6
N_BRANCHES = 3
PEER_HEADS = 8
PEER_N_KEYS = 128
PEER_TOPK = 16
PEER_HALF = 128

LANES = 128
BF16_ROWS = 16
VMEM_LIMIT = 56 * 1024 * 1024
NEG_INF = float("-inf")

COL_FQ, COL_FK, COL_FV = 0, FOX_HEADS, 2 * FOX_HEADS
COL_SQ, COL_SK, COL_SV = 18, 24, 30
COL_DQ, COL_DK, COL_DV = 36, 40, 44
COL_GATE = QKV_WIDTH // LANES


def _params(*sem):
    return pltpu.CompilerParams(dimension_semantics=sem, vmem_limit_bytes=VMEM_LIMIT)


def _mod_kernel(c_ref, w_ref, b_ref, o_ref):
    c = c_ref[...]
    cond = c * jax.nn.sigmoid(c)
    hi = cond.astype(BF16)
    lo = (cond - hi.astype(F32)).astype(BF16)
    w = w_ref[0].astype(BF16)
    acc = jnp.dot(hi, w, preferred_element_type=F32) + jnp.dot(lo, w, preferred_element_type=F32)
    o_ref[0] = acc + b_ref[0]


def _mod_call(c_pad, ada_w, ada_b):
    depth, d, n = ada_w.shape
    tn = 1024
    return pl.pallas_call(
        _mod_kernel,
        grid=(depth, n // tn),
        in_specs=[
            pl.BlockSpec((8, d), lambda l, j: (0, 0)),
            pl.BlockSpec((1, d, tn), lambda l, j: (l, 0, j)),
            pl.BlockSpec((1, 1, tn), lambda l, j: (l, 0, j)),
        ],
        out_specs=pl.BlockSpec((1, 8, tn), lambda l, j: (l, 0, j)),
        out_shape=jax.ShapeDtypeStruct((depth, 8, n), F32),
        compiler_params=_params("arbitrary", "arbitrary"),
        name="adaln_mod",
    )(c_pad, ada_w, ada_b.reshape(depth, 1, n))


def _norm_mod_kernel(x_ref, g_ref, sc_ref, sh_ref, *o_refs, transposed):
    x = x_ref[0]
    y = x * lax.rsqrt(jnp.mean(x * x, axis=-1, keepdims=True) + RMS_EPS) * g_ref[...]
    h = y * (1.0 + sc_ref[0]) + sh_ref[0]
    o_refs[0][0] = h.astype(o_refs[0].dtype)
    if transposed:
        o_refs[1][0] = h.T.astype(o_refs[1].dtype)


def _norm_mod_call(x, g, sc, sh, out_dtype, transposed=False):
    b, s, d = x.shape
    tr = 512
    out_shape = [jax.ShapeDtypeStruct((b, s, d), out_dtype)]
    out_specs = [pl.BlockSpec((1, tr, d), lambda i, j: (i, j, 0))]
    if transposed:
        out_shape.append(jax.ShapeDtypeStruct((b, d, s), out_dtype))
        out_specs.append(pl.BlockSpec((1, d, tr), lambda i, j: (i, 0, j)))
    return pl.pallas_call(
        functools.partial(_norm_mod_kernel, transposed=transposed),
        grid=(b, s // tr),
        in_specs=[
            pl.BlockSpec((1, tr, d), lambda i, j: (i, j, 0)),
            pl.BlockSpec((1, d), lambda i, j: (0, 0)),
            pl.BlockSpec((1, 1, d), lambda i, j: (i, 0, 0)),
            pl.BlockSpec((1, 1, d), lambda i, j: (i, 0, 0)),
        ],
        out_specs=out_specs,
        out_shape=out_shape,
        compiler_params=_params("arbitrary", "arbitrary"),
        name="norm_mod",
    )(x, g, sc, sh)


def _matmul_kernel(a_ref, w_ref, o_ref):
    o_ref[...] = jnp.dot(a_ref[...], w_ref[...], preferred_element_type=F32).astype(o_ref.dtype)


def _matmul_call(a, w, out_dtype, bm=1024, bn=1024):
    m, k = a.shape
    n = w.shape[1]
    bn = min(bn, n)
    return pl.pallas_call(
        _matmul_kernel,
        grid=(m // bm, n // bn),
        in_specs=[pl.BlockSpec((bm, k), lambda i, j: (i, 0)),
                  pl.BlockSpec((k, bn), lambda i, j: (0, j))],
        out_specs=pl.BlockSpec((bm, bn), lambda i, j: (i, j)),
        out_shape=jax.ShapeDtypeStruct((m, n), out_dtype),
        compiler_params=_params("arbitrary", "arbitrary"),
        name="matmul",
    )(a, w)


def _proj_kernel(a_ref, w_ref, *rest, shift, keep):
    o_ref, wb_ref = rest[-2:]

    @pl.when(pl.program_id(1) == 0)
    def _():
        w = w_ref[0]
        if shift:
            w = jnp.concatenate([w[:, shift:], rest[0][0][:, :shift]], axis=1)
        if keep is not None:
            lane = lax.broadcasted_iota(jnp.int32, w.shape, 1)
            w = jnp.where(lane < keep, w, 0.0)
        wb_ref[...] = w.astype(BF16)

    o_ref[...] = jnp.dot(a_ref[...], wb_ref[...], preferred_element_type=F32).astype(o_ref.dtype)


def _proj_call(a, w, layer, col0, n, out_dtype, bn, keep=None, bm=1024):
    m, k = a.shape
    shift = col0 % LANES
    base = (col0 - shift) // bn
    assert (col0 - shift) % bn == 0 and n % bn == 0 and bn % LANES == 0
    next_stride = bn // LANES
    in_specs = [pl.BlockSpec((bm, k), lambda j, i: (i, 0)),
                pl.BlockSpec((1, k, bn), lambda j, i: (layer, 0, base + j))]
    operands = [a, w]
    if shift:
        in_specs.append(pl.BlockSpec((1, k, LANES), lambda j, i: (layer, 0, (base + j + 1) * next_stride)))
        operands.append(w)
    return pl.pallas_call(
        functools.partial(_proj_kernel, shift=shift, keep=keep),
        grid=(n // bn, m // bm),
        in_specs=in_specs,
        out_specs=pl.BlockSpec((bm, bn), lambda j, i: (i, j)),
        out_shape=jax.ShapeDtypeStruct((m, n), out_dtype),
        scratch_shapes=[pltpu.VMEM((k, bn), BF16)],
        compiler_params=_params("arbitrary", "arbitrary"),
        name="proj",
    )(*operands)


def _matmul_residual_kernel(a_ref, w_ref, x_ref, gt_ref, o_ref):
    acc = jnp.dot(a_ref[0], w_ref[...], preferred_element_type=F32)
    o_ref[0] = x_ref[0] + gt_ref[0] * acc


def _matmul_residual_call(a, w, x, gt, bm=1024, bn=1024):
    b, s, k = a.shape
    n = w.shape[1]
    return pl.pallas_call(
        _matmul_residual_kernel,
        grid=(b, s // bm, n // bn),
        in_specs=[pl.BlockSpec((1, bm, k), lambda bi, i, j: (bi, i, 0)),
                  pl.BlockSpec((k, bn), lambda bi, i, j: (0, j)),
                  pl.BlockSpec((1, bm, bn), lambda bi, i, j: (bi, i, j)),
                  pl.BlockSpec((1, 1, bn), lambda bi, i, j: (bi, 0, j))],
        out_specs=pl.BlockSpec((1, bm, bn), lambda bi, i, j: (bi, i, j)),
        out_shape=jax.ShapeDtypeStruct((b, s, n), F32),
        compiler_params=_params("arbitrary", "arbitrary", "arbitrary"),
        name="matmul_residual",
    )(a, w, x, gt)


def _split3(x):
    hi = x.astype(BF16)
    r = x - hi.astype(F32)
    mid = r.astype(BF16)
    lo = (r - mid.astype(F32)).astype(BF16)
    return hi, mid, lo


def _forget_cumsum_kernel(fl_ref, bf_ref, col_ref, row_ref):
    s = fl_ref.shape[1]
    blk = 256
    logf = jax.nn.log_sigmoid(fl_ref[0] + bf_ref[...])
    r = lax.broadcasted_iota(jnp.int32, (blk, blk), 0)
    c = lax.broadcasted_iota(jnp.int32, (blk, blk), 1)
    tri = jnp.where(c <= r, 1.0, 0.0).astype(BF16)
    carry = jnp.zeros((1, LANES), F32)
    parts = []
    for i in range(s // blk):
        hi, mid, lo = _split3(logf[i * blk:(i + 1) * blk])
        cs = (jnp.dot(tri, hi, preferred_element_type=F32)
              + jnp.dot(tri, mid, preferred_element_type=F32)
              + jnp.dot(tri, lo, preferred_element_type=F32)) + carry
        carry = cs[blk - 1:blk, :]
        parts.append(cs)
    cum = jnp.concatenate(parts, axis=0)
    row_ref[0] = cum.T[:8, :]
    for h in range(FOX_HEADS):
        col_ref[0, h] = jnp.broadcast_to(cum[:, h:h + 1], (s, LANES))


def _forget_cumsum_call(f_logit, b_forget_pad):
    b, s, _ = f_logit.shape
    return pl.pallas_call(
        _forget_cumsum_kernel,
        grid=(b,),
        in_specs=[pl.BlockSpec((1, s, LANES), lambda i: (i, 0, 0)),
                  pl.BlockSpec((1, LANES), lambda i: (0, 0))],
        out_specs=[pl.BlockSpec((1, FOX_HEADS, s, LANES), lambda i: (i, 0, 0, 0)),
                   pl.BlockSpec((1, 8, s), lambda i: (i, 0, 0))],
        out_shape=[jax.ShapeDtypeStruct((b, FOX_HEADS, s, LANES), F32),
                   jax.ShapeDtypeStruct((b, 8, s), F32)],
        compiler_params=_params("arbitrary"),
        name="forget_cumsum",
    )(f_logit, b_forget_pad)


ATT_TILE = 512


def _qk(q, k):
    return lax.dot_general(q, k, (((1,), (1,)), ((), ())), preferred_element_type=F32)


def _kv_block(k_ref, v_ref, j):
    start = pl.multiple_of(j * ATT_TILE, ATT_TILE)
    return start, k_ref[pl.ds(start, ATT_TILE), :], v_ref[pl.ds(start, ATT_TILE), :]


def _tile_iotas():
    r = lax.broadcasted_iota(jnp.int32, (ATT_TILE, ATT_TILE), 0)
    c = lax.broadcasted_iota(jnp.int32, (ATT_TILE, ATT_TILE), 1)
    return r, c


def _softmax_state():
    t = ATT_TILE
    return [pltpu.VMEM((t, LANES), F32), pltpu.VMEM((t, LANES), F32), pltpu.VMEM((t, HEAD_DIM), F32)]


def _softmax_reset(m_ref, l_ref, acc_ref):
    m_ref[...] = jnp.full(m_ref.shape, NEG_INF, F32)
    l_ref[...] = jnp.zeros(l_ref.shape, F32)
    acc_ref[...] = jnp.zeros(acc_ref.shape, F32)


def _softmax_update(m_ref, l_ref, acc_ref, s, vb):
    m_old = m_ref[...]
    m_new = jnp.maximum(m_old, jnp.max(s, axis=-1, keepdims=True))
    alpha = jnp.exp(m_old - m_new)
    p = jnp.exp(s - jnp.tile(m_new, (1, s.shape[1] // LANES)))
    l_ref[...] = alpha * l_ref[...] + jnp.sum(p, axis=-1, keepdims=True)
    acc_ref[...] = alpha * acc_ref[...] + jnp.dot(p.astype(BF16), vb, preferred_element_type=F32)
    m_ref[...] = m_new


def _fox_kernel(q_ref, k_ref, v_ref, fc_ref, fr_ref, o_ref, m_ref, l_ref, acc_ref):
    t = ATT_TILE
    qi = pl.program_id(2)
    h = pl.program_id(1)
    q = (q_ref[...].astype(F32) * (HEAD_DIM ** -0.5)).astype(BF16)

    def step(j, masked):
        start, kb, vb = _kv_block(k_ref, v_ref, j)
        fcol = jnp.tile(fc_ref[0, 0], (1, t // LANES))
        frow = fr_ref[0, pl.ds(h, 1), pl.ds(start, t)]
        s = _qk(q, kb) + (fcol - frow)
        if masked:
            r, c = _tile_iotas()
            s = jnp.where(c <= r, s, NEG_INF)
        _softmax_update(m_ref, l_ref, acc_ref, s, vb)

    _softmax_reset(m_ref, l_ref, acc_ref)
    step(qi, masked=True)

    @pl.loop(0, qi)
    def _(j):
        step(j, masked=False)

    o_ref[...] = (acc_ref[...] / l_ref[...]).astype(o_ref.dtype)


def _fox_call(proj, f_col, f_row, batch, seq):
    tq = ATT_TILE
    nq = seq // tq
    return pl.pallas_call(
        _fox_kernel,
        grid=(batch, FOX_HEADS, nq),
        in_specs=[
            pl.BlockSpec((tq, HEAD_DIM), lambda b, h, i: (b * nq + i, COL_FQ + h)),
            pl.BlockSpec((seq, HEAD_DIM), lambda b, h, i: (b, COL_FK + h)),
            pl.BlockSpec((seq, HEAD_DIM), lambda b, h, i: (b, COL_FV + h)),
            pl.BlockSpec((1, 1, tq, LANES), lambda b, h, i: (b, h, i, 0)),
            pl.BlockSpec((1, 8, seq), lambda b, h, i: (b, 0, 0)),
        ],
        out_specs=pl.BlockSpec((tq, HEAD_DIM), lambda b, h, i: (b * nq + i, h)),
        out_shape=jax.ShapeDtypeStruct((batch * seq, FOX_WIDTH), BF16),
        scratch_shapes=_softmax_state(),
        compiler_params=_params("arbitrary", "arbitrary", "arbitrary"),
        name="fox_attention",
    )(proj, proj, proj, f_col, f_row)


SB_SEG = 256


def _sb_kernel(q_ref, k_ref, v_ref, tri_ref, o_ref, run_ref, acc_ref):
    t = ATT_TILE
    qi = pl.program_id(2)
    log2e = math.log2(math.e)
    qn = (q_ref[...].astype(F32) * -(HEAD_DIM ** -0.5 * log2e)).astype(BF16)

    def step(j, masked):
        _, kb, vb = _kv_block(k_ref, v_ref, j)
        tri2 = tri_ref[...]
        u = _qk(qn, kb)
        lk = jnp.minimum(u, 0.0) - jnp.log(1.0 + jnp.exp2(-jnp.abs(u))) * log2e
        if masked:
            r, c = _tile_iotas()
            strictly_causal = c < r
            lk = jnp.where(strictly_causal, lk, 0.0)
        hi = lk.astype(BF16)
        lo = (lk - hi.astype(F32)).astype(BF16)
        run = run_ref[...]
        laters = []
        for g in reversed(range(t // SB_SEG)):
            sl = slice(g * SB_SEG, (g + 1) * SB_SEG)
            hi_lo = jnp.concatenate([hi[:, sl], lo[:, sl]], axis=1)
            laters.append(jnp.dot(hi_lo, tri2, preferred_element_type=F32) + jnp.tile(run, (1, SB_SEG // LANES)))
            run = run + jnp.sum(lk[:, sl], axis=-1, keepdims=True)
        later = jnp.concatenate(laters[::-1], axis=1)
        w = jnp.exp2(lk - u + later)
        if masked:
            w = jnp.where(strictly_causal, w, 0.0)
        acc_ref[...] += jnp.dot(w.astype(BF16), vb, preferred_element_type=F32)
        run_ref[...] = run

    run_ref[...] = jnp.zeros(run_ref.shape, F32)
    acc_ref[...] = jnp.zeros(acc_ref.shape, F32)
    step(qi, masked=True)

    @pl.loop(0, qi)
    def _(i):
        step(qi - 1 - i, masked=False)

    o_ref[...] = acc_ref[...].astype(o_ref.dtype)


def _sb_call(proj, tri, batch, seq):
    tq = ATT_TILE
    nq = seq // tq
    return pl.pallas_call(
        _sb_kernel,
        grid=(batch, SB_HEADS, nq),
        in_specs=[
            pl.BlockSpec((tq, HEAD_DIM), lambda b, h, i: (b * nq + i, COL_SQ + h)),
            pl.BlockSpec((seq, HEAD_DIM), lambda b, h, i: (b, COL_SK + h)),
            pl.BlockSpec((seq, HEAD_DIM), lambda b, h, i: (b, COL_SV + h)),
            pl.BlockSpec((2 * SB_SEG, SB_SEG), lambda b, h, i: (0, 0)),
        ],
        out_specs=pl.BlockSpec((tq, HEAD_DIM), lambda b, h, i: (b * nq + i, h)),
        out_shape=jax.ShapeDtypeStruct((batch * seq, SB_WIDTH), BF16),
        scratch_shapes=[pltpu.VMEM((tq, LANES), F32), pltpu.VMEM((tq, HEAD_DIM), F32)],
        compiler_params=_params("arbitrary", "arbitrary", "arbitrary"),
        name="sb_attention",
    )(proj, proj, proj, tri)


def _rope_kernel(x_ref, pos_ref, freq_ref, o_ref, *, out_scale):
    half = DIFF_QK_DIM // 2
    ang = pos_ref[...].astype(F32) * freq_ref[...]
    cos = jnp.cos(ang)
    sin = jnp.sin(ang)
    lane = lax.broadcasted_iota(jnp.int32, ang.shape, 1)
    first = (lane % DIFF_QK_DIM) < half
    sin_signed = jnp.where(first, -sin, sin)
    for cb in range(x_ref.shape[1] // LANES):
        x = x_ref[:, cb * LANES:(cb + 1) * LANES].astype(F32)
        partner = jnp.where(first, pltpu.roll(x, LANES - half, 1), pltpu.roll(x, half, 1))
        y = x * cos + partner * sin_signed
        o_ref[:, cb * LANES:(cb + 1) * LANES] = (y * out_scale).astype(o_ref.dtype)


def _rope_call(proj, pos_col, freq_row, col_block, out_scale):
    tokens = proj.shape[0]
    tr = 512
    width = DIFF_HEADS * HEAD_DIM
    return pl.pallas_call(
        functools.partial(_rope_kernel, out_scale=out_scale),
        grid=(tokens // tr,),
        in_specs=[pl.BlockSpec((tr, width), lambda i: (i, col_block * LANES // width)),
                  pl.BlockSpec((tr, 1), lambda i: (i, 0)),
                  pl.BlockSpec((1, LANES), lambda i: (0, 0))],
        out_specs=pl.BlockSpec((tr, width), lambda i: (i, 0)),
        out_shape=jax.ShapeDtypeStruct((tokens, width), BF16),
        compiler_params=_params("arbitrary"),
        name="rope",
    )(proj, pos_col, freq_row)


def _diff_kernel(q_ref, k_ref, v_ref, lq1_ref, lk1_ref, lq2_ref, lk2_ref, g_ref, o_ref,
                 m1_ref, l1_ref, a1_ref, m2_ref, l2_ref, a2_ref, *, lam_init):
    t = ATT_TILE
    qi = pl.program_id(2)
    q = q_ref[...]
    lane = lax.broadcasted_iota(jnp.int32, q.shape, 1)
    zero = jnp.zeros_like(q)
    q1 = jnp.where(lane < DIFF_QK_DIM, q, zero)
    q2 = jnp.where(lane < DIFF_QK_DIM, zero, q)
    state1 = (m1_ref, l1_ref, a1_ref)
    state2 = (m2_ref, l2_ref, a2_ref)

    def step(j, masked):
        _, kb, vb = _kv_block(k_ref, v_ref, j)
        for qc, state in ((q1, state1), (q2, state2)):
            s = _qk(qc, kb)
            if masked:
                r, c = _tile_iotas()
                s = jnp.where((c // CHUNK) <= (r // CHUNK), s, NEG_INF)
            _softmax_update(*state, s, vb)

    _softmax_reset(*state1)
    _softmax_reset(*state2)
    step(qi, masked=True)

    @pl.loop(0, qi)
    def _(j):
        step(j, masked=False)

    lam = (jnp.exp(jnp.sum(lq1_ref[0] * lk1_ref[0], axis=-1, keepdims=True))
           - jnp.exp(jnp.sum(lq2_ref[0] * lk2_ref[0], axis=-1, keepdims=True)) + lam_init)
    o = a1_ref[...] / l1_ref[...] - lam * (a2_ref[...] / l2_ref[...])
    o = o * lax.rsqrt(jnp.mean(o * o, axis=-1, keepdims=True) + RMS_EPS) * g_ref[...]
    o_ref[...] = (o * (1.0 - lam_init)).astype(o_ref.dtype)


def _diff_call(qd, kd, proj, lam_q1, lam_k1, lam_q2, lam_k2, subln_g, lam_init, batch, seq):
    t = ATT_TILE
    nq = seq // t
    lam_spec = pl.BlockSpec((1, 1, DIFF_QK_DIM), lambda b, h, i: (h, 0, 0))
    r3 = lambda a: a.reshape(DIFF_HEADS, 1, DIFF_QK_DIM)
    return pl.pallas_call(
        functools.partial(_diff_kernel, lam_init=lam_init),
        grid=(batch, DIFF_HEADS, nq),
        in_specs=[
            pl.BlockSpec((t, HEAD_DIM), lambda b, h, i: (b * nq + i, h)),
            pl.BlockSpec((seq, HEAD_DIM), lambda b, h, i: (b, h)),
            pl.BlockSpec((seq, HEAD_DIM), lambda b, h, i: (b, COL_DV + h)),
            lam_spec, lam_spec, lam_spec, lam_spec,
            pl.BlockSpec((1, HEAD_DIM), lambda b, h, i: (0, 0)),
        ],
        out_specs=pl.BlockSpec((t, HEAD_DIM), lambda b, h, i: (b * nq + i, h)),
        out_shape=jax.ShapeDtypeStruct((batch * seq, DIFF_WIDTH), BF16),
        scratch_shapes=_softmax_state() + _softmax_state(),
        compiler_params=_params("arbitrary", "arbitrary", "arbitrary"),
        name="diff_attention",
    )(qd, kd, proj, r3(lam_q1), r3(lam_k1), r3(lam_q2), r3(lam_k2), subln_g.reshape(1, HEAD_DIM))


def _merge_kernel(of_ref, os_ref, od_ref, wf_ref, ws_ref, wd_ref, gf_ref, gs_ref, gd_ref, o_ref):
    def branch(o_r, w_r, g_r):
        return jax.nn.sigmoid(g_r[...].astype(F32)) * jnp.dot(o_r[...], w_r[...], preferred_element_type=F32)

    y = branch(of_ref, wf_ref, gf_ref) + branch(os_ref, ws_ref, gs_ref) + branch(od_ref, wd_ref, gd_ref)
    o_ref[...] = y.astype(o_ref.dtype)


def _merge_call(o_fox, o_sb, o_diff, w_fox, w_sb, w_diff, gates, bm=1024, bn=1024):
    m = o_fox.shape[0]
    d = w_fox.shape[1]
    gate_blocks = d // bn

    def gate_spec(k):
        return pl.BlockSpec((bm, bn), lambda i, j: (i, k * gate_blocks + j))

    return pl.pallas_call(
        _merge_kernel,
        grid=(m // bm, d // bn),
        in_specs=[
            pl.BlockSpec((bm, FOX_WIDTH), lambda i, j: (i, 0)),
            pl.BlockSpec((bm, SB_WIDTH), lambda i, j: (i, 0)),
            pl.BlockSpec((bm, DIFF_WIDTH), lambda i, j: (i, 0)),
            pl.BlockSpec((FOX_WIDTH, bn), lambda i, j: (0, j)),
            pl.BlockSpec((SB_WIDTH, bn), lambda i, j: (0, j)),
            pl.BlockSpec((DIFF_WIDTH, bn), lambda i, j: (0, j)),
            gate_spec(0), gate_spec(1), gate_spec(2),
        ],
        out_specs=pl.BlockSpec((bm, bn), lambda i, j: (i, j)),
        out_shape=jax.ShapeDtypeStruct((m, d), BF16),
        compiler_params=_params("arbitrary", "arbitrary"),
        name="branch_merge",
    )(o_fox, o_sb, o_diff, w_fox, w_sb, w_diff, gates, gates, gates)


NOT_TOP = 64.0


def _extract_top(cur, count, want_rank=False):
    rows = []
    rank = jnp.full(cur.shape, NOT_TOP, F32) if want_rank else None
    for r in range(count):
        m = jnp.max(cur, axis=0, keepdims=True)
        rows.append(m)
        hit = cur == m
        if want_rank:
            rank = jnp.where(hit, float(r), rank)
        cur = jnp.where(hit, NEG_INF, cur)
    return rows, rank


def _peer_topk_kernel(q_ref, keys_ref, a_ref, l_ref, b_ref, r2_ref):
    k = PEER_TOPK
    for h in range(PEER_HEADS):
        col = 2 * h * PEER_HALF
        s1 = _qk(keys_ref[h, 0], q_ref[:, col:col + PEER_HALF])
        s2 = _qk(keys_ref[h, 1], q_ref[:, col + PEER_HALF:col + 2 * PEER_HALF])
        a_rows, _ = _extract_top(s1, k)
        b_rows, rank2 = _extract_top(s2, k, want_rank=True)
        b_top = jnp.concatenate(b_rows, axis=0)
        counts = [k // (k1 + 1) for k1 in range(k)]
        cands = [a_rows[k1] + b_top[:counts[k1]] for k1 in range(k)]
        n_cand = sum(counts)
        pad = jnp.full((-n_cand % 8, b_top.shape[1]), NEG_INF, F32)
        best, _ = _extract_top(jnp.concatenate(cands + [pad], axis=0), k)
        tau = best[k - 1]
        z = jnp.zeros_like(tau)
        for row in best:
            z = z + jnp.exp(row - best[0])
        l_row = jnp.zeros(s1.shape, F32)
        for k1 in range(k):
            l_k = jnp.sum(jnp.where(cands[k1] >= tau, 1.0, 0.0), axis=0, keepdims=True)
            l_row = jnp.where(s1 == a_rows[k1], l_k, l_row)
        a_ref[h] = jnp.exp(s1 - a_rows[0]) / z
        l_ref[h] = l_row
        b_ref[h] = jnp.exp(s2 - b_rows[0]).astype(b_ref.dtype)
        r2_ref[h] = rank2.astype(r2_ref.dtype)


def _peer_topk_call(qp, keys):
    tokens = qp.shape[0]
    tq = 512
    spec = pl.BlockSpec((PEER_HEADS, PEER_N_KEYS, tq), lambda i: (0, 0, i))
    shape = lambda dt: jax.ShapeDtypeStruct((PEER_HEADS, PEER_N_KEYS, tokens), dt)
    return pl.pallas_call(
        _peer_topk_kernel,
        grid=(tokens // tq,),
        in_specs=[pl.BlockSpec((tq, qp.shape[1]), lambda i: (i, 0)),
                  pl.BlockSpec(keys.shape, lambda i: (0, 0, 0, 0))],
        out_specs=[spec, spec, spec, spec],
        out_shape=[shape(F32), shape(F32), shape(BF16), shape(BF16)],
        compiler_params=_params("arbitrary"),
        name="peer_topk",
    )(qp, keys)


PEER_UP_TOKENS = 1024
PEER_UP_EXPERTS = 1024
PEER_TOKEN_CHUNK = 256
PEER_DOWN_TOKENS = 512
PEER_DOWN_ROWS = 16


def _peer_up_kernel(ht_ref, u_ref, o_ref):
    tc = PEER_TOKEN_CHUNK
    u = u_ref[0].astype(BF16)
    for c in range(o_ref.shape[1] // tc):
        cols = slice(c * tc, (c + 1) * tc)
        hid = jnp.dot(u, ht_ref[0, :, cols], preferred_element_type=F32)
        gelu = 0.5 * hid * (1.0 + lax.erf(hid * (2.0 ** -0.5)))
        o_ref[:, cols] = gelu.astype(o_ref.dtype)


def _peer_down_kernel(vt_ref, g_ref, a_ref, l_ref, b_ref, r2_ref, x_ref, gt_ref, o_ref, acc_ref, act_ref):
    k = pl.program_id(1)
    tn = act_ref.shape[1]

    @pl.when(k == 0)
    def _():
        acc_ref[...] = jnp.zeros_like(acc_ref)

    for r in range(PEER_DOWN_ROWS):
        rows = slice(r * PEER_N_KEYS, (r + 1) * PEER_N_KEYS)

        def key_row(ref, h):
            tile = jnp.broadcast_to(ref[h, r:r + 1, :], (BF16_ROWS, tn)).astype(BF16)
            return jnp.tile(tile, (PEER_N_KEYS // BF16_ROWS, 1))

        gate = jnp.zeros((PEER_N_KEYS, tn), BF16)
        for h in range(PEER_HEADS):
            chosen = r2_ref[h] < key_row(l_ref, h)
            gate = gate + jnp.where(chosen, b_ref[h], jnp.zeros((), BF16)) * key_row(a_ref, h)
        act_ref[rows, :] = gate * g_ref[rows, :]

    acc_ref[...] += jnp.dot(vt_ref[...], act_ref[...], preferred_element_type=F32)

    @pl.when(k == pl.num_programs(1) - 1)
    def _():
        o_ref[0] = x_ref[0] + gt_ref[0] * acc_ref[...].T


def _peer_ffn_call(ht, u, layer, vt, a_t, l_t, b_t, r2_t, x, gt):
    b, d, s = ht.shape
    n_experts = u.shape[1]
    tm, te = PEER_UP_TOKENS, PEER_UP_EXPERTS
    per_b = s // tm
    gelu_t = pl.pallas_call(
        _peer_up_kernel,
        grid=(b * per_b, n_experts // te),
        in_specs=[pl.BlockSpec((1, d, tm), lambda i, e: (i // per_b, 0, i % per_b)),
                  pl.BlockSpec((1, te, d), lambda i, e: (layer, e, 0))],
        out_specs=pl.BlockSpec((te, tm), lambda i, e: (e, i)),
        out_shape=jax.ShapeDtypeStruct((n_experts, b * s), BF16),
        compiler_params=_params("arbitrary", "arbitrary"),
        name="peer_up",
    )(ht, u)

    tn, tk = PEER_DOWN_TOKENS, PEER_DOWN_ROWS * PEER_N_KEYS
    per_b = s // tn
    key_spec = pl.BlockSpec((PEER_HEADS, PEER_N_KEYS, tn), lambda i, k: (0, 0, i))
    row_spec = pl.BlockSpec((PEER_HEADS, PEER_DOWN_ROWS, tn), lambda i, k: (0, k, i))
    return pl.pallas_call(
        _peer_down_kernel,
        grid=(b * per_b, n_experts // tk),
        in_specs=[
            pl.BlockSpec((d, tk), lambda i, k: (0, k)),
            pl.BlockSpec((tk, tn), lambda i, k: (k, i)),
            row_spec, row_spec, key_spec, key_spec,
            pl.BlockSpec((1, tn, d), lambda i, k: (i // per_b, i % per_b, 0)),
            pl.BlockSpec((1, 1, d), lambda i, k: (i // per_b, 0, 0)),
        ],
        out_specs=pl.BlockSpec((1, tn, d), lambda i, k: (i // per_b, i % per_b, 0)),
        out_shape=jax.ShapeDtypeStruct(x.shape, F32),
        scratch_shapes=[pltpu.VMEM((d, tn), F32), pltpu.VMEM((tk, tn), BF16)],
        compiler_params=_params("arbitrary", "arbitrary"),
        name="peer_down",
    )(vt, gelu_t, a_t, l_t, b_t, r2_t, x, gt)


def _peer_vt_kernel(v_ref, vt_ref):
    vt_ref[...] = v_ref[0].T.astype(BF16)


def _peer_vt_call(v, layer):
    _, e, d = v.shape
    te = 512
    return pl.pallas_call(
        _peer_vt_kernel,
        grid=(e // te,),
        in_specs=[pl.BlockSpec((1, te, d), lambda i: (layer, i, 0))],
        out_specs=pl.BlockSpec((d, te), lambda i: (0, i)),
        out_shape=jax.ShapeDtypeStruct((d, e), BF16),
        compiler_params=_params("arbitrary"),
        name="peer_vt",
    )(v)


def kernel(x, c, positions, ada_w, ada_b, norm1_g, norm2_g, w_in, b_forget, lam_q1, lam_k1, lam_q2, lam_k2,
           diff_subln_g, w_br_fox, w_br_sb, w_br_diff, w_o, peer_wq, peer_sub_keys, peer_u, peer_v,
           final_norm_g):
    batch, seq, d = x.shape
    tokens = batch * seq
    assert batch <= 8 and seq % 1024 == 0 and d == D_MODEL

    c_pad = jnp.zeros((8, d), F32).at[:batch].set(c)
    mod = _mod_call(c_pad, ada_w, ada_b)[:, :batch].reshape(DEPTH, batch, N_MOD, 1, d)
    pos_col = positions.reshape(tokens, 1)
    inv_freq = ROPE_THETA ** (-jnp.arange(0, DIFF_QK_DIM, 2, dtype=F32) / DIFF_QK_DIM)
    freq_row = jnp.tile(inv_freq, LANES // inv_freq.shape[0]).reshape(1, LANES)
    ri = np.arange(SB_SEG)
    tri_after = jnp.asarray(np.tile(ri[:, None] > ri[None, :], (2, 1)), BF16)

    for l in range(DEPTH):
        lam_init = 0.8 - 0.6 * math.exp(-0.3 * l)
        sh1, sc1, gt1, sh2, sc2, gt2 = (mod[l, :, k] for k in range(N_MOD))

        h1 = _norm_mod_call(x, norm1_g[l].reshape(1, d), sc1, sh1, BF16)[0].reshape(tokens, d)
        proj = _proj_call(h1, w_in, l, 0, QKV_WIDTH, BF16, bn=1024)
        gates = _proj_call(h1, w_in, l, GATE_OFFSET, N_BRANCHES * d, BF16, bn=1024)
        f_logit = _proj_call(h1, w_in, l, QKV_WIDTH, LANES, F32, bn=LANES, keep=FOX_HEADS)
        f_logit = f_logit.reshape(batch, seq, LANES)
        b_forget_pad = jnp.pad(b_forget[l], (0, LANES - FOX_HEADS)).reshape(1, LANES)
        f_col, f_row = _forget_cumsum_call(f_logit, b_forget_pad)

        o_fox = _fox_call(proj, f_col, f_row, batch, seq)
        o_sb = _sb_call(proj, tri_after, batch, seq)
        qd = _rope_call(proj, pos_col, freq_row, COL_DQ, DIFF_QK_DIM ** -0.5)
        kd = _rope_call(proj, pos_col, freq_row, COL_DK, 1.0)
        o_diff = _diff_call(qd, kd, proj, lam_q1[l], lam_k1[l], lam_q2[l], lam_k2[l], diff_subln_g[l],
                            lam_init, batch, seq)

        y = _merge_call(o_fox, o_sb, o_diff, w_br_fox[l].astype(BF16), w_br_sb[l].astype(BF16),
                        w_br_diff[l].astype(BF16), gates)
        x = _matmul_residual_call(y.reshape(batch, seq, d), w_o[l].astype(BF16), x, gt1)

        h2, h2t = _norm_mod_call(x, norm2_g[l].reshape(1, d), sc2, sh2, BF16, transposed=True)
        qp = _proj_call(h2.reshape(tokens, d), peer_wq, l, 0, peer_wq.shape[2], BF16, bn=1024)
        a_t, l_t, b_t, r2_t = _peer_topk_call(qp, peer_sub_keys[l].astype(BF16))
        x = _peer_ffn_call(h2t, peer_u, l, _peer_vt_call(peer_v, l), a_t, l_t, b_t, r2_t, x, gt2)

    zeros = jnp.zeros((batch, 1, d), F32)
    return _norm_mod_call(x, final_norm_g.reshape(1, d), zeros, zeros, F32)[0]
```

```python
import functools
import math

import jax
import jax.numpy as jnp
import numpy as np
from jax import lax
from jax.experimental import pallas as pl
from jax.experimental.pallas import tpu as pltpu

F32 = jnp.float32
BF16 = jnp.bfloat16

D_MODEL = 2048
DEPTH = 2
CHUNK = 64
HEAD_DIM = 128
FOX_HEADS = 6
SB_HEADS = 6
DIFF_HEADS = 4
DIFF_QK_DIM = HEAD_DIM // 2
FOX_WIDTH = FOX_HEADS * HEAD_DIM
SB_WIDTH = SB_HEADS * HEAD_DIM
DIFF_WIDTH = DIFF_HEADS * HEAD_DIM
QKV_WIDTH = 3 * FOX_WIDTH + 3 * SB_WIDTH + 3 * DIFF_WIDTH
GATE_OFFSET = QKV_WIDTH + FOX_HEADS
ROPE_THETA = 10000.0
RMS_EPS = 1e-6
N_MOD = 6
N_BRANCHES = 3
PEER_HEADS = 8
PEER_N_KEYS = 128
PEER_TOPK = 16
PEER_HALF = 128

LANES = 128
BF16_ROWS = 16
VMEM_LIMIT = 56 * 1024 * 1024
NEG_INF = float("-inf")

COL_FQ, COL_FK, COL_FV = 0, FOX_HEADS, 2 * FOX_HEADS
COL_SQ, COL_SK, COL_SV = 18, 24, 30
COL_DQ, COL_DK, COL_DV = 36, 40, 44
COL_GATE = QKV_WIDTH // LANES


def _params(*sem):
    return pltpu.CompilerParams(dimension_semantics=sem, vmem_limit_bytes=VMEM_LIMIT)


def _mod_kernel(c_ref, w_ref, b_ref, o_ref):
    c = c_ref[...]
    cond = c * jax.nn.sigmoid(c)
    hi = cond.astype(BF16)
    lo = (cond - hi.astype(F32)).astype(BF16)
    w = w_ref[0].astype(BF16)
    acc = jnp.dot(hi, w, preferred_element_type=F32) + jnp.dot(lo, w, preferred_element_type=F32)
    o_ref[0] = acc + b_ref[0]


def _mod_call(c_pad, ada_w, ada_b):
    depth, d, n = ada_w.shape
    tn = 1024
    return pl.pallas_call(
        _mod_kernel,
        grid=(depth, n // tn),
        in_specs=[
            pl.BlockSpec((8, d), lambda l, j: (0, 0)),
            pl.BlockSpec((1, d, tn), lambda l, j: (l, 0, j)),
            pl.BlockSpec((1, 1, tn), lambda l, j: (l, 0, j)),
        ],
        out_specs=pl.BlockSpec((1, 8, tn), lambda l, j: (l, 0, j)),
        out_shape=jax.ShapeDtypeStruct((depth, 8, n), F32),
        compiler_params=_params("arbitrary", "arbitrary"),
        name="adaln_mod",
    )(c_pad, ada_w, ada_b.reshape(depth, 1, n))


def _norm_mod_kernel(x_ref, g_ref, sc_ref, sh_ref, *o_refs, transposed):
    x = x_ref[0]
    y = x * lax.rsqrt(jnp.mean(x * x, axis=-1, keepdims=True) + RMS_EPS) * g_ref[...]
    h = y * (1.0 + sc_ref[0]) + sh_ref[0]
    o_refs[0][0] = h.astype(o_refs[0].dtype)
    if transposed:
        o_refs[1][0] = h.T.astype(o_refs[1].dtype)


def _norm_mod_call(x, g, sc, sh, out_dtype, transposed=False):
    b, s, d = x.shape
    tr = 512
    out_shape = [jax.ShapeDtypeStruct((b, s, d), out_dtype)]
    out_specs = [pl.BlockSpec((1, tr, d), lambda i, j: (i, j, 0))]
    if transposed:
        out_shape.append(jax.ShapeDtypeStruct((b, d, s), out_dtype))
        out_specs.append(pl.BlockSpec((1, d, tr), lambda i, j: (i, 0, j)))
    return pl.pallas_call(
        functools.partial(_norm_mod_kernel, transposed=transposed),
        grid=(b, s // tr),
        in_specs=[
            pl.BlockSpec((1, tr, d), lambda i, j: (i, j, 0)),
            pl.BlockSpec((1, d), lambda i, j: (0, 0)),
            pl.BlockSpec((1, 1, d), lambda i, j: (i, 0, 0)),
            pl.BlockSpec((1, 1, d), lambda i, j: (i, 0, 0)),
        ],
        out_specs=out_specs,
        out_shape=out_shape,
        compiler_params=_params("arbitrary", "arbitrary"),
        name="norm_mod",
    )(x, g, sc, sh)


def _matmul_kernel(a_ref, w_ref, o_ref):
    o_ref[...] = jnp.dot(a_ref[...], w_ref[...], preferred_element_type=F32).astype(o_ref.dtype)


def _matmul_call(a, w, out_dtype, bm=1024, bn=1024):
    m, k = a.shape
    n = w.shape[1]
    bn = min(bn, n)
    return pl.pallas_call(
        _matmul_kernel,
        grid=(m // bm, n // bn),
        in_specs=[pl.BlockSpec((bm, k), lambda i, j: (i, 0)),
                  pl.BlockSpec((k, bn), lambda i, j: (0, j))],
        out_specs=pl.BlockSpec((bm, bn), lambda i, j: (i, j)),
        out_shape=jax.ShapeDtypeStruct((m, n), out_dtype),
        compiler_params=_params("arbitrary", "arbitrary"),
        name="matmul",
    )(a, w)


def _proj_kernel(a_ref, w_ref, *rest, shift, keep):
    o_ref, wb_ref = rest[-2:]

    @pl.when(pl.program_id(1) == 0)
    def _():
        w = w_ref[0]
        if shift:
            w = jnp.concatenate([w[:, shift:], rest[0][0][:, :shift]], axis=1)
        if keep is not None:
            lane = lax.broadcasted_iota(jnp.int32, w.shape, 1)
            w = jnp.where(lane < keep, w, 0.0)
        wb_ref[...] = w.astype(BF16)

    o_ref[...] = jnp.dot(a_ref[...], wb_ref[...], preferred_element_type=F32).astype(o_ref.dtype)


def _proj_call(a, w, layer, col0, n, out_dtype, bn, keep=None, bm=1024):
    m, k = a.shape
    shift = col0 % LANES
    base = (col0 - shift) // bn
    assert (col0 - shift) % bn == 0 and n % bn == 0 and bn % LANES == 0
    next_stride = bn // LANES
    in_specs = [pl.BlockSpec((bm, k), lambda j, i: (i, 0)),
                pl.BlockSpec((1, k, bn), lambda j, i: (layer, 0, base + j))]
    operands = [a, w]
    if shift:
        in_specs.append(pl.BlockSpec((1, k, LANES), lambda j, i: (layer, 0, (base + j + 1) * next_stride)))
        operands.append(w)
    return pl.pallas_call(
        functools.partial(_proj_kernel, shift=shift, keep=keep),
        grid=(n // bn, m // bm),
        in_specs=in_specs,
        out_specs=pl.BlockSpec((bm, bn), lambda j, i: (i, j)),
        out_shape=jax.ShapeDtypeStruct((m, n), out_dtype),
        scratch_shapes=[pltpu.VMEM((k, bn), BF16)],
        compiler_params=_params("arbitrary", "arbitrary"),
        name="proj",
    )(*operands)


def _matmul_residual_kernel(a_ref, w_ref, x_ref, gt_ref, o_ref):
    acc = jnp.dot(a_ref[0], w_ref[...], preferred_element_type=F32)
    o_ref[0] = x_ref[0] + gt_ref[0] * acc


def _matmul_residual_call(a, w, x, gt, bm=1024, bn=1024):
    b, s, k = a.shape
    n = w.shape[1]
    return pl.pallas_call(
        _matmul_residual_kernel,
        grid=(b, s // bm, n // bn),
        in_specs=[pl.BlockSpec((1, bm, k), lambda bi, i, j: (bi, i, 0)),
                  pl.BlockSpec((k, bn), lambda bi, i, j: (0, j)),
                  pl.BlockSpec((1, bm, bn), lambda bi, i, j: (bi, i, j)),
                  pl.BlockSpec((1, 1, bn), lambda bi, i, j: (bi, 0, j))],
        out_specs=pl.BlockSpec((1, bm, bn), lambda bi, i, j: (bi, i, j)),
        out_shape=jax.ShapeDtypeStruct((b, s, n), F32),
        compiler_params=_params("arbitrary", "arbitrary", "arbitrary"),
        name="matmul_residual",
    )(a, w, x, gt)


def _split3(x):
    hi = x.astype(BF16)
    r = x - hi.astype(F32)
    mid = r.astype(BF16)
    lo = (r - mid.astype(F32)).astype(BF16)
    return hi, mid, lo


def _forget_cumsum_kernel(fl_ref, bf_ref, col_ref, row_ref):
    s = fl_ref.shape[1]
    blk = 256
    logf = jax.nn.log_sigmoid(fl_ref[0] + bf_ref[...])
    r = lax.broadcasted_iota(jnp.int32, (blk, blk), 0)
    c = lax.broadcasted_iota(jnp.int32, (blk, blk), 1)
    tri = jnp.where(c <= r, 1.0, 0.0).astype(BF16)
    carry = jnp.zeros((1, LANES), F32)
    parts = []
    for i in range(s // blk):
        hi, mid, lo = _split3(logf[i * blk:(i + 1) * blk])
        cs = (jnp.dot(tri, hi, preferred_element_type=F32)
              + jnp.dot(tri, mid, preferred_element_type=F32)
              + jnp.dot(tri, lo, preferred_element_type=F32)) + carry
        carry = cs[blk - 1:blk, :]
        parts.append(cs)
    cum = jnp.concatenate(parts, axis=0)
    row_ref[0] = cum.T[:8, :]
    for h in range(FOX_HEADS):
        col_ref[0, h] = jnp.broadcast_to(cum[:, h:h + 1], (s, LANES))


def _forget_cumsum_call(f_logit, b_forget_pad):
    b, s, _ = f_logit.shape
    return pl.pallas_call(
        _forget_cumsum_kernel,
        grid=(b,),
        in_specs=[pl.BlockSpec((1, s, LANES), lambda i: (i, 0, 0)),
                  pl.BlockSpec((1, LANES), lambda i: (0, 0))],
        out_specs=[pl.BlockSpec((1, FOX_HEADS, s, LANES), lambda i: (i, 0, 0, 0)),
                   pl.BlockSpec((1, 8, s), lambda i: (i, 0, 0))],
        out_shape=[jax.ShapeDtypeStruct((b, FOX_HEADS, s, LANES), F32),
                   jax.ShapeDtypeStruct((b, 8, s), F32)],
        compiler_params=_params("arbitrary"),
        name="forget_cumsum",
    )(f_logit, b_forget_pad)


ATT_TILE = 512


def _qk(q, k):
    return lax.dot_general(q, k, (((1,), (1,)), ((), ())), preferred_element_type=F32)


def _kv_block(k_ref, v_ref, j):
    start = pl.multiple_of(j * ATT_TILE, ATT_TILE)
    return start, k_ref[pl.ds(start, ATT_TILE), :], v_ref[pl.ds(start, ATT_TILE), :]


def _tile_iotas():
    r = lax.broadcasted_iota(jnp.int32, (ATT_TILE, ATT_TILE), 0)
    c = lax.broadcasted_iota(jnp.int32, (ATT_TILE, ATT_TILE), 1)
    return r, c


def _softmax_state(rows=ATT_TILE):
    return [pltpu.VMEM((rows, LANES), F32), pltpu.VMEM((rows, LANES), F32), pltpu.VMEM((rows, HEAD_DIM), F32)]


def _softmax_reset(m_ref, l_ref, acc_ref):
    m_ref[...] = jnp.full(m_ref.shape, NEG_INF, F32)
    l_ref[...] = jnp.zeros(l_ref.shape, F32)
    acc_ref[...] = jnp.zeros(acc_ref.shape, F32)


def _softmax_update(m_ref, l_ref, acc_ref, s, vb):
    m_old = m_ref[...]
    m_new = jnp.maximum(m_old, jnp.max(s, axis=-1, keepdims=True))
    alpha = jnp.exp(m_old - m_new)
    p = jnp.exp(s - jnp.tile(m_new, (1, s.shape[1] // LANES)))
    l_ref[...] = alpha * l_ref[...] + jnp.sum(p, axis=-1, keepdims=True)
    acc_ref[...] = alpha * acc_ref[...] + jnp.dot(p.astype(BF16), vb, preferred_element_type=F32)
    m_ref[...] = m_new


def _fox_kernel(q_ref, k_ref, v_ref, fc_ref, fr_ref, o_ref, m_ref, l_ref, acc_ref):
    t = ATT_TILE
    qi = pl.program_id(2)
    h = pl.program_id(1)
    q = (q_ref[...].astype(F32) * (HEAD_DIM ** -0.5)).astype(BF16)

    def step(j, masked):
        start, kb, vb = _kv_block(k_ref, v_ref, j)
        fcol = jnp.tile(fc_ref[0, 0], (1, t // LANES))
        frow = fr_ref[0, pl.ds(h, 1), pl.ds(start, t)]
        s = _qk(q, kb) + (fcol - frow)
        if masked:
            r, c = _tile_iotas()
            s = jnp.where(c <= r, s, NEG_INF)
        _softmax_update(m_ref, l_ref, acc_ref, s, vb)

    _softmax_reset(m_ref, l_ref, acc_ref)
    step(qi, masked=True)

    @pl.loop(0, qi)
    def _(j):
        step(j, masked=False)

    o_ref[...] = (acc_ref[...] / l_ref[...]).astype(o_ref.dtype)


def _fox_call(proj, f_col, f_row, batch, seq):
    tq = ATT_TILE
    nq = seq // tq
    return pl.pallas_call(
        _fox_kernel,
        grid=(batch, FOX_HEADS, nq),
        in_specs=[
            pl.BlockSpec((tq, HEAD_DIM), lambda b, h, i: (b * nq + i, COL_FQ + h)),
            pl.BlockSpec((seq, HEAD_DIM), lambda b, h, i: (b, COL_FK + h)),
            pl.BlockSpec((seq, HEAD_DIM), lambda b, h, i: (b, COL_FV + h)),
            pl.BlockSpec((1, 1, tq, LANES), lambda b, h, i: (b, h, i, 0)),
            pl.BlockSpec((1, 8, seq), lambda b, h, i: (b, 0, 0)),
        ],
        out_specs=pl.BlockSpec((tq, HEAD_DIM), lambda b, h, i: (b * nq + i, h)),
        out_shape=jax.ShapeDtypeStruct((batch * seq, FOX_WIDTH), BF16),
        scratch_shapes=_softmax_state(),
        compiler_params=_params("arbitrary", "arbitrary", "arbitrary"),
        name="fox_attention",
    )(proj, proj, proj, f_col, f_row)


SB_SEG = 256


def _sb_kernel(q_ref, k_ref, v_ref, tri_ref, o_ref, run_ref, acc_ref):
    t = ATT_TILE
    qi = pl.program_id(2)
    log2e = math.log2(math.e)
    qn = (q_ref[...].astype(F32) * -(HEAD_DIM ** -0.5 * log2e)).astype(BF16)

    def step(j, masked):
        _, kb, vb = _kv_block(k_ref, v_ref, j)
        tri = tri_ref[...]
        u = _qk(qn, kb)
        lk = jnp.minimum(u, 0.0) - jnp.log(1.0 + jnp.exp2(-jnp.abs(u))) * log2e
        if masked:
            r, c = _tile_iotas()
            strictly_causal = c < r
            lk = jnp.where(strictly_causal, lk, 0.0)
        lk_bf = lk.astype(BF16)
        run = run_ref[...]
        laters = []
        for g in reversed(range(t // SB_SEG)):
            sl = slice(g * SB_SEG, (g + 1) * SB_SEG)
            laters.append(jnp.dot(lk_bf[:, sl], tri, preferred_element_type=F32) + jnp.tile(run, (1, SB_SEG // LANES)))
            run = run + jnp.sum(lk[:, sl], axis=-1, keepdims=True)
        later = jnp.concatenate(laters[::-1], axis=1)
        w = jnp.exp2(lk - u + later)
        if masked:
            w = jnp.where(strictly_causal, w, 0.0)
        acc_ref[...] += jnp.dot(w.astype(BF16), vb, preferred_element_type=F32)
        run_ref[...] = run

    run_ref[...] = jnp.zeros(run_ref.shape, F32)
    acc_ref[...] = jnp.zeros(acc_ref.shape, F32)
    step(qi, masked=True)

    @pl.loop(0, qi)
    def _(i):
        step(qi - 1 - i, masked=False)

    o_ref[...] = acc_ref[...].astype(o_ref.dtype)


def _sb_call(proj, tri, batch, seq):
    tq = ATT_TILE
    nq = seq // tq
    return pl.pallas_call(
        _sb_kernel,
        grid=(batch, SB_HEADS, nq),
        in_specs=[
            pl.BlockSpec((tq, HEAD_DIM), lambda b, h, i: (b * nq + i, COL_SQ + h)),
            pl.BlockSpec((seq, HEAD_DIM), lambda b, h, i: (b, COL_SK + h)),
            pl.BlockSpec((seq, HEAD_DIM), lambda b, h, i: (b, COL_SV + h)),
            pl.BlockSpec((SB_SEG, SB_SEG), lambda b, h, i: (0, 0)),
        ],
        out_specs=pl.BlockSpec((tq, HEAD_DIM), lambda b, h, i: (b * nq + i, h)),
        out_shape=jax.ShapeDtypeStruct((batch * seq, SB_WIDTH), BF16),
        scratch_shapes=[pltpu.VMEM((tq, LANES), F32), pltpu.VMEM((tq, HEAD_DIM), F32)],
        compiler_params=_params("arbitrary", "arbitrary", "arbitrary"),
        name="sb_attention",
    )(proj, proj, proj, tri)


def _rope_kernel(x_ref, pos_ref, freq_ref, o_ref, *, out_scale):
    half = DIFF_QK_DIM // 2
    ang = pos_ref[...].astype(F32) * freq_ref[...]
    cos = jnp.cos(ang)
    sin = jnp.sin(ang)
    lane = lax.broadcasted_iota(jnp.int32, ang.shape, 1)
    first = (lane % DIFF_QK_DIM) < half
    sin_signed = jnp.where(first, -sin, sin)
    for cb in range(x_ref.shape[1] // LANES):
        x = x_ref[:, cb * LANES:(cb + 1) * LANES].astype(F32)
        partner = jnp.where(first, pltpu.roll(x, LANES - half, 1), pltpu.roll(x, half, 1))
        y = x * cos + partner * sin_signed
        o_ref[:, cb * LANES:(cb + 1) * LANES] = (y * out_scale).astype(o_ref.dtype)


def _rope_call(proj, pos_col, freq_row, col_block, out_scale):
    tokens = proj.shape[0]
    tr = 512
    width = DIFF_HEADS * HEAD_DIM
    return pl.pallas_call(
        functools.partial(_rope_kernel, out_scale=out_scale),
        grid=(tokens // tr,),
        in_specs=[pl.BlockSpec((tr, width), lambda i: (i, col_block * LANES // width)),
                  pl.BlockSpec((tr, 1), lambda i: (i, 0)),
                  pl.BlockSpec((1, LANES), lambda i: (0, 0))],
        out_specs=pl.BlockSpec((tr, width), lambda i: (i, 0)),
        out_shape=jax.ShapeDtypeStruct((tokens, width), BF16),
        compiler_params=_params("arbitrary"),
        name="rope",
    )(proj, pos_col, freq_row)


def _diff_kernel(q_ref, k_ref, v_ref, lq1_ref, lk1_ref, lq2_ref, lk2_ref, g_ref, o_ref,
                 m_ref, l_ref, acc_ref, *, lam_init):
    t = ATT_TILE
    qi = pl.program_id(2)
    q = q_ref[...]
    lane = lax.broadcasted_iota(jnp.int32, q.shape, 1)
    zero = jnp.zeros_like(q)
    q1 = jnp.where(lane < DIFF_QK_DIM, q, zero)
    q2 = jnp.where(lane < DIFF_QK_DIM, zero, q)
    q12 = jnp.concatenate([q1, q2], axis=0)
    state = (m_ref, l_ref, acc_ref)

    def step(j, masked):
        _, kb, vb = _kv_block(k_ref, v_ref, j)
        s = _qk(q12, kb)
        if masked:
            r, c = _tile_iotas()
            visible = (c // CHUNK) <= (r // CHUNK)
            s = jnp.where(jnp.concatenate([visible, visible], axis=0), s, NEG_INF)
        _softmax_update(*state, s, vb)

    _softmax_reset(*state)
    step(qi, masked=True)

    @pl.loop(0, qi)
    def _(j):
        step(j, masked=False)

    lam = (jnp.exp(jnp.sum(lq1_ref[0] * lk1_ref[0], axis=-1, keepdims=True))
           - jnp.exp(jnp.sum(lq2_ref[0] * lk2_ref[0], axis=-1, keepdims=True)) + lam_init)
    o = acc_ref[:t, :] / l_ref[:t, :] - lam * (acc_ref[t:, :] / l_ref[t:, :])
    o = o * lax.rsqrt(jnp.mean(o * o, axis=-1, keepdims=True) + RMS_EPS) * g_ref[...]
    o_ref[...] = (o * (1.0 - lam_init)).astype(o_ref.dtype)


def _diff_call(qd, kd, proj, lam_q1, lam_k1, lam_q2, lam_k2, subln_g, lam_init, batch, seq):
    t = ATT_TILE
    nq = seq // t
    lam_spec = pl.BlockSpec((1, 1, DIFF_QK_DIM), lambda b, h, i: (h, 0, 0))
    r3 = lambda a: a.reshape(DIFF_HEADS, 1, DIFF_QK_DIM)
    return pl.pallas_call(
        functools.partial(_diff_kernel, lam_init=lam_init),
        grid=(batch, DIFF_HEADS, nq),
        in_specs=[
            pl.BlockSpec((t, HEAD_DIM), lambda b, h, i: (b * nq + i, h)),
            pl.BlockSpec((seq, HEAD_DIM), lambda b, h, i: (b, h)),
            pl.BlockSpec((seq, HEAD_DIM), lambda b, h, i: (b, COL_DV + h)),
            lam_spec, lam_spec, lam_spec, lam_spec,
            pl.BlockSpec((1, HEAD_DIM), lambda b, h, i: (0, 0)),
        ],
        out_specs=pl.BlockSpec((t, HEAD_DIM), lambda b, h, i: (b * nq + i, h)),
        out_shape=jax.ShapeDtypeStruct((batch * seq, DIFF_WIDTH), BF16),
        scratch_shapes=_softmax_state(rows=2 * ATT_TILE),
        compiler_params=_params("arbitrary", "arbitrary", "arbitrary"),
        name="diff_attention",
    )(qd, kd, proj, r3(lam_q1), r3(lam_k1), r3(lam_q2), r3(lam_k2), subln_g.reshape(1, HEAD_DIM))


def _merge_kernel(of_ref, os_ref, od_ref, wf_ref, ws_ref, wd_ref, gf_ref, gs_ref, gd_ref, o_ref):
    def branch(o_r, w_r, g_r):
        return jax.nn.sigmoid(g_r[...].astype(F32)) * jnp.dot(o_r[...], w_r[...], preferred_element_type=F32)

    y = branch(of_ref, wf_ref, gf_ref) + branch(os_ref, ws_ref, gs_ref) + branch(od_ref, wd_ref, gd_ref)
    o_ref[...] = y.astype(o_ref.dtype)


def _merge_call(o_fox, o_sb, o_diff, w_fox, w_sb, w_diff, gates, bm=1024, bn=1024):
    m = o_fox.shape[0]
    d = w_fox.shape[1]
    gate_blocks = d // bn

    def gate_spec(k):
        return pl.BlockSpec((bm, bn), lambda i, j: (i, k * gate_blocks + j))

    return pl.pallas_call(
        _merge_kernel,
        grid=(m // bm, d // bn),
        in_specs=[
            pl.BlockSpec((bm, FOX_WIDTH), lambda i, j: (i, 0)),
            pl.BlockSpec((bm, SB_WIDTH), lambda i, j: (i, 0)),
            pl.BlockSpec((bm, DIFF_WIDTH), lambda i, j: (i, 0)),
            pl.BlockSpec((FOX_WIDTH, bn), lambda i, j: (0, j)),
            pl.BlockSpec((SB_WIDTH, bn), lambda i, j: (0, j)),
            pl.BlockSpec((DIFF_WIDTH, bn), lambda i, j: (0, j)),
            gate_spec(0), gate_spec(1), gate_spec(2),
        ],
        out_specs=pl.BlockSpec((bm, bn), lambda i, j: (i, j)),
        out_shape=jax.ShapeDtypeStruct((m, d), BF16),
        compiler_params=_params("arbitrary", "arbitrary"),
        name="branch_merge",
    )(o_fox, o_sb, o_diff, w_fox, w_sb, w_diff, gates, gates, gates)


NOT_TOP = 64.0


def _extract_top(cur, count, want_rank=False):
    rows = []
    rank = jnp.full(cur.shape, NOT_TOP, F32) if want_rank else None
    for r in range(count):
        m = jnp.max(cur, axis=0, keepdims=True)
        rows.append(m)
        hit = cur == m
        if want_rank:
            rank = jnp.where(hit, float(r), rank)
        cur = jnp.where(hit, NEG_INF, cur)
    return rows, rank


def _peer_topk_kernel(q_ref, keys_ref, a_ref, l_ref, b_ref, r2_ref):
    k = PEER_TOPK
    for h in range(PEER_HEADS):
        col = 2 * h * PEER_HALF
        s1 = _qk(keys_ref[h, 0], q_ref[:, col:col + PEER_HALF])
        s2 = _qk(keys_ref[h, 1], q_ref[:, col + PEER_HALF:col + 2 * PEER_HALF])
        a_rows, _ = _extract_top(s1, k)
        b_rows, rank2 = _extract_top(s2, k, want_rank=True)
        b_top = jnp.concatenate(b_rows, axis=0)
        counts = [k // (k1 + 1) for k1 in range(k)]
        cands = [a_rows[k1] + b_top[:counts[k1]] for k1 in range(k)]
        n_cand = sum(counts)
        pad = jnp.full((-n_cand % 8, b_top.shape[1]), NEG_INF, F32)
        best, _ = _extract_top(jnp.concatenate(cands + [pad], axis=0), k)
        tau = best[k - 1]
        z = jnp.zeros_like(tau)
        for row in best:
            z = z + jnp.exp(row - best[0])
        l_row = jnp.zeros(s1.shape, F32)
        for k1 in range(k):
            l_k = jnp.sum(jnp.where(cands[k1] >= tau, 1.0, 0.0), axis=0, keepdims=True)
            l_row = jnp.where(s1 == a_rows[k1], l_k, l_row)
        a_ref[h] = jnp.exp(s1 - a_rows[0]) / z
        l_ref[h] = l_row
        b_ref[h] = jnp.exp(s2 - b_rows[0]).astype(b_ref.dtype)
        r2_ref[h] = rank2.astype(r2_ref.dtype)


def _peer_topk_call(qp, keys):
    tokens = qp.shape[0]
    tq = 512
    spec = pl.BlockSpec((PEER_HEADS, PEER_N_KEYS, tq), lambda i: (0, 0, i))
    shape = lambda dt: jax.ShapeDtypeStruct((PEER_HEADS, PEER_N_KEYS, tokens), dt)
    return pl.pallas_call(
        _peer_topk_kernel,
        grid=(tokens // tq,),
        in_specs=[pl.BlockSpec((tq, qp.shape[1]), lambda i: (i, 0)),
                  pl.BlockSpec(keys.shape, lambda i: (0, 0, 0, 0))],
        out_specs=[spec, spec, spec, spec],
        out_shape=[shape(F32), shape(F32), shape(BF16), shape(BF16)],
        compiler_params=_params("arbitrary"),
        name="peer_topk",
    )(qp, keys)


PEER_UP_TOKENS = 1024
PEER_UP_EXPERTS = 1024
PEER_TOKEN_CHUNK = 256
PEER_DOWN_TOKENS = 512
PEER_DOWN_ROWS = 16


def _peer_up_kernel(ht_ref, u_ref, o_ref):
    tc = PEER_TOKEN_CHUNK
    u = u_ref[0].astype(BF16)
    for c in range(o_ref.shape[1] // tc):
        cols = slice(c * tc, (c + 1) * tc)
        hid = jnp.dot(u, ht_ref[0, :, cols], preferred_element_type=F32)
        gelu = 0.5 * hid * (1.0 + lax.erf(hid * (2.0 ** -0.5)))
        o_ref[:, cols] = gelu.astype(o_ref.dtype)


def _peer_down_kernel(vt_ref, g_ref, a_ref, l_ref, b_ref, r2_ref, x_ref, gt_ref, o_ref, acc_ref, act_ref):
    k = pl.program_id(1)
    tn = act_ref.shape[1]

    @pl.when(k == 0)
    def _():
        acc_ref[...] = jnp.zeros_like(acc_ref)

    for r in range(PEER_DOWN_ROWS):
        rows = slice(r * PEER_N_KEYS, (r + 1) * PEER_N_KEYS)

        def key_row(ref, h):
            tile = jnp.broadcast_to(ref[h, r:r + 1, :], (BF16_ROWS, tn)).astype(BF16)
            return jnp.tile(tile, (PEER_N_KEYS // BF16_ROWS, 1))

        gate = jnp.zeros((PEER_N_KEYS, tn), BF16)
        for h in range(PEER_HEADS):
            chosen = r2_ref[h] < key_row(l_ref, h)
            gate = gate + jnp.where(chosen, b_ref[h], jnp.zeros((), BF16)) * key_row(a_ref, h)
        act_ref[rows, :] = gate * g_ref[rows, :]

    acc_ref[...] += jnp.dot(vt_ref[...], act_ref[...], preferred_element_type=F32)

    @pl.when(k == pl.num_programs(1) - 1)
    def _():
        o_ref[0] = x_ref[0] + gt_ref[0] * acc_ref[...].T


def _peer_ffn_call(ht, u, layer, vt, a_t, l_t, b_t, r2_t, x, gt):
    b, d, s = ht.shape
    n_experts = u.shape[1]
    tm, te = PEER_UP_TOKENS, PEER_UP_EXPERTS
    per_b = s // tm
    gelu_t = pl.pallas_call(
        _peer_up_kernel,
        grid=(b * per_b, n_experts // te),
        in_specs=[pl.BlockSpec((1, d, tm), lambda i, e: (i // per_b, 0, i % per_b)),
                  pl.BlockSpec((1, te, d), lambda i, e: (layer, e, 0))],
        out_specs=pl.BlockSpec((te, tm), lambda i, e: (e, i)),
        out_shape=jax.ShapeDtypeStruct((n_experts, b * s), BF16),
        compiler_params=_params("arbitrary", "arbitrary"),
        name="peer_up",
    )(ht, u)

    tn, tk = PEER_DOWN_TOKENS, PEER_DOWN_ROWS * PEER_N_KEYS
    per_b = s // tn
    key_spec = pl.BlockSpec((PEER_HEADS, PEER_N_KEYS, tn), lambda i, k: (0, 0, i))
    row_spec = pl.BlockSpec((PEER_HEADS, PEER_DOWN_ROWS, tn), lambda i, k: (0, k, i))
    return pl.pallas_call(
        _peer_down_kernel,
        grid=(b * per_b, n_experts // tk),
        in_specs=[
            pl.BlockSpec((d, tk), lambda i, k: (0, k)),
            pl.BlockSpec((tk, tn), lambda i, k: (k, i)),
            row_spec, row_spec, key_spec, key_spec,
            pl.BlockSpec((1, tn, d), lambda i, k: (i // per_b, i % per_b, 0)),
            pl.BlockSpec((1, 1, d), lambda i, k: (i // per_b, 0, 0)),
        ],
        out_specs=pl.BlockSpec((1, tn, d), lambda i, k: (i // per_b, i % per_b, 0)),
        out_shape=jax.ShapeDtypeStruct(x.shape, F32),
        scratch_shapes=[pltpu.VMEM((d, tn), F32), pltpu.VMEM((tk, tn), BF16)],
        compiler_params=_params("arbitrary", "arbitrary"),
        name="peer_down",
    )(vt, gelu_t, a_t, l_t, b_t, r2_t, x, gt)


def _peer_vt_kernel(v_ref, vt_ref):
    vt_ref[...] = v_ref[0].T.astype(BF16)


def _peer_vt_call(v, layer):
    _, e, d = v.shape
    te = 512
    return pl.pallas_call(
        _peer_vt_kernel,
        grid=(e // te,),
        in_specs=[pl.BlockSpec((1, te, d), lambda i: (layer, i, 0))],
        out_specs=pl.BlockSpec((d, te), lambda i: (0, i)),
        out_shape=jax.ShapeDtypeStruct((d, e), BF16),
        compiler_params=_params("arbitrary"),
        name="peer_vt",
    )(v)


def kernel(x, c, positions, ada_w, ada_b, norm1_g, norm2_g, w_in, b_forget, lam_q1, lam_k1, lam_q2, lam_k2,
           diff_subln_g, w_br_fox, w_br_sb, w_br_diff, w_o, peer_wq, peer_sub_keys, peer_u, peer_v,
           final_norm_g):
    batch, seq, d = x.shape
    tokens = batch * seq
    assert batch <= 8 and seq % 1024 == 0 and d == D_MODEL

    c_pad = jnp.zeros((8, d), F32).at[:batch].set(c)
    mod = _mod_call(c_pad, ada_w, ada_b)[:, :batch].reshape(DEPTH, batch, N_MOD, 1, d)
    pos_col = positions.reshape(tokens, 1)
    inv_freq = ROPE_THETA ** (-jnp.arange(0, DIFF_QK_DIM, 2, dtype=F32) / DIFF_QK_DIM)
    freq_row = jnp.tile(inv_freq, LANES // inv_freq.shape[0]).reshape(1, LANES)
    ri = np.arange(SB_SEG)
    tri_after = jnp.asarray(ri[:, None] > ri[None, :], BF16)

    for l in range(DEPTH):
        lam_init = 0.8 - 0.6 * math.exp(-0.3 * l)
        sh1, sc1, gt1, sh2, sc2, gt2 = (mod[l, :, k] for k in range(N_MOD))

        h1 = _norm_mod_call(x, norm1_g[l].reshape(1, d), sc1, sh1, BF16)[0].reshape(tokens, d)
        proj = _proj_call(h1, w_in, l, 0, QKV_WIDTH, BF16, bn=1024)
        gates = _proj_call(h1, w_in, l, GATE_OFFSET, N_BRANCHES * d, BF16, bn=1024)
        f_logit = _proj_call(h1, w_in, l, QKV_WIDTH, LANES, F32, bn=LANES, keep=FOX_HEADS)
        f_logit = f_logit.reshape(batch, seq, LANES)
        b_forget_pad = jnp.pad(b_forget[l], (0, LANES - FOX_HEADS)).reshape(1, LANES)
        f_col, f_row = _forget_cumsum_call(f_logit, b_forget_pad)

        o_fox = _fox_call(proj, f_col, f_row, batch, seq)
        o_sb = _sb_call(proj, tri_after, batch, seq)
        qd = _rope_call(proj, pos_col, freq_row, COL_DQ, DIFF_QK_DIM ** -0.5)
        kd = _rope_call(proj, pos_col, freq_row, COL_DK, 1.0)
        o_diff = _diff_call(qd, kd, proj, lam_q1[l], lam_k1[l], lam_q2[l], lam_k2[l], diff_subln_g[l],
                            lam_init, batch, seq)

        y = _merge_call(o_fox, o_sb, o_diff, w_br_fox[l].astype(BF16), w_br_sb[l].astype(BF16),
                        w_br_diff[l].astype(BF16), gates)
        x = _matmul_residual_call(y.reshape(batch, seq, d), w_o[l].astype(BF16), x, gt1)

        h2, h2t = _norm_mod_call(x, norm2_g[l].reshape(1, d), sc2, sh2, BF16, transposed=True)
        qp = _proj_call(h2.reshape(tokens, d), peer_wq, l, 0, peer_wq.shape[2], BF16, bn=1024)
        a_t, l_t, b_t, r2_t = _peer_topk_call(qp, peer_sub_keys[l].astype(BF16))
        x = _peer_ffn_call(h2t, peer_u, l, _peer_vt_call(peer_v, l), a_t, l_t, b_t, r2_t, x, gt2)

    zeros = jnp.zeros((batch, 1, d), F32)
    return _norm_mod_call(x, final_norm_g.reshape(1, d), zeros, zeros, F32)[0]
```

```python
import functools
import math

import jax
import jax.numpy as jnp
import numpy as np
from jax import lax
from jax.experimental import pallas as pl
from jax.experimental.pallas import tpu as pltpu

F32 = jnp.float32
BF16 = jnp.bfloat16

D_MODEL = 2048
DEPTH = 2
CHUNK = 64
HEAD_DIM = 128
FOX_HEADS = 6
SB_HEADS = 6
DIFF_HEADS = 4
DIFF_QK_DIM = HEAD_DIM // 2
FOX_WIDTH = FOX_HEADS * HEAD_DIM
SB_WIDTH = SB_HEADS * HEAD_DIM
DIFF_WIDTH = DIFF_HEADS * HEAD_DIM
QKV_WIDTH = 3 * FOX_WIDTH + 3 * SB_WIDTH + 3 * DIFF_WIDTH
GATE_OFFSET = QKV_WIDTH + FOX_HEADS
ROPE_THETA = 10000.0
RMS_EPS = 1e-6
N_MOD = 6
N_BRANCHES = 3
PEER_HEADS = 8
PEER_N_KEYS = 128
PEER_TOPK = 16
PEER_HALF = 128

LANES = 128
BF16_ROWS = 16
VMEM_LIMIT = 56 * 1024 * 1024
NEG_INF = float("-inf")

COL_FQ, COL_FK, COL_FV = 0, FOX_HEADS, 2 * FOX_HEADS
COL_SQ, COL_SK, COL_SV = 18, 24, 30
COL_DQ, COL_DK, COL_DV = 36, 40, 44
COL_GATE = QKV_WIDTH // LANES


def _params(*sem):
    return pltpu.CompilerParams(dimension_semantics=sem, vmem_limit_bytes=VMEM_LIMIT)


def _mod_kernel(c_ref, w_ref, b_ref, o_ref):
    c = c_ref[...]
    cond = c * jax.nn.sigmoid(c)
    hi = cond.astype(BF16)
    lo = (cond - hi.astype(F32)).astype(BF16)
    w = w_ref[0].astype(BF16)
    acc = jnp.dot(hi, w, preferred_element_type=F32) + jnp.dot(lo, w, preferred_element_type=F32)
    o_ref[0] = acc + b_ref[0]


def _mod_call(c_pad, ada_w, ada_b):
    depth, d, n = ada_w.shape
    tn = 1024
    return pl.pallas_call(
        _mod_kernel,
        grid=(depth, n // tn),
        in_specs=[
            pl.BlockSpec((8, d), lambda l, j: (0, 0)),
            pl.BlockSpec((1, d, tn), lambda l, j: (l, 0, j)),
            pl.BlockSpec((1, 1, tn), lambda l, j: (l, 0, j)),
        ],
        out_specs=pl.BlockSpec((1, 8, tn), lambda l, j: (l, 0, j)),
        out_shape=jax.ShapeDtypeStruct((depth, 8, n), F32),
        compiler_params=_params("arbitrary", "arbitrary"),
        name="adaln_mod",
    )(c_pad, ada_w, ada_b.reshape(depth, 1, n))


def _norm_mod_kernel(x_ref, g_ref, sc_ref, sh_ref, *o_refs, transposed):
    x = x_ref[0]
    y = x * lax.rsqrt(jnp.mean(x * x, axis=-1, keepdims=True) + RMS_EPS) * g_ref[...]
    h = y * (1.0 + sc_ref[0]) + sh_ref[0]
    o_refs[0][0] = h.astype(o_refs[0].dtype)
    if transposed:
        o_refs[1][0] = h.T.astype(o_refs[1].dtype)


def _norm_mod_call(x, g, sc, sh, out_dtype, transposed=False):
    b, s, d = x.shape
    tr = 512
    out_shape = [jax.ShapeDtypeStruct((b, s, d), out_dtype)]
    out_specs = [pl.BlockSpec((1, tr, d), lambda i, j: (i, j, 0))]
    if transposed:
        out_shape.append(jax.ShapeDtypeStruct((b, d, s), out_dtype))
        out_specs.append(pl.BlockSpec((1, d, tr), lambda i, j: (i, 0, j)))
    return pl.pallas_call(
        functools.partial(_norm_mod_kernel, transposed=transposed),
        grid=(b, s // tr),
        in_specs=[
            pl.BlockSpec((1, tr, d), lambda i, j: (i, j, 0)),
            pl.BlockSpec((1, d), lambda i, j: (0, 0)),
            pl.BlockSpec((1, 1, d), lambda i, j: (i, 0, 0)),
            pl.BlockSpec((1, 1, d), lambda i, j: (i, 0, 0)),
        ],
        out_specs=out_specs,
        out_shape=out_shape,
        compiler_params=_params("arbitrary", "arbitrary"),
        name="norm_mod",
    )(x, g, sc, sh)


def _matmul_kernel(a_ref, w_ref, o_ref):
    o_ref[...] = jnp.dot(a_ref[...], w_ref[...], preferred_element_type=F32).astype(o_ref.dtype)


def _matmul_call(a, w, out_dtype, bm=1024, bn=1024):
    m, k = a.shape
    n = w.shape[1]
    bn = min(bn, n)
    return pl.pallas_call(
        _matmul_kernel,
        grid=(m // bm, n // bn),
        in_specs=[pl.BlockSpec((bm, k), lambda i, j: (i, 0)),
                  pl.BlockSpec((k, bn), lambda i, j: (0, j))],
        out_specs=pl.BlockSpec((bm, bn), lambda i, j: (i, j)),
        out_shape=jax.ShapeDtypeStruct((m, n), out_dtype),
        compiler_params=_params("arbitrary", "arbitrary"),
        name="matmul",
    )(a, w)


def _proj_kernel(a_ref, w_ref, *rest, shift, keep):
    o_ref, wb_ref = rest[-2:]

    @pl.when(pl.program_id(1) == 0)
    def _():
        w = w_ref[0]
        if shift:
            w = jnp.concatenate([w[:, shift:], rest[0][0][:, :shift]], axis=1)
        if keep is not None:
            lane = lax.broadcasted_iota(jnp.int32, w.shape, 1)
            w = jnp.where(lane < keep, w, 0.0)
        wb_ref[...] = w.astype(BF16)

    o_ref[...] = jnp.dot(a_ref[...], wb_ref[...], preferred_element_type=F32).astype(o_ref.dtype)


def _proj_call(a, w, layer, col0, n, out_dtype, bn, keep=None, bm=1024):
    m, k = a.shape
    shift = col0 % LANES
    base = (col0 - shift) // bn
    assert (col0 - shift) % bn == 0 and n % bn == 0 and bn % LANES == 0
    next_stride = bn // LANES
    in_specs = [pl.BlockSpec((bm, k), lambda j, i: (i, 0)),
                pl.BlockSpec((1, k, bn), lambda j, i: (layer, 0, base + j))]
    operands = [a, w]
    if shift:
        in_specs.append(pl.BlockSpec((1, k, LANES), lambda j, i: (layer, 0, (base + j + 1) * next_stride)))
        operands.append(w)
    return pl.pallas_call(
        functools.partial(_proj_kernel, shift=shift, keep=keep),
        grid=(n // bn, m // bm),
        in_specs=in_specs,
        out_specs=pl.BlockSpec((bm, bn), lambda j, i: (i, j)),
        out_shape=jax.ShapeDtypeStruct((m, n), out_dtype),
        scratch_shapes=[pltpu.VMEM((k, bn), BF16)],
        compiler_params=_params("arbitrary", "arbitrary"),
        name="proj",
    )(*operands)


def _matmul_residual_kernel(a_ref, w_ref, x_ref, gt_ref, o_ref):
    acc = jnp.dot(a_ref[0], w_ref[...], preferred_element_type=F32)
    o_ref[0] = x_ref[0] + gt_ref[0] * acc


def _matmul_residual_call(a, w, x, gt, bm=1024, bn=1024):
    b, s, k = a.shape
    n = w.shape[1]
    return pl.pallas_call(
        _matmul_residual_kernel,
        grid=(b, s // bm, n // bn),
        in_specs=[pl.BlockSpec((1, bm, k), lambda bi, i, j: (bi, i, 0)),
                  pl.BlockSpec((k, bn), lambda bi, i, j: (0, j)),
                  pl.BlockSpec((1, bm, bn), lambda bi, i, j: (bi, i, j)),
                  pl.BlockSpec((1, 1, bn), lambda bi, i, j: (bi, 0, j))],
        out_specs=pl.BlockSpec((1, bm, bn), lambda bi, i, j: (bi, i, j)),
        out_shape=jax.ShapeDtypeStruct((b, s, n), F32),
        compiler_params=_params("arbitrary", "arbitrary", "arbitrary"),
        name="matmul_residual",
    )(a, w, x, gt)


def _split3(x):
    hi = x.astype(BF16)
    r = x - hi.astype(F32)
    mid = r.astype(BF16)
    lo = (r - mid.astype(F32)).astype(BF16)
    return hi, mid, lo


def _forget_cumsum_kernel(fl_ref, bf_ref, col_ref, row_ref):
    s = fl_ref.shape[1]
    blk = 256
    logf = jax.nn.log_sigmoid(fl_ref[0] + bf_ref[...])
    r = lax.broadcasted_iota(jnp.int32, (blk, blk), 0)
    c = lax.broadcasted_iota(jnp.int32, (blk, blk), 1)
    tri = jnp.where(c <= r, 1.0, 0.0).astype(BF16)
    carry = jnp.zeros((1, LANES), F32)
    parts = []
    for i in range(s // blk):
        hi, mid, lo = _split3(logf[i * blk:(i + 1) * blk])
        cs = (jnp.dot(tri, hi, preferred_element_type=F32)
              + jnp.dot(tri, mid, preferred_element_type=F32)
              + jnp.dot(tri, lo, preferred_element_type=F32)) + carry
        carry = cs[blk - 1:blk, :]
        parts.append(cs)
    cum = jnp.concatenate(parts, axis=0)
    row_ref[0] = cum.T[:8, :]
    for h in range(FOX_HEADS):
        col_ref[0, h] = jnp.broadcast_to(cum[:, h:h + 1], (s, LANES))


def _forget_cumsum_call(f_logit, b_forget_pad):
    b, s, _ = f_logit.shape
    return pl.pallas_call(
        _forget_cumsum_kernel,
        grid=(b,),
        in_specs=[pl.BlockSpec((1, s, LANES), lambda i: (i, 0, 0)),
                  pl.BlockSpec((1, LANES), lambda i: (0, 0))],
        out_specs=[pl.BlockSpec((1, FOX_HEADS, s, LANES), lambda i: (i, 0, 0, 0)),
                   pl.BlockSpec((1, 8, s), lambda i: (i, 0, 0))],
        out_shape=[jax.ShapeDtypeStruct((b, FOX_HEADS, s, LANES), F32),
                   jax.ShapeDtypeStruct((b, 8, s), F32)],
        compiler_params=_params("arbitrary"),
        name="forget_cumsum",
    )(f_logit, b_forget_pad)


ATT_TILE = 512


def _qk(q, k):
    return lax.dot_general(q, k, (((1,), (1,)), ((), ())), preferred_element_type=F32)


def _kv_block(k_ref, v_ref, j):
    start = pl.multiple_of(j * ATT_TILE, ATT_TILE)
    return start, k_ref[pl.ds(start, ATT_TILE), :], v_ref[pl.ds(start, ATT_TILE), :]


def _tile_iotas():
    r = lax.broadcasted_iota(jnp.int32, (ATT_TILE, ATT_TILE), 0)
    c = lax.broadcasted_iota(jnp.int32, (ATT_TILE, ATT_TILE), 1)
    return r, c


def _softmax_state(rows=ATT_TILE):
    return [pltpu.VMEM((rows, LANES), F32), pltpu.VMEM((rows, LANES), F32), pltpu.VMEM((rows, HEAD_DIM), F32)]


def _softmax_reset(m_ref, l_ref, acc_ref):
    m_ref[...] = jnp.full(m_ref.shape, NEG_INF, F32)
    l_ref[...] = jnp.zeros(l_ref.shape, F32)
    acc_ref[...] = jnp.zeros(acc_ref.shape, F32)


def _softmax_update(m_ref, l_ref, acc_ref, s, vb, rows=slice(None)):
    m_old = m_ref[rows, :]
    m_new = jnp.maximum(m_old, jnp.max(s, axis=-1, keepdims=True))
    alpha = jnp.exp(m_old - m_new)
    p = jnp.exp(s - jnp.tile(m_new, (1, s.shape[1] // LANES)))
    l_ref[rows, :] = alpha * l_ref[rows, :] + jnp.sum(p, axis=-1, keepdims=True)
    acc_ref[rows, :] = alpha * acc_ref[rows, :] + jnp.dot(p.astype(BF16), vb, preferred_element_type=F32)
    m_ref[rows, :] = m_new


FOX_SUB = 2


def _fox_kernel(q_ref, k_ref, v_ref, fc_ref, fr_ref, o_ref, m_ref, l_ref, acc_ref):
    t, n = ATT_TILE, FOX_SUB
    qi = pl.program_id(2)
    h = pl.program_id(1)
    q = (q_ref[...].astype(F32) * (HEAD_DIM ** -0.5)).astype(BF16)

    def step(j, rows, masked):
        start, kb, vb = _kv_block(k_ref, v_ref, j)
        fcol = jnp.tile(fc_ref[0, 0, rows, :], (1, t // LANES))
        frow = fr_ref[0, pl.ds(h, 1), pl.ds(start, t)]
        s = _qk(q[rows], kb) + (fcol - frow)
        if masked:
            r, c = _tile_iotas()
            s = jnp.where(c <= r, s, NEG_INF)
        _softmax_update(m_ref, l_ref, acc_ref, s, vb, rows)

    _softmax_reset(m_ref, l_ref, acc_ref)
    for u in range(n):
        rows = slice(u * t, (u + 1) * t)
        step(qi * n + u, rows, masked=True)
        for d in range(u):
            step(qi * n + d, rows, masked=False)

    @pl.loop(0, qi * n)
    def _(j):
        step(j, slice(0, n * t), masked=False)

    o_ref[...] = (acc_ref[...] / l_ref[...]).astype(o_ref.dtype)


def _fox_call(proj, f_col, f_row, batch, seq):
    tq = ATT_TILE * FOX_SUB
    nq = seq // tq
    return pl.pallas_call(
        _fox_kernel,
        grid=(batch, FOX_HEADS, nq),
        in_specs=[
            pl.BlockSpec((tq, HEAD_DIM), lambda b, h, i: (b * nq + i, COL_FQ + h)),
            pl.BlockSpec((seq, HEAD_DIM), lambda b, h, i: (b, COL_FK + h)),
            pl.BlockSpec((seq, HEAD_DIM), lambda b, h, i: (b, COL_FV + h)),
            pl.BlockSpec((1, 1, tq, LANES), lambda b, h, i: (b, h, i, 0)),
            pl.BlockSpec((1, 8, seq), lambda b, h, i: (b, 0, 0)),
        ],
        out_specs=pl.BlockSpec((tq, HEAD_DIM), lambda b, h, i: (b * nq + i, h)),
        out_shape=jax.ShapeDtypeStruct((batch * seq, FOX_WIDTH), BF16),
        scratch_shapes=_softmax_state(rows=tq),
        compiler_params=_params("arbitrary", "arbitrary", "arbitrary"),
        name="fox_attention",
    )(proj, proj, proj, f_col, f_row)


SB_SEG = 256
SB_SUB = 2


def _sb_kernel(q_ref, k_ref, v_ref, tri_ref, o_ref, run_ref, acc_ref):
    t, n = ATT_TILE, SB_SUB
    qi = pl.program_id(2)
    log2e = math.log2(math.e)
    qn = (q_ref[...].astype(F32) * -(HEAD_DIM ** -0.5 * log2e)).astype(BF16)

    def step(j, rows, masked):
        _, kb, vb = _kv_block(k_ref, v_ref, j)
        tri = tri_ref[...]
        u = _qk(qn[rows], kb)
        lk = jnp.minimum(u, 0.0) - jnp.log(1.0 + jnp.exp2(-jnp.abs(u))) * log2e
        if masked:
            r, c = _tile_iotas()
            strictly_causal = c < r
            lk = jnp.where(strictly_causal, lk, 0.0)
        lk_bf = lk.astype(BF16)
        run = run_ref[rows, :]
        laters = []
        for g in reversed(range(t // SB_SEG)):
            sl = slice(g * SB_SEG, (g + 1) * SB_SEG)
            laters.append(jnp.dot(lk_bf[:, sl], tri, preferred_element_type=F32) + jnp.tile(run, (1, SB_SEG // LANES)))
            run = run + jnp.sum(lk[:, sl], axis=-1, keepdims=True)
        later = jnp.concatenate(laters[::-1], axis=1)
        w = jnp.exp2(lk - u + later)
        if masked:
            w = jnp.where(strictly_causal, w, 0.0)
        acc_ref[rows, :] += jnp.dot(w.astype(BF16), vb, preferred_element_type=F32)
        run_ref[rows, :] = run

    run_ref[...] = jnp.zeros(run_ref.shape, F32)
    acc_ref[...] = jnp.zeros(acc_ref.shape, F32)
    for s in range(n):
        rows = slice(s * t, (s + 1) * t)
        step(qi * n + s, rows, masked=True)
        for d in reversed(range(s)):
            step(qi * n + d, rows, masked=False)

    @pl.loop(0, qi * n)
    def _(i):
        step(qi * n - 1 - i, slice(0, n * t), masked=False)

    o_ref[...] = acc_ref[...].astype(o_ref.dtype)


def _sb_call(proj, tri, batch, seq):
    tq = ATT_TILE * SB_SUB
    nq = seq // tq
    return pl.pallas_call(
        _sb_kernel,
        grid=(batch, SB_HEADS, nq),
        in_specs=[
            pl.BlockSpec((tq, HEAD_DIM), lambda b, h, i: (b * nq + i, COL_SQ + h)),
            pl.BlockSpec((seq, HEAD_DIM), lambda b, h, i: (b, COL_SK + h)),
            pl.BlockSpec((seq, HEAD_DIM), lambda b, h, i: (b, COL_SV + h)),
            pl.BlockSpec((SB_SEG, SB_SEG), lambda b, h, i: (0, 0)),
        ],
        out_specs=pl.BlockSpec((tq, HEAD_DIM), lambda b, h, i: (b * nq + i, h)),
        out_shape=jax.ShapeDtypeStruct((batch * seq, SB_WIDTH), BF16),
        scratch_shapes=[pltpu.VMEM((tq, LANES), F32), pltpu.VMEM((tq, HEAD_DIM), F32)],
        compiler_params=_params("arbitrary", "arbitrary", "arbitrary"),
        name="sb_attention",
    )(proj, proj, proj, tri)


def _rope_kernel(x_ref, pos_ref, freq_ref, o_ref, *, out_scale):
    half = DIFF_QK_DIM // 2
    ang = pos_ref[...].astype(F32) * freq_ref[...]
    cos = jnp.cos(ang)
    sin = jnp.sin(ang)
    lane = lax.broadcasted_iota(jnp.int32, ang.shape, 1)
    first = (lane % DIFF_QK_DIM) < half
    sin_signed = jnp.where(first, -sin, sin)
    for cb in range(x_ref.shape[1] // LANES):
        x = x_ref[:, cb * LANES:(cb + 1) * LANES].astype(F32)
        partner = jnp.where(first, pltpu.roll(x, LANES - half, 1), pltpu.roll(x, half, 1))
        y = x * cos + partner * sin_signed
        o_ref[:, cb * LANES:(cb + 1) * LANES] = (y * out_scale).astype(o_ref.dtype)


def _rope_call(proj, pos_col, freq_row, col_block, out_scale):
    tokens = proj.shape[0]
    tr = 512
    width = DIFF_HEADS * HEAD_DIM
    return pl.pallas_call(
        functools.partial(_rope_kernel, out_scale=out_scale),
        grid=(tokens // tr,),
        in_specs=[pl.BlockSpec((tr, width), lambda i: (i, col_block * LANES // width)),
                  pl.BlockSpec((tr, 1), lambda i: (i, 0)),
                  pl.BlockSpec((1, LANES), lambda i: (0, 0))],
        out_specs=pl.BlockSpec((tr, width), lambda i: (i, 0)),
        out_shape=jax.ShapeDtypeStruct((tokens, width), BF16),
        compiler_params=_params("arbitrary"),
        name="rope",
    )(proj, pos_col, freq_row)


def _diff_kernel(q_ref, k_ref, v_ref, lq1_ref, lk1_ref, lq2_ref, lk2_ref, g_ref, o_ref,
                 m_ref, l_ref, acc_ref, *, lam_init):
    t = ATT_TILE
    qi = pl.program_id(2)
    q = q_ref[...]
    lane = lax.broadcasted_iota(jnp.int32, q.shape, 1)
    zero = jnp.zeros_like(q)
    q1 = jnp.where(lane < DIFF_QK_DIM, q, zero)
    q2 = jnp.where(lane < DIFF_QK_DIM, zero, q)
    q12 = jnp.concatenate([q1, q2], axis=0)
    state = (m_ref, l_ref, acc_ref)

    def step(j, masked):
        _, kb, vb = _kv_block(k_ref, v_ref, j)
        s = _qk(q12, kb)
        if masked:
            r, c = _tile_iotas()
            visible = (c // CHUNK) <= (r // CHUNK)
            s = jnp.where(jnp.concatenate([visible, visible], axis=0), s, NEG_INF)
        _softmax_update(*state, s, vb)

    _softmax_reset(*state)
    step(qi, masked=True)

    @pl.loop(0, qi)
    def _(j):
        step(j, masked=False)

    lam = (jnp.exp(jnp.sum(lq1_ref[0] * lk1_ref[0], axis=-1, keepdims=True))
           - jnp.exp(jnp.sum(lq2_ref[0] * lk2_ref[0], axis=-1, keepdims=True)) + lam_init)
    o = acc_ref[:t, :] / l_ref[:t, :] - lam * (acc_ref[t:, :] / l_ref[t:, :])
    o = o * lax.rsqrt(jnp.mean(o * o, axis=-1, keepdims=True) + RMS_EPS) * g_ref[...]
    o_ref[...] = (o * (1.0 - lam_init)).astype(o_ref.dtype)


def _diff_call(qd, kd, proj, lam_q1, lam_k1, lam_q2, lam_k2, subln_g, lam_init, batch, seq):
    t = ATT_TILE
    nq = seq // t
    lam_spec = pl.BlockSpec((1, 1, DIFF_QK_DIM), lambda b, h, i: (h, 0, 0))
    r3 = lambda a: a.reshape(DIFF_HEADS, 1, DIFF_QK_DIM)
    return pl.pallas_call(
        functools.partial(_diff_kernel, lam_init=lam_init),
        grid=(batch, DIFF_HEADS, nq),
        in_specs=[
            pl.BlockSpec((t, HEAD_DIM), lambda b, h, i: (b * nq + i, h)),
            pl.BlockSpec((seq, HEAD_DIM), lambda b, h, i: (b, h)),
            pl.BlockSpec((seq, HEAD_DIM), lambda b, h, i: (b, COL_DV + h)),
            lam_spec, lam_spec, lam_spec, lam_spec,
            pl.BlockSpec((1, HEAD_DIM), lambda b, h, i: (0, 0)),
        ],
        out_specs=pl.BlockSpec((t, HEAD_DIM), lambda b, h, i: (b * nq + i, h)),
        out_shape=jax.ShapeDtypeStruct((batch * seq, DIFF_WIDTH), BF16),
        scratch_shapes=_softmax_state(rows=2 * ATT_TILE),
        compiler_params=_params("arbitrary", "arbitrary", "arbitrary"),
        name="diff_attention",
    )(qd, kd, proj, r3(lam_q1), r3(lam_k1), r3(lam_q2), r3(lam_k2), subln_g.reshape(1, HEAD_DIM))


def _merge_kernel(of_ref, os_ref, od_ref, wf_ref, ws_ref, wd_ref, gf_ref, gs_ref, gd_ref, o_ref):
    def branch(o_r, w_r, g_r):
        return jax.nn.sigmoid(g_r[...].astype(F32)) * jnp.dot(o_r[...], w_r[...], preferred_element_type=F32)

    y = branch(of_ref, wf_ref, gf_ref) + branch(os_ref, ws_ref, gs_ref) + branch(od_ref, wd_ref, gd_ref)
    o_ref[...] = y.astype(o_ref.dtype)


def _merge_call(o_fox, o_sb, o_diff, w_fox, w_sb, w_diff, gates, bm=1024, bn=1024):
    m = o_fox.shape[0]
    d = w_fox.shape[1]
    gate_blocks = d // bn

    def gate_spec(k):
        return pl.BlockSpec((bm, bn), lambda i, j: (i, k * gate_blocks + j))

    return pl.pallas_call(
        _merge_kernel,
        grid=(m // bm, d // bn),
        in_specs=[
            pl.BlockSpec((bm, FOX_WIDTH), lambda i, j: (i, 0)),
            pl.BlockSpec((bm, SB_WIDTH), lambda i, j: (i, 0)),
            pl.BlockSpec((bm, DIFF_WIDTH), lambda i, j: (i, 0)),
            pl.BlockSpec((FOX_WIDTH, bn), lambda i, j: (0, j)),
            pl.BlockSpec((SB_WIDTH, bn), lambda i, j: (0, j)),
            pl.BlockSpec((DIFF_WIDTH, bn), lambda i, j: (0, j)),
            gate_spec(0), gate_spec(1), gate_spec(2),
        ],
        out_specs=pl.BlockSpec((bm, bn), lambda i, j: (i, j)),
        out_shape=jax.ShapeDtypeStruct((m, d), BF16),
        compiler_params=_params("arbitrary", "arbitrary"),
        name="branch_merge",
    )(o_fox, o_sb, o_diff, w_fox, w_sb, w_diff, gates, gates, gates)


NOT_TOP = 64.0


def _extract_top(cur, count, want_rank=False):
    rows = []
    rank = jnp.full(cur.shape, NOT_TOP, F32) if want_rank else None
    for r in range(count):
        m = jnp.max(cur, axis=0, keepdims=True)
        rows.append(m)
        hit = cur == m
        if want_rank:
            rank = jnp.where(hit, float(r), rank)
        cur = jnp.where(hit, NEG_INF, cur)
    return rows, rank


def _peer_topk_kernel(q_ref, keys_ref, a_ref, l_ref, b_ref, r2_ref):
    k = PEER_TOPK
    for h in range(PEER_HEADS):
        col = 2 * h * PEER_HALF
        s1 = _qk(keys_ref[h, 0], q_ref[:, col:col + PEER_HALF])
        s2 = _qk(keys_ref[h, 1], q_ref[:, col + PEER_HALF:col + 2 * PEER_HALF])
        a_rows, _ = _extract_top(s1, k)
        b_rows, rank2 = _extract_top(s2, k, want_rank=True)
        b_top = jnp.concatenate(b_rows, axis=0)
        counts = [k // (k1 + 1) for k1 in range(k)]
        cands = [a_rows[k1] + b_top[:counts[k1]] for k1 in range(k)]
        n_cand = sum(counts)
        pad = jnp.full((-n_cand % 8, b_top.shape[1]), NEG_INF, F32)
        best, _ = _extract_top(jnp.concatenate(cands + [pad], axis=0), k)
        tau = best[k - 1]
        z = jnp.zeros_like(tau)
        for row in best:
            z = z + jnp.exp(row - best[0])
        l_row = jnp.zeros(s1.shape, F32)
        for k1 in range(k):
            l_k = jnp.sum(jnp.where(cands[k1] >= tau, 1.0, 0.0), axis=0, keepdims=True)
            l_row = jnp.where(s1 == a_rows[k1], l_k, l_row)
        a_ref[h] = jnp.exp(s1 - a_rows[0]) / z
        l_ref[h] = l_row
        b_ref[h] = jnp.exp(s2 - b_rows[0]).astype(b_ref.dtype)
        r2_ref[h] = rank2.astype(r2_ref.dtype)


def _peer_topk_call(qp, keys):
    tokens = qp.shape[0]
    tq = 512
    spec = pl.BlockSpec((PEER_HEADS, PEER_N_KEYS, tq), lambda i: (0, 0, i))
    shape = lambda dt: jax.ShapeDtypeStruct((PEER_HEADS, PEER_N_KEYS, tokens), dt)
    return pl.pallas_call(
        _peer_topk_kernel,
        grid=(tokens // tq,),
        in_specs=[pl.BlockSpec((tq, qp.shape[1]), lambda i: (i, 0)),
                  pl.BlockSpec(keys.shape, lambda i: (0, 0, 0, 0))],
        out_specs=[spec, spec, spec, spec],
        out_shape=[shape(F32), shape(F32), shape(BF16), shape(BF16)],
        compiler_params=_params("arbitrary"),
        name="peer_topk",
    )(qp, keys)


PEER_UP_TOKENS = 1024
PEER_UP_EXPERTS = 1024
PEER_TOKEN_CHUNK = 256
PEER_DOWN_TOKENS = 512
PEER_DOWN_ROWS = 16


def _peer_up_kernel(ht_ref, u_ref, o_ref):
    tc = PEER_TOKEN_CHUNK
    u = u_ref[0].astype(BF16)
    for c in range(o_ref.shape[1] // tc):
        cols = slice(c * tc, (c + 1) * tc)
        hid = jnp.dot(u, ht_ref[0, :, cols], preferred_element_type=F32)
        gelu = 0.5 * hid * (1.0 + lax.erf(hid * (2.0 ** -0.5)))
        o_ref[:, cols] = gelu.astype(o_ref.dtype)


def _peer_down_kernel(vt_ref, g_ref, a_ref, l_ref, b_ref, r2_ref, x_ref, gt_ref, o_ref, acc_ref, act_ref):
    k = pl.program_id(1)
    tn = act_ref.shape[1]

    @pl.when(k == 0)
    def _():
        acc_ref[...] = jnp.zeros_like(acc_ref)

    for r in range(PEER_DOWN_ROWS):
        rows = slice(r * PEER_N_KEYS, (r + 1) * PEER_N_KEYS)

        def key_row(ref, h):
            tile = jnp.broadcast_to(ref[h, r:r + 1, :], (BF16_ROWS, tn)).astype(BF16)
            return jnp.tile(tile, (PEER_N_KEYS // BF16_ROWS, 1))

        gate = jnp.zeros((PEER_N_KEYS, tn), BF16)
        for h in range(PEER_HEADS):
            chosen = r2_ref[h] < key_row(l_ref, h)
            gate = gate + jnp.where(chosen, b_ref[h], jnp.zeros((), BF16)) * key_row(a_ref, h)
        act_ref[rows, :] = gate * g_ref[rows, :]

    acc_ref[...] += jnp.dot(vt_ref[...], act_ref[...], preferred_element_type=F32)

    @pl.when(k == pl.num_programs(1) - 1)
    def _():
        o_ref[0] = x_ref[0] + gt_ref[0] * acc_ref[...].T


def _peer_ffn_call(ht, u, layer, vt, a_t, l_t, b_t, r2_t, x, gt):
    b, d, s = ht.shape
    n_experts = u.shape[1]
    tm, te = PEER_UP_TOKENS, PEER_UP_EXPERTS
    per_b = s // tm
    gelu_t = pl.pallas_call(
        _peer_up_kernel,
        grid=(b * per_b, n_experts // te),
        in_specs=[pl.BlockSpec((1, d, tm), lambda i, e: (i // per_b, 0, i % per_b)),
                  pl.BlockSpec((1, te, d), lambda i, e: (layer, e, 0))],
        out_specs=pl.BlockSpec((te, tm), lambda i, e: (e, i)),
        out_shape=jax.ShapeDtypeStruct((n_experts, b * s), BF16),
        compiler_params=_params("arbitrary", "arbitrary"),
        name="peer_up",
    )(ht, u)

    tn, tk = PEER_DOWN_TOKENS, PEER_DOWN_ROWS * PEER_N_KEYS
    per_b = s // tn
    key_spec = pl.BlockSpec((PEER_HEADS, PEER_N_KEYS, tn), lambda i, k: (0, 0, i))
    row_spec = pl.BlockSpec((PEER_HEADS, PEER_DOWN_ROWS, tn), lambda i, k: (0, k, i))
    return pl.pallas_call(
        _peer_down_kernel,
        grid=(b * per_b, n_experts // tk),
        in_specs=[
            pl.BlockSpec((d, tk), lambda i, k: (0, k)),
            pl.BlockSpec((tk, tn), lambda i, k: (k, i)),
            row_spec, row_spec, key_spec, key_spec,
            pl.BlockSpec((1, tn, d), lambda i, k: (i // per_b, i % per_b, 0)),
            pl.BlockSpec((1, 1, d), lambda i, k: (i // per_b, 0, 0)),
        ],
        out_specs=pl.BlockSpec((1, tn, d), lambda i, k: (i // per_b, i % per_b, 0)),
        out_shape=jax.ShapeDtypeStruct(x.shape, F32),
        scratch_shapes=[pltpu.VMEM((d, tn), F32), pltpu.VMEM((tk, tn), BF16)],
        compiler_params=_params("arbitrary", "arbitrary"),
        name="peer_down",
    )(vt, gelu_t, a_t, l_t, b_t, r2_t, x, gt)


def _peer_vt_kernel(v_ref, vt_ref):
    vt_ref[...] = v_ref[0].T.astype(BF16)


def _peer_vt_call(v, layer):
    _, e, d = v.shape
    te = 512
    return pl.pallas_call(
        _peer_vt_kernel,
        grid=(e // te,),
        in_specs=[pl.BlockSpec((1, te, d), lambda i: (layer, i, 0))],
        out_specs=pl.BlockSpec((d, te), lambda i: (0, i)),
        out_shape=jax.ShapeDtypeStruct((d, e), BF16),
        compiler_params=_params("arbitrary"),
        name="peer_vt",
    )(v)


def kernel(x, c, positions, ada_w, ada_b, norm1_g, norm2_g, w_in, b_forget, lam_q1, lam_k1, lam_q2, lam_k2,
           diff_subln_g, w_br_fox, w_br_sb, w_br_diff, w_o, peer_wq, peer_sub_keys, peer_u, peer_v,
           final_norm_g):
    batch, seq, d = x.shape
    tokens = batch * seq
    assert batch <= 8 and seq % 1024 == 0 and d == D_MODEL

    c_pad = jnp.zeros((8, d), F32).at[:batch].set(c)
    mod = _mod_call(c_pad, ada_w, ada_b)[:, :batch].reshape(DEPTH, batch, N_MOD, 1, d)
    pos_col = positions.reshape(tokens, 1)
    inv_freq = ROPE_THETA ** (-jnp.arange(0, DIFF_QK_DIM, 2, dtype=F32) / DIFF_QK_DIM)
    freq_row = jnp.tile(inv_freq, LANES // inv_freq.shape[0]).reshape(1, LANES)
    ri = np.arange(SB_SEG)
    tri_after = jnp.asarray(ri[:, None] > ri[None, :], BF16)

    for l in range(DEPTH):
        lam_init = 0.8 - 0.6 * math.exp(-0.3 * l)
        sh1, sc1, gt1, sh2, sc2, gt2 = (mod[l, :, k] for k in range(N_MOD))

        h1 = _norm_mod_call(x, norm1_g[l].reshape(1, d), sc1, sh1, BF16)[0].reshape(tokens, d)
        proj = _proj_call(h1, w_in, l, 0, QKV_WIDTH, BF16, bn=1024)
        gates = _proj_call(h1, w_in, l, GATE_OFFSET, N_BRANCHES * d, BF16, bn=1024)
        f_logit = _proj_call(h1, w_in, l, QKV_WIDTH, LANES, F32, bn=LANES, keep=FOX_HEADS)
        f_logit = f_logit.reshape(batch, seq, LANES)
        b_forget_pad = jnp.pad(b_forget[l], (0, LANES - FOX_HEADS)).reshape(1, LANES)
        f_col, f_row = _forget_cumsum_call(f_logit, b_forget_pad)

        o_fox = _fox_call(proj, f_col, f_row, batch, seq)
        o_sb = _sb_call(proj, tri_after, batch, seq)
        qd = _rope_call(proj, pos_col, freq_row, COL_DQ, DIFF_QK_DIM ** -0.5)
        kd = _rope_call(proj, pos_col, freq_row, COL_DK, 1.0)
        o_diff = _diff_call(qd, kd, proj, lam_q1[l], lam_k1[l], lam_q2[l], lam_k2[l], diff_subln_g[l],
                            lam_init, batch, seq)

        y = _merge_call(o_fox, o_sb, o_diff, w_br_fox[l].astype(BF16), w_br_sb[l].astype(BF16),
                        w_br_diff[l].astype(BF16), gates)
        x = _matmul_residual_call(y.reshape(batch, seq, d), w_o[l].astype(BF16), x, gt1)

        h2, h2t = _norm_mod_call(x, norm2_g[l].reshape(1, d), sc2, sh2, BF16, transposed=True)
        qp = _proj_call(h2.reshape(tokens, d), peer_wq, l, 0, peer_wq.shape[2], BF16, bn=1024)
        a_t, l_t, b_t, r2_t = _peer_topk_call(qp, peer_sub_keys[l].astype(BF16))
        x = _peer_ffn_call(h2t, peer_u, l, _peer_vt_call(peer_v, l), a_t, l_t, b_t, r2_t, x, gt2)

    zeros = jnp.zeros((batch, 1, d), F32)
    return _norm_mod_call(x, final_norm_g.reshape(1, d), zeros, zeros, F32)[0]
```

```python
import functools
import math

import jax
import jax.numpy as jnp
import numpy as np
from jax import lax
from jax.experimental import pallas as pl
from jax.experimental.pallas import tpu as pltpu

F32 = jnp.float32
BF16 = jnp.bfloat16

D_MODEL = 2048
DEPTH = 2
CHUNK = 64
HEAD_DIM = 128
FOX_HEADS = 6
SB_HEADS = 6
DIFF_HEADS = 4
DIFF_QK_DIM = HEAD_DIM // 2
FOX_WIDTH = FOX_HEADS * HEAD_DIM
SB_WIDTH = SB_HEADS * HEAD_DIM
DIFF_WIDTH = DIFF_HEADS * HEAD_DIM
QKV_WIDTH = 3 * FOX_WIDTH + 3 * SB_WIDTH + 3 * DIFF_WIDTH
GATE_OFFSET = QKV_WIDTH + FOX_HEADS
ROPE_THETA = 10000.0
RMS_EPS = 1e-6
N_MOD = 6
N_BRANCHES = 3
PEER_HEADS = 8
PEER_N_KEYS = 128
PEER_TOPK = 16
PEER_HALF = 128

LANES = 128
BF16_ROWS = 16
VMEM_LIMIT = 56 * 1024 * 1024
NEG_INF = float("-inf")

COL_FQ, COL_FK, COL_FV = 0, FOX_HEADS, 2 * FOX_HEADS
COL_SQ, COL_SK, COL_SV = 18, 24, 30
COL_DQ, COL_DK, COL_DV = 36, 40, 44
COL_GATE = QKV_WIDTH // LANES


def _params(*sem):
    return pltpu.CompilerParams(dimension_semantics=sem, vmem_limit_bytes=VMEM_LIMIT)


def _mod_kernel(c_ref, w_ref, b_ref, o_ref):
    c = c_ref[...]
    cond = c * jax.nn.sigmoid(c)
    hi = cond.astype(BF16)
    lo = (cond - hi.astype(F32)).astype(BF16)
    w = w_ref[0].astype(BF16)
    acc = jnp.dot(hi, w, preferred_element_type=F32) + jnp.dot(lo, w, preferred_element_type=F32)
    o_ref[0] = acc + b_ref[0]


def _mod_call(c_pad, ada_w, ada_b):
    depth, d, n = ada_w.shape
    tn = 1024
    return pl.pallas_call(
        _mod_kernel,
        grid=(depth, n // tn),
        in_specs=[
            pl.BlockSpec((8, d), lambda l, j: (0, 0)),
            pl.BlockSpec((1, d, tn), lambda l, j: (l, 0, j)),
            pl.BlockSpec((1, 1, tn), lambda l, j: (l, 0, j)),
        ],
        out_specs=pl.BlockSpec((1, 8, tn), lambda l, j: (l, 0, j)),
        out_shape=jax.ShapeDtypeStruct((depth, 8, n), F32),
        compiler_params=_params("arbitrary", "arbitrary"),
        name="adaln_mod",
    )(c_pad, ada_w, ada_b.reshape(depth, 1, n))


def _norm_mod_kernel(x_ref, g_ref, sc_ref, sh_ref, *o_refs, transposed):
    x = x_ref[0]
    y = x * lax.rsqrt(jnp.mean(x * x, axis=-1, keepdims=True) + RMS_EPS) * g_ref[...]
    h = y * (1.0 + sc_ref[0]) + sh_ref[0]
    o_refs[0][0] = h.astype(o_refs[0].dtype)
    if transposed:
        o_refs[1][0] = h.T.astype(o_refs[1].dtype)


def _norm_mod_call(x, g, sc, sh, out_dtype, transposed=False):
    b, s, d = x.shape
    tr = 512
    out_shape = [jax.ShapeDtypeStruct((b, s, d), out_dtype)]
    out_specs = [pl.BlockSpec((1, tr, d), lambda i, j: (i, j, 0))]
    if transposed:
        out_shape.append(jax.ShapeDtypeStruct((b, d, s), out_dtype))
        out_specs.append(pl.BlockSpec((1, d, tr), lambda i, j: (i, 0, j)))
    return pl.pallas_call(
        functools.partial(_norm_mod_kernel, transposed=transposed),
        grid=(b, s // tr),
        in_specs=[
            pl.BlockSpec((1, tr, d), lambda i, j: (i, j, 0)),
            pl.BlockSpec((1, d), lambda i, j: (0, 0)),
            pl.BlockSpec((1, 1, d), lambda i, j: (i, 0, 0)),
            pl.BlockSpec((1, 1, d), lambda i, j: (i, 0, 0)),
        ],
        out_specs=out_specs,
        out_shape=out_shape,
        compiler_params=_params("arbitrary", "arbitrary"),
        name="norm_mod",
    )(x, g, sc, sh)


def _matmul_kernel(a_ref, w_ref, o_ref):
    o_ref[...] = jnp.dot(a_ref[...], w_ref[...], preferred_element_type=F32).astype(o_ref.dtype)


def _matmul_call(a, w, out_dtype, bm=1024, bn=1024):
    m, k = a.shape
    n = w.shape[1]
    bn = min(bn, n)
    return pl.pallas_call(
        _matmul_kernel,
        grid=(m // bm, n // bn),
        in_specs=[pl.BlockSpec((bm, k), lambda i, j: (i, 0)),
                  pl.BlockSpec((k, bn), lambda i, j: (0, j))],
        out_specs=pl.BlockSpec((bm, bn), lambda i, j: (i, j)),
        out_shape=jax.ShapeDtypeStruct((m, n), out_dtype),
        compiler_params=_params("arbitrary", "arbitrary"),
        name="matmul",
    )(a, w)


def _proj_kernel(a_ref, w_ref, *rest, shift, keep):
    o_ref, wb_ref = rest[-2:]

    @pl.when(pl.program_id(1) == 0)
    def _():
        w = w_ref[0]
        if shift:
            w = jnp.concatenate([w[:, shift:], rest[0][0][:, :shift]], axis=1)
        if keep is not None:
            lane = lax.broadcasted_iota(jnp.int32, w.shape, 1)
            w = jnp.where(lane < keep, w, 0.0)
        wb_ref[...] = w.astype(BF16)

    o_ref[...] = jnp.dot(a_ref[...], wb_ref[...], preferred_element_type=F32).astype(o_ref.dtype)


def _proj_call(a, w, layer, col0, n, out_dtype, bn, keep=None, bm=1024):
    m, k = a.shape
    shift = col0 % LANES
    base = (col0 - shift) // bn
    assert (col0 - shift) % bn == 0 and n % bn == 0 and bn % LANES == 0
    next_stride = bn // LANES
    in_specs = [pl.BlockSpec((bm, k), lambda j, i: (i, 0)),
                pl.BlockSpec((1, k, bn), lambda j, i: (layer, 0, base + j))]
    operands = [a, w]
    if shift:
        in_specs.append(pl.BlockSpec((1, k, LANES), lambda j, i: (layer, 0, (base + j + 1) * next_stride)))
        operands.append(w)
    return pl.pallas_call(
        functools.partial(_proj_kernel, shift=shift, keep=keep),
        grid=(n // bn, m // bm),
        in_specs=in_specs,
        out_specs=pl.BlockSpec((bm, bn), lambda j, i: (i, j)),
        out_shape=jax.ShapeDtypeStruct((m, n), out_dtype),
        scratch_shapes=[pltpu.VMEM((k, bn), BF16)],
        compiler_params=_params("arbitrary", "arbitrary"),
        name="proj",
    )(*operands)


def _matmul_residual_kernel(a_ref, w_ref, x_ref, gt_ref, o_ref):
    acc = jnp.dot(a_ref[0], w_ref[...], preferred_element_type=F32)
    o_ref[0] = x_ref[0] + gt_ref[0] * acc


def _matmul_residual_call(a, w, x, gt, bm=1024, bn=1024):
    b, s, k = a.shape
    n = w.shape[1]
    return pl.pallas_call(
        _matmul_residual_kernel,
        grid=(b, s // bm, n // bn),
        in_specs=[pl.BlockSpec((1, bm, k), lambda bi, i, j: (bi, i, 0)),
                  pl.BlockSpec((k, bn), lambda bi, i, j: (0, j)),
                  pl.BlockSpec((1, bm, bn), lambda bi, i, j: (bi, i, j)),
                  pl.BlockSpec((1, 1, bn), lambda bi, i, j: (bi, 0, j))],
        out_specs=pl.BlockSpec((1, bm, bn), lambda bi, i, j: (bi, i, j)),
        out_shape=jax.ShapeDtypeStruct((b, s, n), F32),
        compiler_params=_params("arbitrary", "arbitrary", "arbitrary"),
        name="matmul_residual",
    )(a, w, x, gt)


def _split3(x):
    hi = x.astype(BF16)
    r = x - hi.astype(F32)
    mid = r.astype(BF16)
    lo = (r - mid.astype(F32)).astype(BF16)
    return hi, mid, lo


def _forget_cumsum_kernel(fl_ref, bf_ref, col_ref, row_ref):
    s = fl_ref.shape[1]
    blk = 256
    logf = jax.nn.log_sigmoid(fl_ref[0] + bf_ref[...])
    r = lax.broadcasted_iota(jnp.int32, (blk, blk), 0)
    c = lax.broadcasted_iota(jnp.int32, (blk, blk), 1)
    tri = jnp.where(c <= r, 1.0, 0.0).astype(BF16)
    carry = jnp.zeros((1, LANES), F32)
    parts = []
    for i in range(s // blk):
        hi, mid, lo = _split3(logf[i * blk:(i + 1) * blk])
        cs = (jnp.dot(tri, hi, preferred_element_type=F32)
              + jnp.dot(tri, mid, preferred_element_type=F32)
              + jnp.dot(tri, lo, preferred_element_type=F32)) + carry
        carry = cs[blk - 1:blk, :]
        parts.append(cs)
    cum = jnp.concatenate(parts, axis=0)
    row_ref[0] = cum.T[:8, :]
    for h in range(FOX_HEADS):
        col_ref[0, h] = jnp.broadcast_to(cum[:, h:h + 1], (s, LANES))


def _forget_cumsum_call(f_logit, b_forget_pad):
    b, s, _ = f_logit.shape
    return pl.pallas_call(
        _forget_cumsum_kernel,
        grid=(b,),
        in_specs=[pl.BlockSpec((1, s, LANES), lambda i: (i, 0, 0)),
                  pl.BlockSpec((1, LANES), lambda i: (0, 0))],
        out_specs=[pl.BlockSpec((1, FOX_HEADS, s, LANES), lambda i: (i, 0, 0, 0)),
                   pl.BlockSpec((1, 8, s), lambda i: (i, 0, 0))],
        out_shape=[jax.ShapeDtypeStruct((b, FOX_HEADS, s, LANES), F32),
                   jax.ShapeDtypeStruct((b, 8, s), F32)],
        compiler_params=_params("arbitrary"),
        name="forget_cumsum",
    )(f_logit, b_forget_pad)


ATT_TILE = 512


def _qk(q, k):
    return lax.dot_general(q, k, (((1,), (1,)), ((), ())), preferred_element_type=F32)


def _kv_block(k_ref, v_ref, j):
    start = pl.multiple_of(j * ATT_TILE, ATT_TILE)
    return start, k_ref[pl.ds(start, ATT_TILE), :], v_ref[pl.ds(start, ATT_TILE), :]


def _tile_iotas():
    r = lax.broadcasted_iota(jnp.int32, (ATT_TILE, ATT_TILE), 0)
    c = lax.broadcasted_iota(jnp.int32, (ATT_TILE, ATT_TILE), 1)
    return r, c


def _softmax_state(rows=ATT_TILE):
    return [pltpu.VMEM((rows, LANES), F32), pltpu.VMEM((rows, LANES), F32), pltpu.VMEM((rows, HEAD_DIM), F32)]


def _softmax_reset(m_ref, l_ref, acc_ref):
    m_ref[...] = jnp.full(m_ref.shape, NEG_INF, F32)
    l_ref[...] = jnp.zeros(l_ref.shape, F32)
    acc_ref[...] = jnp.zeros(acc_ref.shape, F32)


def _softmax_update(m_ref, l_ref, acc_ref, s, vb, rows=slice(None)):
    m_old = m_ref[rows, :]
    m_new = jnp.maximum(m_old, jnp.max(s, axis=-1, keepdims=True))
    alpha = jnp.exp(m_old - m_new)
    p = jnp.exp(s - jnp.tile(m_new, (1, s.shape[1] // LANES)))
    l_ref[rows, :] = alpha * l_ref[rows, :] + jnp.sum(p, axis=-1, keepdims=True)
    acc_ref[rows, :] = alpha * acc_ref[rows, :] + jnp.dot(p.astype(BF16), vb, preferred_element_type=F32)
    m_ref[rows, :] = m_new


FOX_SUB = 2


def _fox_kernel(q_ref, k_ref, v_ref, fc_ref, fr_ref, o_ref, m_ref, l_ref, acc_ref):
    t, n = ATT_TILE, FOX_SUB
    qi = pl.program_id(2)
    h = pl.program_id(1)
    q = (q_ref[...].astype(F32) * (HEAD_DIM ** -0.5)).astype(BF16)

    def step(j, rows, masked):
        start, kb, vb = _kv_block(k_ref, v_ref, j)
        fcol = jnp.tile(fc_ref[0, 0, rows, :], (1, t // LANES))
        frow = fr_ref[0, pl.ds(h, 1), pl.ds(start, t)]
        s = _qk(q[rows], kb) + (fcol - frow)
        if masked:
            r, c = _tile_iotas()
            s = jnp.where(c <= r, s, NEG_INF)
        _softmax_update(m_ref, l_ref, acc_ref, s, vb, rows)

    _softmax_reset(m_ref, l_ref, acc_ref)
    for u in range(n):
        rows = slice(u * t, (u + 1) * t)
        step(qi * n + u, rows, masked=True)
        for d in range(u):
            step(qi * n + d, rows, masked=False)

    @pl.loop(0, qi * n)
    def _(j):
        step(j, slice(0, n * t), masked=False)

    o_ref[...] = (acc_ref[...] / l_ref[...]).astype(o_ref.dtype)


def _fox_call(proj, f_col, f_row, batch, seq):
    tq = ATT_TILE * FOX_SUB
    nq = seq // tq
    return pl.pallas_call(
        _fox_kernel,
        grid=(batch, FOX_HEADS, nq),
        in_specs=[
            pl.BlockSpec((tq, HEAD_DIM), lambda b, h, i: (b * nq + i, COL_FQ + h)),
            pl.BlockSpec((seq, HEAD_DIM), lambda b, h, i: (b, COL_FK + h)),
            pl.BlockSpec((seq, HEAD_DIM), lambda b, h, i: (b, COL_FV + h)),
            pl.BlockSpec((1, 1, tq, LANES), lambda b, h, i: (b, h, i, 0)),
            pl.BlockSpec((1, 8, seq), lambda b, h, i: (b, 0, 0)),
        ],
        out_specs=pl.BlockSpec((tq, HEAD_DIM), lambda b, h, i: (b * nq + i, h)),
        out_shape=jax.ShapeDtypeStruct((batch * seq, FOX_WIDTH), BF16),
        scratch_shapes=_softmax_state(rows=tq),
        compiler_params=_params("arbitrary", "arbitrary", "arbitrary"),
        name="fox_attention",
    )(proj, proj, proj, f_col, f_row)


SB_SEG = 256
SB_SUB = 2


def _sb_kernel(q_ref, k_ref, v_ref, tri_ref, o_ref, run_ref, acc_ref):
    t, n = ATT_TILE, SB_SUB
    qi = pl.program_id(2)
    log2e = math.log2(math.e)
    qn = (q_ref[...].astype(F32) * -(HEAD_DIM ** -0.5 * log2e)).astype(BF16)

    def step(j, rows, masked):
        _, kb, vb = _kv_block(k_ref, v_ref, j)
        tri = tri_ref[...]
        u = _qk(qn[rows], kb)
        lk = jnp.minimum(u, 0.0) - jnp.log(1.0 + jnp.exp2(-jnp.abs(u))) * log2e
        if masked:
            r, c = _tile_iotas()
            strictly_causal = c < r
            lk = jnp.where(strictly_causal, lk, 0.0)
        lk_bf = lk.astype(BF16)
        run = run_ref[rows, :]
        laters = []
        for g in reversed(range(t // SB_SEG)):
            sl = slice(g * SB_SEG, (g + 1) * SB_SEG)
            laters.append(jnp.dot(lk_bf[:, sl], tri, preferred_element_type=F32) + jnp.tile(run, (1, SB_SEG // LANES)))
            run = run + jnp.sum(lk[:, sl], axis=-1, keepdims=True)
        later = jnp.concatenate(laters[::-1], axis=1)
        w = jnp.exp2(lk - u + later)
        if masked:
            w = jnp.where(strictly_causal, w, 0.0)
        acc_ref[rows, :] += jnp.dot(w.astype(BF16), vb, preferred_element_type=F32)
        run_ref[rows, :] = run

    run_ref[...] = jnp.zeros(run_ref.shape, F32)
    acc_ref[...] = jnp.zeros(acc_ref.shape, F32)
    for s in range(n):
        rows = slice(s * t, (s + 1) * t)
        step(qi * n + s, rows, masked=True)
        for d in reversed(range(s)):
            step(qi * n + d, rows, masked=False)

    @pl.loop(0, qi * n)
    def _(i):
        step(qi * n - 1 - i, slice(0, n * t), masked=False)

    o_ref[...] = acc_ref[...].astype(o_ref.dtype)


def _sb_call(proj, tri, batch, seq):
    tq = ATT_TILE * SB_SUB
    nq = seq // tq
    return pl.pallas_call(
        _sb_kernel,
        grid=(batch, SB_HEADS, nq),
        in_specs=[
            pl.BlockSpec((tq, HEAD_DIM), lambda b, h, i: (b * nq + i, COL_SQ + h)),
            pl.BlockSpec((seq, HEAD_DIM), lambda b, h, i: (b, COL_SK + h)),
            pl.BlockSpec((seq, HEAD_DIM), lambda b, h, i: (b, COL_SV + h)),
            pl.BlockSpec((SB_SEG, SB_SEG), lambda b, h, i: (0, 0)),
        ],
        out_specs=pl.BlockSpec((tq, HEAD_DIM), lambda b, h, i: (b * nq + i, h)),
        out_shape=jax.ShapeDtypeStruct((batch * seq, SB_WIDTH), BF16),
        scratch_shapes=[pltpu.VMEM((tq, LANES), F32), pltpu.VMEM((tq, HEAD_DIM), F32)],
        compiler_params=_params("arbitrary", "arbitrary", "arbitrary"),
        name="sb_attention",
    )(proj, proj, proj, tri)


def _rope_kernel(x_ref, pos_ref, freq_ref, o_ref, *, out_scale):
    half = DIFF_QK_DIM // 2
    ang = pos_ref[...].astype(F32) * freq_ref[...]
    cos = jnp.cos(ang)
    sin = jnp.sin(ang)
    lane = lax.broadcasted_iota(jnp.int32, ang.shape, 1)
    first = (lane % DIFF_QK_DIM) < half
    sin_signed = jnp.where(first, -sin, sin)
    for cb in range(x_ref.shape[1] // LANES):
        x = x_ref[:, cb * LANES:(cb + 1) * LANES].astype(F32)
        partner = jnp.where(first, pltpu.roll(x, LANES - half, 1), pltpu.roll(x, half, 1))
        y = x * cos + partner * sin_signed
        o_ref[:, cb * LANES:(cb + 1) * LANES] = (y * out_scale).astype(o_ref.dtype)


def _rope_call(proj, pos_col, freq_row, col_block, out_scale):
    tokens = proj.shape[0]
    tr = 512
    width = DIFF_HEADS * HEAD_DIM
    return pl.pallas_call(
        functools.partial(_rope_kernel, out_scale=out_scale),
        grid=(tokens // tr,),
        in_specs=[pl.BlockSpec((tr, width), lambda i: (i, col_block * LANES // width)),
                  pl.BlockSpec((tr, 1), lambda i: (i, 0)),
                  pl.BlockSpec((1, LANES), lambda i: (0, 0))],
        out_specs=pl.BlockSpec((tr, width), lambda i: (i, 0)),
        out_shape=jax.ShapeDtypeStruct((tokens, width), BF16),
        compiler_params=_params("arbitrary"),
        name="rope",
    )(proj, pos_col, freq_row)


def _diff_kernel(q_ref, k_ref, v_ref, lq1_ref, lk1_ref, lq2_ref, lk2_ref, g_ref, o_ref,
                 m_ref, l_ref, acc_ref, *, lam_init):
    t = ATT_TILE
    qi = pl.program_id(2)
    n = DIFF_SUB
    lane = lax.broadcasted_iota(jnp.int32, (t, HEAD_DIM), 1)
    stacked = []
    for u in range(n):
        q = q_ref[u * t:(u + 1) * t, :]
        zero = jnp.zeros_like(q)
        stacked += [jnp.where(lane < DIFF_QK_DIM, q, zero), jnp.where(lane < DIFF_QK_DIM, zero, q)]
    q12 = jnp.concatenate(stacked, axis=0)
    state = (m_ref, l_ref, acc_ref)

    def step(j, rows, masked):
        _, kb, vb = _kv_block(k_ref, v_ref, j)
        s = _qk(q12[rows], kb)
        if masked:
            r, c = _tile_iotas()
            visible = (c // CHUNK) <= (r // CHUNK)
            s = jnp.where(jnp.concatenate([visible, visible], axis=0), s, NEG_INF)
        _softmax_update(*state, s, vb, rows)

    _softmax_reset(*state)
    for u in range(n):
        rows = slice(u * 2 * t, (u + 1) * 2 * t)
        step(qi * n + u, rows, masked=True)
        for d in range(u):
            step(qi * n + d, rows, masked=False)

    @pl.loop(0, qi * n)
    def _(j):
        step(j, slice(0, n * 2 * t), masked=False)

    lam = (jnp.exp(jnp.sum(lq1_ref[0] * lk1_ref[0], axis=-1, keepdims=True))
           - jnp.exp(jnp.sum(lq2_ref[0] * lk2_ref[0], axis=-1, keepdims=True)) + lam_init)
    for u in range(n):
        first = slice(u * 2 * t, u * 2 * t + t)
        second = slice(u * 2 * t + t, (u + 1) * 2 * t)
        o = acc_ref[first, :] / l_ref[first, :] - lam * (acc_ref[second, :] / l_ref[second, :])
        o = o * lax.rsqrt(jnp.mean(o * o, axis=-1, keepdims=True) + RMS_EPS) * g_ref[...]
        o_ref[u * t:(u + 1) * t, :] = (o * (1.0 - lam_init)).astype(o_ref.dtype)


DIFF_SUB = 2


def _diff_call(qd, kd, proj, lam_q1, lam_k1, lam_q2, lam_k2, subln_g, lam_init, batch, seq):
    t = ATT_TILE * DIFF_SUB
    nq = seq // t
    lam_spec = pl.BlockSpec((1, 1, DIFF_QK_DIM), lambda b, h, i: (h, 0, 0))
    r3 = lambda a: a.reshape(DIFF_HEADS, 1, DIFF_QK_DIM)
    return pl.pallas_call(
        functools.partial(_diff_kernel, lam_init=lam_init),
        grid=(batch, DIFF_HEADS, nq),
        in_specs=[
            pl.BlockSpec((t, HEAD_DIM), lambda b, h, i: (b * nq + i, h)),
            pl.BlockSpec((seq, HEAD_DIM), lambda b, h, i: (b, h)),
            pl.BlockSpec((seq, HEAD_DIM), lambda b, h, i: (b, COL_DV + h)),
            lam_spec, lam_spec, lam_spec, lam_spec,
            pl.BlockSpec((1, HEAD_DIM), lambda b, h, i: (0, 0)),
        ],
        out_specs=pl.BlockSpec((t, HEAD_DIM), lambda b, h, i: (b * nq + i, h)),
        out_shape=jax.ShapeDtypeStruct((batch * seq, DIFF_WIDTH), BF16),
        scratch_shapes=_softmax_state(rows=2 * t),
        compiler_params=_params("arbitrary", "arbitrary", "arbitrary"),
        name="diff_attention",
    )(qd, kd, proj, r3(lam_q1), r3(lam_k1), r3(lam_q2), r3(lam_k2), subln_g.reshape(1, HEAD_DIM))


def _merge_kernel(of_ref, os_ref, od_ref, wf_ref, ws_ref, wd_ref, gf_ref, gs_ref, gd_ref, o_ref):
    def branch(o_r, w_r, g_r):
        return jax.nn.sigmoid(g_r[...].astype(F32)) * jnp.dot(o_r[...], w_r[...], preferred_element_type=F32)

    y = branch(of_ref, wf_ref, gf_ref) + branch(os_ref, ws_ref, gs_ref) + branch(od_ref, wd_ref, gd_ref)
    o_ref[...] = y.astype(o_ref.dtype)


def _merge_call(o_fox, o_sb, o_diff, w_fox, w_sb, w_diff, gates, bm=1024, bn=1024):
    m = o_fox.shape[0]
    d = w_fox.shape[1]
    gate_blocks = d // bn

    def gate_spec(k):
        return pl.BlockSpec((bm, bn), lambda i, j: (i, k * gate_blocks + j))

    return pl.pallas_call(
        _merge_kernel,
        grid=(m // bm, d // bn),
        in_specs=[
            pl.BlockSpec((bm, FOX_WIDTH), lambda i, j: (i, 0)),
            pl.BlockSpec((bm, SB_WIDTH), lambda i, j: (i, 0)),
            pl.BlockSpec((bm, DIFF_WIDTH), lambda i, j: (i, 0)),
            pl.BlockSpec((FOX_WIDTH, bn), lambda i, j: (0, j)),
            pl.BlockSpec((SB_WIDTH, bn), lambda i, j: (0, j)),
            pl.BlockSpec((DIFF_WIDTH, bn), lambda i, j: (0, j)),
            gate_spec(0), gate_spec(1), gate_spec(2),
        ],
        out_specs=pl.BlockSpec((bm, bn), lambda i, j: (i, j)),
        out_shape=jax.ShapeDtypeStruct((m, d), BF16),
        compiler_params=_params("arbitrary", "arbitrary"),
        name="branch_merge",
    )(o_fox, o_sb, o_diff, w_fox, w_sb, w_diff, gates, gates, gates)


NOT_TOP = 64.0


def _extract_top(cur, count, want_rank=False):
    rows = []
    rank = jnp.full(cur.shape, NOT_TOP, F32) if want_rank else None
    for r in range(count):
        m = jnp.max(cur, axis=0, keepdims=True)
        rows.append(m)
        hit = cur == m
        if want_rank:
            rank = jnp.where(hit, float(r), rank)
        cur = jnp.where(hit, NEG_INF, cur)
    return rows, rank


def _peer_topk_kernel(q_ref, keys_ref, a_ref, l_ref, b_ref, r2_ref):
    k = PEER_TOPK
    for h in range(PEER_HEADS):
        col = 2 * h * PEER_HALF
        s1 = _qk(keys_ref[h, 0], q_ref[:, col:col + PEER_HALF])
        s2 = _qk(keys_ref[h, 1], q_ref[:, col + PEER_HALF:col + 2 * PEER_HALF])
        a_rows, _ = _extract_top(s1, k)
        b_rows, rank2 = _extract_top(s2, k, want_rank=True)
        b_top = jnp.concatenate(b_rows, axis=0)
        counts = [k // (k1 + 1) for k1 in range(k)]
        cands = [a_rows[k1] + b_top[:counts[k1]] for k1 in range(k)]
        n_cand = sum(counts)
        pad = jnp.full((-n_cand % 8, b_top.shape[1]), NEG_INF, F32)
        best, _ = _extract_top(jnp.concatenate(cands + [pad], axis=0), k)
        tau = best[k - 1]
        z = jnp.zeros_like(tau)
        for row in best:
            z = z + jnp.exp(row - best[0])
        l_row = jnp.zeros(s1.shape, F32)
        for k1 in range(k):
            l_k = jnp.sum(jnp.where(cands[k1] >= tau, 1.0, 0.0), axis=0, keepdims=True)
            l_row = jnp.where(s1 == a_rows[k1], l_k, l_row)
        a_ref[h] = jnp.exp(s1 - a_rows[0]) / z
        l_ref[h] = l_row
        b_ref[h] = jnp.exp(s2 - b_rows[0]).astype(b_ref.dtype)
        r2_ref[h] = rank2.astype(r2_ref.dtype)


def _peer_topk_call(qp, keys):
    tokens = qp.shape[0]
    tq = 512
    spec = pl.BlockSpec((PEER_HEADS, PEER_N_KEYS, tq), lambda i: (0, 0, i))
    shape = lambda dt: jax.ShapeDtypeStruct((PEER_HEADS, PEER_N_KEYS, tokens), dt)
    return pl.pallas_call(
        _peer_topk_kernel,
        grid=(tokens // tq,),
        in_specs=[pl.BlockSpec((tq, qp.shape[1]), lambda i: (i, 0)),
                  pl.BlockSpec(keys.shape, lambda i: (0, 0, 0, 0))],
        out_specs=[spec, spec, spec, spec],
        out_shape=[shape(F32), shape(F32), shape(BF16), shape(BF16)],
        compiler_params=_params("arbitrary"),
        name="peer_topk",
    )(qp, keys)


PEER_UP_TOKENS = 1024
PEER_UP_EXPERTS = 1024
PEER_TOKEN_CHUNK = 256
PEER_DOWN_TOKENS = 512
PEER_DOWN_ROWS = 16


def _peer_up_kernel(ht_ref, u_ref, o_ref):
    tc = PEER_TOKEN_CHUNK
    u = u_ref[0].astype(BF16)
    for c in range(o_ref.shape[1] // tc):
        cols = slice(c * tc, (c + 1) * tc)
        hid = jnp.dot(u, ht_ref[0, :, cols], preferred_element_type=F32)
        gelu = 0.5 * hid * (1.0 + lax.erf(hid * (2.0 ** -0.5)))
        o_ref[:, cols] = gelu.astype(o_ref.dtype)


def _peer_down_kernel(vt_ref, g_ref, a_ref, l_ref, b_ref, r2_ref, x_ref, gt_ref, o_ref, acc_ref, act_ref):
    k = pl.program_id(1)
    tn = act_ref.shape[1]

    @pl.when(k == 0)
    def _():
        acc_ref[...] = jnp.zeros_like(acc_ref)

    for r in range(PEER_DOWN_ROWS):
        rows = slice(r * PEER_N_KEYS, (r + 1) * PEER_N_KEYS)

        def key_row(ref, h):
            tile = jnp.broadcast_to(ref[h, r:r + 1, :], (BF16_ROWS, tn)).astype(BF16)
            return jnp.tile(tile, (PEER_N_KEYS // BF16_ROWS, 1))

        gate = jnp.zeros((PEER_N_KEYS, tn), BF16)
        for h in range(PEER_HEADS):
            chosen = r2_ref[h] < key_row(l_ref, h)
            gate = gate + jnp.where(chosen, b_ref[h], jnp.zeros((), BF16)) * key_row(a_ref, h)
        act_ref[rows, :] = gate * g_ref[rows, :]

    acc_ref[...] += jnp.dot(vt_ref[...], act_ref[...], preferred_element_type=F32)

    @pl.when(k == pl.num_programs(1) - 1)
    def _():
        o_ref[0] = x_ref[0] + gt_ref[0] * acc_ref[...].T


def _peer_ffn_call(ht, u, layer, vt, a_t, l_t, b_t, r2_t, x, gt):
    b, d, s = ht.shape
    n_experts = u.shape[1]
    tm, te = PEER_UP_TOKENS, PEER_UP_EXPERTS
    per_b = s // tm
    gelu_t = pl.pallas_call(
        _peer_up_kernel,
        grid=(b * per_b, n_experts // te),
        in_specs=[pl.BlockSpec((1, d, tm), lambda i, e: (i // per_b, 0, i % per_b)),
                  pl.BlockSpec((1, te, d), lambda i, e: (layer, e, 0))],
        out_specs=pl.BlockSpec((te, tm), lambda i, e: (e, i)),
        out_shape=jax.ShapeDtypeStruct((n_experts, b * s), BF16),
        compiler_params=_params("arbitrary", "arbitrary"),
        name="peer_up",
    )(ht, u)

    tn, tk = PEER_DOWN_TOKENS, PEER_DOWN_ROWS * PEER_N_KEYS
    per_b = s // tn
    key_spec = pl.BlockSpec((PEER_HEADS, PEER_N_KEYS, tn), lambda i, k: (0, 0, i))
    row_spec = pl.BlockSpec((PEER_HEADS, PEER_DOWN_ROWS, tn), lambda i, k: (0, k, i))
    return pl.pallas_call(
        _peer_down_kernel,
        grid=(b * per_b, n_experts // tk),
        in_specs=[
            pl.BlockSpec((d, tk), lambda i, k: (0, k)),
            pl.BlockSpec((tk, tn), lambda i, k: (k, i)),
            row_spec, row_spec, key_spec, key_spec,
            pl.BlockSpec((1, tn, d), lambda i, k: (i // per_b, i % per_b, 0)),
            pl.BlockSpec((1, 1, d), lambda i, k: (i // per_b, 0, 0)),
        ],
        out_specs=pl.BlockSpec((1, tn, d), lambda i, k: (i // per_b, i % per_b, 0)),
        out_shape=jax.ShapeDtypeStruct(x.shape, F32),
        scratch_shapes=[pltpu.VMEM((d, tn), F32), pltpu.VMEM((tk, tn), BF16)],
        compiler_params=_params("arbitrary", "arbitrary"),
        name="peer_down",
    )(vt, gelu_t, a_t, l_t, b_t, r2_t, x, gt)


def _peer_vt_kernel(v_ref, vt_ref):
    vt_ref[...] = v_ref[0].T.astype(BF16)


def _peer_vt_call(v, layer):
    _, e, d = v.shape
    te = 512
    return pl.pallas_call(
        _peer_vt_kernel,
        grid=(e // te,),
        in_specs=[pl.BlockSpec((1, te, d), lambda i: (layer, i, 0))],
        out_specs=pl.BlockSpec((d, te), lambda i: (0, i)),
        out_shape=jax.ShapeDtypeStruct((d, e), BF16),
        compiler_params=_params("arbitrary"),
        name="peer_vt",
    )(v)


def kernel(x, c, positions, ada_w, ada_b, norm1_g, norm2_g, w_in, b_forget, lam_q1, lam_k1, lam_q2, lam_k2,
           diff_subln_g, w_br_fox, w_br_sb, w_br_diff, w_o, peer_wq, peer_sub_keys, peer_u, peer_v,
           final_norm_g):
    batch, seq, d = x.shape
    tokens = batch * seq
    assert batch <= 8 and seq % 1024 == 0 and d == D_MODEL

    c_pad = jnp.zeros((8, d), F32).at[:batch].set(c)
    mod = _mod_call(c_pad, ada_w, ada_b)[:, :batch].reshape(DEPTH, batch, N_MOD, 1, d)
    pos_col = positions.reshape(tokens, 1)
    inv_freq = ROPE_THETA ** (-jnp.arange(0, DIFF_QK_DIM, 2, dtype=F32) / DIFF_QK_DIM)
    freq_row = jnp.tile(inv_freq, LANES // inv_freq.shape[0]).reshape(1, LANES)
    ri = np.arange(SB_SEG)
    tri_after = jnp.asarray(ri[:, None] > ri[None, :], BF16)

    for l in range(DEPTH):
        lam_init = 0.8 - 0.6 * math.exp(-0.3 * l)
        sh1, sc1, gt1, sh2, sc2, gt2 = (mod[l, :, k] for k in range(N_MOD))

        h1 = _norm_mod_call(x, norm1_g[l].reshape(1, d), sc1, sh1, BF16)[0].reshape(tokens, d)
        proj = _proj_call(h1, w_in, l, 0, QKV_WIDTH, BF16, bn=1024)
        gates = _proj_call(h1, w_in, l, GATE_OFFSET, N_BRANCHES * d, BF16, bn=1024)
        f_logit = _proj_call(h1, w_in, l, QKV_WIDTH, LANES, F32, bn=LANES, keep=FOX_HEADS)
        f_logit = f_logit.reshape(batch, seq, LANES)
        b_forget_pad = jnp.pad(b_forget[l], (0, LANES - FOX_HEADS)).reshape(1, LANES)
        f_col, f_row = _forget_cumsum_call(f_logit, b_forget_pad)

        o_fox = _fox_call(proj, f_col, f_row, batch, seq)
        o_sb = _sb_call(proj, tri_after, batch, seq)
        qd = _rope_call(proj, pos_col, freq_row, COL_DQ, DIFF_QK_DIM ** -0.5)
        kd = _rope_call(proj, pos_col, freq_row, COL_DK, 1.0)
        o_diff = _diff_call(qd, kd, proj, lam_q1[l], lam_k1[l], lam_q2[l], lam_k2[l], diff_subln_g[l],
                            lam_init, batch, seq)

        y = _merge_call(o_fox, o_sb, o_diff, w_br_fox[l].astype(BF16), w_br_sb[l].astype(BF16),
                        w_br_diff[l].astype(BF16), gates)
        x = _matmul_residual_call(y.reshape(batch, seq, d), w_o[l].astype(BF16), x, gt1)

        h2, h2t = _norm_mod_call(x, norm2_g[l].reshape(1, d), sc2, sh2, BF16, transposed=True)
        qp = _proj_call(h2.reshape(tokens, d), peer_wq, l, 0, peer_wq.shape[2], BF16, bn=1024)
        a_t, l_t, b_t, r2_t = _peer_topk_call(qp, peer_sub_keys[l].astype(BF16))
        x = _peer_ffn_call(h2t, peer_u, l, _peer_vt_call(peer_v, l), a_t, l_t, b_t, r2_t, x, gt2)

    zeros = jnp.zeros((batch, 1, d), F32)
    return _norm_mod_call(x, final_norm_g.reshape(1, d), zeros, zeros, F32)[0]
```

```python
import functools
import math

import jax
import jax.numpy as jnp
import numpy as np
from jax import lax
from jax.experimental import pallas as pl
from jax.experimental.pallas import tpu as pltpu

F32 = jnp.float32
BF16 = jnp.bfloat16

D_MODEL = 2048
DEPTH = 2
CHUNK = 64
HEAD_DIM = 128
FOX_HEADS = 6
SB_HEADS = 6
DIFF_HEADS = 4
DIFF_QK_DIM = HEAD_DIM // 2
FOX_WIDTH = FOX_HEADS * HEAD_DIM
SB_WIDTH = SB_HEADS * HEAD_DIM
DIFF_WIDTH = DIFF_HEADS * HEAD_DIM
QKV_WIDTH = 3 * FOX_WIDTH + 3 * SB_WIDTH + 3 * DIFF_WIDTH
GATE_OFFSET = QKV_WIDTH + FOX_HEADS
ROPE_THETA = 10000.0
RMS_EPS = 1e-6
N_MOD = 6
N_BRANCHES = 3
PEER_HEADS = 8
PEER_N_KEYS = 128
PEER_TOPK = 16
PEER_HALF = 128

LANES = 128
BF16_ROWS = 16
VMEM_LIMIT = 56 * 1024 * 1024
NEG_INF = float("-inf")

COL_FQ, COL_FK, COL_FV = 0, FOX_HEADS, 2 * FOX_HEADS
COL_SQ, COL_SK, COL_SV = 18, 24, 30
COL_DQ, COL_DK, COL_DV = 36, 40, 44
COL_GATE = QKV_WIDTH // LANES


def _params(*sem):
    return pltpu.CompilerParams(dimension_semantics=sem, vmem_limit_bytes=VMEM_LIMIT)


def _mod_kernel(c_ref, w_ref, b_ref, o_ref):
    c = c_ref[...]
    cond = c * jax.nn.sigmoid(c)
    hi = cond.astype(BF16)
    lo = (cond - hi.astype(F32)).astype(BF16)
    w = w_ref[0].astype(BF16)
    acc = jnp.dot(hi, w, preferred_element_type=F32) + jnp.dot(lo, w, preferred_element_type=F32)
    o_ref[0] = acc + b_ref[0]


def _mod_call(c_pad, ada_w, ada_b):
    depth, d, n = ada_w.shape
    tn = 1024
    return pl.pallas_call(
        _mod_kernel,
        grid=(depth, n // tn),
        in_specs=[
            pl.BlockSpec((8, d), lambda l, j: (0, 0)),
            pl.BlockSpec((1, d, tn), lambda l, j: (l, 0, j)),
            pl.BlockSpec((1, 1, tn), lambda l, j: (l, 0, j)),
        ],
        out_specs=pl.BlockSpec((1, 8, tn), lambda l, j: (l, 0, j)),
        out_shape=jax.ShapeDtypeStruct((depth, 8, n), F32),
        compiler_params=_params("arbitrary", "arbitrary"),
        name="adaln_mod",
    )(c_pad, ada_w, ada_b.reshape(depth, 1, n))


def _norm_mod_kernel(x_ref, g_ref, sc_ref, sh_ref, *o_refs, transposed):
    x = x_ref[0]
    y = x * lax.rsqrt(jnp.mean(x * x, axis=-1, keepdims=True) + RMS_EPS) * g_ref[...]
    h = y * (1.0 + sc_ref[0]) + sh_ref[0]
    o_refs[0][0] = h.astype(o_refs[0].dtype)
    if transposed:
        o_refs[1][0] = h.T.astype(o_refs[1].dtype)


def _norm_mod_call(x, g, sc, sh, out_dtype, transposed=False):
    b, s, d = x.shape
    tr = 512
    out_shape = [jax.ShapeDtypeStruct((b, s, d), out_dtype)]
    out_specs = [pl.BlockSpec((1, tr, d), lambda i, j: (i, j, 0))]
    if transposed:
        out_shape.append(jax.ShapeDtypeStruct((b, d, s), out_dtype))
        out_specs.append(pl.BlockSpec((1, d, tr), lambda i, j: (i, 0, j)))
    return pl.pallas_call(
        functools.partial(_norm_mod_kernel, transposed=transposed),
        grid=(b, s // tr),
        in_specs=[
            pl.BlockSpec((1, tr, d), lambda i, j: (i, j, 0)),
            pl.BlockSpec((1, d), lambda i, j: (0, 0)),
            pl.BlockSpec((1, 1, d), lambda i, j: (i, 0, 0)),
            pl.BlockSpec((1, 1, d), lambda i, j: (i, 0, 0)),
        ],
        out_specs=out_specs,
        out_shape=out_shape,
        compiler_params=_params("arbitrary", "arbitrary"),
        name="norm_mod",
    )(x, g, sc, sh)


def _matmul_kernel(a_ref, w_ref, o_ref):
    o_ref[...] = jnp.dot(a_ref[...], w_ref[...], preferred_element_type=F32).astype(o_ref.dtype)


def _matmul_call(a, w, out_dtype, bm=1024, bn=1024):
    m, k = a.shape
    n = w.shape[1]
    bn = min(bn, n)
    return pl.pallas_call(
        _matmul_kernel,
        grid=(m // bm, n // bn),
        in_specs=[pl.BlockSpec((bm, k), lambda i, j: (i, 0)),
                  pl.BlockSpec((k, bn), lambda i, j: (0, j))],
        out_specs=pl.BlockSpec((bm, bn), lambda i, j: (i, j)),
        out_shape=jax.ShapeDtypeStruct((m, n), out_dtype),
        compiler_params=_params("arbitrary", "arbitrary"),
        name="matmul",
    )(a, w)


def _proj_kernel(a_ref, w_ref, *rest, shift, keep):
    o_ref, wb_ref = rest[-2:]

    @pl.when(pl.program_id(1) == 0)
    def _():
        w = w_ref[0]
        if shift:
            w = jnp.concatenate([w[:, shift:], rest[0][0][:, :shift]], axis=1)
        if keep is not None:
            lane = lax.broadcasted_iota(jnp.int32, w.shape, 1)
            w = jnp.where(lane < keep, w, 0.0)
        wb_ref[...] = w.astype(BF16)

    o_ref[...] = jnp.dot(a_ref[...], wb_ref[...], preferred_element_type=F32).astype(o_ref.dtype)


def _proj_call(a, w, layer, col0, n, out_dtype, bn, keep=None, bm=1024):
    m, k = a.shape
    shift = col0 % LANES
    base = (col0 - shift) // bn
    assert (col0 - shift) % bn == 0 and n % bn == 0 and bn % LANES == 0
    next_stride = bn // LANES
    in_specs = [pl.BlockSpec((bm, k), lambda j, i: (i, 0)),
                pl.BlockSpec((1, k, bn), lambda j, i: (layer, 0, base + j))]
    operands = [a, w]
    if shift:
        in_specs.append(pl.BlockSpec((1, k, LANES), lambda j, i: (layer, 0, (base + j + 1) * next_stride)))
        operands.append(w)
    return pl.pallas_call(
        functools.partial(_proj_kernel, shift=shift, keep=keep),
        grid=(n // bn, m // bm),
        in_specs=in_specs,
        out_specs=pl.BlockSpec((bm, bn), lambda j, i: (i, j)),
        out_shape=jax.ShapeDtypeStruct((m, n), out_dtype),
        scratch_shapes=[pltpu.VMEM((k, bn), BF16)],
        compiler_params=_params("arbitrary", "arbitrary"),
        name="proj",
    )(*operands)


def _matmul_residual_kernel(a_ref, w_ref, x_ref, gt_ref, o_ref):
    acc = jnp.dot(a_ref[0], w_ref[...], preferred_element_type=F32)
    o_ref[0] = x_ref[0] + gt_ref[0] * acc


def _matmul_residual_call(a, w, x, gt, bm=1024, bn=1024):
    b, s, k = a.shape
    n = w.shape[1]
    return pl.pallas_call(
        _matmul_residual_kernel,
        grid=(b, s // bm, n // bn),
        in_specs=[pl.BlockSpec((1, bm, k), lambda bi, i, j: (bi, i, 0)),
                  pl.BlockSpec((k, bn), lambda bi, i, j: (0, j)),
                  pl.BlockSpec((1, bm, bn), lambda bi, i, j: (bi, i, j)),
                  pl.BlockSpec((1, 1, bn), lambda bi, i, j: (bi, 0, j))],
        out_specs=pl.BlockSpec((1, bm, bn), lambda bi, i, j: (bi, i, j)),
        out_shape=jax.ShapeDtypeStruct((b, s, n), F32),
        compiler_params=_params("arbitrary", "arbitrary", "arbitrary"),
        name="matmul_residual",
    )(a, w, x, gt)


def _split3(x):
    hi = x.astype(BF16)
    r = x - hi.astype(F32)
    mid = r.astype(BF16)
    lo = (r - mid.astype(F32)).astype(BF16)
    return hi, mid, lo


def _forget_cumsum_kernel(fl_ref, bf_ref, col_ref, row_ref):
    s = fl_ref.shape[1]
    blk = 256
    logf = jax.nn.log_sigmoid(fl_ref[0] + bf_ref[...])
    r = lax.broadcasted_iota(jnp.int32, (blk, blk), 0)
    c = lax.broadcasted_iota(jnp.int32, (blk, blk), 1)
    tri = jnp.where(c <= r, 1.0, 0.0).astype(BF16)
    carry = jnp.zeros((1, LANES), F32)
    parts = []
    for i in range(s // blk):
        hi, mid, lo = _split3(logf[i * blk:(i + 1) * blk])
        cs = (jnp.dot(tri, hi, preferred_element_type=F32)
              + jnp.dot(tri, mid, preferred_element_type=F32)
              + jnp.dot(tri, lo, preferred_element_type=F32)) + carry
        carry = cs[blk - 1:blk, :]
        parts.append(cs)
    cum = jnp.concatenate(parts, axis=0)
    row_ref[0] = cum.T[:8, :]
    for h in range(FOX_HEADS):
        col_ref[0, h] = jnp.broadcast_to(cum[:, h:h + 1], (s, LANES))


def _forget_cumsum_call(f_logit, b_forget_pad):
    b, s, _ = f_logit.shape
    return pl.pallas_call(
        _forget_cumsum_kernel,
        grid=(b,),
        in_specs=[pl.BlockSpec((1, s, LANES), lambda i: (i, 0, 0)),
                  pl.BlockSpec((1, LANES), lambda i: (0, 0))],
        out_specs=[pl.BlockSpec((1, FOX_HEADS, s, LANES), lambda i: (i, 0, 0, 0)),
                   pl.BlockSpec((1, 8, s), lambda i: (i, 0, 0))],
        out_shape=[jax.ShapeDtypeStruct((b, FOX_HEADS, s, LANES), F32),
                   jax.ShapeDtypeStruct((b, 8, s), F32)],
        compiler_params=_params("arbitrary"),
        name="forget_cumsum",
    )(f_logit, b_forget_pad)


ATT_TILE = 512


def _qk(q, k):
    return lax.dot_general(q, k, (((1,), (1,)), ((), ())), preferred_element_type=F32)


def _kv_block(k_ref, v_ref, j):
    start = pl.multiple_of(j * ATT_TILE, ATT_TILE)
    return start, k_ref[pl.ds(start, ATT_TILE), :], v_ref[pl.ds(start, ATT_TILE), :]


def _tile_iotas():
    r = lax.broadcasted_iota(jnp.int32, (ATT_TILE, ATT_TILE), 0)
    c = lax.broadcasted_iota(jnp.int32, (ATT_TILE, ATT_TILE), 1)
    return r, c


def _softmax_state(rows=ATT_TILE):
    return [pltpu.VMEM((rows, LANES), F32), pltpu.VMEM((rows, LANES), F32), pltpu.VMEM((rows, HEAD_DIM), F32)]


def _softmax_reset(m_ref, l_ref, acc_ref):
    m_ref[...] = jnp.full(m_ref.shape, NEG_INF, F32)
    l_ref[...] = jnp.zeros(l_ref.shape, F32)
    acc_ref[...] = jnp.zeros(acc_ref.shape, F32)


def _softmax_update(m_ref, l_ref, acc_ref, s, vb, rows=slice(None)):
    m_old = m_ref[rows, :]
    m_new = jnp.maximum(m_old, jnp.max(s, axis=-1, keepdims=True))
    alpha = jnp.exp(m_old - m_new)
    p = jnp.exp(s - jnp.tile(m_new, (1, s.shape[1] // LANES)))
    l_ref[rows, :] = alpha * l_ref[rows, :] + jnp.sum(p, axis=-1, keepdims=True)
    acc_ref[rows, :] = alpha * acc_ref[rows, :] + jnp.dot(p.astype(BF16), vb, preferred_element_type=F32)
    m_ref[rows, :] = m_new


FOX_SUB = 2


def _fox_kernel(q_ref, k_ref, v_ref, fc_ref, fr_ref, o_ref, m_ref, l_ref, acc_ref):
    t, n = ATT_TILE, FOX_SUB
    qi = pl.program_id(2)
    h = pl.program_id(1)
    q = (q_ref[...].astype(F32) * (HEAD_DIM ** -0.5)).astype(BF16)

    def step(j, rows, masked):
        start, kb, vb = _kv_block(k_ref, v_ref, j)
        fcol = jnp.tile(fc_ref[0, 0, rows, :], (1, t // LANES))
        frow = fr_ref[0, pl.ds(h, 1), pl.ds(start, t)]
        s = _qk(q[rows], kb) + (fcol - frow)
        if masked:
            r, c = _tile_iotas()
            s = jnp.where(c <= r, s, NEG_INF)
        _softmax_update(m_ref, l_ref, acc_ref, s, vb, rows)

    _softmax_reset(m_ref, l_ref, acc_ref)
    for u in range(n):
        rows = slice(u * t, (u + 1) * t)
        step(qi * n + u, rows, masked=True)
        for d in range(u):
            step(qi * n + d, rows, masked=False)

    @pl.loop(0, qi * n)
    def _(j):
        step(j, slice(0, n * t), masked=False)

    o_ref[...] = (acc_ref[...] / l_ref[...]).astype(o_ref.dtype)


def _fox_call(proj, f_col, f_row, batch, seq):
    tq = ATT_TILE * FOX_SUB
    nq = seq // tq
    return pl.pallas_call(
        _fox_kernel,
        grid=(batch, FOX_HEADS, nq),
        in_specs=[
            pl.BlockSpec((tq, HEAD_DIM), lambda b, h, i: (b * nq + i, COL_FQ + h)),
            pl.BlockSpec((seq, HEAD_DIM), lambda b, h, i: (b, COL_FK + h)),
            pl.BlockSpec((seq, HEAD_DIM), lambda b, h, i: (b, COL_FV + h)),
            pl.BlockSpec((1, 1, tq, LANES), lambda b, h, i: (b, h, i, 0)),
            pl.BlockSpec((1, 8, seq), lambda b, h, i: (b, 0, 0)),
        ],
        out_specs=pl.BlockSpec((tq, HEAD_DIM), lambda b, h, i: (b * nq + i, h)),
        out_shape=jax.ShapeDtypeStruct((batch * seq, FOX_WIDTH), BF16),
        scratch_shapes=_softmax_state(rows=tq),
        compiler_params=_params("arbitrary", "arbitrary", "arbitrary"),
        name="fox_attention",
    )(proj, proj, proj, f_col, f_row)


SB_SEG = 256
SB_SUB = 2


def _sb_kernel(q_ref, k_ref, v_ref, tri_ref, o_ref, run_ref, acc_ref):
    t, n = ATT_TILE, SB_SUB
    qi = pl.program_id(2)
    log2e = math.log2(math.e)
    qn = (q_ref[...].astype(F32) * -(HEAD_DIM ** -0.5 * log2e)).astype(BF16)

    def step(j, rows, masked):
        _, kb, vb = _kv_block(k_ref, v_ref, j)
        tri = tri_ref[...]
        u = _qk(qn[rows], kb)
        lk = jnp.minimum(u, 0.0) - jnp.log(1.0 + jnp.exp2(-jnp.abs(u))) * log2e
        if masked:
            r, c = _tile_iotas()
            strictly_causal = c < r
            lk = jnp.where(strictly_causal, lk, 0.0)
        lk_bf = lk.astype(BF16)
        run = run_ref[rows, :]
        laters = []
        for g in reversed(range(t // SB_SEG)):
            sl = slice(g * SB_SEG, (g + 1) * SB_SEG)
            laters.append(jnp.dot(lk_bf[:, sl], tri, preferred_element_type=F32) + jnp.tile(run, (1, SB_SEG // LANES)))
            run = run + jnp.sum(lk[:, sl], axis=-1, keepdims=True)
        later = jnp.concatenate(laters[::-1], axis=1)
        w = jnp.exp2(lk - u + later)
        if masked:
            w = jnp.where(strictly_causal, w, 0.0)
        acc_ref[rows, :] += jnp.dot(w.astype(BF16), vb, preferred_element_type=F32)
        run_ref[rows, :] = run

    run_ref[...] = jnp.zeros(run_ref.shape, F32)
    acc_ref[...] = jnp.zeros(acc_ref.shape, F32)
    for s in range(n):
        rows = slice(s * t, (s + 1) * t)
        step(qi * n + s, rows, masked=True)
        for d in reversed(range(s)):
            step(qi * n + d, rows, masked=False)

    @pl.loop(0, qi * n)
    def _(i):
        step(qi * n - 1 - i, slice(0, n * t), masked=False)

    o_ref[...] = acc_ref[...].astype(o_ref.dtype)


def _sb_call(proj, tri, batch, seq):
    tq = ATT_TILE * SB_SUB
    nq = seq // tq
    return pl.pallas_call(
        _sb_kernel,
        grid=(batch, SB_HEADS, nq),
        in_specs=[
            pl.BlockSpec((tq, HEAD_DIM), lambda b, h, i: (b * nq + i, COL_SQ + h)),
            pl.BlockSpec((seq, HEAD_DIM), lambda b, h, i: (b, COL_SK + h)),
            pl.BlockSpec((seq, HEAD_DIM), lambda b, h, i: (b, COL_SV + h)),
            pl.BlockSpec((SB_SEG, SB_SEG), lambda b, h, i: (0, 0)),
        ],
        out_specs=pl.BlockSpec((tq, HEAD_DIM), lambda b, h, i: (b * nq + i, h)),
        out_shape=jax.ShapeDtypeStruct((batch * seq, SB_WIDTH), BF16),
        scratch_shapes=[pltpu.VMEM((tq, LANES), F32), pltpu.VMEM((tq, HEAD_DIM), F32)],
        compiler_params=_params("arbitrary", "arbitrary", "arbitrary"),
        name="sb_attention",
    )(proj, proj, proj, tri)


def _rope_kernel(x_ref, pos_ref, freq_ref, o_ref, *, out_scale):
    half = DIFF_QK_DIM // 2
    ang = pos_ref[...].astype(F32) * freq_ref[...]
    cos = jnp.cos(ang)
    sin = jnp.sin(ang)
    lane = lax.broadcasted_iota(jnp.int32, ang.shape, 1)
    first = (lane % DIFF_QK_DIM) < half
    sin_signed = jnp.where(first, -sin, sin)
    for cb in range(x_ref.shape[1] // LANES):
        x = x_ref[:, cb * LANES:(cb + 1) * LANES].astype(F32)
        partner = jnp.where(first, pltpu.roll(x, LANES - half, 1), pltpu.roll(x, half, 1))
        y = x * cos + partner * sin_signed
        o_ref[:, cb * LANES:(cb + 1) * LANES] = (y * out_scale).astype(o_ref.dtype)


def _rope_call(proj, pos_col, freq_row, col_block, out_scale):
    tokens = proj.shape[0]
    tr = 512
    width = DIFF_HEADS * HEAD_DIM
    return pl.pallas_call(
        functools.partial(_rope_kernel, out_scale=out_scale),
        grid=(tokens // tr,),
        in_specs=[pl.BlockSpec((tr, width), lambda i: (i, col_block * LANES // width)),
                  pl.BlockSpec((tr, 1), lambda i: (i, 0)),
                  pl.BlockSpec((1, LANES), lambda i: (0, 0))],
        out_specs=pl.BlockSpec((tr, width), lambda i: (i, 0)),
        out_shape=jax.ShapeDtypeStruct((tokens, width), BF16),
        compiler_params=_params("arbitrary"),
        name="rope",
    )(proj, pos_col, freq_row)


def _diff_kernel(q_ref, k_ref, v_ref, lq1_ref, lk1_ref, lq2_ref, lk2_ref, g_ref, o_ref,
                 m_ref, l_ref, acc_ref, *, lam_init):
    t = ATT_TILE
    qi = pl.program_id(2)
    n = DIFF_SUB
    lane = lax.broadcasted_iota(jnp.int32, (t, HEAD_DIM), 1)
    stacked = []
    for u in range(n):
        q = q_ref[u * t:(u + 1) * t, :]
        zero = jnp.zeros_like(q)
        stacked += [jnp.where(lane < DIFF_QK_DIM, q, zero), jnp.where(lane < DIFF_QK_DIM, zero, q)]
    q12 = jnp.concatenate(stacked, axis=0)
    state = (m_ref, l_ref, acc_ref)

    def step(j, rows, masked):
        _, kb, vb = _kv_block(k_ref, v_ref, j)
        s = _qk(q12[rows], kb)
        if masked:
            r, c = _tile_iotas()
            visible = (c // CHUNK) <= (r // CHUNK)
            s = jnp.where(jnp.concatenate([visible, visible], axis=0), s, NEG_INF)
        _softmax_update(*state, s, vb, rows)

    _softmax_reset(*state)
    for u in range(n):
        rows = slice(u * 2 * t, (u + 1) * 2 * t)
        step(qi * n + u, rows, masked=True)
        for d in range(u):
            step(qi * n + d, rows, masked=False)

    @pl.loop(0, qi * n)
    def _(j):
        step(j, slice(0, n * 2 * t), masked=False)

    lam = (jnp.exp(jnp.sum(lq1_ref[0] * lk1_ref[0], axis=-1, keepdims=True))
           - jnp.exp(jnp.sum(lq2_ref[0] * lk2_ref[0], axis=-1, keepdims=True)) + lam_init)
    for u in range(n):
        first = slice(u * 2 * t, u * 2 * t + t)
        second = slice(u * 2 * t + t, (u + 1) * 2 * t)
        o = acc_ref[first, :] / l_ref[first, :] - lam * (acc_ref[second, :] / l_ref[second, :])
        o = o * lax.rsqrt(jnp.mean(o * o, axis=-1, keepdims=True) + RMS_EPS) * g_ref[...]
        o_ref[u * t:(u + 1) * t, :] = (o * (1.0 - lam_init)).astype(o_ref.dtype)


DIFF_SUB = 2


def _diff_call(qd, kd, proj, lam_q1, lam_k1, lam_q2, lam_k2, subln_g, lam_init, batch, seq):
    t = ATT_TILE * DIFF_SUB
    nq = seq // t
    lam_spec = pl.BlockSpec((1, 1, DIFF_QK_DIM), lambda b, h, i: (h, 0, 0))
    r3 = lambda a: a.reshape(DIFF_HEADS, 1, DIFF_QK_DIM)
    return pl.pallas_call(
        functools.partial(_diff_kernel, lam_init=lam_init),
        grid=(batch, DIFF_HEADS, nq),
        in_specs=[
            pl.BlockSpec((t, HEAD_DIM), lambda b, h, i: (b * nq + i, h)),
            pl.BlockSpec((seq, HEAD_DIM), lambda b, h, i: (b, h)),
            pl.BlockSpec((seq, HEAD_DIM), lambda b, h, i: (b, COL_DV + h)),
            lam_spec, lam_spec, lam_spec, lam_spec,
            pl.BlockSpec((1, HEAD_DIM), lambda b, h, i: (0, 0)),
        ],
        out_specs=pl.BlockSpec((t, HEAD_DIM), lambda b, h, i: (b * nq + i, h)),
        out_shape=jax.ShapeDtypeStruct((batch * seq, DIFF_WIDTH), BF16),
        scratch_shapes=_softmax_state(rows=2 * t),
        compiler_params=_params("arbitrary", "arbitrary", "arbitrary"),
        name="diff_attention",
    )(qd, kd, proj, r3(lam_q1), r3(lam_k1), r3(lam_q2), r3(lam_k2), subln_g.reshape(1, HEAD_DIM))


def _merge_kernel(of_ref, os_ref, od_ref, wf_ref, ws_ref, wd_ref, gf_ref, gs_ref, gd_ref, o_ref):
    def branch(o_r, w_r, g_r):
        return jax.nn.sigmoid(g_r[...].astype(F32)) * jnp.dot(o_r[...], w_r[...], preferred_element_type=F32)

    y = branch(of_ref, wf_ref, gf_ref) + branch(os_ref, ws_ref, gs_ref) + branch(od_ref, wd_ref, gd_ref)
    o_ref[...] = y.astype(o_ref.dtype)


def _merge_call(o_fox, o_sb, o_diff, w_fox, w_sb, w_diff, gates, bm=1024, bn=1024):
    m = o_fox.shape[0]
    d = w_fox.shape[1]
    gate_blocks = d // bn

    def gate_spec(k):
        return pl.BlockSpec((bm, bn), lambda i, j: (i, k * gate_blocks + j))

    return pl.pallas_call(
        _merge_kernel,
        grid=(m // bm, d // bn),
        in_specs=[
            pl.BlockSpec((bm, FOX_WIDTH), lambda i, j: (i, 0)),
            pl.BlockSpec((bm, SB_WIDTH), lambda i, j: (i, 0)),
            pl.BlockSpec((bm, DIFF_WIDTH), lambda i, j: (i, 0)),
            pl.BlockSpec((FOX_WIDTH, bn), lambda i, j: (0, j)),
            pl.BlockSpec((SB_WIDTH, bn), lambda i, j: (0, j)),
            pl.BlockSpec((DIFF_WIDTH, bn), lambda i, j: (0, j)),
            gate_spec(0), gate_spec(1), gate_spec(2),
        ],
        out_specs=pl.BlockSpec((bm, bn), lambda i, j: (i, j)),
        out_shape=jax.ShapeDtypeStruct((m, d), BF16),
        compiler_params=_params("arbitrary", "arbitrary"),
        name="branch_merge",
    )(o_fox, o_sb, o_diff, w_fox, w_sb, w_diff, gates, gates, gates)


NOT_TOP = 64.0


def _extract_top(cur, count, want_rank=False):
    rows = []
    rank = jnp.full(cur.shape, NOT_TOP, F32) if want_rank else None
    for r in range(count):
        m = jnp.max(cur, axis=0, keepdims=True)
        rows.append(m)
        hit = cur == m
        if want_rank:
            rank = jnp.where(hit, float(r), rank)
        cur = jnp.where(hit, NEG_INF, cur)
    return rows, rank


def _peer_topk_kernel(q_ref, keys_ref, a_ref, l_ref, b_ref, r2_ref):
    k = PEER_TOPK
    for h in range(PEER_HEADS):
        col = 2 * h * PEER_HALF
        s1 = _qk(keys_ref[h, 0], q_ref[:, col:col + PEER_HALF])
        s2 = _qk(keys_ref[h, 1], q_ref[:, col + PEER_HALF:col + 2 * PEER_HALF])
        a_rows, _ = _extract_top(s1, k)
        b_rows, rank2 = _extract_top(s2, k, want_rank=True)
        b_top = jnp.concatenate(b_rows, axis=0)
        counts = [k // (k1 + 1) for k1 in range(k)]
        cands = [a_rows[k1] + b_top[:counts[k1]] for k1 in range(k)]
        n_cand = sum(counts)
        pad = jnp.full((-n_cand % 8, b_top.shape[1]), NEG_INF, F32)
        best, _ = _extract_top(jnp.concatenate(cands + [pad], axis=0), k)
        tau = best[k - 1]
        z = jnp.zeros_like(tau)
        for row in best:
            z = z + jnp.exp(row - best[0])
        l_row = jnp.zeros(s1.shape, F32)
        for k1 in range(k):
            l_k = jnp.sum(jnp.where(cands[k1] >= tau, 1.0, 0.0), axis=0, keepdims=True)
            l_row = jnp.where(s1 == a_rows[k1], l_k, l_row)
        a_ref[h] = jnp.exp(s1 - a_rows[0]) / z
        l_ref[h] = l_row
        b_ref[h] = jnp.exp(s2 - b_rows[0]).astype(b_ref.dtype)
        r2_ref[h] = rank2.astype(r2_ref.dtype)


def _peer_topk_call(qp, keys):
    tokens = qp.shape[0]
    tq = 512
    spec = pl.BlockSpec((PEER_HEADS, PEER_N_KEYS, tq), lambda i: (0, 0, i))
    shape = lambda dt: jax.ShapeDtypeStruct((PEER_HEADS, PEER_N_KEYS, tokens), dt)
    return pl.pallas_call(
        _peer_topk_kernel,
        grid=(tokens // tq,),
        in_specs=[pl.BlockSpec((tq, qp.shape[1]), lambda i: (i, 0)),
                  pl.BlockSpec(keys.shape, lambda i: (0, 0, 0, 0))],
        out_specs=[spec, spec, spec, spec],
        out_shape=[shape(F32), shape(F32), shape(BF16), shape(BF16)],
        compiler_params=_params("arbitrary"),
        name="peer_topk",
    )(qp, keys)


PEER_UP_TOKENS = 1024
PEER_UP_EXPERTS = 1024
PEER_TOKEN_CHUNK = 256
PEER_DOWN_TOKENS = 512
PEER_DOWN_SPLIT = 8
PEER_DOWN_ROWS = 16


def _peer_up_kernel(ht_ref, u_ref, o_ref):
    tc = PEER_TOKEN_CHUNK
    u = u_ref[0].astype(BF16)
    for c in range(o_ref.shape[1] // tc):
        cols = slice(c * tc, (c + 1) * tc)
        hid = jnp.dot(u, ht_ref[0, :, cols], preferred_element_type=F32)
        gelu = 0.5 * hid * (1.0 + lax.erf(hid * (2.0 ** -0.5)))
        o_ref[:, cols] = gelu.astype(o_ref.dtype)


def _peer_down_kernel(vt_ref, g_ref, a_ref, l_ref, b_ref, r2_ref, x_ref, gt_ref, o_ref, acc_ref, act_ref):
    k = pl.program_id(1)
    tn = act_ref.shape[1]

    @pl.when(k == 0)
    def _():
        acc_ref[...] = jnp.zeros_like(acc_ref)

    for r in range(PEER_DOWN_ROWS):
        rows = slice(r * PEER_N_KEYS, (r + 1) * PEER_N_KEYS)

        def key_row(ref, h):
            tile = jnp.broadcast_to(ref[h, r:r + 1, :], (BF16_ROWS, tn)).astype(BF16)
            return jnp.tile(tile, (PEER_N_KEYS // BF16_ROWS, 1))

        gate = jnp.zeros((PEER_N_KEYS, tn), BF16)
        for h in range(PEER_HEADS):
            chosen = r2_ref[h] < key_row(l_ref, h)
            gate = gate + jnp.where(chosen, b_ref[h], jnp.zeros((), BF16)) * key_row(a_ref, h)
        act_ref[rows, :] = gate * g_ref[rows, :]

        group = PEER_DOWN_ROWS // PEER_DOWN_SPLIT
        if (r + 1) % group == 0:
            cols = slice((r + 1 - group) * PEER_N_KEYS, (r + 1) * PEER_N_KEYS)
            acc_ref[...] += jnp.dot(vt_ref[:, cols], act_ref[cols, :], preferred_element_type=F32)

    @pl.when(k == pl.num_programs(1) - 1)
    def _():
        o_ref[0] = x_ref[0] + gt_ref[0] * acc_ref[...].T


def _peer_ffn_call(ht, u, layer, vt, a_t, l_t, b_t, r2_t, x, gt):
    b, d, s = ht.shape
    n_experts = u.shape[1]
    tm, te = PEER_UP_TOKENS, PEER_UP_EXPERTS
    per_b = s // tm
    gelu_t = pl.pallas_call(
        _peer_up_kernel,
        grid=(b * per_b, n_experts // te),
        in_specs=[pl.BlockSpec((1, d, tm), lambda i, e: (i // per_b, 0, i % per_b)),
                  pl.BlockSpec((1, te, d), lambda i, e: (layer, e, 0))],
        out_specs=pl.BlockSpec((te, tm), lambda i, e: (e, i)),
        out_shape=jax.ShapeDtypeStruct((n_experts, b * s), BF16),
        compiler_params=_params("arbitrary", "arbitrary"),
        name="peer_up",
    )(ht, u)

    tn, tk = PEER_DOWN_TOKENS, PEER_DOWN_ROWS * PEER_N_KEYS
    per_b = s // tn
    key_spec = pl.BlockSpec((PEER_HEADS, PEER_N_KEYS, tn), lambda i, k: (0, 0, i))
    row_spec = pl.BlockSpec((PEER_HEADS, PEER_DOWN_ROWS, tn), lambda i, k: (0, k, i))
    return pl.pallas_call(
        _peer_down_kernel,
        grid=(b * per_b, n_experts // tk),
        in_specs=[
            pl.BlockSpec((d, tk), lambda i, k: (0, k)),
            pl.BlockSpec((tk, tn), lambda i, k: (k, i)),
            row_spec, row_spec, key_spec, key_spec,
            pl.BlockSpec((1, tn, d), lambda i, k: (i // per_b, i % per_b, 0)),
            pl.BlockSpec((1, 1, d), lambda i, k: (i // per_b, 0, 0)),
        ],
        out_specs=pl.BlockSpec((1, tn, d), lambda i, k: (i // per_b, i % per_b, 0)),
        out_shape=jax.ShapeDtypeStruct(x.shape, F32),
        scratch_shapes=[pltpu.VMEM((d, tn), F32), pltpu.VMEM((tk, tn), BF16)],
        compiler_params=_params("arbitrary", "arbitrary"),
        name="peer_down",
    )(vt, gelu_t, a_t, l_t, b_t, r2_t, x, gt)


def _peer_vt_kernel(v_ref, vt_ref):
    vt_ref[...] = v_ref[0].T.astype(BF16)


def _peer_vt_call(v, layer):
    _, e, d = v.shape
    te = 512
    return pl.pallas_call(
        _peer_vt_kernel,
        grid=(e // te,),
        in_specs=[pl.BlockSpec((1, te, d), lambda i: (layer, i, 0))],
        out_specs=pl.BlockSpec((d, te), lambda i: (0, i)),
        out_shape=jax.ShapeDtypeStruct((d, e), BF16),
        compiler_params=_params("arbitrary"),
        name="peer_vt",
    )(v)


def kernel(x, c, positions, ada_w, ada_b, norm1_g, norm2_g, w_in, b_forget, lam_q1, lam_k1, lam_q2, lam_k2,
           diff_subln_g, w_br_fox, w_br_sb, w_br_diff, w_o, peer_wq, peer_sub_keys, peer_u, peer_v,
           final_norm_g):
    batch, seq, d = x.shape
    tokens = batch * seq
    assert batch <= 8 and seq % 1024 == 0 and d == D_MODEL

    c_pad = jnp.zeros((8, d), F32).at[:batch].set(c)
    mod = _mod_call(c_pad, ada_w, ada_b)[:, :batch].reshape(DEPTH, batch, N_MOD, 1, d)
    pos_col = positions.reshape(tokens, 1)
    inv_freq = ROPE_THETA ** (-jnp.arange(0, DIFF_QK_DIM, 2, dtype=F32) / DIFF_QK_DIM)
    freq_row = jnp.tile(inv_freq, LANES // inv_freq.shape[0]).reshape(1, LANES)
    ri = np.arange(SB_SEG)
    tri_after = jnp.asarray(ri[:, None] > ri[None, :], BF16)

    for l in range(DEPTH):
        lam_init = 0.8 - 0.6 * math.exp(-0.3 * l)
        sh1, sc1, gt1, sh2, sc2, gt2 = (mod[l, :, k] for k in range(N_MOD))

        h1 = _norm_mod_call(x, norm1_g[l].reshape(1, d), sc1, sh1, BF16)[0].reshape(tokens, d)
        proj = _proj_call(h1, w_in, l, 0, QKV_WIDTH, BF16, bn=1024)
        gates = _proj_call(h1, w_in, l, GATE_OFFSET, N_BRANCHES * d, BF16, bn=1024)
        f_logit = _proj_call(h1, w_in, l, QKV_WIDTH, LANES, F32, bn=LANES, keep=FOX_HEADS)
        f_logit = f_logit.reshape(batch, seq, LANES)
        b_forget_pad = jnp.pad(b_forget[l], (0, LANES - FOX_HEADS)).reshape(1, LANES)
        f_col, f_row = _forget_cumsum_call(f_logit, b_forget_pad)

        o_fox = _fox_call(proj, f_col, f_row, batch, seq)
        o_sb = _sb_call(proj, tri_after, batch, seq)
        qd = _rope_call(proj, pos_col, freq_row, COL_DQ, DIFF_QK_DIM ** -0.5)
        kd = _rope_call(proj, pos_col, freq_row, COL_DK, 1.0)
        o_diff = _diff_call(qd, kd, proj, lam_q1[l], lam_k1[l], lam_q2[l], lam_k2[l], diff_subln_g[l],
                            lam_init, batch, seq)

        y = _merge_call(o_fox, o_sb, o_diff, w_br_fox[l].astype(BF16), w_br_sb[l].astype(BF16),
                        w_br_diff[l].astype(BF16), gates)
        x = _matmul_residual_call(y.reshape(batch, seq, d), w_o[l].astype(BF16), x, gt1)

        h2, h2t = _norm_mod_call(x, norm2_g[l].reshape(1, d), sc2, sh2, BF16, transposed=True)
        qp = _proj_call(h2.reshape(tokens, d), peer_wq, l, 0, peer_wq.shape[2], BF16, bn=1024)
        a_t, l_t, b_t, r2_t = _peer_topk_call(qp, peer_sub_keys[l].astype(BF16))
        x = _peer_ffn_call(h2t, peer_u, l, _peer_vt_call(peer_v, l), a_t, l_t, b_t, r2_t, x, gt2)

    zeros = jnp.zeros((batch, 1, d), F32)
    return _norm_mod_call(x, final_norm_g.reshape(1, d), zeros, zeros, F32)[0]
```

```python
import functools
import math

import jax
import jax.numpy as jnp
import numpy as np
from jax import lax
from jax.experimental import pallas as pl
from jax.experimental.pallas import tpu as pltpu

F32 = jnp.float32
BF16 = jnp.bfloat16

D_MODEL = 2048
DEPTH = 2
CHUNK = 64
HEAD_DIM = 128
FOX_HEADS = 6
SB_HEADS = 6
DIFF_HEADS = 4
DIFF_QK_DIM = HEAD_DIM // 2
FOX_WIDTH = FOX_HEADS * HEAD_DIM
SB_WIDTH = SB_HEADS * HEAD_DIM
DIFF_WIDTH = DIFF_HEADS * HEAD_DIM
QKV_WIDTH = 3 * FOX_WIDTH + 3 * SB_WIDTH + 3 * DIFF_WIDTH
GATE_OFFSET = QKV_WIDTH + FOX_HEADS
ROPE_THETA = 10000.0
RMS_EPS = 1e-6
N_MOD = 6
N_BRANCHES = 3
PEER_HEADS = 8
PEER_N_KEYS = 128
PEER_TOPK = 16
PEER_HALF = 128

LANES = 128
BF16_ROWS = 16
VMEM_LIMIT = 56 * 1024 * 1024
NEG_INF = float("-inf")

COL_FQ, COL_FK, COL_FV = 0, FOX_HEADS, 2 * FOX_HEADS
COL_SQ, COL_SK, COL_SV = 18, 24, 30
COL_DQ, COL_DK, COL_DV = 36, 40, 44
COL_GATE = QKV_WIDTH // LANES


def _params(*sem):
    return pltpu.CompilerParams(dimension_semantics=sem, vmem_limit_bytes=VMEM_LIMIT)


def _mod_kernel(c_ref, w_ref, b_ref, o_ref):
    c = c_ref[...]
    cond = c * jax.nn.sigmoid(c)
    hi = cond.astype(BF16)
    lo = (cond - hi.astype(F32)).astype(BF16)
    w = w_ref[0].astype(BF16)
    acc = jnp.dot(hi, w, preferred_element_type=F32) + jnp.dot(lo, w, preferred_element_type=F32)
    o_ref[0] = acc + b_ref[0]


def _mod_call(c_pad, ada_w, ada_b):
    depth, d, n = ada_w.shape
    tn = 1024
    return pl.pallas_call(
        _mod_kernel,
        grid=(depth, n // tn),
        in_specs=[
            pl.BlockSpec((8, d), lambda l, j: (0, 0)),
            pl.BlockSpec((1, d, tn), lambda l, j: (l, 0, j)),
            pl.BlockSpec((1, 1, tn), lambda l, j: (l, 0, j)),
        ],
        out_specs=pl.BlockSpec((1, 8, tn), lambda l, j: (l, 0, j)),
        out_shape=jax.ShapeDtypeStruct((depth, 8, n), F32),
        compiler_params=_params("arbitrary", "arbitrary"),
        name="adaln_mod",
    )(c_pad, ada_w, ada_b.reshape(depth, 1, n))


def _norm_mod_kernel(x_ref, g_ref, sc_ref, sh_ref, *o_refs, transposed):
    x = x_ref[0]
    y = x * lax.rsqrt(jnp.mean(x * x, axis=-1, keepdims=True) + RMS_EPS) * g_ref[...]
    h = y * (1.0 + sc_ref[0]) + sh_ref[0]
    o_refs[0][0] = h.astype(o_refs[0].dtype)
    if transposed:
        o_refs[1][0] = h.T.astype(o_refs[1].dtype)


def _norm_mod_call(x, g, sc, sh, out_dtype, transposed=False):
    b, s, d = x.shape
    tr = 512
    out_shape = [jax.ShapeDtypeStruct((b, s, d), out_dtype)]
    out_specs = [pl.BlockSpec((1, tr, d), lambda i, j: (i, j, 0))]
    if transposed:
        out_shape.append(jax.ShapeDtypeStruct((b, d, s), out_dtype))
        out_specs.append(pl.BlockSpec((1, d, tr), lambda i, j: (i, 0, j)))
    return pl.pallas_call(
        functools.partial(_norm_mod_kernel, transposed=transposed),
        grid=(b, s // tr),
        in_specs=[
            pl.BlockSpec((1, tr, d), lambda i, j: (i, j, 0)),
            pl.BlockSpec((1, d), lambda i, j: (0, 0)),
            pl.BlockSpec((1, 1, d), lambda i, j: (i, 0, 0)),
            pl.BlockSpec((1, 1, d), lambda i, j: (i, 0, 0)),
        ],
        out_specs=out_specs,
        out_shape=out_shape,
        compiler_params=_params("arbitrary", "arbitrary"),
        name="norm_mod",
    )(x, g, sc, sh)


def _matmul_kernel(a_ref, w_ref, o_ref):
    o_ref[...] = jnp.dot(a_ref[...], w_ref[...], preferred_element_type=F32).astype(o_ref.dtype)


def _matmul_call(a, w, out_dtype, bm=1024, bn=1024):
    m, k = a.shape
    n = w.shape[1]
    bn = min(bn, n)
    return pl.pallas_call(
        _matmul_kernel,
        grid=(m // bm, n // bn),
        in_specs=[pl.BlockSpec((bm, k), lambda i, j: (i, 0)),
                  pl.BlockSpec((k, bn), lambda i, j: (0, j))],
        out_specs=pl.BlockSpec((bm, bn), lambda i, j: (i, j)),
        out_shape=jax.ShapeDtypeStruct((m, n), out_dtype),
        compiler_params=_params("arbitrary", "arbitrary"),
        name="matmul",
    )(a, w)


def _proj_kernel(a_ref, w_ref, *rest, shift, keep):
    o_ref, wb_ref = rest[-2:]

    @pl.when(pl.program_id(1) == 0)
    def _():
        w = w_ref[0]
        if shift:
            w = jnp.concatenate([w[:, shift:], rest[0][0][:, :shift]], axis=1)
        if keep is not None:
            lane = lax.broadcasted_iota(jnp.int32, w.shape, 1)
            w = jnp.where(lane < keep, w, 0.0)
        wb_ref[...] = w.astype(BF16)

    o_ref[...] = jnp.dot(a_ref[...], wb_ref[...], preferred_element_type=F32).astype(o_ref.dtype)


def _proj_call(a, w, layer, col0, n, out_dtype, bn, keep=None, bm=1024):
    m, k = a.shape
    shift = col0 % LANES
    base = (col0 - shift) // bn
    assert (col0 - shift) % bn == 0 and n % bn == 0 and bn % LANES == 0
    next_stride = bn // LANES
    in_specs = [pl.BlockSpec((bm, k), lambda j, i: (i, 0)),
                pl.BlockSpec((1, k, bn), lambda j, i: (layer, 0, base + j))]
    operands = [a, w]
    if shift:
        in_specs.append(pl.BlockSpec((1, k, LANES), lambda j, i: (layer, 0, (base + j + 1) * next_stride)))
        operands.append(w)
    return pl.pallas_call(
        functools.partial(_proj_kernel, shift=shift, keep=keep),
        grid=(n // bn, m // bm),
        in_specs=in_specs,
        out_specs=pl.BlockSpec((bm, bn), lambda j, i: (i, j)),
        out_shape=jax.ShapeDtypeStruct((m, n), out_dtype),
        scratch_shapes=[pltpu.VMEM((k, bn), BF16)],
        compiler_params=_params("arbitrary", "arbitrary"),
        name="proj",
    )(*operands)


def _matmul_residual_kernel(a_ref, w_ref, x_ref, gt_ref, o_ref):
    acc = jnp.dot(a_ref[0], w_ref[...], preferred_element_type=F32)
    o_ref[0] = x_ref[0] + gt_ref[0] * acc


def _matmul_residual_call(a, w, x, gt, bm=1024, bn=1024):
    b, s, k = a.shape
    n = w.shape[1]
    return pl.pallas_call(
        _matmul_residual_kernel,
        grid=(b, s // bm, n // bn),
        in_specs=[pl.BlockSpec((1, bm, k), lambda bi, i, j: (bi, i, 0)),
                  pl.BlockSpec((k, bn), lambda bi, i, j: (0, j)),
                  pl.BlockSpec((1, bm, bn), lambda bi, i, j: (bi, i, j)),
                  pl.BlockSpec((1, 1, bn), lambda bi, i, j: (bi, 0, j))],
        out_specs=pl.BlockSpec((1, bm, bn), lambda bi, i, j: (bi, i, j)),
        out_shape=jax.ShapeDtypeStruct((b, s, n), F32),
        compiler_params=_params("arbitrary", "arbitrary", "arbitrary"),
        name="matmul_residual",
    )(a, w, x, gt)


def _split3(x):
    hi = x.astype(BF16)
    r = x - hi.astype(F32)
    mid = r.astype(BF16)
    lo = (r - mid.astype(F32)).astype(BF16)
    return hi, mid, lo


def _forget_cumsum_kernel(fl_ref, bf_ref, col_ref, row_ref):
    s = fl_ref.shape[1]
    blk = 256
    logf = jax.nn.log_sigmoid(fl_ref[0] + bf_ref[...])
    r = lax.broadcasted_iota(jnp.int32, (blk, blk), 0)
    c = lax.broadcasted_iota(jnp.int32, (blk, blk), 1)
    tri = jnp.where(c <= r, 1.0, 0.0).astype(BF16)
    carry = jnp.zeros((1, LANES), F32)
    parts = []
    for i in range(s // blk):
        hi, mid, lo = _split3(logf[i * blk:(i + 1) * blk])
        cs = (jnp.dot(tri, hi, preferred_element_type=F32)
              + jnp.dot(tri, mid, preferred_element_type=F32)
              + jnp.dot(tri, lo, preferred_element_type=F32)) + carry
        carry = cs[blk - 1:blk, :]
        parts.append(cs)
    cum = jnp.concatenate(parts, axis=0)
    row_ref[0] = cum.T[:8, :]
    for h in range(FOX_HEADS):
        col_ref[0, h] = jnp.broadcast_to(cum[:, h:h + 1], (s, LANES))


def _forget_cumsum_call(f_logit, b_forget_pad):
    b, s, _ = f_logit.shape
    return pl.pallas_call(
        _forget_cumsum_kernel,
        grid=(b,),
        in_specs=[pl.BlockSpec((1, s, LANES), lambda i: (i, 0, 0)),
                  pl.BlockSpec((1, LANES), lambda i: (0, 0))],
        out_specs=[pl.BlockSpec((1, FOX_HEADS, s, LANES), lambda i: (i, 0, 0, 0)),
                   pl.BlockSpec((1, 8, s), lambda i: (i, 0, 0))],
        out_shape=[jax.ShapeDtypeStruct((b, FOX_HEADS, s, LANES), F32),
                   jax.ShapeDtypeStruct((b, 8, s), F32)],
        compiler_params=_params("arbitrary"),
        name="forget_cumsum",
    )(f_logit, b_forget_pad)


ATT_TILE = 512


def _qk(q, k):
    return lax.dot_general(q, k, (((1,), (1,)), ((), ())), preferred_element_type=F32)


def _kv_block(k_ref, v_ref, j):
    start = pl.multiple_of(j * ATT_TILE, ATT_TILE)
    return start, k_ref[pl.ds(start, ATT_TILE), :], v_ref[pl.ds(start, ATT_TILE), :]


def _tile_iotas():
    r = lax.broadcasted_iota(jnp.int32, (ATT_TILE, ATT_TILE), 0)
    c = lax.broadcasted_iota(jnp.int32, (ATT_TILE, ATT_TILE), 1)
    return r, c


def _softmax_state(rows=ATT_TILE):
    return [pltpu.VMEM((rows, LANES), F32), pltpu.VMEM((rows, LANES), F32), pltpu.VMEM((rows, HEAD_DIM), F32)]


def _softmax_reset(m_ref, l_ref, acc_ref):
    m_ref[...] = jnp.full(m_ref.shape, NEG_INF, F32)
    l_ref[...] = jnp.zeros(l_ref.shape, F32)
    acc_ref[...] = jnp.zeros(acc_ref.shape, F32)


def _softmax_update(m_ref, l_ref, acc_ref, s, vb, rows=slice(None)):
    m_old = m_ref[rows, :]
    m_new = jnp.maximum(m_old, jnp.max(s, axis=-1, keepdims=True))
    alpha = jnp.exp(m_old - m_new)
    p = jnp.exp(s - jnp.tile(m_new, (1, s.shape[1] // LANES)))
    l_ref[rows, :] = alpha * l_ref[rows, :] + jnp.sum(p, axis=-1, keepdims=True)
    acc_ref[rows, :] = alpha * acc_ref[rows, :] + jnp.dot(p.astype(BF16), vb, preferred_element_type=F32)
    m_ref[rows, :] = m_new


FOX_SUB = 2


def _fox_kernel(q_ref, k_ref, v_ref, fc_ref, fr_ref, o_ref, m_ref, l_ref, acc_ref):
    t, n = ATT_TILE, FOX_SUB
    qi = pl.program_id(2)
    h = pl.program_id(1)
    q = (q_ref[...].astype(F32) * (HEAD_DIM ** -0.5)).astype(BF16)

    def step(j, rows, masked):
        start, kb, vb = _kv_block(k_ref, v_ref, j)
        fcol = jnp.tile(fc_ref[0, 0, rows, :], (1, t // LANES))
        frow = fr_ref[0, pl.ds(h, 1), pl.ds(start, t)]
        s = _qk(q[rows], kb) + (fcol - frow)
        if masked:
            r, c = _tile_iotas()
            s = jnp.where(c <= r, s, NEG_INF)
        _softmax_update(m_ref, l_ref, acc_ref, s, vb, rows)

    _softmax_reset(m_ref, l_ref, acc_ref)
    for u in range(n):
        rows = slice(u * t, (u + 1) * t)
        step(qi * n + u, rows, masked=True)
        for d in range(u):
            step(qi * n + d, rows, masked=False)

    @pl.loop(0, qi * n)
    def _(j):
        step(j, slice(0, n * t), masked=False)

    o_ref[...] = (acc_ref[...] / l_ref[...]).astype(o_ref.dtype)


def _fox_call(proj, f_col, f_row, batch, seq):
    tq = ATT_TILE * FOX_SUB
    nq = seq // tq
    return pl.pallas_call(
        _fox_kernel,
        grid=(batch, FOX_HEADS, nq),
        in_specs=[
            pl.BlockSpec((tq, HEAD_DIM), lambda b, h, i: (b * nq + i, COL_FQ + h)),
            pl.BlockSpec((seq, HEAD_DIM), lambda b, h, i: (b, COL_FK + h)),
            pl.BlockSpec((seq, HEAD_DIM), lambda b, h, i: (b, COL_FV + h)),
            pl.BlockSpec((1, 1, tq, LANES), lambda b, h, i: (b, h, i, 0)),
            pl.BlockSpec((1, 8, seq), lambda b, h, i: (b, 0, 0)),
        ],
        out_specs=pl.BlockSpec((tq, HEAD_DIM), lambda b, h, i: (b * nq + i, h)),
        out_shape=jax.ShapeDtypeStruct((batch * seq, FOX_WIDTH), BF16),
        scratch_shapes=_softmax_state(rows=tq),
        compiler_params=_params("arbitrary", "arbitrary", "arbitrary"),
        name="fox_attention",
    )(proj, proj, proj, f_col, f_row)


SB_SEG = 256
SB_SUB = 2


def _sb_kernel(q_ref, k_ref, v_ref, tri_ref, o_ref, run_ref, acc_ref):
    t, n = ATT_TILE, SB_SUB
    qi = pl.program_id(2)
    log2e = math.log2(math.e)
    qn = (q_ref[...].astype(F32) * -(HEAD_DIM ** -0.5 * log2e)).astype(BF16)

    def step(j, rows, masked):
        _, kb, vb = _kv_block(k_ref, v_ref, j)
        tri = tri_ref[...]
        u = _qk(qn[rows], kb)
        lk = jnp.minimum(u, 0.0) - jnp.log(1.0 + jnp.exp2(-jnp.abs(u))) * log2e
        if masked:
            r, c = _tile_iotas()
            strictly_causal = c < r
            lk = jnp.where(strictly_causal, lk, 0.0)
        lk_bf = lk.astype(BF16)
        run = run_ref[rows, :]
        laters = []
        for g in reversed(range(t // SB_SEG)):
            sl = slice(g * SB_SEG, (g + 1) * SB_SEG)
            laters.append(jnp.dot(lk_bf[:, sl], tri, preferred_element_type=F32) + jnp.tile(run, (1, SB_SEG // LANES)))
            run = run + jnp.sum(lk[:, sl], axis=-1, keepdims=True)
        later = jnp.concatenate(laters[::-1], axis=1)
        w = jnp.exp2(lk - u + later)
        if masked:
            w = jnp.where(strictly_causal, w, 0.0)
        acc_ref[rows, :] += jnp.dot(w.astype(BF16), vb, preferred_element_type=F32)
        run_ref[rows, :] = run

    run_ref[...] = jnp.zeros(run_ref.shape, F32)
    acc_ref[...] = jnp.zeros(acc_ref.shape, F32)
    for s in range(n):
        rows = slice(s * t, (s + 1) * t)
        step(qi * n + s, rows, masked=True)
        for d in reversed(range(s)):
            step(qi * n + d, rows, masked=False)

    @pl.loop(0, qi * n)
    def _(i):
        step(qi * n - 1 - i, slice(0, n * t), masked=False)

    o_ref[...] = acc_ref[...].astype(o_ref.dtype)


def _sb_call(proj, tri, batch, seq):
    tq = ATT_TILE * SB_SUB
    nq = seq // tq
    return pl.pallas_call(
        _sb_kernel,
        grid=(batch, SB_HEADS, nq),
        in_specs=[
            pl.BlockSpec((tq, HEAD_DIM), lambda b, h, i: (b * nq + i, COL_SQ + h)),
            pl.BlockSpec((seq, HEAD_DIM), lambda b, h, i: (b, COL_SK + h)),
            pl.BlockSpec((seq, HEAD_DIM), lambda b, h, i: (b, COL_SV + h)),
            pl.BlockSpec((SB_SEG, SB_SEG), lambda b, h, i: (0, 0)),
        ],
        out_specs=pl.BlockSpec((tq, HEAD_DIM), lambda b, h, i: (b * nq + i, h)),
        out_shape=jax.ShapeDtypeStruct((batch * seq, SB_WIDTH), BF16),
        scratch_shapes=[pltpu.VMEM((tq, LANES), F32), pltpu.VMEM((tq, HEAD_DIM), F32)],
        compiler_params=_params("arbitrary", "arbitrary", "arbitrary"),
        name="sb_attention",
    )(proj, proj, proj, tri)


def _rope_kernel(x_ref, pos_ref, freq_ref, o_ref, *, out_scale):
    half = DIFF_QK_DIM // 2
    ang = pos_ref[...].astype(F32) * freq_ref[...]
    cos = jnp.cos(ang)
    sin = jnp.sin(ang)
    lane = lax.broadcasted_iota(jnp.int32, ang.shape, 1)
    first = (lane % DIFF_QK_DIM) < half
    sin_signed = jnp.where(first, -sin, sin)
    for cb in range(x_ref.shape[1] // LANES):
        x = x_ref[:, cb * LANES:(cb + 1) * LANES].astype(F32)
        partner = jnp.where(first, pltpu.roll(x, LANES - half, 1), pltpu.roll(x, half, 1))
        y = x * cos + partner * sin_signed
        o_ref[:, cb * LANES:(cb + 1) * LANES] = (y * out_scale).astype(o_ref.dtype)


def _rope_call(proj, pos_col, freq_row, col_block, out_scale):
    tokens = proj.shape[0]
    tr = 512
    width = DIFF_HEADS * HEAD_DIM
    return pl.pallas_call(
        functools.partial(_rope_kernel, out_scale=out_scale),
        grid=(tokens // tr,),
        in_specs=[pl.BlockSpec((tr, width), lambda i: (i, col_block * LANES // width)),
                  pl.BlockSpec((tr, 1), lambda i: (i, 0)),
                  pl.BlockSpec((1, LANES), lambda i: (0, 0))],
        out_specs=pl.BlockSpec((tr, width), lambda i: (i, 0)),
        out_shape=jax.ShapeDtypeStruct((tokens, width), BF16),
        compiler_params=_params("arbitrary"),
        name="rope",
    )(proj, pos_col, freq_row)


def _diff_kernel(q_ref, k_ref, v_ref, lq1_ref, lk1_ref, lq2_ref, lk2_ref, g_ref, o_ref,
                 m_ref, l_ref, acc_ref, *, lam_init):
    t = ATT_TILE
    qi = pl.program_id(2)
    n = DIFF_SUB
    lane = lax.broadcasted_iota(jnp.int32, (t, HEAD_DIM), 1)
    stacked = []
    for u in range(n):
        q = q_ref[u * t:(u + 1) * t, :]
        zero = jnp.zeros_like(q)
        stacked += [jnp.where(lane < DIFF_QK_DIM, q, zero), jnp.where(lane < DIFF_QK_DIM, zero, q)]
    q12 = jnp.concatenate(stacked, axis=0)
    state = (m_ref, l_ref, acc_ref)

    def step(j, rows, masked):
        _, kb, vb = _kv_block(k_ref, v_ref, j)
        s = _qk(q12[rows], kb)
        if masked:
            r, c = _tile_iotas()
            visible = (c // CHUNK) <= (r // CHUNK)
            s = jnp.where(jnp.concatenate([visible, visible], axis=0), s, NEG_INF)
        _softmax_update(*state, s, vb, rows)

    _softmax_reset(*state)
    for u in range(n):
        rows = slice(u * 2 * t, (u + 1) * 2 * t)
        step(qi * n + u, rows, masked=True)
        for d in range(u):
            step(qi * n + d, rows, masked=False)

    @pl.loop(0, qi * n)
    def _(j):
        step(j, slice(0, n * 2 * t), masked=False)

    lam = (jnp.exp(jnp.sum(lq1_ref[0] * lk1_ref[0], axis=-1, keepdims=True))
           - jnp.exp(jnp.sum(lq2_ref[0] * lk2_ref[0], axis=-1, keepdims=True)) + lam_init)
    for u in range(n):
        first = slice(u * 2 * t, u * 2 * t + t)
        second = slice(u * 2 * t + t, (u + 1) * 2 * t)
        o = acc_ref[first, :] / l_ref[first, :] - lam * (acc_ref[second, :] / l_ref[second, :])
        o = o * lax.rsqrt(jnp.mean(o * o, axis=-1, keepdims=True) + RMS_EPS) * g_ref[...]
        o_ref[u * t:(u + 1) * t, :] = (o * (1.0 - lam_init)).astype(o_ref.dtype)


DIFF_SUB = 2


def _diff_call(qd, kd, proj, lam_q1, lam_k1, lam_q2, lam_k2, subln_g, lam_init, batch, seq):
    t = ATT_TILE * DIFF_SUB
    nq = seq // t
    lam_spec = pl.BlockSpec((1, 1, DIFF_QK_DIM), lambda b, h, i: (h, 0, 0))
    r3 = lambda a: a.reshape(DIFF_HEADS, 1, DIFF_QK_DIM)
    return pl.pallas_call(
        functools.partial(_diff_kernel, lam_init=lam_init),
        grid=(batch, DIFF_HEADS, nq),
        in_specs=[
            pl.BlockSpec((t, HEAD_DIM), lambda b, h, i: (b * nq + i, h)),
            pl.BlockSpec((seq, HEAD_DIM), lambda b, h, i: (b, h)),
            pl.BlockSpec((seq, HEAD_DIM), lambda b, h, i: (b, COL_DV + h)),
            lam_spec, lam_spec, lam_spec, lam_spec,
            pl.BlockSpec((1, HEAD_DIM), lambda b, h, i: (0, 0)),
        ],
        out_specs=pl.BlockSpec((t, HEAD_DIM), lambda b, h, i: (b * nq + i, h)),
        out_shape=jax.ShapeDtypeStruct((batch * seq, DIFF_WIDTH), BF16),
        scratch_shapes=_softmax_state(rows=2 * t),
        compiler_params=_params("arbitrary", "arbitrary", "arbitrary"),
        name="diff_attention",
    )(qd, kd, proj, r3(lam_q1), r3(lam_k1), r3(lam_q2), r3(lam_k2), subln_g.reshape(1, HEAD_DIM))


def _merge_kernel(of_ref, os_ref, od_ref, wf_ref, ws_ref, wd_ref, gf_ref, gs_ref, gd_ref, o_ref):
    def branch(o_r, w_r, g_r):
        return jax.nn.sigmoid(g_r[...].astype(F32)) * jnp.dot(o_r[...], w_r[...], preferred_element_type=F32)

    y = branch(of_ref, wf_ref, gf_ref) + branch(os_ref, ws_ref, gs_ref) + branch(od_ref, wd_ref, gd_ref)
    o_ref[...] = y.astype(o_ref.dtype)


def _merge_call(o_fox, o_sb, o_diff, w_fox, w_sb, w_diff, gates, bm=1024, bn=1024):
    m = o_fox.shape[0]
    d = w_fox.shape[1]
    gate_blocks = d // bn

    def gate_spec(k):
        return pl.BlockSpec((bm, bn), lambda i, j: (i, k * gate_blocks + j))

    return pl.pallas_call(
        _merge_kernel,
        grid=(m // bm, d // bn),
        in_specs=[
            pl.BlockSpec((bm, FOX_WIDTH), lambda i, j: (i, 0)),
            pl.BlockSpec((bm, SB_WIDTH), lambda i, j: (i, 0)),
            pl.BlockSpec((bm, DIFF_WIDTH), lambda i, j: (i, 0)),
            pl.BlockSpec((FOX_WIDTH, bn), lambda i, j: (0, j)),
            pl.BlockSpec((SB_WIDTH, bn), lambda i, j: (0, j)),
            pl.BlockSpec((DIFF_WIDTH, bn), lambda i, j: (0, j)),
            gate_spec(0), gate_spec(1), gate_spec(2),
        ],
        out_specs=pl.BlockSpec((bm, bn), lambda i, j: (i, j)),
        out_shape=jax.ShapeDtypeStruct((m, d), BF16),
        compiler_params=_params("arbitrary", "arbitrary"),
        name="branch_merge",
    )(o_fox, o_sb, o_diff, w_fox, w_sb, w_diff, gates, gates, gates)


NOT_TOP = 64.0


def _extract_top(cur, count, want_rank=False):
    rows = []
    rank = jnp.full(cur.shape, NOT_TOP, F32) if want_rank else None
    for r in range(count):
        m = jnp.max(cur, axis=0, keepdims=True)
        rows.append(m)
        hit = cur == m
        if want_rank:
            rank = jnp.where(hit, float(r), rank)
        cur = jnp.where(hit, NEG_INF, cur)
    return rows, rank


def _peer_topk_kernel(q_ref, keys_ref, a_ref, l_ref, b_ref, r2_ref):
    k = PEER_TOPK
    for h in range(PEER_HEADS):
        col = 2 * h * PEER_HALF
        s1 = _qk(keys_ref[h, 0], q_ref[:, col:col + PEER_HALF])
        s2 = _qk(keys_ref[h, 1], q_ref[:, col + PEER_HALF:col + 2 * PEER_HALF])
        a_rows, _ = _extract_top(s1, k)
        b_rows, rank2 = _extract_top(s2, k, want_rank=True)
        b_top = jnp.concatenate(b_rows, axis=0)
        counts = [k // (k1 + 1) for k1 in range(k)]
        cands = [a_rows[k1] + b_top[:counts[k1]] for k1 in range(k)]
        n_cand = sum(counts)
        pad = jnp.full((-n_cand % 8, b_top.shape[1]), NEG_INF, F32)
        best, _ = _extract_top(jnp.concatenate(cands + [pad], axis=0), k)
        tau = best[k - 1]
        z = jnp.zeros_like(tau)
        for row in best:
            z = z + jnp.exp(row - best[0])
        l_row = jnp.zeros(s1.shape, F32)
        for k1 in range(k):
            l_k = jnp.sum(jnp.where(cands[k1] >= tau, 1.0, 0.0), axis=0, keepdims=True)
            l_row = jnp.where(s1 == a_rows[k1], l_k, l_row)
        a_ref[h] = jnp.exp(s1 - a_rows[0]) / z
        l_ref[h] = l_row
        b_ref[h] = jnp.exp(s2 - b_rows[0]).astype(b_ref.dtype)
        r2_ref[h] = rank2.astype(r2_ref.dtype)


def _peer_topk_call(qp, keys):
    tokens = qp.shape[0]
    tq = 256
    spec = pl.BlockSpec((PEER_HEADS, PEER_N_KEYS, tq), lambda i: (0, 0, i))
    shape = lambda dt: jax.ShapeDtypeStruct((PEER_HEADS, PEER_N_KEYS, tokens), dt)
    return pl.pallas_call(
        _peer_topk_kernel,
        grid=(tokens // tq,),
        in_specs=[pl.BlockSpec((tq, qp.shape[1]), lambda i: (i, 0)),
                  pl.BlockSpec(keys.shape, lambda i: (0, 0, 0, 0))],
        out_specs=[spec, spec, spec, spec],
        out_shape=[shape(F32), shape(F32), shape(BF16), shape(BF16)],
        compiler_params=_params("arbitrary"),
        name="peer_topk",
    )(qp, keys)


PEER_UP_TOKENS = 1024
PEER_UP_EXPERTS = 1024
PEER_TOKEN_CHUNK = 256
PEER_DOWN_TOKENS = 512
PEER_DOWN_SPLIT = 8
PEER_DOWN_ROWS = 16


def _peer_up_kernel(ht_ref, u_ref, o_ref):
    tc = PEER_TOKEN_CHUNK
    u = u_ref[0].astype(BF16)
    for c in range(o_ref.shape[1] // tc):
        cols = slice(c * tc, (c + 1) * tc)
        hid = jnp.dot(u, ht_ref[0, :, cols], preferred_element_type=F32)
        gelu = 0.5 * hid * (1.0 + lax.erf(hid * (2.0 ** -0.5)))
        o_ref[:, cols] = gelu.astype(o_ref.dtype)


def _peer_down_kernel(vt_ref, g_ref, a_ref, l_ref, b_ref, r2_ref, x_ref, gt_ref, o_ref, acc_ref, act_ref):
    k = pl.program_id(1)
    tn = act_ref.shape[1]

    @pl.when(k == 0)
    def _():
        acc_ref[...] = jnp.zeros_like(acc_ref)

    for r in range(PEER_DOWN_ROWS):
        rows = slice(r * PEER_N_KEYS, (r + 1) * PEER_N_KEYS)

        def key_row(ref, h):
            tile = jnp.broadcast_to(ref[h, r:r + 1, :], (BF16_ROWS, tn)).astype(BF16)
            return jnp.tile(tile, (PEER_N_KEYS // BF16_ROWS, 1))

        gate = jnp.zeros((PEER_N_KEYS, tn), BF16)
        for h in range(PEER_HEADS):
            chosen = r2_ref[h] < key_row(l_ref, h)
            gate = gate + jnp.where(chosen, b_ref[h], jnp.zeros((), BF16)) * key_row(a_ref, h)
        act_ref[rows, :] = gate * g_ref[rows, :]

        group = PEER_DOWN_ROWS // PEER_DOWN_SPLIT
        if (r + 1) % group == 0:
            cols = slice((r + 1 - group) * PEER_N_KEYS, (r + 1) * PEER_N_KEYS)
            acc_ref[...] += jnp.dot(vt_ref[:, cols], act_ref[cols, :], preferred_element_type=F32)

    @pl.when(k == pl.num_programs(1) - 1)
    def _():
        o_ref[0] = x_ref[0] + gt_ref[0] * acc_ref[...].T


def _peer_ffn_call(ht, u, layer, vt, a_t, l_t, b_t, r2_t, x, gt):
    b, d, s = ht.shape
    n_experts = u.shape[1]
    tm, te = PEER_UP_TOKENS, PEER_UP_EXPERTS
    per_b = s // tm
    gelu_t = pl.pallas_call(
        _peer_up_kernel,
        grid=(b * per_b, n_experts // te),
        in_specs=[pl.BlockSpec((1, d, tm), lambda i, e: (i // per_b, 0, i % per_b)),
                  pl.BlockSpec((1, te, d), lambda i, e: (layer, e, 0))],
        out_specs=pl.BlockSpec((te, tm), lambda i, e: (e, i)),
        out_shape=jax.ShapeDtypeStruct((n_experts, b * s), BF16),
        compiler_params=_params("arbitrary", "arbitrary"),
        name="peer_up",
    )(ht, u)

    tn, tk = PEER_DOWN_TOKENS, PEER_DOWN_ROWS * PEER_N_KEYS
    per_b = s // tn
    key_spec = pl.BlockSpec((PEER_HEADS, PEER_N_KEYS, tn), lambda i, k: (0, 0, i))
    row_spec = pl.BlockSpec((PEER_HEADS, PEER_DOWN_ROWS, tn), lambda i, k: (0, k, i))
    return pl.pallas_call(
        _peer_down_kernel,
        grid=(b * per_b, n_experts // tk),
        in_specs=[
            pl.BlockSpec((d, tk), lambda i, k: (0, k)),
            pl.BlockSpec((tk, tn), lambda i, k: (k, i)),
            row_spec, row_spec, key_spec, key_spec,
            pl.BlockSpec((1, tn, d), lambda i, k: (i // per_b, i % per_b, 0)),
            pl.BlockSpec((1, 1, d), lambda i, k: (i // per_b, 0, 0)),
        ],
        out_specs=pl.BlockSpec((1, tn, d), lambda i, k: (i // per_b, i % per_b, 0)),
        out_shape=jax.ShapeDtypeStruct(x.shape, F32),
        scratch_shapes=[pltpu.VMEM((d, tn), F32), pltpu.VMEM((tk, tn), BF16)],
        compiler_params=_params("arbitrary", "arbitrary"),
        name="peer_down",
    )(vt, gelu_t, a_t, l_t, b_t, r2_t, x, gt)


def _peer_vt_kernel(v_ref, vt_ref):
    vt_ref[...] = v_ref[0].T.astype(BF16)


def _peer_vt_call(v, layer):
    _, e, d = v.shape
    te = 512
    return pl.pallas_call(
        _peer_vt_kernel,
        grid=(e // te,),
        in_specs=[pl.BlockSpec((1, te, d), lambda i: (layer, i, 0))],
        out_specs=pl.BlockSpec((d, te), lambda i: (0, i)),
        out_shape=jax.ShapeDtypeStruct((d, e), BF16),
        compiler_params=_params("arbitrary"),
        name="peer_vt",
    )(v)


def kernel(x, c, positions, ada_w, ada_b, norm1_g, norm2_g, w_in, b_forget, lam_q1, lam_k1, lam_q2, lam_k2,
           diff_subln_g, w_br_fox, w_br_sb, w_br_diff, w_o, peer_wq, peer_sub_keys, peer_u, peer_v,
           final_norm_g):
    batch, seq, d = x.shape
    tokens = batch * seq
    assert batch <= 8 and seq % 1024 == 0 and d == D_MODEL

    c_pad = jnp.zeros((8, d), F32).at[:batch].set(c)
    mod = _mod_call(c_pad, ada_w, ada_b)[:, :batch].reshape(DEPTH, batch, N_MOD, 1, d)
    pos_col = positions.reshape(tokens, 1)
    inv_freq = ROPE_THETA ** (-jnp.arange(0, DIFF_QK_DIM, 2, dtype=F32) / DIFF_QK_DIM)
    freq_row = jnp.tile(inv_freq, LANES // inv_freq.shape[0]).reshape(1, LANES)
    ri = np.arange(SB_SEG)
    tri_after = jnp.asarray(ri[:, None] > ri[None, :], BF16)

    for l in range(DEPTH):
        lam_init = 0.8 - 0.6 * math.exp(-0.3 * l)
        sh1, sc1, gt1, sh2, sc2, gt2 = (mod[l, :, k] for k in range(N_MOD))

        h1 = _norm_mod_call(x, norm1_g[l].reshape(1, d), sc1, sh1, BF16)[0].reshape(tokens, d)
        proj = _proj_call(h1, w_in, l, 0, QKV_WIDTH, BF16, bn=1024)
        gates = _proj_call(h1, w_in, l, GATE_OFFSET, N_BRANCHES * d, BF16, bn=1024)
        f_logit = _proj_call(h1, w_in, l, QKV_WIDTH, LANES, F32, bn=LANES, keep=FOX_HEADS)
        f_logit = f_logit.reshape(batch, seq, LANES)
        b_forget_pad = jnp.pad(b_forget[l], (0, LANES - FOX_HEADS)).reshape(1, LANES)
        f_col, f_row = _forget_cumsum_call(f_logit, b_forget_pad)

        o_fox = _fox_call(proj, f_col, f_row, batch, seq)
        o_sb = _sb_call(proj, tri_after, batch, seq)
        qd = _rope_call(proj, pos_col, freq_row, COL_DQ, DIFF_QK_DIM ** -0.5)
        kd = _rope_call(proj, pos_col, freq_row, COL_DK, 1.0)
        o_diff = _diff_call(qd, kd, proj, lam_q1[l], lam_k1[l], lam_q2[l], lam_k2[l], diff_subln_g[l],
                            lam_init, batch, seq)

        y = _merge_call(o_fox, o_sb, o_diff, w_br_fox[l].astype(BF16), w_br_sb[l].astype(BF16),
                        w_br_diff[l].astype(BF16), gates)
        x = _matmul_residual_call(y.reshape(batch, seq, d), w_o[l].astype(BF16), x, gt1)

        h2, h2t = _norm_mod_call(x, norm2_g[l].reshape(1, d), sc2, sh2, BF16, transposed=True)
        qp = _proj_call(h2.reshape(tokens, d), peer_wq, l, 0, peer_wq.shape[2], BF16, bn=1024)
        a_t, l_t, b_t, r2_t = _peer_topk_call(qp, peer_sub_keys[l].astype(BF16))
        x = _peer_ffn_call(h2t, peer_u, l, _peer_vt_call(peer_v, l), a_t, l_t, b_t, r2_t, x, gt2)

    zeros = jnp.zeros((batch, 1, d), F32)
    return _norm_mod_call(x, final_norm_g.reshape(1, d), zeros, zeros, F32)[0]
```

```python
import functools
import math

import jax
import jax.numpy as jnp
import numpy as np
from jax import lax
from jax.experimental import pallas as pl
from jax.experimental.pallas import tpu as pltpu

F32 = jnp.float32
BF16 = jnp.bfloat16

D_MODEL = 2048
DEPTH = 2
CHUNK = 64
HEAD_DIM = 128
FOX_HEADS = 6
SB_HEADS = 6
DIFF_HEADS = 4
DIFF_QK_DIM = HEAD_DIM // 2
FOX_WIDTH = FOX_HEADS * HEAD_DIM
SB_WIDTH = SB_HEADS * HEAD_DIM
DIFF_WIDTH = DIFF_HEADS * HEAD_DIM
QKV_WIDTH = 3 * FOX_WIDTH + 3 * SB_WIDTH + 3 * DIFF_WIDTH
GATE_OFFSET = QKV_WIDTH + FOX_HEADS
ROPE_THETA = 10000.0
RMS_EPS = 1e-6
N_MOD = 6
N_BRANCHES = 3
PEER_HEADS = 8
PEER_N_KEYS = 128
PEER_TOPK = 16
PEER_HALF = 128

LANES = 128
BF16_ROWS = 16
VMEM_LIMIT = 56 * 1024 * 1024
NEG_INF = float("-inf")

COL_FQ, COL_FK, COL_FV = 0, FOX_HEADS, 2 * FOX_HEADS
COL_SQ, COL_SK, COL_SV = 18, 24, 30
COL_DQ, COL_DK, COL_DV = 36, 40, 44


def _params(*sem):
    return pltpu.CompilerParams(dimension_semantics=sem, vmem_limit_bytes=VMEM_LIMIT)


def _mod_kernel(c_ref, w_ref, b_ref, o_ref):
    c = c_ref[...]
    cond = c * jax.nn.sigmoid(c)
    hi = cond.astype(BF16)
    lo = (cond - hi.astype(F32)).astype(BF16)
    w = w_ref[0].astype(BF16)
    acc = jnp.dot(hi, w, preferred_element_type=F32) + jnp.dot(lo, w, preferred_element_type=F32)
    o_ref[0] = acc + b_ref[0]


def _mod_call(c_pad, ada_w, ada_b):
    depth, d, n = ada_w.shape
    tn = 1024
    return pl.pallas_call(
        _mod_kernel,
        grid=(depth, n // tn),
        in_specs=[
            pl.BlockSpec((8, d), lambda l, j: (0, 0)),
            pl.BlockSpec((1, d, tn), lambda l, j: (l, 0, j)),
            pl.BlockSpec((1, 1, tn), lambda l, j: (l, 0, j)),
        ],
        out_specs=pl.BlockSpec((1, 8, tn), lambda l, j: (l, 0, j)),
        out_shape=jax.ShapeDtypeStruct((depth, 8, n), F32),
        compiler_params=_params("arbitrary", "arbitrary"),
        name="adaln_mod",
    )(c_pad, ada_w, ada_b.reshape(depth, 1, n))


def _norm_mod_kernel(x_ref, g_ref, sc_ref, sh_ref, *o_refs, transposed):
    x = x_ref[0]
    y = x * lax.rsqrt(jnp.mean(x * x, axis=-1, keepdims=True) + RMS_EPS) * g_ref[...]
    h = y * (1.0 + sc_ref[0]) + sh_ref[0]
    o_refs[0][0] = h.astype(o_refs[0].dtype)
    if transposed:
        o_refs[1][0] = h.T.astype(o_refs[1].dtype)


def _norm_mod_call(x, g, sc, sh, out_dtype, transposed=False):
    b, s, d = x.shape
    tr = 512
    out_shape = [jax.ShapeDtypeStruct((b, s, d), out_dtype)]
    out_specs = [pl.BlockSpec((1, tr, d), lambda i, j: (i, j, 0))]
    if transposed:
        out_shape.append(jax.ShapeDtypeStruct((b, d, s), out_dtype))
        out_specs.append(pl.BlockSpec((1, d, tr), lambda i, j: (i, 0, j)))
    return pl.pallas_call(
        functools.partial(_norm_mod_kernel, transposed=transposed),
        grid=(b, s // tr),
        in_specs=[
            pl.BlockSpec((1, tr, d), lambda i, j: (i, j, 0)),
            pl.BlockSpec((1, d), lambda i, j: (0, 0)),
            pl.BlockSpec((1, 1, d), lambda i, j: (i, 0, 0)),
            pl.BlockSpec((1, 1, d), lambda i, j: (i, 0, 0)),
        ],
        out_specs=out_specs,
        out_shape=out_shape,
        compiler_params=_params("arbitrary", "arbitrary"),
        name="norm_mod",
    )(x, g, sc, sh)


def _proj_kernel(a_ref, w_ref, *rest, shift, keep):
    o_ref, wb_ref = rest[-2:]

    @pl.when(pl.program_id(1) == 0)
    def _():
        w = w_ref[0]
        if shift:
            w = jnp.concatenate([w[:, shift:], rest[0][0][:, :shift]], axis=1)
        if keep is not None:
            lane = lax.broadcasted_iota(jnp.int32, w.shape, 1)
            w = jnp.where(lane < keep, w, 0.0)
        wb_ref[...] = w.astype(BF16)

    o_ref[...] = jnp.dot(a_ref[...], wb_ref[...], preferred_element_type=F32).astype(o_ref.dtype)


def _proj_call(a, w, layer, col0, n, out_dtype, bn, keep=None, bm=1024):
    m, k = a.shape
    shift = col0 % LANES
    base = (col0 - shift) // bn
    assert (col0 - shift) % bn == 0 and n % bn == 0 and bn % LANES == 0
    next_stride = bn // LANES
    in_specs = [pl.BlockSpec((bm, k), lambda j, i: (i, 0)),
                pl.BlockSpec((1, k, bn), lambda j, i: (layer, 0, base + j))]
    operands = [a, w]
    if shift:
        in_specs.append(pl.BlockSpec((1, k, LANES), lambda j, i: (layer, 0, (base + j + 1) * next_stride)))
        operands.append(w)
    return pl.pallas_call(
        functools.partial(_proj_kernel, shift=shift, keep=keep),
        grid=(n // bn, m // bm),
        in_specs=in_specs,
        out_specs=pl.BlockSpec((bm, bn), lambda j, i: (i, j)),
        out_shape=jax.ShapeDtypeStruct((m, n), out_dtype),
        scratch_shapes=[pltpu.VMEM((k, bn), BF16)],
        compiler_params=_params("arbitrary", "arbitrary"),
        name="proj",
    )(*operands)


def _matmul_residual_kernel(a_ref, w_ref, x_ref, gt_ref, o_ref):
    acc = jnp.dot(a_ref[0], w_ref[...], preferred_element_type=F32)
    o_ref[0] = x_ref[0] + gt_ref[0] * acc


def _matmul_residual_call(a, w, x, gt, bm=1024, bn=1024):
    b, s, k = a.shape
    n = w.shape[1]
    return pl.pallas_call(
        _matmul_residual_kernel,
        grid=(b, s // bm, n // bn),
        in_specs=[pl.BlockSpec((1, bm, k), lambda bi, i, j: (bi, i, 0)),
                  pl.BlockSpec((k, bn), lambda bi, i, j: (0, j)),
                  pl.BlockSpec((1, bm, bn), lambda bi, i, j: (bi, i, j)),
                  pl.BlockSpec((1, 1, bn), lambda bi, i, j: (bi, 0, j))],
        out_specs=pl.BlockSpec((1, bm, bn), lambda bi, i, j: (bi, i, j)),
        out_shape=jax.ShapeDtypeStruct((b, s, n), F32),
        compiler_params=_params("arbitrary", "arbitrary", "arbitrary"),
        name="matmul_residual",
    )(a, w, x, gt)


def _split3(x):
    hi = x.astype(BF16)
    r = x - hi.astype(F32)
    mid = r.astype(BF16)
    lo = (r - mid.astype(F32)).astype(BF16)
    return hi, mid, lo


def _forget_cumsum_kernel(fl_ref, bf_ref, col_ref, row_ref):
    s = fl_ref.shape[1]
    blk = 256
    logf = jax.nn.log_sigmoid(fl_ref[0] + bf_ref[...])
    r = lax.broadcasted_iota(jnp.int32, (blk, blk), 0)
    c = lax.broadcasted_iota(jnp.int32, (blk, blk), 1)
    tri = jnp.where(c <= r, 1.0, 0.0).astype(BF16)
    carry = jnp.zeros((1, LANES), F32)
    parts = []
    for i in range(s // blk):
        hi, mid, lo = _split3(logf[i * blk:(i + 1) * blk])
        cs = (jnp.dot(tri, hi, preferred_element_type=F32)
              + jnp.dot(tri, mid, preferred_element_type=F32)
              + jnp.dot(tri, lo, preferred_element_type=F32)) + carry
        carry = cs[blk - 1:blk, :]
        parts.append(cs)
    cum = jnp.concatenate(parts, axis=0)
    row_ref[0] = cum.T[:8, :]
    for h in range(FOX_HEADS):
        col_ref[0, h] = jnp.broadcast_to(cum[:, h:h + 1], (s, LANES))


def _forget_cumsum_call(f_logit, b_forget_pad):
    b, s, _ = f_logit.shape
    return pl.pallas_call(
        _forget_cumsum_kernel,
        grid=(b,),
        in_specs=[pl.BlockSpec((1, s, LANES), lambda i: (i, 0, 0)),
                  pl.BlockSpec((1, LANES), lambda i: (0, 0))],
        out_specs=[pl.BlockSpec((1, FOX_HEADS, s, LANES), lambda i: (i, 0, 0, 0)),
                   pl.BlockSpec((1, 8, s), lambda i: (i, 0, 0))],
        out_shape=[jax.ShapeDtypeStruct((b, FOX_HEADS, s, LANES), F32),
                   jax.ShapeDtypeStruct((b, 8, s), F32)],
        compiler_params=_params("arbitrary"),
        name="forget_cumsum",
    )(f_logit, b_forget_pad)


ATT_TILE = 512


def _qk(q, k):
    return lax.dot_general(q, k, (((1,), (1,)), ((), ())), preferred_element_type=F32)


def _kv_block(k_ref, v_ref, j):
    start = pl.multiple_of(j * ATT_TILE, ATT_TILE)
    return start, k_ref[pl.ds(start, ATT_TILE), :], v_ref[pl.ds(start, ATT_TILE), :]


def _tile_iotas():
    r = lax.broadcasted_iota(jnp.int32, (ATT_TILE, ATT_TILE), 0)
    c = lax.broadcasted_iota(jnp.int32, (ATT_TILE, ATT_TILE), 1)
    return r, c


def _softmax_state(rows=ATT_TILE):
    return [pltpu.VMEM((rows, LANES), F32), pltpu.VMEM((rows, LANES), F32), pltpu.VMEM((rows, HEAD_DIM), F32)]


def _softmax_reset(m_ref, l_ref, acc_ref):
    m_ref[...] = jnp.full(m_ref.shape, NEG_INF, F32)
    l_ref[...] = jnp.zeros(l_ref.shape, F32)
    acc_ref[...] = jnp.zeros(acc_ref.shape, F32)


def _softmax_update(m_ref, l_ref, acc_ref, s, vb, rows=slice(None)):
    m_old = m_ref[rows, :]
    m_new = jnp.maximum(m_old, jnp.max(s, axis=-1, keepdims=True))
    alpha = jnp.exp(m_old - m_new)
    p = jnp.exp(s - jnp.tile(m_new, (1, s.shape[1] // LANES)))
    l_ref[rows, :] = alpha * l_ref[rows, :] + jnp.sum(p, axis=-1, keepdims=True)
    acc_ref[rows, :] = alpha * acc_ref[rows, :] + jnp.dot(p.astype(BF16), vb, preferred_element_type=F32)
    m_ref[rows, :] = m_new


FOX_SUB = 2


def _fox_kernel(q_ref, k_ref, v_ref, fc_ref, fr_ref, o_ref, m_ref, l_ref, acc_ref):
    t, n = ATT_TILE, FOX_SUB
    qi = pl.program_id(2)
    h = pl.program_id(1)
    q = (q_ref[...].astype(F32) * (HEAD_DIM ** -0.5)).astype(BF16)

    def step(j, rows, masked):
        start, kb, vb = _kv_block(k_ref, v_ref, j)
        fcol = jnp.tile(fc_ref[0, 0, rows, :], (1, t // LANES))
        frow = fr_ref[0, pl.ds(h, 1), pl.ds(start, t)]
        s = _qk(q[rows], kb) + (fcol - frow)
        if masked:
            r, c = _tile_iotas()
            s = jnp.where(c <= r, s, NEG_INF)
        _softmax_update(m_ref, l_ref, acc_ref, s, vb, rows)

    _softmax_reset(m_ref, l_ref, acc_ref)
    for u in range(n):
        rows = slice(u * t, (u + 1) * t)
        step(qi * n + u, rows, masked=True)
        for d in range(u):
            step(qi * n + d, rows, masked=False)

    @pl.loop(0, qi * n)
    def _(j):
        step(j, slice(0, n * t), masked=False)

    o_ref[...] = (acc_ref[...] / l_ref[...]).astype(o_ref.dtype)


def _fox_call(proj, f_col, f_row, batch, seq):
    tq = ATT_TILE * FOX_SUB
    nq = seq // tq
    return pl.pallas_call(
        _fox_kernel,
        grid=(batch, FOX_HEADS, nq),
        in_specs=[
            pl.BlockSpec((tq, HEAD_DIM), lambda b, h, i: (b * nq + i, COL_FQ + h)),
            pl.BlockSpec((seq, HEAD_DIM), lambda b, h, i: (b, COL_FK + h)),
            pl.BlockSpec((seq, HEAD_DIM), lambda b, h, i: (b, COL_FV + h)),
            pl.BlockSpec((1, 1, tq, LANES), lambda b, h, i: (b, h, i, 0)),
            pl.BlockSpec((1, 8, seq), lambda b, h, i: (b, 0, 0)),
        ],
        out_specs=pl.BlockSpec((tq, HEAD_DIM), lambda b, h, i: (b * nq + i, h)),
        out_shape=jax.ShapeDtypeStruct((batch * seq, FOX_WIDTH), BF16),
        scratch_shapes=_softmax_state(rows=tq),
        compiler_params=_params("arbitrary", "arbitrary", "arbitrary"),
        name="fox_attention",
    )(proj, proj, proj, f_col, f_row)


SB_SEG = 256
SB_SUB = 2


def _sb_kernel(q_ref, k_ref, v_ref, tri_ref, o_ref, run_ref, acc_ref):
    t, n = ATT_TILE, SB_SUB
    qi = pl.program_id(2)
    log2e = math.log2(math.e)
    qn = (q_ref[...].astype(F32) * -(HEAD_DIM ** -0.5 * log2e)).astype(BF16)

    def step(j, rows, masked):
        _, kb, vb = _kv_block(k_ref, v_ref, j)
        tri = tri_ref[...]
        u = _qk(qn[rows], kb)
        lk = jnp.minimum(u, 0.0) - jnp.log(1.0 + jnp.exp2(-jnp.abs(u))) * log2e
        if masked:
            r, c = _tile_iotas()
            strictly_causal = c < r
            lk = jnp.where(strictly_causal, lk, 0.0)
        lk_bf = lk.astype(BF16)
        run = run_ref[rows, :]
        laters = []
        for g in reversed(range(t // SB_SEG)):
            sl = slice(g * SB_SEG, (g + 1) * SB_SEG)
            laters.append(jnp.dot(lk_bf[:, sl], tri, preferred_element_type=F32) + jnp.tile(run, (1, SB_SEG // LANES)))
            run = run + jnp.sum(lk[:, sl], axis=-1, keepdims=True)
        later = jnp.concatenate(laters[::-1], axis=1)
        w = jnp.exp2(lk - u + later)
        if masked:
            w = jnp.where(strictly_causal, w, 0.0)
        acc_ref[rows, :] += jnp.dot(w.astype(BF16), vb, preferred_element_type=F32)
        run_ref[rows, :] = run

    run_ref[...] = jnp.zeros(run_ref.shape, F32)
    acc_ref[...] = jnp.zeros(acc_ref.shape, F32)
    for s in range(n):
        rows = slice(s * t, (s + 1) * t)
        step(qi * n + s, rows, masked=True)
        for d in reversed(range(s)):
            step(qi * n + d, rows, masked=False)

    @pl.loop(0, qi * n)
    def _(i):
        step(qi * n - 1 - i, slice(0, n * t), masked=False)

    o_ref[...] = acc_ref[...].astype(o_ref.dtype)


def _sb_call(proj, tri, batch, seq):
    tq = ATT_TILE * SB_SUB
    nq = seq // tq
    return pl.pallas_call(
        _sb_kernel,
        grid=(batch, SB_HEADS, nq),
        in_specs=[
            pl.BlockSpec((tq, HEAD_DIM), lambda b, h, i: (b * nq + i, COL_SQ + h)),
            pl.BlockSpec((seq, HEAD_DIM), lambda b, h, i: (b, COL_SK + h)),
            pl.BlockSpec((seq, HEAD_DIM), lambda b, h, i: (b, COL_SV + h)),
            pl.BlockSpec((SB_SEG, SB_SEG), lambda b, h, i: (0, 0)),
        ],
        out_specs=pl.BlockSpec((tq, HEAD_DIM), lambda b, h, i: (b * nq + i, h)),
        out_shape=jax.ShapeDtypeStruct((batch * seq, SB_WIDTH), BF16),
        scratch_shapes=[pltpu.VMEM((tq, LANES), F32), pltpu.VMEM((tq, HEAD_DIM), F32)],
        compiler_params=_params("arbitrary", "arbitrary", "arbitrary"),
        name="sb_attention",
    )(proj, proj, proj, tri)


def _rope_kernel(x_ref, pos_ref, freq_ref, o_ref, *, out_scale):
    half = DIFF_QK_DIM // 2
    ang = pos_ref[...].astype(F32) * freq_ref[...]
    cos = jnp.cos(ang)
    sin = jnp.sin(ang)
    lane = lax.broadcasted_iota(jnp.int32, ang.shape, 1)
    first = (lane % DIFF_QK_DIM) < half
    sin_signed = jnp.where(first, -sin, sin)
    for cb in range(x_ref.shape[1] // LANES):
        x = x_ref[:, cb * LANES:(cb + 1) * LANES].astype(F32)
        partner = jnp.where(first, pltpu.roll(x, LANES - half, 1), pltpu.roll(x, half, 1))
        y = x * cos + partner * sin_signed
        o_ref[:, cb * LANES:(cb + 1) * LANES] = (y * out_scale).astype(o_ref.dtype)


def _rope_call(proj, pos_col, freq_row, col_block, out_scale):
    tokens = proj.shape[0]
    tr = 512
    width = DIFF_HEADS * HEAD_DIM
    return pl.pallas_call(
        functools.partial(_rope_kernel, out_scale=out_scale),
        grid=(tokens // tr,),
        in_specs=[pl.BlockSpec((tr, width), lambda i: (i, col_block * LANES // width)),
                  pl.BlockSpec((tr, 1), lambda i: (i, 0)),
                  pl.BlockSpec((1, LANES), lambda i: (0, 0))],
        out_specs=pl.BlockSpec((tr, width), lambda i: (i, 0)),
        out_shape=jax.ShapeDtypeStruct((tokens, width), BF16),
        compiler_params=_params("arbitrary"),
        name="rope",
    )(proj, pos_col, freq_row)


def _diff_kernel(q_ref, k_ref, v_ref, lq1_ref, lk1_ref, lq2_ref, lk2_ref, g_ref, o_ref,
                 m_ref, l_ref, acc_ref, *, lam_init):
    t = ATT_TILE
    qi = pl.program_id(2)
    n = DIFF_SUB
    lane = lax.broadcasted_iota(jnp.int32, (t, HEAD_DIM), 1)
    stacked = []
    for u in range(n):
        q = q_ref[u * t:(u + 1) * t, :]
        zero = jnp.zeros_like(q)
        stacked += [jnp.where(lane < DIFF_QK_DIM, q, zero), jnp.where(lane < DIFF_QK_DIM, zero, q)]
    q12 = jnp.concatenate(stacked, axis=0)
    state = (m_ref, l_ref, acc_ref)

    def step(j, rows, masked):
        _, kb, vb = _kv_block(k_ref, v_ref, j)
        s = _qk(q12[rows], kb)
        if masked:
            r, c = _tile_iotas()
            visible = (c // CHUNK) <= (r // CHUNK)
            s = jnp.where(jnp.concatenate([visible, visible], axis=0), s, NEG_INF)
        _softmax_update(*state, s, vb, rows)

    _softmax_reset(*state)
    for u in range(n):
        rows = slice(u * 2 * t, (u + 1) * 2 * t)
        step(qi * n + u, rows, masked=True)
        for d in range(u):
            step(qi * n + d, rows, masked=False)

    @pl.loop(0, qi * n)
    def _(j):
        step(j, slice(0, n * 2 * t), masked=False)

    lam = (jnp.exp(jnp.sum(lq1_ref[0] * lk1_ref[0], axis=-1, keepdims=True))
           - jnp.exp(jnp.sum(lq2_ref[0] * lk2_ref[0], axis=-1, keepdims=True)) + lam_init)
    for u in range(n):
        first = slice(u * 2 * t, u * 2 * t + t)
        second = slice(u * 2 * t + t, (u + 1) * 2 * t)
        o = acc_ref[first, :] / l_ref[first, :] - lam * (acc_ref[second, :] / l_ref[second, :])
        o = o * lax.rsqrt(jnp.mean(o * o, axis=-1, keepdims=True) + RMS_EPS) * g_ref[...]
        o_ref[u * t:(u + 1) * t, :] = (o * (1.0 - lam_init)).astype(o_ref.dtype)


DIFF_SUB = 2


def _diff_call(qd, kd, proj, lam_q1, lam_k1, lam_q2, lam_k2, subln_g, lam_init, batch, seq):
    t = ATT_TILE * DIFF_SUB
    nq = seq // t
    lam_spec = pl.BlockSpec((1, 1, DIFF_QK_DIM), lambda b, h, i: (h, 0, 0))
    r3 = lambda a: a.reshape(DIFF_HEADS, 1, DIFF_QK_DIM)
    return pl.pallas_call(
        functools.partial(_diff_kernel, lam_init=lam_init),
        grid=(batch, DIFF_HEADS, nq),
        in_specs=[
            pl.BlockSpec((t, HEAD_DIM), lambda b, h, i: (b * nq + i, h)),
            pl.BlockSpec((seq, HEAD_DIM), lambda b, h, i: (b, h)),
            pl.BlockSpec((seq, HEAD_DIM), lambda b, h, i: (b, COL_DV + h)),
            lam_spec, lam_spec, lam_spec, lam_spec,
            pl.BlockSpec((1, HEAD_DIM), lambda b, h, i: (0, 0)),
        ],
        out_specs=pl.BlockSpec((t, HEAD_DIM), lambda b, h, i: (b * nq + i, h)),
        out_shape=jax.ShapeDtypeStruct((batch * seq, DIFF_WIDTH), BF16),
        scratch_shapes=_softmax_state(rows=2 * t),
        compiler_params=_params("arbitrary", "arbitrary", "arbitrary"),
        name="diff_attention",
    )(qd, kd, proj, r3(lam_q1), r3(lam_k1), r3(lam_q2), r3(lam_k2), subln_g.reshape(1, HEAD_DIM))


def _merge_kernel(of_ref, os_ref, od_ref, wf_ref, ws_ref, wd_ref, gf_ref, gs_ref, gd_ref, o_ref):
    def branch(o_r, w_r, g_r):
        return jax.nn.sigmoid(g_r[...].astype(F32)) * jnp.dot(o_r[...], w_r[...], preferred_element_type=F32)

    y = branch(of_ref, wf_ref, gf_ref) + branch(os_ref, ws_ref, gs_ref) + branch(od_ref, wd_ref, gd_ref)
    o_ref[...] = y.astype(o_ref.dtype)


def _merge_call(o_fox, o_sb, o_diff, w_fox, w_sb, w_diff, gates, bm=1024, bn=1024):
    m = o_fox.shape[0]
    d = w_fox.shape[1]
    gate_blocks = d // bn

    def gate_spec(k):
        return pl.BlockSpec((bm, bn), lambda i, j: (i, k * gate_blocks + j))

    return pl.pallas_call(
        _merge_kernel,
        grid=(m // bm, d // bn),
        in_specs=[
            pl.BlockSpec((bm, FOX_WIDTH), lambda i, j: (i, 0)),
            pl.BlockSpec((bm, SB_WIDTH), lambda i, j: (i, 0)),
            pl.BlockSpec((bm, DIFF_WIDTH), lambda i, j: (i, 0)),
            pl.BlockSpec((FOX_WIDTH, bn), lambda i, j: (0, j)),
            pl.BlockSpec((SB_WIDTH, bn), lambda i, j: (0, j)),
            pl.BlockSpec((DIFF_WIDTH, bn), lambda i, j: (0, j)),
            gate_spec(0), gate_spec(1), gate_spec(2),
        ],
        out_specs=pl.BlockSpec((bm, bn), lambda i, j: (i, j)),
        out_shape=jax.ShapeDtypeStruct((m, d), BF16),
        compiler_params=_params("arbitrary", "arbitrary"),
        name="branch_merge",
    )(o_fox, o_sb, o_diff, w_fox, w_sb, w_diff, gates, gates, gates)


NOT_TOP = 64.0


def _extract_top(cur, count, want_rank=False):
    rows = []
    rank = jnp.full(cur.shape, NOT_TOP, F32) if want_rank else None
    for r in range(count):
        m = jnp.max(cur, axis=0, keepdims=True)
        rows.append(m)
        hit = cur == m
        if want_rank:
            rank = jnp.where(hit, float(r), rank)
        cur = jnp.where(hit, NEG_INF, cur)
    return rows, rank


def _peer_topk_kernel(q_ref, keys_ref, a_ref, l_ref, b_ref, r2_ref):
    k = PEER_TOPK
    for h in range(PEER_HEADS):
        col = 2 * h * PEER_HALF
        s1 = _qk(keys_ref[h, 0], q_ref[:, col:col + PEER_HALF])
        s2 = _qk(keys_ref[h, 1], q_ref[:, col + PEER_HALF:col + 2 * PEER_HALF])
        a_rows, _ = _extract_top(s1, k)
        b_rows, rank2 = _extract_top(s2, k, want_rank=True)
        b_top = jnp.concatenate(b_rows, axis=0)
        counts = [k // (k1 + 1) for k1 in range(k)]
        cands = [a_rows[k1] + b_top[:counts[k1]] for k1 in range(k)]
        n_cand = sum(counts)
        pad = jnp.full((-n_cand % 8, b_top.shape[1]), NEG_INF, F32)
        best, _ = _extract_top(jnp.concatenate(cands + [pad], axis=0), k)
        tau = best[k - 1]
        z = jnp.zeros_like(tau)
        for row in best:
            z = z + jnp.exp(row - best[0])
        l_row = jnp.zeros(s1.shape, F32)
        for k1 in range(k):
            l_k = jnp.sum(jnp.where(cands[k1] >= tau, 1.0, 0.0), axis=0, keepdims=True)
            l_row = jnp.where(s1 == a_rows[k1], l_k, l_row)
        a_ref[h] = jnp.exp(s1 - a_rows[0]) / z
        l_ref[h] = l_row
        b_ref[h] = jnp.exp(s2 - b_rows[0]).astype(b_ref.dtype)
        r2_ref[h] = rank2.astype(r2_ref.dtype)


def _peer_topk_call(qp, keys):
    tokens = qp.shape[0]
    tq = 256
    spec = pl.BlockSpec((PEER_HEADS, PEER_N_KEYS, tq), lambda i: (0, 0, i))
    shape = lambda dt: jax.ShapeDtypeStruct((PEER_HEADS, PEER_N_KEYS, tokens), dt)
    return pl.pallas_call(
        _peer_topk_kernel,
        grid=(tokens // tq,),
        in_specs=[pl.BlockSpec((tq, qp.shape[1]), lambda i: (i, 0)),
                  pl.BlockSpec(keys.shape, lambda i: (0, 0, 0, 0))],
        out_specs=[spec, spec, spec, spec],
        out_shape=[shape(F32), shape(F32), shape(BF16), shape(BF16)],
        compiler_params=_params("arbitrary"),
        name="peer_topk",
    )(qp, keys)


PEER_UP_TOKENS = 1024
PEER_UP_EXPERTS = 1024
PEER_TOKEN_CHUNK = 256
PEER_DOWN_TOKENS = 512
PEER_DOWN_SPLIT = 8
PEER_DOWN_ROWS = 16


def _peer_up_kernel(ht_ref, u_ref, o_ref):
    tc = PEER_TOKEN_CHUNK
    u = u_ref[0].astype(BF16)
    for c in range(o_ref.shape[1] // tc):
        cols = slice(c * tc, (c + 1) * tc)
        hid = jnp.dot(u, ht_ref[0, :, cols], preferred_element_type=F32)
        gelu = 0.5 * hid * (1.0 + lax.erf(hid * (2.0 ** -0.5)))
        o_ref[:, cols] = gelu.astype(o_ref.dtype)


def _peer_down_kernel(vt_ref, g_ref, a_ref, l_ref, b_ref, r2_ref, x_ref, gt_ref, o_ref, acc_ref, act_ref):
    k = pl.program_id(1)
    tn = act_ref.shape[1]

    @pl.when(k == 0)
    def _():
        acc_ref[...] = jnp.zeros_like(acc_ref)

    for r in range(PEER_DOWN_ROWS):
        rows = slice(r * PEER_N_KEYS, (r + 1) * PEER_N_KEYS)

        def key_row(ref, h):
            tile = jnp.broadcast_to(ref[h, r:r + 1, :], (BF16_ROWS, tn)).astype(BF16)
            return jnp.tile(tile, (PEER_N_KEYS // BF16_ROWS, 1))

        gate = jnp.zeros((PEER_N_KEYS, tn), BF16)
        for h in range(PEER_HEADS):
            chosen = r2_ref[h] < key_row(l_ref, h)
            gate = gate + jnp.where(chosen, b_ref[h], jnp.zeros((), BF16)) * key_row(a_ref, h)
        act_ref[rows, :] = gate * g_ref[rows, :]

        group = PEER_DOWN_ROWS // PEER_DOWN_SPLIT
        if (r + 1) % group == 0:
            cols = slice((r + 1 - group) * PEER_N_KEYS, (r + 1) * PEER_N_KEYS)
            acc_ref[...] += jnp.dot(vt_ref[:, cols], act_ref[cols, :], preferred_element_type=F32)

    @pl.when(k == pl.num_programs(1) - 1)
    def _():
        o_ref[0] = x_ref[0] + gt_ref[0] * acc_ref[...].T


def _peer_ffn_call(ht, u, layer, vt, a_t, l_t, b_t, r2_t, x, gt):
    b, d, s = ht.shape
    n_experts = u.shape[1]
    tm, te = PEER_UP_TOKENS, PEER_UP_EXPERTS
    per_b = s // tm
    gelu_t = pl.pallas_call(
        _peer_up_kernel,
        grid=(b * per_b, n_experts // te),
        in_specs=[pl.BlockSpec((1, d, tm), lambda i, e: (i // per_b, 0, i % per_b)),
                  pl.BlockSpec((1, te, d), lambda i, e: (layer, e, 0))],
        out_specs=pl.BlockSpec((te, tm), lambda i, e: (e, i)),
        out_shape=jax.ShapeDtypeStruct((n_experts, b * s), BF16),
        compiler_params=_params("arbitrary", "arbitrary"),
        name="peer_up",
    )(ht, u)

    tn, tk = PEER_DOWN_TOKENS, PEER_DOWN_ROWS * PEER_N_KEYS
    per_b = s // tn
    key_spec = pl.BlockSpec((PEER_HEADS, PEER_N_KEYS, tn), lambda i, k: (0, 0, i))
    row_spec = pl.BlockSpec((PEER_HEADS, PEER_DOWN_ROWS, tn), lambda i, k: (0, k, i))
    return pl.pallas_call(
        _peer_down_kernel,
        grid=(b * per_b, n_experts // tk),
        in_specs=[
            pl.BlockSpec((d, tk), lambda i, k: (0, k)),
            pl.BlockSpec((tk, tn), lambda i, k: (k, i)),
            row_spec, row_spec, key_spec, key_spec,
            pl.BlockSpec((1, tn, d), lambda i, k: (i // per_b, i % per_b, 0)),
            pl.BlockSpec((1, 1, d), lambda i, k: (i // per_b, 0, 0)),
        ],
        out_specs=pl.BlockSpec((1, tn, d), lambda i, k: (i // per_b, i % per_b, 0)),
        out_shape=jax.ShapeDtypeStruct(x.shape, F32),
        scratch_shapes=[pltpu.VMEM((d, tn), F32), pltpu.VMEM((tk, tn), BF16)],
        compiler_params=_params("arbitrary", "arbitrary"),
        name="peer_down",
    )(vt, gelu_t, a_t, l_t, b_t, r2_t, x, gt)


def _peer_vt_kernel(v_ref, vt_ref):
    vt_ref[...] = v_ref[0].T.astype(BF16)


def _peer_vt_call(v, layer):
    _, e, d = v.shape
    te = 512
    return pl.pallas_call(
        _peer_vt_kernel,
        grid=(e // te,),
        in_specs=[pl.BlockSpec((1, te, d), lambda i: (layer, i, 0))],
        out_specs=pl.BlockSpec((d, te), lambda i: (0, i)),
        out_shape=jax.ShapeDtypeStruct((d, e), BF16),
        compiler_params=_params("arbitrary"),
        name="peer_vt",
    )(v)


def kernel(x, c, positions, ada_w, ada_b, norm1_g, norm2_g, w_in, b_forget, lam_q1, lam_k1, lam_q2, lam_k2,
           diff_subln_g, w_br_fox, w_br_sb, w_br_diff, w_o, peer_wq, peer_sub_keys, peer_u, peer_v,
           final_norm_g):
    batch, seq, d = x.shape
    tokens = batch * seq
    assert batch <= 8 and seq % 1024 == 0 and d == D_MODEL

    c_pad = jnp.zeros((8, d), F32).at[:batch].set(c)
    mod = _mod_call(c_pad, ada_w, ada_b)[:, :batch].reshape(DEPTH, batch, N_MOD, 1, d)
    pos_col = positions.reshape(tokens, 1)
    inv_freq = ROPE_THETA ** (-jnp.arange(0, DIFF_QK_DIM, 2, dtype=F32) / DIFF_QK_DIM)
    freq_row = jnp.tile(inv_freq, LANES // inv_freq.shape[0]).reshape(1, LANES)
    ri = np.arange(SB_SEG)
    tri_after = jnp.asarray(ri[:, None] > ri[None, :], BF16)

    for l in range(DEPTH):
        lam_init = 0.8 - 0.6 * math.exp(-0.3 * l)
        sh1, sc1, gt1, sh2, sc2, gt2 = (mod[l, :, k] for k in range(N_MOD))

        h1 = _norm_mod_call(x, norm1_g[l].reshape(1, d), sc1, sh1, BF16)[0].reshape(tokens, d)
        proj = _proj_call(h1, w_in, l, 0, QKV_WIDTH, BF16, bn=1024)
        gates = _proj_call(h1, w_in, l, GATE_OFFSET, N_BRANCHES * d, BF16, bn=1024)
        f_logit = _proj_call(h1, w_in, l, QKV_WIDTH, LANES, F32, bn=LANES, keep=FOX_HEADS)
        f_logit = f_logit.reshape(batch, seq, LANES)
        b_forget_pad = jnp.pad(b_forget[l], (0, LANES - FOX_HEADS)).reshape(1, LANES)
        f_col, f_row = _forget_cumsum_call(f_logit, b_forget_pad)

        o_fox = _fox_call(proj, f_col, f_row, batch, seq)
        o_sb = _sb_call(proj, tri_after, batch, seq)
        qd = _rope_call(proj, pos_col, freq_row, COL_DQ, DIFF_QK_DIM ** -0.5)
        kd = _rope_call(proj, pos_col, freq_row, COL_DK, 1.0)
        o_diff = _diff_call(qd, kd, proj, lam_q1[l], lam_k1[l], lam_q2[l], lam_k2[l], diff_subln_g[l],
                            lam_init, batch, seq)

        y = _merge_call(o_fox, o_sb, o_diff, w_br_fox[l].astype(BF16), w_br_sb[l].astype(BF16),
                        w_br_diff[l].astype(BF16), gates)
        x = _matmul_residual_call(y.reshape(batch, seq, d), w_o[l].astype(BF16), x, gt1)

        h2, h2t = _norm_mod_call(x, norm2_g[l].reshape(1, d), sc2, sh2, BF16, transposed=True)
        qp = _proj_call(h2.reshape(tokens, d), peer_wq, l, 0, peer_wq.shape[2], BF16, bn=1024)
        a_t, l_t, b_t, r2_t = _peer_topk_call(qp, peer_sub_keys[l].astype(BF16))
        x = _peer_ffn_call(h2t, peer_u, l, _peer_vt_call(peer_v, l), a_t, l_t, b_t, r2_t, x, gt2)

    zeros = jnp.zeros((batch, 1, d), F32)
    return _norm_mod_call(x, final_norm_g.reshape(1, d), zeros, zeros, F32)[0]
```

```python
import functools
import math

import jax
import jax.numpy as jnp
import numpy as np
from jax import lax
from jax.experimental import pallas as pl
from jax.experimental.pallas import tpu as pltpu

F32 = jnp.float32
BF16 = jnp.bfloat16

D_MODEL = 2048
DEPTH = 2
CHUNK = 64
HEAD_DIM = 128
FOX_HEADS = 6
SB_HEADS = 6
DIFF_HEADS = 4
DIFF_QK_DIM = HEAD_DIM // 2
FOX_WIDTH = FOX_HEADS * HEAD_DIM
SB_WIDTH = SB_HEADS * HEAD_DIM
DIFF_WIDTH = DIFF_HEADS * HEAD_DIM
QKV_WIDTH = 3 * FOX_WIDTH + 3 * SB_WIDTH + 3 * DIFF_WIDTH
GATE_OFFSET = QKV_WIDTH + FOX_HEADS
ROPE_THETA = 10000.0
RMS_EPS = 1e-6
N_MOD = 6
N_BRANCHES = 3
PEER_HEADS = 8
PEER_N_KEYS = 128
PEER_TOPK = 16
PEER_HALF = 128

LANES = 128
BF16_ROWS = 16
VMEM_LIMIT = 56 * 1024 * 1024
NEG_INF = float("-inf")

COL_FQ, COL_FK, COL_FV = 0, FOX_HEADS, 2 * FOX_HEADS
COL_SQ, COL_SK, COL_SV = 18, 24, 30
COL_DQ, COL_DK, COL_DV = 36, 40, 44


def _params(*sem):
    return pltpu.CompilerParams(dimension_semantics=sem, vmem_limit_bytes=VMEM_LIMIT)


def _mod_kernel(c_ref, w_ref, b_ref, o_ref):
    c = c_ref[...]
    cond = c * jax.nn.sigmoid(c)
    hi = cond.astype(BF16)
    lo = (cond - hi.astype(F32)).astype(BF16)
    w = w_ref[0].astype(BF16)
    acc = jnp.dot(hi, w, preferred_element_type=F32) + jnp.dot(lo, w, preferred_element_type=F32)
    o_ref[0] = acc + b_ref[0]


def _mod_call(c_pad, ada_w, ada_b):
    depth, d, n = ada_w.shape
    tn = 1024
    return pl.pallas_call(
        _mod_kernel,
        grid=(depth, n // tn),
        in_specs=[
            pl.BlockSpec((8, d), lambda l, j: (0, 0)),
            pl.BlockSpec((1, d, tn), lambda l, j: (l, 0, j)),
            pl.BlockSpec((1, 1, tn), lambda l, j: (l, 0, j)),
        ],
        out_specs=pl.BlockSpec((1, 8, tn), lambda l, j: (l, 0, j)),
        out_shape=jax.ShapeDtypeStruct((depth, 8, n), F32),
        compiler_params=_params("arbitrary", "arbitrary"),
        name="adaln_mod",
    )(c_pad, ada_w, ada_b.reshape(depth, 1, n))


def _norm_mod_kernel(x_ref, g_ref, sc_ref, sh_ref, *o_refs, transposed):
    x = x_ref[0]
    y = x * lax.rsqrt(jnp.mean(x * x, axis=-1, keepdims=True) + RMS_EPS) * g_ref[...]
    h = y * (1.0 + sc_ref[0]) + sh_ref[0]
    o_refs[0][0] = h.astype(o_refs[0].dtype)
    if transposed:
        o_refs[1][0] = h.T.astype(o_refs[1].dtype)


def _norm_mod_call(x, g, sc, sh, out_dtype, transposed=False):
    b, s, d = x.shape
    tr = 512
    out_shape = [jax.ShapeDtypeStruct((b, s, d), out_dtype)]
    out_specs = [pl.BlockSpec((1, tr, d), lambda i, j: (i, j, 0))]
    if transposed:
        out_shape.append(jax.ShapeDtypeStruct((b, d, s), out_dtype))
        out_specs.append(pl.BlockSpec((1, d, tr), lambda i, j: (i, 0, j)))
    return pl.pallas_call(
        functools.partial(_norm_mod_kernel, transposed=transposed),
        grid=(b, s // tr),
        in_specs=[
            pl.BlockSpec((1, tr, d), lambda i, j: (i, j, 0)),
            pl.BlockSpec((1, d), lambda i, j: (0, 0)),
            pl.BlockSpec((1, 1, d), lambda i, j: (i, 0, 0)),
            pl.BlockSpec((1, 1, d), lambda i, j: (i, 0, 0)),
        ],
        out_specs=out_specs,
        out_shape=out_shape,
        compiler_params=_params("arbitrary", "arbitrary"),
        name="norm_mod",
    )(x, g, sc, sh)


def _proj_kernel(a_ref, w_ref, *rest, shift, keep):
    o_ref, wb_ref = rest[-2:]

    @pl.when(pl.program_id(1) == 0)
    def _():
        w = w_ref[0]
        if shift:
            w = jnp.concatenate([w[:, shift:], rest[0][0][:, :shift]], axis=1)
        if keep is not None:
            lane = lax.broadcasted_iota(jnp.int32, w.shape, 1)
            w = jnp.where(lane < keep, w, 0.0)
        wb_ref[...] = w.astype(BF16)

    o_ref[...] = jnp.dot(a_ref[...], wb_ref[...], preferred_element_type=F32).astype(o_ref.dtype)


def _proj_call(a, w, layer, col0, n, out_dtype, bn, keep=None, bm=1024):
    m, k = a.shape
    shift = col0 % LANES
    base = (col0 - shift) // bn
    assert (col0 - shift) % bn == 0 and n % bn == 0 and bn % LANES == 0
    next_stride = bn // LANES
    in_specs = [pl.BlockSpec((bm, k), lambda j, i: (i, 0)),
                pl.BlockSpec((1, k, bn), lambda j, i: (layer, 0, base + j))]
    operands = [a, w]
    if shift:
        in_specs.append(pl.BlockSpec((1, k, LANES), lambda j, i: (layer, 0, (base + j + 1) * next_stride)))
        operands.append(w)
    return pl.pallas_call(
        functools.partial(_proj_kernel, shift=shift, keep=keep),
        grid=(n // bn, m // bm),
        in_specs=in_specs,
        out_specs=pl.BlockSpec((bm, bn), lambda j, i: (i, j)),
        out_shape=jax.ShapeDtypeStruct((m, n), out_dtype),
        scratch_shapes=[pltpu.VMEM((k, bn), BF16)],
        compiler_params=_params("arbitrary", "arbitrary"),
        name="proj",
    )(*operands)


def _matmul_residual_kernel(a_ref, w_ref, x_ref, gt_ref, o_ref):
    acc = jnp.dot(a_ref[0], w_ref[...], preferred_element_type=F32)
    o_ref[0] = x_ref[0] + gt_ref[0] * acc


def _matmul_residual_call(a, w, x, gt, bm=1024, bn=1024):
    b, s, k = a.shape
    n = w.shape[1]
    return pl.pallas_call(
        _matmul_residual_kernel,
        grid=(b, s // bm, n // bn),
        in_specs=[pl.BlockSpec((1, bm, k), lambda bi, i, j: (bi, i, 0)),
                  pl.BlockSpec((k, bn), lambda bi, i, j: (0, j)),
                  pl.BlockSpec((1, bm, bn), lambda bi, i, j: (bi, i, j)),
                  pl.BlockSpec((1, 1, bn), lambda bi, i, j: (bi, 0, j))],
        out_specs=pl.BlockSpec((1, bm, bn), lambda bi, i, j: (bi, i, j)),
        out_shape=jax.ShapeDtypeStruct((b, s, n), F32),
        compiler_params=_params("arbitrary", "arbitrary", "arbitrary"),
        name="matmul_residual",
    )(a, w, x, gt)


def _split3(x):
    hi = x.astype(BF16)
    r = x - hi.astype(F32)
    mid = r.astype(BF16)
    lo = (r - mid.astype(F32)).astype(BF16)
    return hi, mid, lo


def _forget_cumsum_kernel(fl_ref, bf_ref, col_ref, row_ref):
    s = fl_ref.shape[1]
    blk = 256
    logf = jax.nn.log_sigmoid(fl_ref[0] + bf_ref[...])
    r = lax.broadcasted_iota(jnp.int32, (blk, blk), 0)
    c = lax.broadcasted_iota(jnp.int32, (blk, blk), 1)
    tri = jnp.where(c <= r, 1.0, 0.0).astype(BF16)
    carry = jnp.zeros((1, LANES), F32)
    parts = []
    for i in range(s // blk):
        hi, mid, lo = _split3(logf[i * blk:(i + 1) * blk])
        cs = (jnp.dot(tri, hi, preferred_element_type=F32)
              + jnp.dot(tri, mid, preferred_element_type=F32)
              + jnp.dot(tri, lo, preferred_element_type=F32)) + carry
        carry = cs[blk - 1:blk, :]
        parts.append(cs)
    cum = jnp.concatenate(parts, axis=0)
    row_ref[0] = cum.T[:8, :]
    for h in range(FOX_HEADS):
        col_ref[0, h] = jnp.broadcast_to(cum[:, h:h + 1], (s, LANES))


def _forget_cumsum_call(f_logit, b_forget_pad):
    b, s, _ = f_logit.shape
    return pl.pallas_call(
        _forget_cumsum_kernel,
        grid=(b,),
        in_specs=[pl.BlockSpec((1, s, LANES), lambda i: (i, 0, 0)),
                  pl.BlockSpec((1, LANES), lambda i: (0, 0))],
        out_specs=[pl.BlockSpec((1, FOX_HEADS, s, LANES), lambda i: (i, 0, 0, 0)),
                   pl.BlockSpec((1, 8, s), lambda i: (i, 0, 0))],
        out_shape=[jax.ShapeDtypeStruct((b, FOX_HEADS, s, LANES), F32),
                   jax.ShapeDtypeStruct((b, 8, s), F32)],
        compiler_params=_params("arbitrary"),
        name="forget_cumsum",
    )(f_logit, b_forget_pad)


ATT_TILE = 512


def _qk(q, k):
    return lax.dot_general(q, k, (((1,), (1,)), ((), ())), preferred_element_type=F32)


def _kv_block(k_ref, v_ref, j):
    start = pl.multiple_of(j * ATT_TILE, ATT_TILE)
    return start, k_ref[pl.ds(start, ATT_TILE), :], v_ref[pl.ds(start, ATT_TILE), :]


def _tile_iotas():
    r = lax.broadcasted_iota(jnp.int32, (ATT_TILE, ATT_TILE), 0)
    c = lax.broadcasted_iota(jnp.int32, (ATT_TILE, ATT_TILE), 1)
    return r, c


def _softmax_state(rows=ATT_TILE):
    return [pltpu.VMEM((rows, LANES), F32), pltpu.VMEM((rows, LANES), F32), pltpu.VMEM((rows, HEAD_DIM), F32)]


def _softmax_reset(m_ref, l_ref, acc_ref):
    m_ref[...] = jnp.full(m_ref.shape, NEG_INF, F32)
    l_ref[...] = jnp.zeros(l_ref.shape, F32)
    acc_ref[...] = jnp.zeros(acc_ref.shape, F32)


def _softmax_update(m_ref, l_ref, acc_ref, s, vb, rows=slice(None)):
    m_old = m_ref[rows, :]
    m_new = jnp.maximum(m_old, jnp.max(s, axis=-1, keepdims=True))
    alpha = jnp.exp(m_old - m_new)
    p = jnp.exp(s - jnp.tile(m_new, (1, s.shape[1] // LANES)))
    l_ref[rows, :] = alpha * l_ref[rows, :] + jnp.sum(p, axis=-1, keepdims=True)
    acc_ref[rows, :] = alpha * acc_ref[rows, :] + jnp.dot(p.astype(BF16), vb, preferred_element_type=F32)
    m_ref[rows, :] = m_new


FOX_SUB = 2


def _fox_kernel(q_ref, k_ref, v_ref, fc_ref, fr_ref, o_ref, m_ref, l_ref, acc_ref):
    t, n = ATT_TILE, FOX_SUB
    qi = pl.program_id(2)
    h = pl.program_id(1)
    q = (q_ref[...].astype(F32) * (HEAD_DIM ** -0.5)).astype(BF16)

    def step(j, rows, masked):
        start, kb, vb = _kv_block(k_ref, v_ref, j)
        fcol = jnp.tile(fc_ref[0, 0, rows, :], (1, t // LANES))
        frow = fr_ref[0, pl.ds(h, 1), pl.ds(start, t)]
        s = _qk(q[rows], kb) + (fcol - frow)
        if masked:
            r, c = _tile_iotas()
            s = jnp.where(c <= r, s, NEG_INF)
        _softmax_update(m_ref, l_ref, acc_ref, s, vb, rows)

    _softmax_reset(m_ref, l_ref, acc_ref)
    for u in range(n):
        rows = slice(u * t, (u + 1) * t)
        step(qi * n + u, rows, masked=True)
        for d in range(u):
            step(qi * n + d, rows, masked=False)

    @pl.loop(0, qi * n)
    def _(j):
        step(j, slice(0, n * t), masked=False)

    o_ref[...] = (acc_ref[...] / l_ref[...]).astype(o_ref.dtype)


def _fox_call(proj, f_col, f_row, batch, seq):
    tq = ATT_TILE * FOX_SUB
    nq = seq // tq
    return pl.pallas_call(
        _fox_kernel,
        grid=(batch, FOX_HEADS, nq),
        in_specs=[
            pl.BlockSpec((tq, HEAD_DIM), lambda b, h, i: (b * nq + i, COL_FQ + h)),
            pl.BlockSpec((seq, HEAD_DIM), lambda b, h, i: (b, COL_FK + h)),
            pl.BlockSpec((seq, HEAD_DIM), lambda b, h, i: (b, COL_FV + h)),
            pl.BlockSpec((1, 1, tq, LANES), lambda b, h, i: (b, h, i, 0)),
            pl.BlockSpec((1, 8, seq), lambda b, h, i: (b, 0, 0)),
        ],
        out_specs=pl.BlockSpec((tq, HEAD_DIM), lambda b, h, i: (b * nq + i, h)),
        out_shape=jax.ShapeDtypeStruct((batch * seq, FOX_WIDTH), BF16),
        scratch_shapes=_softmax_state(rows=tq),
        compiler_params=_params("arbitrary", "arbitrary", "arbitrary"),
        name="fox_attention",
    )(proj, proj, proj, f_col, f_row)


SB_SEG = 256
SB_SUB = 2


def _sb_kernel(q_ref, k_ref, v_ref, tri_ref, o_ref, run_ref, acc_ref):
    t, n = ATT_TILE, SB_SUB
    qi = pl.program_id(2)
    log2e = math.log2(math.e)
    qn = (q_ref[...].astype(F32) * -(HEAD_DIM ** -0.5 * log2e)).astype(BF16)

    def step(j, rows, masked):
        _, kb, vb = _kv_block(k_ref, v_ref, j)
        tri = tri_ref[...]
        u = _qk(qn[rows], kb)
        lk = jnp.minimum(u, 0.0) - jnp.log(1.0 + jnp.exp2(-jnp.abs(u))) * log2e
        if masked:
            r, c = _tile_iotas()
            strictly_causal = c < r
            lk = jnp.where(strictly_causal, lk, 0.0)
        lk_bf = lk.astype(BF16)
        run = run_ref[rows, :]
        acc = acc_ref[rows, :]
        for g in reversed(range(t // SB_SEG)):
            sl = slice(g * SB_SEG, (g + 1) * SB_SEG)
            later = jnp.dot(lk_bf[:, sl], tri, preferred_element_type=F32) + jnp.tile(run, (1, SB_SEG // LANES))
            w = jnp.exp2(lk[:, sl] - u[:, sl] + later)
            if masked:
                w = jnp.where(strictly_causal[:, sl], w, 0.0)
            acc = acc + jnp.dot(w.astype(BF16), vb[sl, :], preferred_element_type=F32)
            run = run + jnp.sum(lk[:, sl], axis=-1, keepdims=True)
        acc_ref[rows, :] = acc
        run_ref[rows, :] = run

    run_ref[...] = jnp.zeros(run_ref.shape, F32)
    acc_ref[...] = jnp.zeros(acc_ref.shape, F32)
    for s in range(n):
        rows = slice(s * t, (s + 1) * t)
        step(qi * n + s, rows, masked=True)
        for d in reversed(range(s)):
            step(qi * n + d, rows, masked=False)

    @pl.loop(0, qi * n)
    def _(i):
        step(qi * n - 1 - i, slice(0, n * t), masked=False)

    o_ref[...] = acc_ref[...].astype(o_ref.dtype)


def _sb_call(proj, tri, batch, seq):
    tq = ATT_TILE * SB_SUB
    nq = seq // tq
    return pl.pallas_call(
        _sb_kernel,
        grid=(batch, SB_HEADS, nq),
        in_specs=[
            pl.BlockSpec((tq, HEAD_DIM), lambda b, h, i: (b * nq + i, COL_SQ + h)),
            pl.BlockSpec((seq, HEAD_DIM), lambda b, h, i: (b, COL_SK + h)),
            pl.BlockSpec((seq, HEAD_DIM), lambda b, h, i: (b, COL_SV + h)),
            pl.BlockSpec((SB_SEG, SB_SEG), lambda b, h, i: (0, 0)),
        ],
        out_specs=pl.BlockSpec((tq, HEAD_DIM), lambda b, h, i: (b * nq + i, h)),
        out_shape=jax.ShapeDtypeStruct((batch * seq, SB_WIDTH), BF16),
        scratch_shapes=[pltpu.VMEM((tq, LANES), F32), pltpu.VMEM((tq, HEAD_DIM), F32)],
        compiler_params=_params("arbitrary", "arbitrary", "arbitrary"),
        name="sb_attention",
    )(proj, proj, proj, tri)


def _rope_kernel(x_ref, pos_ref, freq_ref, o_ref, *, out_scale):
    half = DIFF_QK_DIM // 2
    ang = pos_ref[...].astype(F32) * freq_ref[...]
    cos = jnp.cos(ang)
    sin = jnp.sin(ang)
    lane = lax.broadcasted_iota(jnp.int32, ang.shape, 1)
    first = (lane % DIFF_QK_DIM) < half
    sin_signed = jnp.where(first, -sin, sin)
    for cb in range(x_ref.shape[1] // LANES):
        x = x_ref[:, cb * LANES:(cb + 1) * LANES].astype(F32)
        partner = jnp.where(first, pltpu.roll(x, LANES - half, 1), pltpu.roll(x, half, 1))
        y = x * cos + partner * sin_signed
        o_ref[:, cb * LANES:(cb + 1) * LANES] = (y * out_scale).astype(o_ref.dtype)


def _rope_call(proj, pos_col, freq_row, col_block, out_scale):
    tokens = proj.shape[0]
    tr = 512
    width = DIFF_HEADS * HEAD_DIM
    return pl.pallas_call(
        functools.partial(_rope_kernel, out_scale=out_scale),
        grid=(tokens // tr,),
        in_specs=[pl.BlockSpec((tr, width), lambda i: (i, col_block * LANES // width)),
                  pl.BlockSpec((tr, 1), lambda i: (i, 0)),
                  pl.BlockSpec((1, LANES), lambda i: (0, 0))],
        out_specs=pl.BlockSpec((tr, width), lambda i: (i, 0)),
        out_shape=jax.ShapeDtypeStruct((tokens, width), BF16),
        compiler_params=_params("arbitrary"),
        name="rope",
    )(proj, pos_col, freq_row)


def _diff_kernel(q_ref, k_ref, v_ref, lq1_ref, lk1_ref, lq2_ref, lk2_ref, g_ref, o_ref,
                 m_ref, l_ref, acc_ref, *, lam_init):
    t = ATT_TILE
    qi = pl.program_id(2)
    n = DIFF_SUB
    lane = lax.broadcasted_iota(jnp.int32, (t, HEAD_DIM), 1)
    stacked = []
    for u in range(n):
        q = q_ref[u * t:(u + 1) * t, :]
        zero = jnp.zeros_like(q)
        stacked += [jnp.where(lane < DIFF_QK_DIM, q, zero), jnp.where(lane < DIFF_QK_DIM, zero, q)]
    q12 = jnp.concatenate(stacked, axis=0)
    state = (m_ref, l_ref, acc_ref)

    def step(j, rows, masked):
        _, kb, vb = _kv_block(k_ref, v_ref, j)
        s = _qk(q12[rows], kb)
        if masked:
            r, c = _tile_iotas()
            visible = (c // CHUNK) <= (r // CHUNK)
            s = jnp.where(jnp.concatenate([visible, visible], axis=0), s, NEG_INF)
        _softmax_update(*state, s, vb, rows)

    _softmax_reset(*state)
    for u in range(n):
        rows = slice(u * 2 * t, (u + 1) * 2 * t)
        step(qi * n + u, rows, masked=True)
        for d in range(u):
            step(qi * n + d, rows, masked=False)

    @pl.loop(0, qi * n)
    def _(j):
        step(j, slice(0, n * 2 * t), masked=False)

    lam = (jnp.exp(jnp.sum(lq1_ref[0] * lk1_ref[0], axis=-1, keepdims=True))
           - jnp.exp(jnp.sum(lq2_ref[0] * lk2_ref[0], axis=-1, keepdims=True)) + lam_init)
    for u in range(n):
        first = slice(u * 2 * t, u * 2 * t + t)
        second = slice(u * 2 * t + t, (u + 1) * 2 * t)
        o = acc_ref[first, :] / l_ref[first, :] - lam * (acc_ref[second, :] / l_ref[second, :])
        o = o * lax.rsqrt(jnp.mean(o * o, axis=-1, keepdims=True) + RMS_EPS) * g_ref[...]
        o_ref[u * t:(u + 1) * t, :] = (o * (1.0 - lam_init)).astype(o_ref.dtype)


DIFF_SUB = 2


def _diff_call(qd, kd, proj, lam_q1, lam_k1, lam_q2, lam_k2, subln_g, lam_init, batch, seq):
    t = ATT_TILE * DIFF_SUB
    nq = seq // t
    lam_spec = pl.BlockSpec((1, 1, DIFF_QK_DIM), lambda b, h, i: (h, 0, 0))
    r3 = lambda a: a.reshape(DIFF_HEADS, 1, DIFF_QK_DIM)
    return pl.pallas_call(
        functools.partial(_diff_kernel, lam_init=lam_init),
        grid=(batch, DIFF_HEADS, nq),
        in_specs=[
            pl.BlockSpec((t, HEAD_DIM), lambda b, h, i: (b * nq + i, h)),
            pl.BlockSpec((seq, HEAD_DIM), lambda b, h, i: (b, h)),
            pl.BlockSpec((seq, HEAD_DIM), lambda b, h, i: (b, COL_DV + h)),
            lam_spec, lam_spec, lam_spec, lam_spec,
            pl.BlockSpec((1, HEAD_DIM), lambda b, h, i: (0, 0)),
        ],
        out_specs=pl.BlockSpec((t, HEAD_DIM), lambda b, h, i: (b * nq + i, h)),
        out_shape=jax.ShapeDtypeStruct((batch * seq, DIFF_WIDTH), BF16),
        scratch_shapes=_softmax_state(rows=2 * t),
        compiler_params=_params("arbitrary", "arbitrary", "arbitrary"),
        name="diff_attention",
    )(qd, kd, proj, r3(lam_q1), r3(lam_k1), r3(lam_q2), r3(lam_k2), subln_g.reshape(1, HEAD_DIM))


def _merge_kernel(of_ref, os_ref, od_ref, wf_ref, ws_ref, wd_ref, gf_ref, gs_ref, gd_ref, o_ref):
    def branch(o_r, w_r, g_r):
        return jax.nn.sigmoid(g_r[...].astype(F32)) * jnp.dot(o_r[...], w_r[...], preferred_element_type=F32)

    y = branch(of_ref, wf_ref, gf_ref) + branch(os_ref, ws_ref, gs_ref) + branch(od_ref, wd_ref, gd_ref)
    o_ref[...] = y.astype(o_ref.dtype)


def _merge_call(o_fox, o_sb, o_diff, w_fox, w_sb, w_diff, gates, bm=1024, bn=1024):
    m = o_fox.shape[0]
    d = w_fox.shape[1]
    gate_blocks = d // bn

    def gate_spec(k):
        return pl.BlockSpec((bm, bn), lambda i, j: (i, k * gate_blocks + j))

    return pl.pallas_call(
        _merge_kernel,
        grid=(m // bm, d // bn),
        in_specs=[
            pl.BlockSpec((bm, FOX_WIDTH), lambda i, j: (i, 0)),
            pl.BlockSpec((bm, SB_WIDTH), lambda i, j: (i, 0)),
            pl.BlockSpec((bm, DIFF_WIDTH), lambda i, j: (i, 0)),
            pl.BlockSpec((FOX_WIDTH, bn), lambda i, j: (0, j)),
            pl.BlockSpec((SB_WIDTH, bn), lambda i, j: (0, j)),
            pl.BlockSpec((DIFF_WIDTH, bn), lambda i, j: (0, j)),
            gate_spec(0), gate_spec(1), gate_spec(2),
        ],
        out_specs=pl.BlockSpec((bm, bn), lambda i, j: (i, j)),
        out_shape=jax.ShapeDtypeStruct((m, d), BF16),
        compiler_params=_params("arbitrary", "arbitrary"),
        name="branch_merge",
    )(o_fox, o_sb, o_diff, w_fox, w_sb, w_diff, gates, gates, gates)


NOT_TOP = 64.0


def _extract_top(cur, count, want_rank=False):
    rows = []
    rank = jnp.full(cur.shape, NOT_TOP, F32) if want_rank else None
    for r in range(count):
        m = jnp.max(cur, axis=0, keepdims=True)
        rows.append(m)
        hit = cur == m
        if want_rank:
            rank = jnp.where(hit, float(r), rank)
        cur = jnp.where(hit, NEG_INF, cur)
    return rows, rank


def _peer_topk_kernel(q_ref, keys_ref, a_ref, l_ref, b_ref, r2_ref):
    k = PEER_TOPK
    for h in range(PEER_HEADS):
        col = 2 * h * PEER_HALF
        s1 = _qk(keys_ref[h, 0], q_ref[:, col:col + PEER_HALF])
        s2 = _qk(keys_ref[h, 1], q_ref[:, col + PEER_HALF:col + 2 * PEER_HALF])
        a_rows, _ = _extract_top(s1, k)
        b_rows, rank2 = _extract_top(s2, k, want_rank=True)
        b_top = jnp.concatenate(b_rows, axis=0)
        counts = [k // (k1 + 1) for k1 in range(k)]
        cands = [a_rows[k1] + b_top[:counts[k1]] for k1 in range(k)]
        n_cand = sum(counts)
        pad = jnp.full((-n_cand % 8, b_top.shape[1]), NEG_INF, F32)
        best, _ = _extract_top(jnp.concatenate(cands + [pad], axis=0), k)
        tau = best[k - 1]
        z = jnp.zeros_like(tau)
        for row in best:
            z = z + jnp.exp(row - best[0])
        l_row = jnp.zeros(s1.shape, F32)
        for k1 in range(k):
            l_k = jnp.sum(jnp.where(cands[k1] >= tau, 1.0, 0.0), axis=0, keepdims=True)
            l_row = jnp.where(s1 == a_rows[k1], l_k, l_row)
        a_ref[h] = jnp.exp(s1 - a_rows[0]) / z
        l_ref[h] = l_row
        b_ref[h] = jnp.exp(s2 - b_rows[0]).astype(b_ref.dtype)
        r2_ref[h] = rank2.astype(r2_ref.dtype)


def _peer_topk_call(qp, keys):
    tokens = qp.shape[0]
    tq = 256
    spec = pl.BlockSpec((PEER_HEADS, PEER_N_KEYS, tq), lambda i: (0, 0, i))
    shape = lambda dt: jax.ShapeDtypeStruct((PEER_HEADS, PEER_N_KEYS, tokens), dt)
    return pl.pallas_call(
        _peer_topk_kernel,
        grid=(tokens // tq,),
        in_specs=[pl.BlockSpec((tq, qp.shape[1]), lambda i: (i, 0)),
                  pl.BlockSpec(keys.shape, lambda i: (0, 0, 0, 0))],
        out_specs=[spec, spec, spec, spec],
        out_shape=[shape(F32), shape(F32), shape(BF16), shape(BF16)],
        compiler_params=_params("arbitrary"),
        name="peer_topk",
    )(qp, keys)


PEER_UP_TOKENS = 1024
PEER_UP_EXPERTS = 1024
PEER_TOKEN_CHUNK = 256
PEER_DOWN_TOKENS = 512
PEER_DOWN_SPLIT = 8
PEER_DOWN_ROWS = 16


def _peer_up_kernel(ht_ref, u_ref, o_ref):
    tc = PEER_TOKEN_CHUNK
    u = u_ref[0].astype(BF16)
    for c in range(o_ref.shape[1] // tc):
        cols = slice(c * tc, (c + 1) * tc)
        hid = jnp.dot(u, ht_ref[0, :, cols], preferred_element_type=F32)
        gelu = 0.5 * hid * (1.0 + lax.erf(hid * (2.0 ** -0.5)))
        o_ref[:, cols] = gelu.astype(o_ref.dtype)


def _peer_down_kernel(vt_ref, g_ref, a_ref, l_ref, b_ref, r2_ref, x_ref, gt_ref, o_ref, acc_ref, act_ref):
    k = pl.program_id(1)
    tn = act_ref.shape[1]

    @pl.when(k == 0)
    def _():
        acc_ref[...] = jnp.zeros_like(acc_ref)

    for r in range(PEER_DOWN_ROWS):
        rows = slice(r * PEER_N_KEYS, (r + 1) * PEER_N_KEYS)

        def key_row(ref, h):
            tile = jnp.broadcast_to(ref[h, r:r + 1, :], (BF16_ROWS, tn)).astype(BF16)
            return jnp.tile(tile, (PEER_N_KEYS // BF16_ROWS, 1))

        gate = jnp.zeros((PEER_N_KEYS, tn), BF16)
        for h in range(PEER_HEADS):
            chosen = r2_ref[h] < key_row(l_ref, h)
            gate = gate + jnp.where(chosen, b_ref[h], jnp.zeros((), BF16)) * key_row(a_ref, h)
        act_ref[rows, :] = gate * g_ref[rows, :]

        group = PEER_DOWN_ROWS // PEER_DOWN_SPLIT
        if (r + 1) % group == 0:
            cols = slice((r + 1 - group) * PEER_N_KEYS, (r + 1) * PEER_N_KEYS)
            acc_ref[...] += jnp.dot(vt_ref[:, cols], act_ref[cols, :], preferred_element_type=F32)

    @pl.when(k == pl.num_programs(1) - 1)
    def _():
        o_ref[0] = x_ref[0] + gt_ref[0] * acc_ref[...].T


def _peer_ffn_call(ht, u, layer, vt, a_t, l_t, b_t, r2_t, x, gt):
    b, d, s = ht.shape
    n_experts = u.shape[1]
    tm, te = PEER_UP_TOKENS, PEER_UP_EXPERTS
    per_b = s // tm
    gelu_t = pl.pallas_call(
        _peer_up_kernel,
        grid=(b * per_b, n_experts // te),
        in_specs=[pl.BlockSpec((1, d, tm), lambda i, e: (i // per_b, 0, i % per_b)),
                  pl.BlockSpec((1, te, d), lambda i, e: (layer, e, 0))],
        out_specs=pl.BlockSpec((te, tm), lambda i, e: (e, i)),
        out_shape=jax.ShapeDtypeStruct((n_experts, b * s), BF16),
        compiler_params=_params("arbitrary", "arbitrary"),
        name="peer_up",
    )(ht, u)

    tn, tk = PEER_DOWN_TOKENS, PEER_DOWN_ROWS * PEER_N_KEYS
    per_b = s // tn
    key_spec = pl.BlockSpec((PEER_HEADS, PEER_N_KEYS, tn), lambda i, k: (0, 0, i))
    row_spec = pl.BlockSpec((PEER_HEADS, PEER_DOWN_ROWS, tn), lambda i, k: (0, k, i))
    return pl.pallas_call(
        _peer_down_kernel,
        grid=(b * per_b, n_experts // tk),
        in_specs=[
            pl.BlockSpec((d, tk), lambda i, k: (0, k)),
            pl.BlockSpec((tk, tn), lambda i, k: (k, i)),
            row_spec, row_spec, key_spec, key_spec,
            pl.BlockSpec((1, tn, d), lambda i, k: (i // per_b, i % per_b, 0)),
            pl.BlockSpec((1, 1, d), lambda i, k: (i // per_b, 0, 0)),
        ],
        out_specs=pl.BlockSpec((1, tn, d), lambda i, k: (i // per_b, i % per_b, 0)),
        out_shape=jax.ShapeDtypeStruct(x.shape, F32),
        scratch_shapes=[pltpu.VMEM((d, tn), F32), pltpu.VMEM((tk, tn), BF16)],
        compiler_params=_params("arbitrary", "arbitrary"),
        name="peer_down",
    )(vt, gelu_t, a_t, l_t, b_t, r2_t, x, gt)


def _peer_vt_kernel(v_ref, vt_ref):
    vt_ref[...] = v_ref[0].T.astype(BF16)


def _peer_vt_call(v, layer):
    _, e, d = v.shape
    te = 512
    return pl.pallas_call(
        _peer_vt_kernel,
        grid=(e // te,),
        in_specs=[pl.BlockSpec((1, te, d), lambda i: (layer, i, 0))],
        out_specs=pl.BlockSpec((d, te), lambda i: (0, i)),
        out_shape=jax.ShapeDtypeStruct((d, e), BF16),
        compiler_params=_params("arbitrary"),
        name="peer_vt",
    )(v)


def kernel(x, c, positions, ada_w, ada_b, norm1_g, norm2_g, w_in, b_forget, lam_q1, lam_k1, lam_q2, lam_k2,
           diff_subln_g, w_br_fox, w_br_sb, w_br_diff, w_o, peer_wq, peer_sub_keys, peer_u, peer_v,
           final_norm_g):
    batch, seq, d = x.shape
    tokens = batch * seq
    assert batch <= 8 and seq % 1024 == 0 and d == D_MODEL

    c_pad = jnp.zeros((8, d), F32).at[:batch].set(c)
    mod = _mod_call(c_pad, ada_w, ada_b)[:, :batch].reshape(DEPTH, batch, N_MOD, 1, d)
    pos_col = positions.reshape(tokens, 1)
    inv_freq = ROPE_THETA ** (-jnp.arange(0, DIFF_QK_DIM, 2, dtype=F32) / DIFF_QK_DIM)
    freq_row = jnp.tile(inv_freq, LANES // inv_freq.shape[0]).reshape(1, LANES)
    ri = np.arange(SB_SEG)
    tri_after = jnp.asarray(ri[:, None] > ri[None, :], BF16)

    for l in range(DEPTH):
        lam_init = 0.8 - 0.6 * math.exp(-0.3 * l)
        sh1, sc1, gt1, sh2, sc2, gt2 = (mod[l, :, k] for k in range(N_MOD))

        h1 = _norm_mod_call(x, norm1_g[l].reshape(1, d), sc1, sh1, BF16)[0].reshape(tokens, d)
        proj = _proj_call(h1, w_in, l, 0, QKV_WIDTH, BF16, bn=1024)
        gates = _proj_call(h1, w_in, l, GATE_OFFSET, N_BRANCHES * d, BF16, bn=1024)
        f_logit = _proj_call(h1, w_in, l, QKV_WIDTH, LANES, F32, bn=LANES, keep=FOX_HEADS)
        f_logit = f_logit.reshape(batch, seq, LANES)
        b_forget_pad = jnp.pad(b_forget[l], (0, LANES - FOX_HEADS)).reshape(1, LANES)
        f_col, f_row = _forget_cumsum_call(f_logit, b_forget_pad)

        o_fox = _fox_call(proj, f_col, f_row, batch, seq)
        o_sb = _sb_call(proj, tri_after, batch, seq)
        qd = _rope_call(proj, pos_col, freq_row, COL_DQ, DIFF_QK_DIM ** -0.5)
        kd = _rope_call(proj, pos_col, freq_row, COL_DK, 1.0)
        o_diff = _diff_call(qd, kd, proj, lam_q1[l], lam_k1[l], lam_q2[l], lam_k2[l], diff_subln_g[l],
                            lam_init, batch, seq)

        y = _merge_call(o_fox, o_sb, o_diff, w_br_fox[l].astype(BF16), w_br_sb[l].astype(BF16),
                        w_br_diff[l].astype(BF16), gates)
        x = _matmul_residual_call(y.reshape(batch, seq, d), w_o[l].astype(BF16), x, gt1)

        h2, h2t = _norm_mod_call(x, norm2_g[l].reshape(1, d), sc2, sh2, BF16, transposed=True)
        qp = _proj_call(h2.reshape(tokens, d), peer_wq, l, 0, peer_wq.shape[2], BF16, bn=1024)
        a_t, l_t, b_t, r2_t = _peer_topk_call(qp, peer_sub_keys[l].astype(BF16))
        x = _peer_ffn_call(h2t, peer_u, l, _peer_vt_call(peer_v, l), a_t, l_t, b_t, r2_t, x, gt2)

    zeros = jnp.zeros((batch, 1, d), F32)
    return _norm_mod_call(x, final_norm_g.reshape(1, d), zeros, zeros, F32)[0]
```

```python
import functools
import math

import jax
import jax.numpy as jnp
import numpy as np
from jax import lax
from jax.experimental import pallas as pl
from jax.experimental.pallas import tpu as pltpu

F32 = jnp.float32
BF16 = jnp.bfloat16

D_MODEL = 2048
DEPTH = 2
CHUNK = 64
HEAD_DIM = 128
FOX_HEADS = 6
SB_HEADS = 6
DIFF_HEADS = 4
DIFF_QK_DIM = HEAD_DIM // 2
FOX_WIDTH = FOX_HEADS * HEAD_DIM
SB_WIDTH = SB_HEADS * HEAD_DIM
DIFF_WIDTH = DIFF_HEADS * HEAD_DIM
QKV_WIDTH = 3 * FOX_WIDTH + 3 * SB_WIDTH + 3 * DIFF_WIDTH
GATE_OFFSET = QKV_WIDTH + FOX_HEADS
ROPE_THETA = 10000.0
RMS_EPS = 1e-6
N_MOD = 6
N_BRANCHES = 3
PEER_HEADS = 8
PEER_N_KEYS = 128
PEER_TOPK = 16
PEER_HALF = 128

LANES = 128
BF16_ROWS = 16
VMEM_LIMIT = 56 * 1024 * 1024
NEG_INF = float("-inf")

COL_FQ, COL_FK, COL_FV = 0, FOX_HEADS, 2 * FOX_HEADS
COL_SQ, COL_SK, COL_SV = 18, 24, 30
COL_DQ, COL_DK, COL_DV = 36, 40, 44


def _params(*sem):
    return pltpu.CompilerParams(dimension_semantics=sem, vmem_limit_bytes=VMEM_LIMIT)


def _mod_kernel(c_ref, w_ref, b_ref, o_ref):
    c = c_ref[...]
    cond = c * jax.nn.sigmoid(c)
    hi = cond.astype(BF16)
    lo = (cond - hi.astype(F32)).astype(BF16)
    w = w_ref[0].astype(BF16)
    acc = jnp.dot(hi, w, preferred_element_type=F32) + jnp.dot(lo, w, preferred_element_type=F32)
    o_ref[0] = acc + b_ref[0]


def _mod_call(c_pad, ada_w, ada_b):
    depth, d, n = ada_w.shape
    tn = 1024
    return pl.pallas_call(
        _mod_kernel,
        grid=(depth, n // tn),
        in_specs=[
            pl.BlockSpec((8, d), lambda l, j: (0, 0)),
            pl.BlockSpec((1, d, tn), lambda l, j: (l, 0, j)),
            pl.BlockSpec((1, 1, tn), lambda l, j: (l, 0, j)),
        ],
        out_specs=pl.BlockSpec((1, 8, tn), lambda l, j: (l, 0, j)),
        out_shape=jax.ShapeDtypeStruct((depth, 8, n), F32),
        compiler_params=_params("arbitrary", "arbitrary"),
        name="adaln_mod",
    )(c_pad, ada_w, ada_b.reshape(depth, 1, n))


def _norm_mod_kernel(x_ref, g_ref, sc_ref, sh_ref, *o_refs, transposed):
    x = x_ref[0]
    y = x * lax.rsqrt(jnp.mean(x * x, axis=-1, keepdims=True) + RMS_EPS) * g_ref[...]
    h = y * (1.0 + sc_ref[0]) + sh_ref[0]
    o_refs[0][0] = h.astype(o_refs[0].dtype)
    if transposed:
        o_refs[1][0] = h.T.astype(o_refs[1].dtype)


def _norm_mod_call(x, g, sc, sh, out_dtype, transposed=False):
    b, s, d = x.shape
    tr = 512
    out_shape = [jax.ShapeDtypeStruct((b, s, d), out_dtype)]
    out_specs = [pl.BlockSpec((1, tr, d), lambda i, j: (i, j, 0))]
    if transposed:
        out_shape.append(jax.ShapeDtypeStruct((b, d, s), out_dtype))
        out_specs.append(pl.BlockSpec((1, d, tr), lambda i, j: (i, 0, j)))
    return pl.pallas_call(
        functools.partial(_norm_mod_kernel, transposed=transposed),
        grid=(b, s // tr),
        in_specs=[
            pl.BlockSpec((1, tr, d), lambda i, j: (i, j, 0)),
            pl.BlockSpec((1, d), lambda i, j: (0, 0)),
            pl.BlockSpec((1, 1, d), lambda i, j: (i, 0, 0)),
            pl.BlockSpec((1, 1, d), lambda i, j: (i, 0, 0)),
        ],
        out_specs=out_specs,
        out_shape=out_shape,
        compiler_params=_params("arbitrary", "arbitrary"),
        name="norm_mod",
    )(x, g, sc, sh)


def _proj_kernel(a_ref, w_ref, *rest, shift, keep):
    o_ref, wb_ref = rest[-2:]

    @pl.when(pl.program_id(1) == 0)
    def _():
        w = w_ref[0]
        if shift:
            w = jnp.concatenate([w[:, shift:], rest[0][0][:, :shift]], axis=1)
        if keep is not None:
            lane = lax.broadcasted_iota(jnp.int32, w.shape, 1)
            w = jnp.where(lane < keep, w, 0.0)
        wb_ref[...] = w.astype(BF16)

    o_ref[...] = jnp.dot(a_ref[...], wb_ref[...], preferred_element_type=F32).astype(o_ref.dtype)


def _proj_call(a, w, layer, col0, n, out_dtype, bn, keep=None, bm=1024):
    m, k = a.shape
    shift = col0 % LANES
    base = (col0 - shift) // bn
    assert (col0 - shift) % bn == 0 and n % bn == 0 and bn % LANES == 0
    next_stride = bn // LANES
    in_specs = [pl.BlockSpec((bm, k), lambda j, i: (i, 0)),
                pl.BlockSpec((1, k, bn), lambda j, i: (layer, 0, base + j))]
    operands = [a, w]
    if shift:
        in_specs.append(pl.BlockSpec((1, k, LANES), lambda j, i: (layer, 0, (base + j + 1) * next_stride)))
        operands.append(w)
    return pl.pallas_call(
        functools.partial(_proj_kernel, shift=shift, keep=keep),
        grid=(n // bn, m // bm),
        in_specs=in_specs,
        out_specs=pl.BlockSpec((bm, bn), lambda j, i: (i, j)),
        out_shape=jax.ShapeDtypeStruct((m, n), out_dtype),
        scratch_shapes=[pltpu.VMEM((k, bn), BF16)],
        compiler_params=_params("arbitrary", "arbitrary"),
        name="proj",
    )(*operands)


def _matmul_residual_kernel(a_ref, w_ref, x_ref, gt_ref, o_ref):
    acc = jnp.dot(a_ref[0], w_ref[...], preferred_element_type=F32)
    o_ref[0] = x_ref[0] + gt_ref[0] * acc


def _matmul_residual_call(a, w, x, gt, bm=1024, bn=1024):
    b, s, k = a.shape
    n = w.shape[1]
    return pl.pallas_call(
        _matmul_residual_kernel,
        grid=(b, s // bm, n // bn),
        in_specs=[pl.BlockSpec((1, bm, k), lambda bi, i, j: (bi, i, 0)),
                  pl.BlockSpec((k, bn), lambda bi, i, j: (0, j)),
                  pl.BlockSpec((1, bm, bn), lambda bi, i, j: (bi, i, j)),
                  pl.BlockSpec((1, 1, bn), lambda bi, i, j: (bi, 0, j))],
        out_specs=pl.BlockSpec((1, bm, bn), lambda bi, i, j: (bi, i, j)),
        out_shape=jax.ShapeDtypeStruct((b, s, n), F32),
        compiler_params=_params("arbitrary", "arbitrary", "arbitrary"),
        name="matmul_residual",
    )(a, w, x, gt)


def _split3(x):
    hi = x.astype(BF16)
    r = x - hi.astype(F32)
    mid = r.astype(BF16)
    lo = (r - mid.astype(F32)).astype(BF16)
    return hi, mid, lo


def _forget_cumsum_kernel(fl_ref, bf_ref, col_ref, row_ref):
    s = fl_ref.shape[1]
    blk = 256
    logf = jax.nn.log_sigmoid(fl_ref[0] + bf_ref[...])
    r = lax.broadcasted_iota(jnp.int32, (blk, blk), 0)
    c = lax.broadcasted_iota(jnp.int32, (blk, blk), 1)
    tri = jnp.where(c <= r, 1.0, 0.0).astype(BF16)
    carry = jnp.zeros((1, LANES), F32)
    parts = []
    for i in range(s // blk):
        hi, mid, lo = _split3(logf[i * blk:(i + 1) * blk])
        cs = (jnp.dot(tri, hi, preferred_element_type=F32)
              + jnp.dot(tri, mid, preferred_element_type=F32)
              + jnp.dot(tri, lo, preferred_element_type=F32)) + carry
        carry = cs[blk - 1:blk, :]
        parts.append(cs)
    cum = jnp.concatenate(parts, axis=0)
    row_ref[0] = cum.T[:8, :]
    for h in range(FOX_HEADS):
        col_ref[0, h] = jnp.broadcast_to(cum[:, h:h + 1], (s, LANES))


def _forget_cumsum_call(f_logit, b_forget_pad):
    b, s, _ = f_logit.shape
    return pl.pallas_call(
        _forget_cumsum_kernel,
        grid=(b,),
        in_specs=[pl.BlockSpec((1, s, LANES), lambda i: (i, 0, 0)),
                  pl.BlockSpec((1, LANES), lambda i: (0, 0))],
        out_specs=[pl.BlockSpec((1, FOX_HEADS, s, LANES), lambda i: (i, 0, 0, 0)),
                   pl.BlockSpec((1, 8, s), lambda i: (i, 0, 0))],
        out_shape=[jax.ShapeDtypeStruct((b, FOX_HEADS, s, LANES), F32),
                   jax.ShapeDtypeStruct((b, 8, s), F32)],
        compiler_params=_params("arbitrary"),
        name="forget_cumsum",
    )(f_logit, b_forget_pad)


ATT_TILE = 512


def _qk(q, k):
    return lax.dot_general(q, k, (((1,), (1,)), ((), ())), preferred_element_type=F32)


def _kv_block(k_ref, v_ref, j):
    start = pl.multiple_of(j * ATT_TILE, ATT_TILE)
    return start, k_ref[pl.ds(start, ATT_TILE), :], v_ref[pl.ds(start, ATT_TILE), :]


def _tile_iotas():
    r = lax.broadcasted_iota(jnp.int32, (ATT_TILE, ATT_TILE), 0)
    c = lax.broadcasted_iota(jnp.int32, (ATT_TILE, ATT_TILE), 1)
    return r, c


def _softmax_state(rows=ATT_TILE):
    return [pltpu.VMEM((rows, LANES), F32), pltpu.VMEM((rows, LANES), F32), pltpu.VMEM((rows, HEAD_DIM), F32)]


def _softmax_reset(m_ref, l_ref, acc_ref):
    m_ref[...] = jnp.full(m_ref.shape, NEG_INF, F32)
    l_ref[...] = jnp.zeros(l_ref.shape, F32)
    acc_ref[...] = jnp.zeros(acc_ref.shape, F32)


def _softmax_update(m_ref, l_ref, acc_ref, s, vb, rows=slice(None)):
    m_old = m_ref[rows, :]
    m_new = jnp.maximum(m_old, jnp.max(s, axis=-1, keepdims=True))
    alpha = jnp.exp(m_old - m_new)
    p = jnp.exp(s - jnp.tile(m_new, (1, s.shape[1] // LANES)))
    l_ref[rows, :] = alpha * l_ref[rows, :] + jnp.sum(p, axis=-1, keepdims=True)
    acc_ref[rows, :] = alpha * acc_ref[rows, :] + jnp.dot(p.astype(BF16), vb, preferred_element_type=F32)
    m_ref[rows, :] = m_new


FOX_SUB = 2


def _fox_kernel(q_ref, k_ref, v_ref, fc_ref, fr_ref, o_ref, m_ref, l_ref, acc_ref):
    t, n = ATT_TILE, FOX_SUB
    qi = pl.program_id(2)
    h = pl.program_id(1)
    q = (q_ref[...].astype(F32) * (HEAD_DIM ** -0.5)).astype(BF16)

    def step(j, rows, masked):
        start, kb, vb = _kv_block(k_ref, v_ref, j)
        fcol = jnp.tile(fc_ref[0, 0, rows, :], (1, t // LANES))
        frow = fr_ref[0, pl.ds(h, 1), pl.ds(start, t)]
        s = _qk(q[rows], kb) + (fcol - frow)
        if masked:
            r, c = _tile_iotas()
            s = jnp.where(c <= r, s, NEG_INF)
        _softmax_update(m_ref, l_ref, acc_ref, s, vb, rows)

    _softmax_reset(m_ref, l_ref, acc_ref)
    for u in range(n):
        rows = slice(u * t, (u + 1) * t)
        step(qi * n + u, rows, masked=True)
        for d in range(u):
            step(qi * n + d, rows, masked=False)

    @pl.loop(0, qi * n)
    def _(j):
        step(j, slice(0, n * t), masked=False)

    o_ref[...] = (acc_ref[...] / l_ref[...]).astype(o_ref.dtype)


def _fox_call(proj, f_col, f_row, batch, seq):
    tq = ATT_TILE * FOX_SUB
    nq = seq // tq
    return pl.pallas_call(
        _fox_kernel,
        grid=(batch, FOX_HEADS, nq),
        in_specs=[
            pl.BlockSpec((tq, HEAD_DIM), lambda b, h, i: (b * nq + i, COL_FQ + h)),
            pl.BlockSpec((seq, HEAD_DIM), lambda b, h, i: (b, COL_FK + h)),
            pl.BlockSpec((seq, HEAD_DIM), lambda b, h, i: (b, COL_FV + h)),
            pl.BlockSpec((1, 1, tq, LANES), lambda b, h, i: (b, h, i, 0)),
            pl.BlockSpec((1, 8, seq), lambda b, h, i: (b, 0, 0)),
        ],
        out_specs=pl.BlockSpec((tq, HEAD_DIM), lambda b, h, i: (b * nq + i, h)),
        out_shape=jax.ShapeDtypeStruct((batch * seq, FOX_WIDTH), BF16),
        scratch_shapes=_softmax_state(rows=tq),
        compiler_params=_params("arbitrary", "arbitrary", "arbitrary"),
        name="fox_attention",
    )(proj, proj, proj, f_col, f_row)


SB_SEG = 256
SB_SUB = 2


def _sb_kernel(q_ref, k_ref, v_ref, tri_ref, o_ref, run_ref, acc_ref):
    t, n = ATT_TILE, SB_SUB
    qi = pl.program_id(2)
    log2e = math.log2(math.e)
    qn = (q_ref[...].astype(F32) * -(HEAD_DIM ** -0.5 * log2e)).astype(BF16)

    def step(j, rows, masked):
        _, kb, vb = _kv_block(k_ref, v_ref, j)
        tri = tri_ref[...]
        u = _qk(qn[rows], kb)
        lk = jnp.minimum(u, 0.0) - jnp.log(1.0 + jnp.exp2(-jnp.abs(u))) * log2e
        if masked:
            r, c = _tile_iotas()
            strictly_causal = c < r
            lk = jnp.where(strictly_causal, lk, 0.0)
        lk_bf = lk.astype(BF16)
        run = run_ref[rows, :]
        laters = []
        for g in reversed(range(t // SB_SEG)):
            sl = slice(g * SB_SEG, (g + 1) * SB_SEG)
            laters.append(jnp.dot(lk_bf[:, sl], tri, preferred_element_type=F32) + jnp.tile(run, (1, SB_SEG // LANES)))
            run = run + jnp.sum(lk[:, sl], axis=-1, keepdims=True)
        later = jnp.concatenate(laters[::-1], axis=1)
        w = jnp.exp2(lk - u + later)
        if masked:
            w = jnp.where(strictly_causal, w, 0.0)
        acc_ref[rows, :] += jnp.dot(w.astype(BF16), vb, preferred_element_type=F32)
        run_ref[rows, :] = run

    run_ref[...] = jnp.zeros(run_ref.shape, F32)
    acc_ref[...] = jnp.zeros(acc_ref.shape, F32)
    for s in range(n):
        rows = slice(s * t, (s + 1) * t)
        step(qi * n + s, rows, masked=True)
        for d in reversed(range(s)):
            step(qi * n + d, rows, masked=False)

    @pl.loop(0, qi * n)
    def _(i):
        step(qi * n - 1 - i, slice(0, n * t), masked=False)

    o_ref[...] = acc_ref[...].astype(o_ref.dtype)


def _sb_call(proj, tri, batch, seq):
    tq = ATT_TILE * SB_SUB
    nq = seq // tq
    return pl.pallas_call(
        _sb_kernel,
        grid=(batch, SB_HEADS, nq),
        in_specs=[
            pl.BlockSpec((tq, HEAD_DIM), lambda b, h, i: (b * nq + i, COL_SQ + h)),
            pl.BlockSpec((seq, HEAD_DIM), lambda b, h, i: (b, COL_SK + h)),
            pl.BlockSpec((seq, HEAD_DIM), lambda b, h, i: (b, COL_SV + h)),
            pl.BlockSpec((SB_SEG, SB_SEG), lambda b, h, i: (0, 0)),
        ],
        out_specs=pl.BlockSpec((tq, HEAD_DIM), lambda b, h, i: (b * nq + i, h)),
        out_shape=jax.ShapeDtypeStruct((batch * seq, SB_WIDTH), BF16),
        scratch_shapes=[pltpu.VMEM((tq, LANES), F32), pltpu.VMEM((tq, HEAD_DIM), F32)],
        compiler_params=_params("arbitrary", "arbitrary", "arbitrary"),
        name="sb_attention",
    )(proj, proj, proj, tri)


def _rope_kernel(xq_ref, xk_ref, pos_ref, freq_ref, oq_ref, ok_ref, *, q_scale):
    half = DIFF_QK_DIM // 2
    ang = pos_ref[...].astype(F32) * freq_ref[...]
    cos = jnp.cos(ang)
    sin = jnp.sin(ang)
    lane = lax.broadcasted_iota(jnp.int32, ang.shape, 1)
    first = (lane % DIFF_QK_DIM) < half
    sin_signed = jnp.where(first, -sin, sin)
    for x_ref, o_ref, scale in ((xq_ref, oq_ref, q_scale), (xk_ref, ok_ref, 1.0)):
        for cb in range(x_ref.shape[1] // LANES):
            x = x_ref[:, cb * LANES:(cb + 1) * LANES].astype(F32)
            partner = jnp.where(first, pltpu.roll(x, LANES - half, 1), pltpu.roll(x, half, 1))
            y = x * cos + partner * sin_signed
            o_ref[:, cb * LANES:(cb + 1) * LANES] = (y * scale).astype(o_ref.dtype)


def _rope_call(proj, pos_col, freq_row, q_scale):
    tokens = proj.shape[0]
    tr = 512
    width = DIFF_HEADS * HEAD_DIM
    col_spec = lambda col_block: pl.BlockSpec((tr, width), lambda i: (i, col_block * LANES // width))
    out_spec = pl.BlockSpec((tr, width), lambda i: (i, 0))
    out_shape = jax.ShapeDtypeStruct((tokens, width), BF16)
    return pl.pallas_call(
        functools.partial(_rope_kernel, q_scale=q_scale),
        grid=(tokens // tr,),
        in_specs=[col_spec(COL_DQ), col_spec(COL_DK),
                  pl.BlockSpec((tr, 1), lambda i: (i, 0)),
                  pl.BlockSpec((1, LANES), lambda i: (0, 0))],
        out_specs=[out_spec, out_spec],
        out_shape=[out_shape, out_shape],
        compiler_params=_params("arbitrary"),
        name="rope",
    )(proj, proj, pos_col, freq_row)


def _diff_kernel(q_ref, k_ref, v_ref, lq1_ref, lk1_ref, lq2_ref, lk2_ref, g_ref, o_ref,
                 m_ref, l_ref, acc_ref, *, lam_init):
    t = ATT_TILE
    qi = pl.program_id(2)
    n = DIFF_SUB
    lane = lax.broadcasted_iota(jnp.int32, (t, HEAD_DIM), 1)
    stacked = []
    for u in range(n):
        q = q_ref[u * t:(u + 1) * t, :]
        zero = jnp.zeros_like(q)
        stacked += [jnp.where(lane < DIFF_QK_DIM, q, zero), jnp.where(lane < DIFF_QK_DIM, zero, q)]
    q12 = jnp.concatenate(stacked, axis=0)
    state = (m_ref, l_ref, acc_ref)

    def step(j, rows, masked):
        _, kb, vb = _kv_block(k_ref, v_ref, j)
        s = _qk(q12[rows], kb)
        if masked:
            r, c = _tile_iotas()
            visible = (c // CHUNK) <= (r // CHUNK)
            s = jnp.where(jnp.concatenate([visible, visible], axis=0), s, NEG_INF)
        _softmax_update(*state, s, vb, rows)

    _softmax_reset(*state)
    for u in range(n):
        rows = slice(u * 2 * t, (u + 1) * 2 * t)
        step(qi * n + u, rows, masked=True)
        for d in range(u):
            step(qi * n + d, rows, masked=False)

    @pl.loop(0, qi * n)
    def _(j):
        step(j, slice(0, n * 2 * t), masked=False)

    lam = (jnp.exp(jnp.sum(lq1_ref[0] * lk1_ref[0], axis=-1, keepdims=True))
           - jnp.exp(jnp.sum(lq2_ref[0] * lk2_ref[0], axis=-1, keepdims=True)) + lam_init)
    for u in range(n):
        first = slice(u * 2 * t, u * 2 * t + t)
        second = slice(u * 2 * t + t, (u + 1) * 2 * t)
        o = acc_ref[first, :] / l_ref[first, :] - lam * (acc_ref[second, :] / l_ref[second, :])
        o = o * lax.rsqrt(jnp.mean(o * o, axis=-1, keepdims=True) + RMS_EPS) * g_ref[...]
        o_ref[u * t:(u + 1) * t, :] = (o * (1.0 - lam_init)).astype(o_ref.dtype)


DIFF_SUB = 2


def _diff_call(qd, kd, proj, lam_q1, lam_k1, lam_q2, lam_k2, subln_g, lam_init, batch, seq):
    t = ATT_TILE * DIFF_SUB
    nq = seq // t
    lam_spec = pl.BlockSpec((1, 1, DIFF_QK_DIM), lambda b, h, i: (h, 0, 0))
    r3 = lambda a: a.reshape(DIFF_HEADS, 1, DIFF_QK_DIM)
    return pl.pallas_call(
        functools.partial(_diff_kernel, lam_init=lam_init),
        grid=(batch, DIFF_HEADS, nq),
        in_specs=[
            pl.BlockSpec((t, HEAD_DIM), lambda b, h, i: (b * nq + i, h)),
            pl.BlockSpec((seq, HEAD_DIM), lambda b, h, i: (b, h)),
            pl.BlockSpec((seq, HEAD_DIM), lambda b, h, i: (b, COL_DV + h)),
            lam_spec, lam_spec, lam_spec, lam_spec,
            pl.BlockSpec((1, HEAD_DIM), lambda b, h, i: (0, 0)),
        ],
        out_specs=pl.BlockSpec((t, HEAD_DIM), lambda b, h, i: (b * nq + i, h)),
        out_shape=jax.ShapeDtypeStruct((batch * seq, DIFF_WIDTH), BF16),
        scratch_shapes=_softmax_state(rows=2 * t),
        compiler_params=_params("arbitrary", "arbitrary", "arbitrary"),
        name="diff_attention",
    )(qd, kd, proj, r3(lam_q1), r3(lam_k1), r3(lam_q2), r3(lam_k2), subln_g.reshape(1, HEAD_DIM))


def _merge_kernel(of_ref, os_ref, od_ref, wf_ref, ws_ref, wd_ref, gf_ref, gs_ref, gd_ref, o_ref):
    def branch(o_r, w_r, g_r):
        return jax.nn.sigmoid(g_r[...].astype(F32)) * jnp.dot(o_r[...], w_r[...], preferred_element_type=F32)

    y = branch(of_ref, wf_ref, gf_ref) + branch(os_ref, ws_ref, gs_ref) + branch(od_ref, wd_ref, gd_ref)
    o_ref[...] = y.astype(o_ref.dtype)


def _merge_call(o_fox, o_sb, o_diff, w_fox, w_sb, w_diff, gates, bm=1024, bn=1024):
    m = o_fox.shape[0]
    d = w_fox.shape[1]
    gate_blocks = d // bn

    def gate_spec(k):
        return pl.BlockSpec((bm, bn), lambda i, j: (i, k * gate_blocks + j))

    return pl.pallas_call(
        _merge_kernel,
        grid=(m // bm, d // bn),
        in_specs=[
            pl.BlockSpec((bm, FOX_WIDTH), lambda i, j: (i, 0)),
            pl.BlockSpec((bm, SB_WIDTH), lambda i, j: (i, 0)),
            pl.BlockSpec((bm, DIFF_WIDTH), lambda i, j: (i, 0)),
            pl.BlockSpec((FOX_WIDTH, bn), lambda i, j: (0, j)),
            pl.BlockSpec((SB_WIDTH, bn), lambda i, j: (0, j)),
            pl.BlockSpec((DIFF_WIDTH, bn), lambda i, j: (0, j)),
            gate_spec(0), gate_spec(1), gate_spec(2),
        ],
        out_specs=pl.BlockSpec((bm, bn), lambda i, j: (i, j)),
        out_shape=jax.ShapeDtypeStruct((m, d), BF16),
        compiler_params=_params("arbitrary", "arbitrary"),
        name="branch_merge",
    )(o_fox, o_sb, o_diff, w_fox, w_sb, w_diff, gates, gates, gates)


NOT_TOP = 64.0


def _extract_top(cur, count, want_rank=False):
    rows = []
    rank = jnp.full(cur.shape, NOT_TOP, F32) if want_rank else None
    for r in range(count):
        m = jnp.max(cur, axis=0, keepdims=True)
        rows.append(m)
        hit = cur == m
        if want_rank:
            rank = jnp.where(hit, float(r), rank)
        cur = jnp.where(hit, NEG_INF, cur)
    return rows, rank


def _peer_topk_kernel(q_ref, keys_ref, a_ref, l_ref, b_ref, r2_ref):
    k = PEER_TOPK
    for h in range(PEER_HEADS):
        col = 2 * h * PEER_HALF
        s1 = _qk(keys_ref[h, 0], q_ref[:, col:col + PEER_HALF])
        s2 = _qk(keys_ref[h, 1], q_ref[:, col + PEER_HALF:col + 2 * PEER_HALF])
        a_rows, _ = _extract_top(s1, k)
        b_rows, rank2 = _extract_top(s2, k, want_rank=True)
        b_top = jnp.concatenate(b_rows, axis=0)
        counts = [k // (k1 + 1) for k1 in range(k)]
        cands = [a_rows[k1] + b_top[:counts[k1]] for k1 in range(k)]
        n_cand = sum(counts)
        pad = jnp.full((-n_cand % 8, b_top.shape[1]), NEG_INF, F32)
        best, _ = _extract_top(jnp.concatenate(cands + [pad], axis=0), k)
        tau = best[k - 1]
        z = jnp.zeros_like(tau)
        for row in best:
            z = z + jnp.exp(row - best[0])
        l_row = jnp.zeros(s1.shape, F32)
        for k1 in range(k):
            l_k = jnp.sum(jnp.where(cands[k1] >= tau, 1.0, 0.0), axis=0, keepdims=True)
            l_row = jnp.where(s1 == a_rows[k1], l_k, l_row)
        a_ref[h] = jnp.exp(s1 - a_rows[0]) / z
        l_ref[h] = l_row
        b_ref[h] = jnp.exp(s2 - b_rows[0]).astype(b_ref.dtype)
        r2_ref[h] = rank2.astype(r2_ref.dtype)


def _peer_topk_call(qp, keys):
    tokens = qp.shape[0]
    tq = 256
    spec = pl.BlockSpec((PEER_HEADS, PEER_N_KEYS, tq), lambda i: (0, 0, i))
    shape = lambda dt: jax.ShapeDtypeStruct((PEER_HEADS, PEER_N_KEYS, tokens), dt)
    return pl.pallas_call(
        _peer_topk_kernel,
        grid=(tokens // tq,),
        in_specs=[pl.BlockSpec((tq, qp.shape[1]), lambda i: (i, 0)),
                  pl.BlockSpec(keys.shape, lambda i: (0, 0, 0, 0))],
        out_specs=[spec, spec, spec, spec],
        out_shape=[shape(F32), shape(F32), shape(BF16), shape(BF16)],
        compiler_params=_params("arbitrary"),
        name="peer_topk",
    )(qp, keys)


PEER_UP_TOKENS = 1024
PEER_UP_EXPERTS = 1024
PEER_TOKEN_CHUNK = 256
PEER_DOWN_TOKENS = 512
PEER_DOWN_SPLIT = 8
PEER_DOWN_ROWS = 16


def _peer_up_kernel(ht_ref, u_ref, o_ref):
    tc = PEER_TOKEN_CHUNK
    u = u_ref[0].astype(BF16)
    for c in range(o_ref.shape[1] // tc):
        cols = slice(c * tc, (c + 1) * tc)
        hid = jnp.dot(u, ht_ref[0, :, cols], preferred_element_type=F32)
        gelu = 0.5 * hid * (1.0 + lax.erf(hid * (2.0 ** -0.5)))
        o_ref[:, cols] = gelu.astype(o_ref.dtype)


def _peer_down_kernel(vt_ref, g_ref, a_ref, l_ref, b_ref, r2_ref, x_ref, gt_ref, o_ref, acc_ref, act_ref):
    k = pl.program_id(1)
    tn = act_ref.shape[1]

    @pl.when(k == 0)
    def _():
        acc_ref[...] = jnp.zeros_like(acc_ref)

    for r in range(PEER_DOWN_ROWS):
        rows = slice(r * PEER_N_KEYS, (r + 1) * PEER_N_KEYS)

        def key_row(ref, h):
            tile = jnp.broadcast_to(ref[h, r:r + 1, :], (BF16_ROWS, tn)).astype(BF16)
            return jnp.tile(tile, (PEER_N_KEYS // BF16_ROWS, 1))

        gate = jnp.zeros((PEER_N_KEYS, tn), BF16)
        for h in range(PEER_HEADS):
            chosen = r2_ref[h] < key_row(l_ref, h)
            gate = gate + jnp.where(chosen, b_ref[h], jnp.zeros((), BF16)) * key_row(a_ref, h)
        act_ref[rows, :] = gate * g_ref[rows, :]

        group = PEER_DOWN_ROWS // PEER_DOWN_SPLIT
        if (r + 1) % group == 0:
            cols = slice((r + 1 - group) * PEER_N_KEYS, (r + 1) * PEER_N_KEYS)
            acc_ref[...] += jnp.dot(vt_ref[:, cols], act_ref[cols, :], preferred_element_type=F32)

    @pl.when(k == pl.num_programs(1) - 1)
    def _():
        o_ref[0] = x_ref[0] + gt_ref[0] * acc_ref[...].T


def _peer_ffn_call(ht, u, layer, vt, a_t, l_t, b_t, r2_t, x, gt):
    b, d, s = ht.shape
    n_experts = u.shape[1]
    tm, te = PEER_UP_TOKENS, PEER_UP_EXPERTS
    per_b = s // tm
    gelu_t = pl.pallas_call(
        _peer_up_kernel,
        grid=(b * per_b, n_experts // te),
        in_specs=[pl.BlockSpec((1, d, tm), lambda i, e: (i // per_b, 0, i % per_b)),
                  pl.BlockSpec((1, te, d), lambda i, e: (layer, e, 0))],
        out_specs=pl.BlockSpec((te, tm), lambda i, e: (e, i)),
        out_shape=jax.ShapeDtypeStruct((n_experts, b * s), BF16),
        compiler_params=_params("arbitrary", "arbitrary"),
        name="peer_up",
    )(ht, u)

    tn, tk = PEER_DOWN_TOKENS, PEER_DOWN_ROWS * PEER_N_KEYS
    per_b = s // tn
    key_spec = pl.BlockSpec((PEER_HEADS, PEER_N_KEYS, tn), lambda i, k: (0, 0, i))
    row_spec = pl.BlockSpec((PEER_HEADS, PEER_DOWN_ROWS, tn), lambda i, k: (0, k, i))
    return pl.pallas_call(
        _peer_down_kernel,
        grid=(b * per_b, n_experts // tk),
        in_specs=[
            pl.BlockSpec((d, tk), lambda i, k: (0, k)),
            pl.BlockSpec((tk, tn), lambda i, k: (k, i)),
            row_spec, row_spec, key_spec, key_spec,
            pl.BlockSpec((1, tn, d), lambda i, k: (i // per_b, i % per_b, 0)),
            pl.BlockSpec((1, 1, d), lambda i, k: (i // per_b, 0, 0)),
        ],
        out_specs=pl.BlockSpec((1, tn, d), lambda i, k: (i // per_b, i % per_b, 0)),
        out_shape=jax.ShapeDtypeStruct(x.shape, F32),
        scratch_shapes=[pltpu.VMEM((d, tn), F32), pltpu.VMEM((tk, tn), BF16)],
        compiler_params=_params("arbitrary", "arbitrary"),
        name="peer_down",
    )(vt, gelu_t, a_t, l_t, b_t, r2_t, x, gt)


def _peer_vt_kernel(v_ref, vt_ref):
    vt_ref[...] = v_ref[0].T.astype(BF16)


def _peer_vt_call(v, layer):
    _, e, d = v.shape
    te = 512
    return pl.pallas_call(
        _peer_vt_kernel,
        grid=(e // te,),
        in_specs=[pl.BlockSpec((1, te, d), lambda i: (layer, i, 0))],
        out_specs=pl.BlockSpec((d, te), lambda i: (0, i)),
        out_shape=jax.ShapeDtypeStruct((d, e), BF16),
        compiler_params=_params("arbitrary"),
        name="peer_vt",
    )(v)


def kernel(x, c, positions, ada_w, ada_b, norm1_g, norm2_g, w_in, b_forget, lam_q1, lam_k1, lam_q2, lam_k2,
           diff_subln_g, w_br_fox, w_br_sb, w_br_diff, w_o, peer_wq, peer_sub_keys, peer_u, peer_v,
           final_norm_g):
    batch, seq, d = x.shape
    tokens = batch * seq
    assert batch <= 8 and seq % 1024 == 0 and d == D_MODEL

    c_pad = jnp.zeros((8, d), F32).at[:batch].set(c)
    mod = _mod_call(c_pad, ada_w, ada_b)[:, :batch].reshape(DEPTH, batch, N_MOD, 1, d)
    pos_col = positions.reshape(tokens, 1)
    inv_freq = ROPE_THETA ** (-jnp.arange(0, DIFF_QK_DIM, 2, dtype=F32) / DIFF_QK_DIM)
    freq_row = jnp.tile(inv_freq, LANES // inv_freq.shape[0]).reshape(1, LANES)
    ri = np.arange(SB_SEG)
    tri_after = jnp.asarray(ri[:, None] > ri[None, :], BF16)

    for l in range(DEPTH):
        lam_init = 0.8 - 0.6 * math.exp(-0.3 * l)
        sh1, sc1, gt1, sh2, sc2, gt2 = (mod[l, :, k] for k in range(N_MOD))

        h1 = _norm_mod_call(x, norm1_g[l].reshape(1, d), sc1, sh1, BF16)[0].reshape(tokens, d)
        proj = _proj_call(h1, w_in, l, 0, QKV_WIDTH, BF16, bn=1024)
        gates = _proj_call(h1, w_in, l, GATE_OFFSET, N_BRANCHES * d, BF16, bn=1024)
        f_logit = _proj_call(h1, w_in, l, QKV_WIDTH, LANES, F32, bn=LANES, keep=FOX_HEADS)
        f_logit = f_logit.reshape(batch, seq, LANES)
        b_forget_pad = jnp.pad(b_forget[l], (0, LANES - FOX_HEADS)).reshape(1, LANES)
        f_col, f_row = _forget_cumsum_call(f_logit, b_forget_pad)

        o_fox = _fox_call(proj, f_col, f_row, batch, seq)
        o_sb = _sb_call(proj, tri_after, batch, seq)
        qd, kd = _rope_call(proj, pos_col, freq_row, DIFF_QK_DIM ** -0.5)
        o_diff = _diff_call(qd, kd, proj, lam_q1[l], lam_k1[l], lam_q2[l], lam_k2[l], diff_subln_g[l],
                            lam_init, batch, seq)

        y = _merge_call(o_fox, o_sb, o_diff, w_br_fox[l].astype(BF16), w_br_sb[l].astype(BF16),
                        w_br_diff[l].astype(BF16), gates)
        x = _matmul_residual_call(y.reshape(batch, seq, d), w_o[l].astype(BF16), x, gt1)

        h2, h2t = _norm_mod_call(x, norm2_g[l].reshape(1, d), sc2, sh2, BF16, transposed=True)
        qp = _proj_call(h2.reshape(tokens, d), peer_wq, l, 0, peer_wq.shape[2], BF16, bn=1024)
        a_t, l_t, b_t, r2_t = _peer_topk_call(qp, peer_sub_keys[l].astype(BF16))
        x = _peer_ffn_call(h2t, peer_u, l, _peer_vt_call(peer_v, l), a_t, l_t, b_t, r2_t, x, gt2)

    zeros = jnp.zeros((batch, 1, d), F32)
    return _norm_mod_call(x, final_norm_g.reshape(1, d), zeros, zeros, F32)[0]
```
